```python
import jax, jax.numpy as jnp
from jax import lax
import numpy as np

D_MODEL = 1024
BATCH = 8
SEQ = 4096
DEPTH = 1

GRID_W = 64
CTX_LEN = 256
D_RWKV = 512
D_CONV = 512
RWKV_HEAD = 64
RWKV_HEADS = D_RWKV // RWKV_HEAD
CONV_WIDTH = 3
LORA_W = 64
LORA_A = 64
LORA_G = 128
N_EXPERTS = 16
EC_CAPACITY = 2
D_EXPERT = 1024
NORM_EPS = 1e-6
GN_EPS = 64e-5
RWKV_COLS = 3 * D_RWKV + LORA_W + LORA_A + LORA_G
CONV_COLS = 3 * D_CONV
D_IN = RWKV_COLS + CONV_COLS

kernel_name = "hybrid_rwkv7_shortconv_ecmoe_dit_block"


def rmsnorm(x, g):
    xf = x.astype(jnp.float32)
    y = xf * lax.rsqrt(jnp.mean(xf * xf, axis=-1, keepdims=True) + NORM_EPS)
    return (y * g.astype(jnp.float32)).astype(x.dtype)


def modulate(h, shift, scale):
    return h * (1 + scale) + shift


def grid_shift(p, rows):
    b, t, ch = p.shape
    q = p.reshape(b, rows, GRID_W, ch // 4, 4)
    zc = jnp.zeros_like(q[:, :, :1, :, 0])
    zr = jnp.zeros_like(q[:, :1, :, :, 0])
    left = jnp.concatenate([zc, q[:, :, :-1, :, 0]], axis=2)
    right = jnp.concatenate([q[:, :, 1:, :, 1], zc], axis=2)
    up = jnp.concatenate([zr, q[:, :-1, :, :, 2]], axis=1)
    down = jnp.concatenate([q[:, 1:, :, :, 3], zr], axis=1)
    return jnp.stack([left, right, up, down], axis=-1).reshape(b, t, ch)


def seq_shift(p):
    b, t, ch = p.shape
    q = p.reshape(b, t, ch // 2, 2)
    z = jnp.zeros_like(q[:, :1, :, 0])
    prev = jnp.concatenate([z, q[:, :-1, :, 0]], axis=1)
    nxt = jnp.concatenate([q[:, 1:, :, 1], z], axis=1)
    return jnp.stack([prev, nxt], axis=-1).reshape(b, t, ch)


def split_rwkv(p, mu, shifted):
    p = p + mu * (shifted - p)
    o = 3 * D_RWKV
    return jnp.split(p, [D_RWKV, 2 * D_RWKV, o, o + LORA_W, o + LORA_W + LORA_A], axis=-1)


def wkv_scan(s0, r, w, k, v, a, b, reverse):
    def step(s, inp):
        r_t, w_t, k_t, v_t, a_t, b_t = inp
        sa = jnp.einsum('bhvk,bhk->bhv', s, a_t)
        s = s * w_t[:, :, None, :] + sa[..., :, None] * b_t[:, :, None, :] + v_t[..., :, None] * k_t[:, :, None, :]
        return s, jnp.einsum('bhvk,bhk->bhv', s, r_t)
    xs = tuple(jnp.moveaxis(z, 1, 0) for z in (r, w, k, v, a, b))
    s, y = lax.scan(step, s0, xs, reverse=reverse)
    return jnp.moveaxis(y, 0, 1), s


def rwkv_bidir(xr, xk, xv, xw, xa, w0, w_up, a0, a_up, k_k, k_a, s0_f, s0_b):
    bsz, t = xr.shape[0], xr.shape[1]
    f32 = jnp.float32
    heads = lambda z: z.astype(f32).reshape(bsz, t, RWKV_HEADS, RWKV_HEAD)
    per_head = lambda z: z.astype(f32).reshape(RWKV_HEADS, RWKV_HEAD)
    r, k, v = heads(xr), heads(xk), heads(xv)
    kk = k * per_head(k_k)
    kk = kk / jnp.maximum(jnp.sqrt(jnp.sum(kk * kk, axis=-1, keepdims=True)), 1e-12)
    tw = jnp.tanh(xw.astype(f32))
    xa = xa.astype(f32)
    k_a_h = per_head(k_a)
    y_sum, k_sum, states = 0.0, 0.0, []
    for d, (rev, s0) in enumerate(((False, s0_f), (True, s0_b))):
        wl = -jax.nn.softplus(-(w0[d].astype(f32) + tw @ w_up[d].astype(f32))) - 0.5
        decay = jnp.exp(-jnp.exp(heads(wl)))
        a = heads(jax.nn.sigmoid(a0[d].astype(f32) + xa @ a_up[d].astype(f32)))
        kd = k * (1 + (a - 1) * k_a_h)
        y, s = wkv_scan(s0, r, decay, kd, v, -kk, kk * a, rev)
        y_sum = y_sum + y
        k_sum = k_sum + kd
        states.append(s)
    return y_sum, states[0], states[1], r, k_sum, v


def rwkv_out(y, r, k_sum, v, xg, g_up, r_k, gn_g, gn_b):
    bsz, t = y.shape[0], y.shape[1]
    f32 = jnp.float32
    mu = jnp.mean(y, axis=-1, keepdims=True)
    var = jnp.mean(jnp.square(y - mu), axis=-1, keepdims=True)
    yn = ((y - mu) * lax.rsqrt(var + GN_EPS)).reshape(bsz, t, D_RWKV) * gn_g.astype(f32) + gn_b.astype(f32)
    bonus = (jnp.sum(r * k_sum * r_k.astype(f32), axis=-1, keepdims=True) * v).reshape(bsz, t, D_RWKV)
    g = jax.nn.sigmoid(xg.astype(f32)) @ g_up.astype(f32)
    return (yn + bonus) * g


def dwconv3(u, w):
    return lax.conv_general_dilated(u, w[:, None, :].astype(u.dtype), window_strides=(1,), padding=((1, 1),),
                                    dimension_numbers=('NWC', 'WIO', 'NWC'), feature_group_count=u.shape[-1])


def short_conv(p, conv_w):
    b_gate, c_gate, u = jnp.split(p, 3, axis=-1)
    return b_gate * dwconv3(c_gate * u, conv_w)


def expert_choice_ffn(h, router_w, w_gate, w_up, w_down):
    bsz, t, d = h.shape
    cap = EC_CAPACITY * t // N_EXPERTS
    aff = jax.nn.softmax((h @ router_w).astype(jnp.float32), axis=-1)
    vals, idx = lax.top_k(jnp.swapaxes(aff, 1, 2), cap)
    xs = jax.vmap(lambda hb, ib: hb[ib])(h, idx)
    hid = jax.nn.silu(jnp.einsum('becd,edf->becf', xs, w_gate)) * jnp.einsum('becd,edf->becf', xs, w_up)
    out = jnp.einsum('becf,efd->becd', hid, w_down) * vals[..., None].astype(h.dtype)
    return jax.vmap(lambda ob, ib: jnp.zeros((t, d), h.dtype).at[ib.reshape(-1)].add(ob.reshape(-1, d)))(out, idx)


def setup_inputs(seed: int = 0) -> dict:
    key = jax.random.key(seed)
    ks = jax.random.split(key, 28)
    L = DEPTH
    nrm = lambda k, shape, scale: jax.random.normal(k, shape, jnp.float32) * scale
    return {
        'x': nrm(ks[0], (BATCH, SEQ, D_MODEL), 1.0),
        'c': nrm(ks[1], (BATCH, D_MODEL), 1.0),
        'ctx': nrm(ks[2], (BATCH, CTX_LEN, D_MODEL), 1.0),
        'c_ctx': nrm(ks[3], (D_MODEL,), 1.0),
        'ada_w': nrm(ks[4], (L, D_MODEL, 6 * D_MODEL), 0.5 * D_MODEL ** -0.5),
        'ada_b': nrm(ks[5], (L, 6 * D_MODEL), 0.02),
        'norm1_g': 1.0 + nrm(ks[6], (L, D_MODEL), 0.02),
        'norm2_g': 1.0 + nrm(ks[7], (L, D_MODEL), 0.02),
        'w_in': nrm(ks[8], (L, D_MODEL, D_IN), D_MODEL ** -0.5),
        'shift_mu': jax.random.uniform(ks[9], (L, RWKV_COLS), jnp.float32),
        'w0': jax.random.uniform(ks[10], (L, 2, D_RWKV), jnp.float32, minval=-6.0, maxval=1.0),
        'w_lora_up': nrm(ks[11], (L, 2, LORA_W, D_RWKV), 0.1 * LORA_W ** -0.5),
        'a0': nrm(ks[12], (L, 2, D_RWKV), 0.5),
        'a_lora_up': nrm(ks[13], (L, 2, LORA_A, D_RWKV), LORA_A ** -0.5),
        'k_k': 0.85 + nrm(ks[14], (L, D_RWKV), 0.05),
        'k_a': 1.0 + nrm(ks[15], (L, D_RWKV), 0.05),
        'r_k': nrm(ks[16], (L, RWKV_HEADS, RWKV_HEAD), 0.1),
        'g_lora_up': nrm(ks[17], (L, LORA_G, D_RWKV), LORA_G ** -0.5),
        'gn_g': 1.0 + nrm(ks[18], (L, D_RWKV), 0.02),
        'gn_b': nrm(ks[19], (L, D_RWKV), 0.02),
        'conv_w': nrm(ks[20], (L, CONV_WIDTH, D_CONV), CONV_WIDTH ** -0.5),
        'w_out': nrm(ks[21], (L, D_RWKV + D_CONV, D_MODEL), (D_RWKV + D_CONV) ** -0.5),
        'router_w': nrm(ks[22], (L, D_MODEL, N_EXPERTS), D_MODEL ** -0.5),
        'exp_w_gate': nrm(ks[23], (L, N_EXPERTS, D_MODEL, D_EXPERT), D_MODEL ** -0.5),
        'exp_w_up': nrm(ks[24], (L, N_EXPERTS, D_MODEL, D_EXPERT), D_MODEL ** -0.5),
        'exp_w_down': nrm(ks[25], (L, N_EXPERTS, D_EXPERT, D_MODEL), D_EXPERT ** -0.5),
        'final_g': 1.0 + nrm(ks[26], (D_MODEL,), 0.02),
    }


def reference(x, c, ctx, c_ctx, ada_w, ada_b, norm1_g, norm2_g, w_in, shift_mu, w0, w_lora_up, a0, a_lora_up,
              k_k, k_a, r_k, g_lora_up, gn_g, gn_b, conv_w, w_out, router_w, exp_w_gate, exp_w_up, exp_w_down,
              final_g):
    bsz, t, _ = x.shape
    rows = t // GRID_W
    s_zero = jnp.zeros((bsz, RWKV_HEADS, RWKV_HEAD, RWKV_HEAD), jnp.float32)
    for l in range(DEPTH):
        last = l == DEPTH - 1
        mod = jax.nn.silu(c) @ ada_w[l] + ada_b[l]
        mod_c = jax.nn.silu(c_ctx) @ ada_w[l] + ada_b[l]
        sh1, sc1, g1, sh2, sc2, g2 = jnp.split(mod[:, None, :], 6, axis=-1)
        csh1, csc1, cg1, csh2, csc2, cg2 = jnp.split(mod_c, 6, axis=-1)

        pc = modulate(rmsnorm(ctx, norm1_g[l]), csh1, csc1) @ w_in[l]
        px = modulate(rmsnorm(x, norm1_g[l]), sh1, sc1) @ w_in[l]
        pc_rw, pc_cv = pc[..., :RWKV_COLS], pc[..., RWKV_COLS:]
        px_rw, px_cv = px[..., :RWKV_COLS], px[..., RWKV_COLS:]

        cr, ck, cv, cw, ca, cgl = split_rwkv(pc_rw, shift_mu[l], seq_shift(pc_rw))
        xr, xk, xv, xw, xa, xgl = split_rwkv(px_rw, shift_mu[l], grid_shift(px_rw, rows))

        yc, sf, sb, rc, kc, vc = rwkv_bidir(cr, ck, cv, cw, ca, w0[l], w_lora_up[l], a0[l], a_lora_up[l],
                                            k_k[l], k_a[l], s_zero, s_zero)
        yx, _, _, rx, kx, vx = rwkv_bidir(xr, xk, xv, xw, xa, w0[l], w_lora_up[l], a0[l], a_lora_up[l],
                                          k_k[l], k_a[l], sf, sb)
        a_x = rwkv_out(yx, rx, kx, vx, xgl, g_lora_up[l], r_k[l], gn_g[l], gn_b[l]).astype(x.dtype)
        b_x = short_conv(px_cv, conv_w[l])
        x = x + g1 * (jnp.concatenate([a_x, b_x], axis=-1) @ w_out[l])
        if not last:
            a_c = rwkv_out(yc, rc, kc, vc, cgl, g_lora_up[l], r_k[l], gn_g[l], gn_b[l]).astype(ctx.dtype)
            b_c = short_conv(pc_cv, conv_w[l])
            ctx = ctx + cg1 * (jnp.concatenate([a_c, b_c], axis=-1) @ w_out[l])

        hx = modulate(rmsnorm(x, norm2_g[l]), sh2, sc2)
        x = x + g2 * expert_choice_ffn(hx, router_w[l], exp_w_gate[l], exp_w_up[l], exp_w_down[l])
        if not last:
            hc = modulate(rmsnorm(ctx, norm2_g[l]), csh2, csc2)
            ctx = ctx + cg2 * expert_choice_ffn(hc, router_w[l], exp_w_gate[l], exp_w_up[l], exp_w_down[l])
    return rmsnorm(x, final_g)
```

```python
import functools
import math

import jax
import jax.numpy as jnp
from jax import lax
from jax.experimental import pallas as pl
from jax.experimental.pallas import tpu as pltpu

F32 = jnp.float32
BF16 = jnp.bfloat16
HIGHEST = lax.Precision.HIGHEST

GRID_W = 64
D_RWKV = 512
D_CONV = 512
HEAD = 64
LORA_W = 64
LORA_A = 64
LORA_G = 128
N_EXPERTS = 16
EC_CAPACITY = 2
NORM_EPS = 1e-6
GN_EPS = 64e-5
RWKV_COLS = 3 * D_RWKV + LORA_W + LORA_A + LORA_G
CONV_COLS = 3 * D_CONV

CHUNK = 64
PAIR = 2 * HEAD
N_PAIR = D_RWKV // PAIR
OPS_COLS = 9 * D_RWKV
VMEM_LIMIT = 48 * 1024 * 1024


def _cparams(sem):
    return pltpu.CompilerParams(dimension_semantics=sem, vmem_limit_bytes=VMEM_LIMIT)


def _dot(a, b):
    return jnp.dot(a, b, preferred_element_type=F32)


def _dot_nt(a, b):
    return lax.dot_general(a, b, (((1,), (1,)), ((), ())), preferred_element_type=F32)


def _split2(x):
    hi = x.astype(BF16)
    lo = (x - hi.astype(F32)).astype(BF16)
    return hi, lo


def _split3(x):
    hi = x.astype(BF16)
    r = x - hi.astype(F32)
    mid = r.astype(BF16)
    lo = (r - mid.astype(F32)).astype(BF16)
    return hi, mid, lo


def _seg_dot(x, m):
    hi, lo = _split2(x)
    return _dot(hi, m) + _dot(lo, m)


def _mod_kernel(c_ref, w_ref, b_ref, o_ref):
    c = c_ref[...]
    s = c * jax.nn.sigmoid(c)
    o_ref[...] = jnp.dot(s, w_ref[...], precision=HIGHEST, preferred_element_type=F32) + b_ref[...]


def _mod(cc, w, b):
    rows, d = cc.shape
    n = w.shape[1]
    tn = 1024
    return pl.pallas_call(
        _mod_kernel,
        grid=(n // tn,),
        in_specs=[pl.BlockSpec((rows, d), lambda j: (0, 0)),
                  pl.BlockSpec((d, tn), lambda j: (0, j)),
                  pl.BlockSpec((1, tn), lambda j: (0, j))],
        out_specs=pl.BlockSpec((rows, tn), lambda j: (0, j)),
        out_shape=jax.ShapeDtypeStruct((rows, n), F32),
        compiler_params=_cparams(("parallel",)),
        name="mod",
    )(cc, w, b)


def _in_proj_kernel(x_ref, sh_ref, sc_ref, g_ref, wrw_ref, wcv_ref, orw_ref, ocv_ref):
    x = x_ref[0]
    ms = jnp.mean(x * x, axis=-1, keepdims=True)
    h = x * lax.rsqrt(ms + NORM_EPS) * g_ref[...]
    h = (h * (1.0 + sc_ref[0]) + sh_ref[0]).astype(BF16)
    orw_ref[0] = _dot(h, wrw_ref[...])
    ocv_ref[0] = _dot(h, wcv_ref[...])


def _in_proj(x, sh, sc, g, w_rw, w_cv, tm):
    bsz, l, d = x.shape
    return pl.pallas_call(
        _in_proj_kernel,
        grid=(bsz, l // tm),
        in_specs=[pl.BlockSpec((1, tm, d), lambda b, i: (b, i, 0)),
                  pl.BlockSpec((1, 1, d), lambda b, i: (b, 0, 0)),
                  pl.BlockSpec((1, 1, d), lambda b, i: (b, 0, 0)),
                  pl.BlockSpec((1, d), lambda b, i: (0, 0)),
                  pl.BlockSpec((d, RWKV_COLS), lambda b, i: (0, 0)),
                  pl.BlockSpec((d, CONV_COLS), lambda b, i: (0, 0))],
        out_specs=[pl.BlockSpec((1, tm, RWKV_COLS), lambda b, i: (b, i, 0)),
                   pl.BlockSpec((1, tm, CONV_COLS), lambda b, i: (b, i, 0))],
        out_shape=[jax.ShapeDtypeStruct((bsz, l, RWKV_COLS), F32),
                   jax.ShapeDtypeStruct((bsz, l, CONV_COLS), F32)],
        compiler_params=_cparams(("parallel", "parallel")),
        name="in_proj",
    )(x, sh, sc, g, w_rw, w_cv)


def _prep_body(tt, p, shifted, mu_ref, lw_ref, w0_ref, a0_ref, kk_ref, ka_ref, rk_ref, gup_ref,
               ones_ref, tril_ref, triu_ref, ops_ref, wcf_ref, wcb_ref, bonus_ref, g_ref):
    pm = p + mu_ref[...] * (shifted - p)
    xr = pm[:, 0:D_RWKV]
    xk = pm[:, D_RWKV:2 * D_RWKV]
    xv = pm[:, 2 * D_RWKV:3 * D_RWKV]
    xwa = pm[:, 3 * D_RWKV:3 * D_RWKV + LORA_W + LORA_A]
    xg = pm[:, 3 * D_RWKV + LORA_W + LORA_A:]

    ones_bd = ones_ref[...]
    kraw = xk * kk_ref[...]
    ss = _seg_dot(kraw * kraw, ones_bd)
    kk = kraw / jnp.maximum(jnp.sqrt(ss), 1e-12)

    lane = lax.broadcasted_iota(jnp.int32, xwa.shape, 1)
    lin = jnp.where(lane < LORA_W, jnp.tanh(xwa), xwa).astype(BF16)
    lo = _dot(lin, lw_ref[...])

    ops_ref[0, :, 8 * D_RWKV:] = xv.astype(BF16)
    ksum = None
    for d in range(2):
        z = w0_ref[d:d + 1, :] + lo[:, d * D_RWKV:(d + 1) * D_RWKV]
        ld = -math.exp(-0.5) * jax.nn.sigmoid(z)
        ag = jax.nn.sigmoid(a0_ref[d:d + 1, :] + lo[:, (2 + d) * D_RWKV:(3 + d) * D_RWKV])
        kd = xk * (1.0 + (ag - 1.0) * ka_ref[...])
        bb = kk * ag
        ksum = kd if ksum is None else ksum + kd
        tri = tril_ref[...] if d == 0 else triu_ref[...]
        h3 = _split3(ld)
        cs = _dot(tri, h3[0]) + _dot(tri, h3[1]) + _dot(tri, h3[2])
        e_in = jnp.exp(cs)
        e_ex = jnp.exp(cs - ld)
        e_neg = jnp.exp(-cs)
        base = 4 * d * D_RWKV
        ops_ref[0, :, base:base + D_RWKV] = (-kk * e_ex).astype(BF16)
        ops_ref[0, :, base + D_RWKV:base + 2 * D_RWKV] = (xr * e_in).astype(BF16)
        ops_ref[0, :, base + 2 * D_RWKV:base + 3 * D_RWKV] = (bb * e_neg).astype(BF16)
        ops_ref[0, :, base + 3 * D_RWKV:base + 4 * D_RWKV] = (kd * e_neg).astype(BF16)
        wc_ref = wcf_ref if d == 0 else wcb_ref
        for c in range(tt // CHUNK):
            row = c * CHUNK + (CHUNK - 1 if d == 0 else 0)
            wc_ref[0, c] = e_in[row:row + 1, :]

    bonus_ref[0] = _seg_dot(xr * ksum * rk_ref[...], ones_bd) * xv
    g_ref[0] = _dot(jax.nn.sigmoid(xg).astype(BF16), gup_ref[...])


def _prep_latent_kernel(tt, p_ref, prev_ref, next_ref, *rest):
    ext_ref = rest[-1]
    i = pl.program_id(1)
    n = pl.num_programs(1)
    ext_ref[0:GRID_W] = jnp.where(i > 0, prev_ref[0], 0.0)
    ext_ref[GRID_W:GRID_W + tt] = p_ref[0]
    ext_ref[GRID_W + tt:] = jnp.where(i < n - 1, next_ref[0], 0.0)
    shape = (tt, RWKV_COLS)
    col = lax.broadcasted_iota(jnp.int32, shape, 0) & (GRID_W - 1)
    q = lax.broadcasted_iota(jnp.int32, shape, 1) & 3
    left = jnp.where(col == 0, 0.0, ext_ref[GRID_W - 1:GRID_W - 1 + tt])
    right = jnp.where(col == GRID_W - 1, 0.0, ext_ref[GRID_W + 1:GRID_W + 1 + tt])
    up = ext_ref[0:tt]
    down = ext_ref[2 * GRID_W:2 * GRID_W + tt]
    shifted = jnp.where(q == 0, left, jnp.where(q == 1, right, jnp.where(q == 2, up, down)))
    _prep_body(tt, p_ref[0], shifted, *rest[:-1])


def _prep_ctx_kernel(tt, p_ref, *rest):
    ext_ref = rest[-1]
    ext_ref[0:8] = jnp.zeros((8, RWKV_COLS), F32)
    ext_ref[8:8 + tt] = p_ref[0]
    ext_ref[8 + tt:] = jnp.zeros((8, RWKV_COLS), F32)
    q = lax.broadcasted_iota(jnp.int32, (tt, RWKV_COLS), 1) & 1
    shifted = jnp.where(q == 0, ext_ref[7:7 + tt], ext_ref[9:9 + tt])
    _prep_body(tt, p_ref[0], shifted, *rest[:-1])


def _prep(p_rw, consts, latent, tt):
    bsz, l, _ = p_rw.shape
    nc = l // CHUNK
    cpt = tt // CHUNK
    const_specs = [pl.BlockSpec(c.shape, lambda b, i, nd=c.ndim: (0,) * nd) for c in consts]
    if latent:
        nblk = l // GRID_W
        in_specs = [pl.BlockSpec((1, tt, RWKV_COLS), lambda b, i: (b, i, 0)),
                    pl.BlockSpec((1, GRID_W, RWKV_COLS),
                                 lambda b, i: (b, jnp.maximum(i * (tt // GRID_W) - 1, 0), 0)),
                    pl.BlockSpec((1, GRID_W, RWKV_COLS),
                                 lambda b, i: (b, jnp.minimum((i + 1) * (tt // GRID_W), nblk - 1), 0))]
        args = (p_rw, p_rw, p_rw)
        kern = functools.partial(_prep_latent_kernel, tt)
        ext_rows = tt + 2 * GRID_W
    else:
        assert tt == l
        in_specs = [pl.BlockSpec((1, tt, RWKV_COLS), lambda b, i: (b, i, 0))]
        args = (p_rw,)
        kern = functools.partial(_prep_ctx_kernel, tt)
        ext_rows = tt + 16
    return pl.pallas_call(
        kern,
        grid=(bsz, l // tt),
        in_specs=in_specs + const_specs,
        out_specs=[pl.BlockSpec((1, tt, OPS_COLS), lambda b, i: (b, i, 0)),
                   pl.BlockSpec((1, cpt, 1, D_RWKV), lambda b, i: (b, i, 0, 0)),
                   pl.BlockSpec((1, cpt, 1, D_RWKV), lambda b, i: (b, i, 0, 0)),
                   pl.BlockSpec((1, tt, D_RWKV), lambda b, i: (b, i, 0)),
                   pl.BlockSpec((1, tt, D_RWKV), lambda b, i: (b, i, 0))],
        out_shape=[jax.ShapeDtypeStruct((bsz, l, OPS_COLS), BF16),
                   jax.ShapeDtypeStruct((bsz, nc, 1, D_RWKV), F32),
                   jax.ShapeDtypeStruct((bsz, nc, 1, D_RWKV), F32),
                   jax.ShapeDtypeStruct((bsz, l, D_RWKV), F32),
                   jax.ShapeDtypeStruct((bsz, l, D_RWKV), F32)],
        scratch_shapes=[pltpu.VMEM((ext_rows, RWKV_COLS), F32)],
        compiler_params=_cparams(("parallel", "parallel")),
        name="prep_latent" if latent else "prep_ctx",
    )(*args, *consts)


def _scan_chunk(reverse, ops, v, wc, s):
    c = CHUNK
    a, r, bt, kt = (ops[:, j * PAIR:(j + 1) * PAIR] for j in range(4))
    lane = lax.broadcasted_iota(jnp.int32, (c, PAIR), 1)
    h0 = lane < HEAD
    zero = jnp.zeros_like(a)
    ar = jnp.concatenate([jnp.where(h0, a, zero), jnp.where(h0, zero, a),
                          jnp.where(h0, r, zero), jnp.where(h0, zero, r)], axis=0)
    bk = jnp.concatenate([bt, bt, kt, kt], axis=0)
    gram = _dot_nt(ar, bk)
    row = lax.broadcasted_iota(jnp.int32, (PAIR, PAIR), 0)
    col = lax.broadcasted_iota(jnp.int32, (PAIR, PAIR), 1)
    same = (row >= c) == (col >= c)
    tr, tc = row & (c - 1), col & (c - 1)
    strict = same & ((tc > tr) if reverse else (tc < tr))
    incl = same & ((tc >= tr) if reverse else (tc <= tr))
    lab = jnp.where(strict, gram[:PAIR, :PAIR], 0.0)
    lak = jnp.where(strict, gram[:PAIR, PAIR:], 0.0)
    mrb = jnp.where(incl, gram[PAIR:, :PAIR], 0.0)
    mrk = jnp.where(incl, gram[PAIR:, PAIR:], 0.0)

    p0 = _dot_nt(ar, s.astype(BF16))

    tinv = jnp.where(row == col, 1.0, 0.0) + lab
    pw = lab
    for _ in range(5):
        pwb = pw.astype(BF16)
        pw = _dot(pwb, pwb)
        tinv = tinv + _dot(tinv.astype(BF16), pw.astype(BF16))

    v_rep = jnp.concatenate([v, v], axis=0)
    rhs = p0[:PAIR] + _dot(lak.astype(BF16), v_rep)
    u_st = _dot(tinv.astype(BF16), rhs.astype(BF16))
    u = jnp.where(h0, u_st[:c], u_st[c:]).astype(BF16)
    u_rep = jnp.concatenate([u, u], axis=0)
    y_st = p0[PAIR:] + _dot(jnp.concatenate([mrb, mrk], axis=1).astype(BF16),
                            jnp.concatenate([u_rep, v_rep], axis=0))
    y = jnp.where(h0, y_st[:c], y_st[c:])

    uv = jnp.concatenate([u, v], axis=0).astype(F32)
    ds = _dot(uv.T.astype(BF16), jnp.concatenate([bt, kt], axis=0))
    s_new = jnp.where(same, (s + ds) * wc, 0.0)
    return y, s_new


def _scan_kernel(opsf_ref, opsb_ref, vf_ref, vb_ref, wcf_ref, wcb_ref, s0_ref, yf_ref, yb_ref, s_ref):
    i = pl.program_id(1)

    @pl.when(i == 0)
    def _():
        s_ref[...] = s0_ref[...]

    for d, (ops_ref, v_ref, wc_ref, y_ref) in enumerate(((opsf_ref, vf_ref, wcf_ref, yf_ref),
                                                          (opsb_ref, vb_ref, wcb_ref, yb_ref))):
        for p in range(N_PAIR):
            lanes = slice(p * PAIR, (p + 1) * PAIR)
            ops = jnp.concatenate([ops_ref[0, :, j * D_RWKV + p * PAIR:j * D_RWKV + (p + 1) * PAIR]
                                   for j in range(4)], axis=1)
            y, s_new = _scan_chunk(d == 1, ops, v_ref[0, :, lanes], wc_ref[0, 0, :, lanes], s_ref[0, d, p])
            y_ref[0, :, lanes] = y
            s_ref[0, d, p] = s_new


def _scan(ops, wcf, wcb, s0):
    bsz, l, _ = ops.shape
    nc = l // CHUNK
    fwd = lambda b, i: (b, i, 0)
    bwd = lambda b, i: (b, nc - 1 - i, 0)
    st_spec = pl.BlockSpec((1, 2, N_PAIR, PAIR, PAIR), lambda b, i: (b, 0, 0, 0, 0))
    return pl.pallas_call(
        _scan_kernel,
        grid=(bsz, nc),
        in_specs=[pl.BlockSpec((1, CHUNK, 4 * D_RWKV), fwd),
                  pl.BlockSpec((1, CHUNK, 4 * D_RWKV), lambda b, i: (b, nc - 1 - i, 1)),
                  pl.BlockSpec((1, CHUNK, D_RWKV), lambda b, i: (b, i, 8)),
                  pl.BlockSpec((1, CHUNK, D_RWKV), lambda b, i: (b, nc - 1 - i, 8)),
                  pl.BlockSpec((1, 1, 1, D_RWKV), lambda b, i: (b, i, 0, 0)),
                  pl.BlockSpec((1, 1, 1, D_RWKV), lambda b, i: (b, nc - 1 - i, 0, 0)),
                  st_spec],
        out_specs=[pl.BlockSpec((1, CHUNK, D_RWKV), fwd),
                   pl.BlockSpec((1, CHUNK, D_RWKV), bwd),
                   st_spec],
        out_shape=[jax.ShapeDtypeStruct((bsz, l, D_RWKV), F32),
                   jax.ShapeDtypeStruct((bsz, l, D_RWKV), F32),
                   jax.ShapeDtypeStruct((bsz, 2, N_PAIR, PAIR, PAIR), F32)],
        compiler_params=_cparams(("parallel", "arbitrary")),
        name="scan",
    )(ops, ops, ops, ops, wcf, wcb, s0)


def _out_kernel(tt, yf_ref, yb_ref, bonus_ref, g_ref, cv_ref, cvp_ref, cvn_ref, x_ref, g1_ref, sh2_ref,
                sc2_ref, n2g_ref, gng_ref, gnb_ref, convw_ref, wout_ref, rwh_ref, rwl_ref, avg_ref,
                xm_ref, hx_ref, aff_ref):
    i = pl.program_id(1)
    n = pl.num_programs(1)
    avg = avg_ref[...]
    y = yf_ref[0] + yb_ref[0]
    mu = _seg_dot(y, avg)
    dlt = y - mu
    var = _seg_dot(dlt * dlt, avg)
    yn = dlt * lax.rsqrt(var + GN_EPS) * gng_ref[...] + gnb_ref[...]
    ax = ((yn + bonus_ref[0]) * g_ref[0]).astype(BF16)

    cv = cv_ref[0]
    b_gate = cv[:, 0:D_CONV]
    cu = cv[:, D_CONV:2 * D_CONV] * cv[:, 2 * D_CONV:]
    cvp = cvp_ref[0]
    cvn = cvn_ref[0]
    cu_prev = jnp.where(i > 0, cvp[7:8, D_CONV:2 * D_CONV] * cvp[7:8, 2 * D_CONV:], 0.0)
    cu_next = jnp.where(i < n - 1, cvn[0:1, D_CONV:2 * D_CONV] * cvn[0:1, 2 * D_CONV:], 0.0)
    ridx = lax.broadcasted_iota(jnp.int32, cu.shape, 0)
    cu_m1 = jnp.where(ridx == 0, cu_prev, pltpu.roll(cu, 1, 0))
    cu_p1 = jnp.where(ridx == tt - 1, cu_next, pltpu.roll(cu, tt - 1, 0))
    conv = convw_ref[0:1, :] * cu_m1 + convw_ref[1:2, :] * cu + convw_ref[2:3, :] * cu_p1
    bx = (b_gate * conv).astype(BF16)

    mix = _dot(ax, wout_ref[0:D_RWKV, :]) + _dot(bx, wout_ref[D_RWKV:, :])
    xm = x_ref[0] + g1_ref[0] * mix
    xm_ref[0] = xm
    ms = jnp.mean(xm * xm, axis=-1, keepdims=True)
    hx = xm * lax.rsqrt(ms + NORM_EPS) * n2g_ref[...]
    hx = hx * (1.0 + sc2_ref[0]) + sh2_ref[0]
    hi, lo = _split2(hx)
    hx_ref[0] = hi
    rwh = rwh_ref[...]
    logits = _dot_nt(rwh, hi) + _dot_nt(rwh, lo) + _dot_nt(rwl_ref[...], hi)
    m = jnp.max(logits, axis=0, keepdims=True)
    ex = jnp.exp(logits - m)
    aff_ref[0] = ex / jnp.sum(ex, axis=0, keepdims=True)


def _out(yf, yb, bonus, g, p_cv, x, g1, sh2, sc2, n2g, gng, gnb, convw, wout, rwh, rwl, avg, tt):
    bsz, t, d = x.shape
    ne = rwh.shape[0]
    nb8 = t // 8
    row = lambda b, i: (b, i, 0)
    per_b = lambda b, i: (b, 0, 0)
    const2 = lambda b, i: (0, 0)
    return pl.pallas_call(
        functools.partial(_out_kernel, tt),
        grid=(bsz, t // tt),
        in_specs=[pl.BlockSpec((1, tt, D_RWKV), row), pl.BlockSpec((1, tt, D_RWKV), row),
                  pl.BlockSpec((1, tt, D_RWKV), row), pl.BlockSpec((1, tt, D_RWKV), row),
                  pl.BlockSpec((1, tt, CONV_COLS), row),
                  pl.BlockSpec((1, 8, CONV_COLS), lambda b, i: (b, jnp.maximum(i * (tt // 8) - 1, 0), 0)),
                  pl.BlockSpec((1, 8, CONV_COLS), lambda b, i: (b, jnp.minimum((i + 1) * (tt // 8), nb8 - 1), 0)),
                  pl.BlockSpec((1, tt, d), row),
                  pl.BlockSpec((1, 1, d), per_b), pl.BlockSpec((1, 1, d), per_b), pl.BlockSpec((1, 1, d), per_b),
                  pl.BlockSpec((1, d), const2), pl.BlockSpec((1, D_RWKV), const2), pl.BlockSpec((1, D_RWKV), const2),
                  pl.BlockSpec((3, D_CONV), const2), pl.BlockSpec((D_RWKV + D_CONV, d), const2),
                  pl.BlockSpec((ne, d), const2), pl.BlockSpec((ne, d), const2),
                  pl.BlockSpec((D_RWKV, D_RWKV), const2)],
        out_specs=[pl.BlockSpec((1, tt, d), row), pl.BlockSpec((1, tt, d), row),
                   pl.BlockSpec((1, ne, tt), lambda b, i: (b, 0, i))],
        out_shape=[jax.ShapeDtypeStruct((bsz, t, d), F32), jax.ShapeDtypeStruct((bsz, t, d), BF16),
                   jax.ShapeDtypeStruct((bsz, ne, t), F32)],
        compiler_params=_cparams(("parallel", "parallel")),
        name="out",
    )(yf, yb, bonus, g, p_cv, p_cv, p_cv, x, g1, sh2, sc2, n2g, gng, gnb, convw, wout, rwh, rwl, avg)


def _prefix_blocks(mask_fn, t, tri, emit):
    carry = None
    for j in range(t // 128):
        m = mask_fn(j)
        inc = _dot(m.astype(BF16), tri)
        carry = jnp.zeros_like(inc[:, 0:1]) if carry is None else carry
        emit(j, m, inc - m + carry)
        carry = carry + inc[:, 127:128]


def _topk_kernel(cap, aff_ref, tri_ref, sel_ref, pos_ref):
    t = aff_ref.shape[2]
    aff = aff_ref[0]

    def body(k, bits):
        cand = bits | jnp.left_shift(jnp.int32(1), 30 - k)
        cnt = jnp.sum(jnp.where(aff >= pltpu.bitcast(cand, F32), 1, 0), axis=-1, keepdims=True)
        return jnp.where(cnt >= cap, cand, bits)

    bits = lax.fori_loop(0, 31, body, jnp.zeros((aff.shape[0], 1), jnp.int32))
    thr = pltpu.bitcast(bits, F32)
    above = pltpu.bitcast(bits + 1, F32)
    n_gt = jnp.sum(jnp.where(aff >= above, 1, 0), axis=-1, keepdims=True)
    need = (cap - n_gt).astype(F32)
    tri = tri_ref[...]

    def blk(j):
        return aff[:, j * 128:(j + 1) * 128]

    def emit_sel(j, eq, before):
        take = (blk(j) >= above) | ((eq > 0.5) & (before < need))
        sel_ref[0, :, j * 128:(j + 1) * 128] = jnp.where(take, 1, 0)

    _prefix_blocks(lambda j: jnp.where((blk(j) >= thr) & (blk(j) < above), 1.0, 0.0), t, tri, emit_sel)

    def emit_pos(j, m, before):
        pos_ref[0, :, j * 128:(j + 1) * 128] = before.astype(jnp.int32)

    _prefix_blocks(lambda j: sel_ref[0, :, j * 128:(j + 1) * 128].astype(F32), t, tri, emit_pos)


def _topk(aff_t, tri, cap):
    bsz, ne, t = aff_t.shape
    spec = pl.BlockSpec((1, ne, t), lambda b: (b, 0, 0))
    return pl.pallas_call(
        functools.partial(_topk_kernel, cap),
        grid=(bsz,),
        in_specs=[spec, pl.BlockSpec((128, 128), lambda b: (0, 0))],
        out_specs=[spec, spec],
        out_shape=[jax.ShapeDtypeStruct((bsz, ne, t), jnp.int32)] * 2,
        compiler_params=_cparams(("parallel",)),
        name="topk",
    )(aff_t, tri)


def _slot_block_range(lo, hi, sb):
    shift = sb.bit_length() - 1
    assert sb == 1 << shift
    first = lo >> shift
    return first, jnp.where(hi > lo, ((hi - 1) >> shift) + 1, first)


def _moe_kernel(tk, sb, tsp_ref, hx_ref, sel_ref, pos_ref, wg_ref, wu_ref, wd_ref, ye_ref, xs_ref):
    b = pl.program_id(0)
    e = pl.program_id(1)
    ne = pl.num_programs(1)
    nt = hx_ref.shape[1] // tk
    base = (b * ne + e) * (nt + 1)
    xs_ref[...] = jnp.zeros_like(xs_ref)

    def tile_body(j, carry):
        t0 = pl.multiple_of(j * tk, tk)
        hxt = hx_ref[0, pl.ds(t0, tk), :]
        posr = pos_ref[0, 0, pl.ds(j, 1), :]
        selr = sel_ref[0, 0, pl.ds(j, 1), :]

        def sb_body(s, c2):
            s0 = pl.multiple_of(s * sb, sb)
            slot = lax.broadcasted_iota(jnp.int32, (sb, tk), 0) + s0
            onehot = jnp.where((posr == slot) & (selr > 0), 1.0, 0.0).astype(BF16)
            xs_ref[pl.ds(s0, sb), :] += _dot(onehot, hxt)
            return c2

        first, last = _slot_block_range(tsp_ref[base + j], tsp_ref[base + j + 1], sb)
        lax.fori_loop(first, last, sb_body, 0)
        return carry

    lax.fori_loop(0, nt, tile_body, 0)
    xs = xs_ref[...].astype(BF16)
    h1 = _dot(xs, wg_ref[0])
    h2 = _dot(xs, wu_ref[0])
    hid = (h1 * jax.nn.sigmoid(h1) * h2).astype(BF16)
    ye_ref[0, 0] = _dot(hid, wd_ref[0]).astype(BF16)


def _moe(tsp, hx, sel4, pos4, wg, wu, wd, cap, tk, sb):
    bsz, t, d = hx.shape
    ne, _, f = wg.shape
    nt = t // tk
    grid_spec = pltpu.PrefetchScalarGridSpec(
        num_scalar_prefetch=1,
        grid=(bsz, ne),
        in_specs=[pl.BlockSpec((1, t, d), lambda b, e, s: (b, 0, 0)),
                  pl.BlockSpec((1, 1, nt, tk), lambda b, e, s: (b, e, 0, 0)),
                  pl.BlockSpec((1, 1, nt, tk), lambda b, e, s: (b, e, 0, 0)),
                  pl.BlockSpec((1, d, f), lambda b, e, s: (e, 0, 0)),
                  pl.BlockSpec((1, d, f), lambda b, e, s: (e, 0, 0)),
                  pl.BlockSpec((1, f, d), lambda b, e, s: (e, 0, 0))],
        out_specs=pl.BlockSpec((1, 1, cap, d), lambda b, e, s: (b, e, 0, 0)),
        scratch_shapes=[pltpu.VMEM((cap, d), F32)],
    )
    return pl.pallas_call(
        functools.partial(_moe_kernel, tk, sb),
        grid_spec=grid_spec,
        out_shape=jax.ShapeDtypeStruct((bsz, ne, cap, d), BF16),
        compiler_params=_cparams(("parallel", "arbitrary")),
        name="moe",
    )(tsp, hx, sel4, pos4, wg, wu, wd)


def _comb_kernel(tk, sb, tsp_ref, xm_ref, ye_ref, selc_ref, posc_ref, affc_ref, g2_ref, fg_ref, o_ref, acc_ref):
    b = pl.program_id(0)
    j = pl.program_id(1)
    nt = pl.num_programs(1)
    ne = ye_ref.shape[1]
    acc_ref[...] = jnp.zeros_like(acc_ref)
    selc = selc_ref[0]
    posc = posc_ref[0]
    affc = affc_ref[0]
    for e in range(ne):
        base = (b * ne + e) * (nt + 1)
        pos_e = posc[:, e:e + 1]
        sel_e = selc[:, e:e + 1]
        val_e = affc[:, e:e + 1]

        def sb_body(s, c2, e=e, pos_e=pos_e, sel_e=sel_e, val_e=val_e):
            s0 = pl.multiple_of(s * sb, sb)
            slot = lax.broadcasted_iota(jnp.int32, (tk, sb), 1) + s0
            onehot = jnp.where((pos_e == slot) & (sel_e > 0), 1.0, 0.0).astype(BF16)
            rows = ye_ref[0, e, pl.ds(s0, sb), :]
            acc_ref[...] += val_e * _dot(onehot, rows)
            return c2

        first, last = _slot_block_range(tsp_ref[base + j], tsp_ref[base + j + 1], sb)
        lax.fori_loop(first, last, sb_body, 0)

    xo = xm_ref[0] + g2_ref[0] * acc_ref[...]
    ms = jnp.mean(xo * xo, axis=-1, keepdims=True)
    o_ref[0] = xo * lax.rsqrt(ms + NORM_EPS) * fg_ref[...]


def _comb(tsp, xm, ye, selc, posc, affc, g2, fg, tk, sb):
    bsz, t, d = xm.shape
    ne, cap = ye.shape[1], ye.shape[2]
    grid_spec = pltpu.PrefetchScalarGridSpec(
        num_scalar_prefetch=1,
        grid=(bsz, t // tk),
        in_specs=[pl.BlockSpec((1, tk, d), lambda b, j, s: (b, j, 0)),
                  pl.BlockSpec((1, ne, cap, d), lambda b, j, s: (b, 0, 0, 0)),
                  pl.BlockSpec((1, tk, ne), lambda b, j, s: (b, j, 0)),
                  pl.BlockSpec((1, tk, ne), lambda b, j, s: (b, j, 0)),
                  pl.BlockSpec((1, tk, ne), lambda b, j, s: (b, j, 0)),
                  pl.BlockSpec((1, 1, d), lambda b, j, s: (b, 0, 0)),
                  pl.BlockSpec((1, d), lambda b, j, s: (0, 0))],
        out_specs=pl.BlockSpec((1, tk, d), lambda b, j, s: (b, j, 0)),
        scratch_shapes=[pltpu.VMEM((tk, d), F32)],
    )
    return pl.pallas_call(
        functools.partial(_comb_kernel, tk, sb),
        grid_spec=grid_spec,
        out_shape=jax.ShapeDtypeStruct((bsz, t, d), F32),
        compiler_params=_cparams(("parallel", "arbitrary")),
        name="comb",
    )(tsp, xm, ye, selc, posc, affc, g2, fg)


def _block_diag_ones(n, blk, value=1.0):
    r = jnp.arange(n)
    return jnp.where((r[:, None] // blk) == (r[None, :] // blk), value, 0.0)


def kernel(x, c, ctx, c_ctx, ada_w, ada_b, norm1_g, norm2_g, w_in, shift_mu, w0, w_lora_up, a0, a_lora_up, k_k, k_a,
           r_k, g_lora_up, gn_g, gn_b, conv_w, w_out, router_w, exp_w_gate, exp_w_up, exp_w_down, final_g):
    bsz, t, d = x.shape
    lc = ctx.shape[1]
    ne = router_w.shape[-1]
    cap = EC_CAPACITY * t // ne
    tt = 256
    tk = 256
    sb = min(128, cap)
    l = 0

    rows = ((bsz + 1 + 7) // 8) * 8
    cc = jnp.zeros((rows, d), F32).at[:bsz].set(c).at[bsz].set(c_ctx)
    mod = _mod(cc, ada_w[l], ada_b[l][None, :])
    sh1, sc1, g1, sh2, sc2, g2 = (m[:, None, :] for m in jnp.split(mod[:bsz], 6, axis=-1))
    csh1, csc1 = (jnp.broadcast_to(m[None, None, :], (bsz, 1, d)) for m in jnp.split(mod[bsz], 6)[:2])

    w_rw = w_in[l][:, :RWKV_COLS].astype(BF16)
    w_cv = w_in[l][:, RWKV_COLS:].astype(BF16)
    n1g = norm1_g[l][None, :]
    px_rw, px_cv = _in_proj(x, sh1, sc1, n1g, w_rw, w_cv, 256)
    pc_rw, _ = _in_proj(ctx, csh1, csc1, n1g, w_rw, w_cv, 256)

    zw = jnp.zeros((LORA_W, 2 * D_RWKV), F32)
    lora = jnp.concatenate([
        jnp.concatenate([w_lora_up[l, 0], w_lora_up[l, 1], zw], axis=1),
        jnp.concatenate([zw, a_lora_up[l, 0], a_lora_up[l, 1]], axis=1)], axis=0).astype(BF16)
    ridx = jnp.arange(tt)
    same_chunk = (ridx[:, None] // CHUNK) == (ridx[None, :] // CHUNK)
    tril = jnp.where(same_chunk & (ridx[None, :] <= ridx[:, None]), 1.0, 0.0).astype(BF16)
    triu = jnp.where(same_chunk & (ridx[None, :] >= ridx[:, None]), 1.0, 0.0).astype(BF16)
    ones_bd = _block_diag_ones(D_RWKV, HEAD).astype(BF16)
    consts = (shift_mu[l][None, :], lora, w0[l], a0[l], k_k[l][None, :], k_a[l][None, :],
              r_k[l].reshape(1, D_RWKV), g_lora_up[l].astype(BF16), ones_bd, tril, triu)

    ops_c, wcf_c, wcb_c, _, _ = _prep(pc_rw, consts, False, lc)
    ops_x, wcf_x, wcb_x, bonus, gate = _prep(px_rw, consts, True, tt)

    s_zero = jnp.zeros((bsz, 2, N_PAIR, PAIR, PAIR), F32)
    _, _, s_ctx = _scan(ops_c, wcf_c, wcb_c, s_zero)
    yf, yb, _ = _scan(ops_x, wcf_x, wcb_x, s_ctx)

    rw_t = router_w[l].T
    rwh = rw_t.astype(BF16)
    rwl = (rw_t - rwh.astype(F32)).astype(BF16)
    avg = _block_diag_ones(D_RWKV, HEAD, 1.0 / HEAD).astype(BF16)
    xm, hx, aff_t = _out(yf, yb, bonus, gate, px_cv, x, g1, sh2, sc2, norm2_g[l][None, :], gn_g[l][None, :],
                         gn_b[l][None, :], conv_w[l], w_out[l].astype(BF16), rwh, rwl, avg, tt)

    r128 = jnp.arange(128)
    tri128 = jnp.where(r128[:, None] <= r128[None, :], 1.0, 0.0).astype(BF16)
    sel, pos = _topk(aff_t, tri128, cap)

    nt = t // tk
    tsp = jnp.concatenate([pos[:, :, ::tk], jnp.full((bsz, ne, 1), cap, jnp.int32)], axis=-1).reshape(-1)
    ye = _moe(tsp, hx, sel.reshape(bsz, ne, nt, tk), pos.reshape(bsz, ne, nt, tk),
              exp_w_gate[l].astype(BF16), exp_w_up[l].astype(BF16), exp_w_down[l].astype(BF16), cap, tk, sb)
    tr = lambda a: jnp.transpose(a, (0, 2, 1))
    return _comb(tsp, xm, ye, tr(sel), tr(pos), tr(aff_t), g2, final_g[None, :], tk, sb)
```

```python
import functools
import math

import jax
import jax.numpy as jnp
from jax import lax
from jax.experimental import pallas as pl
from jax.experimental.pallas import tpu as pltpu

F32 = jnp.float32
BF16 = jnp.bfloat16
HIGHEST = lax.Precision.HIGHEST

GRID_W = 64
D_RWKV = 512
D_CONV = 512
HEAD = 64
LORA_W = 64
LORA_A = 64
LORA_G = 128
N_EXPERTS = 16
EC_CAPACITY = 2
NORM_EPS = 1e-6
GN_EPS = 64e-5
RWKV_COLS = 3 * D_RWKV + LORA_W + LORA_A + LORA_G
CONV_COLS = 3 * D_CONV

CHUNK = 64
PAIR = 2 * HEAD
N_PAIR = D_RWKV // PAIR
VMEM_LIMIT = 48 * 1024 * 1024


def _cparams(sem):
    return pltpu.CompilerParams(dimension_semantics=sem, vmem_limit_bytes=VMEM_LIMIT)


def _dot(a, b):
    return jnp.dot(a, b, preferred_element_type=F32)


def _dot_nt(a, b):
    return lax.dot_general(a, b, (((1,), (1,)), ((), ())), preferred_element_type=F32)


def _split2(x):
    hi = x.astype(BF16)
    lo = (x - hi.astype(F32)).astype(BF16)
    return hi, lo


def _split3(x):
    hi = x.astype(BF16)
    r = x - hi.astype(F32)
    mid = r.astype(BF16)
    lo = (r - mid.astype(F32)).astype(BF16)
    return hi, mid, lo


def _seg_dot(x, m):
    hi, lo = _split2(x)
    return _dot(hi, m) + _dot(lo, m)


def _mod_kernel(c_ref, w_ref, b_ref, o_ref):
    c = c_ref[...]
    s = c * jax.nn.sigmoid(c)
    o_ref[...] = jnp.dot(s, w_ref[...], precision=HIGHEST, preferred_element_type=F32) + b_ref[...]


def _mod(cc, w, b):
    rows, d = cc.shape
    n = w.shape[1]
    tn = 1024
    return pl.pallas_call(
        _mod_kernel,
        grid=(n // tn,),
        in_specs=[pl.BlockSpec((rows, d), lambda j: (0, 0)),
                  pl.BlockSpec((d, tn), lambda j: (0, j)),
                  pl.BlockSpec((1, tn), lambda j: (0, j))],
        out_specs=pl.BlockSpec((rows, tn), lambda j: (0, j)),
        out_shape=jax.ShapeDtypeStruct((rows, n), F32),
        compiler_params=_cparams(("parallel",)),
        name="mod",
    )(cc, w, b)


def _in_proj_kernel(x_ref, sh_ref, sc_ref, g_ref, wrw_ref, wcv_ref, orw_ref, ocv_ref):
    x = x_ref[0]
    ms = jnp.mean(x * x, axis=-1, keepdims=True)
    h = x * lax.rsqrt(ms + NORM_EPS) * g_ref[...]
    h = (h * (1.0 + sc_ref[0]) + sh_ref[0]).astype(BF16)
    orw_ref[0] = _dot(h, wrw_ref[...])
    ocv_ref[0] = _dot(h, wcv_ref[...])


def _in_proj(x, sh, sc, g, w_rw, w_cv, tm):
    bsz, l, d = x.shape
    return pl.pallas_call(
        _in_proj_kernel,
        grid=(bsz, l // tm),
        in_specs=[pl.BlockSpec((1, tm, d), lambda b, i: (b, i, 0)),
                  pl.BlockSpec((1, 1, d), lambda b, i: (b, 0, 0)),
                  pl.BlockSpec((1, 1, d), lambda b, i: (b, 0, 0)),
                  pl.BlockSpec((1, d), lambda b, i: (0, 0)),
                  pl.BlockSpec((d, RWKV_COLS), lambda b, i: (0, 0)),
                  pl.BlockSpec((d, CONV_COLS), lambda b, i: (0, 0))],
        out_specs=[pl.BlockSpec((1, tm, RWKV_COLS), lambda b, i: (b, i, 0)),
                   pl.BlockSpec((1, tm, CONV_COLS), lambda b, i: (b, i, 0))],
        out_shape=[jax.ShapeDtypeStruct((bsz, l, RWKV_COLS), F32),
                   jax.ShapeDtypeStruct((bsz, l, CONV_COLS), F32)],
        compiler_params=_cparams(("parallel", "parallel")),
        name="in_proj",
    )(x, sh, sc, g, w_rw, w_cv)


def _bf(x):
    return x.astype(BF16)


def _chunk_local(chains):
    c = CHUNK
    n = range(len(chains))
    lane = lax.broadcasted_iota(jnp.int32, (c, PAIR), 1)
    h0 = lane < HEAD
    row = lax.broadcasted_iota(jnp.int32, (PAIR, PAIR), 0)
    col = lax.broadcasted_iota(jnp.int32, (PAIR, PAIR), 1)
    same = (row >= c) == (col >= c)
    tr, tc = row & (c - 1), col & (c - 1)
    eye = jnp.where(row == col, 1.0, 0.0)
    masks = {rev: (same & ((tc > tr) if rev else (tc < tr)), same & ((tc >= tr) if rev else (tc <= tr)))
             for rev in (False, True)}
    zb = jnp.zeros((c, PAIR), BF16)
    zf = jnp.zeros((c, PAIR), F32)

    def stack(x):
        return jnp.concatenate([jnp.where(h0, x, zb), jnp.where(h0, zb, x)], axis=0)

    a_st = [stack(ch[1]) for ch in chains]
    r_st = [stack(ch[2]) for ch in chains]
    gram = [_dot_nt(jnp.concatenate([a_st[i], r_st[i]], axis=0),
                    jnp.concatenate([chains[i][3], chains[i][3], chains[i][4], chains[i][4]], axis=0)) for i in n]
    lab = [jnp.where(masks[chains[i][0]][0], gram[i][:PAIR, :PAIR], 0.0) for i in n]
    lak = [jnp.where(masks[chains[i][0]][0], gram[i][:PAIR, PAIR:], 0.0) for i in n]
    mrbk = [jnp.concatenate([jnp.where(masks[chains[i][0]][1], gram[i][PAIR:, :PAIR], 0.0),
                             jnp.where(masks[chains[i][0]][1], gram[i][PAIR:, PAIR:], 0.0)], axis=1) for i in n]

    labb = [_bf(x) for x in lab]
    pw = [_dot(labb[i], labb[i]) for i in n]
    tp = [eye + lab[i] for i in n]
    for _ in range(4):
        pwb = [_bf(x) for x in pw]
        both = [_dot(pwb[i], jnp.concatenate([pwb[i], _bf(tp[i])], axis=1)) for i in n]
        pw = [x[:, :PAIR] for x in both]
        tp = [tp[i] + both[i][:, PAIR:] for i in n]
    tinv = [tp[i] + _dot(_bf(pw[i]), _bf(tp[i])) for i in n]

    v_rep = [jnp.concatenate([ch[7], ch[7]], axis=0) for ch in chains]
    lakv = [_dot(_bf(lak[i]), v_rep[i]) for i in n]
    x = [_dot(_bf(tinv[i]), jnp.concatenate([a_st[i], _bf(lakv[i])], axis=1)) for i in n]
    abar = [xi[:, :PAIR] for xi in x]
    u0 = [jnp.where(same, xi[:, PAIR:], 0.0) for xi in x]
    zpad = jnp.zeros((PAIR, PAIR), BF16)
    z = [_dot(_bf(mrbk[i]),
              jnp.concatenate([jnp.concatenate([_bf(abar[i]), _bf(u0[i])], axis=1),
                               jnp.concatenate([zpad, v_rep[i]], axis=1)], axis=0)) for i in n]
    rbar = [r_st[i].astype(F32) + z[i][:, :PAIR] for i in n]
    rbar = [_bf(x[:c] + x[c:]) for x in rbar]
    y0 = [jnp.where(h0, zi[:c, PAIR:], zi[c:, PAIR:]) for zi in z]

    au = [jnp.concatenate([jnp.concatenate([abar[i], u0[i]], axis=1),
                           jnp.concatenate([zf, chains[i][7].astype(F32)], axis=1),
                           jnp.zeros((c, 2 * PAIR), F32)], axis=0) for i in n]
    mn = [_dot(_bf(au[i].T), jnp.concatenate([chains[i][5], chains[i][5], chains[i][6], zb], axis=0)) for i in n]
    mx = [_bf(jnp.where(h0, m[:c], m[c:PAIR])) for m in mn]
    nn = [jnp.where(h0, m[PAIR:PAIR + c], m[PAIR + c:]) for m in mn]
    return list(zip(rbar, y0, mx, nn))


DIR_COLS = 6 * D_RWKV
SCR_COLS = 2 * DIR_COLS + D_RWKV


def _prep_body(tt, p, shifted, mu_ref, lw_ref, w0_ref, a0_ref, kk_ref, ka_ref, rk_ref, gup_ref,
               ones_ref, tril_ref, triu_ref, rbar_ref, y0_ref, mx_ref, nn_ref, wc_ref, bonus_ref, g_ref, scr_ref):
    pm = p + mu_ref[...] * (shifted - p)
    xr = pm[:, 0:D_RWKV]
    xk = pm[:, D_RWKV:2 * D_RWKV]
    xv = pm[:, 2 * D_RWKV:3 * D_RWKV]
    xwa = pm[:, 3 * D_RWKV:3 * D_RWKV + LORA_W + LORA_A]
    xg = pm[:, 3 * D_RWKV + LORA_W + LORA_A:]

    ones_bd = ones_ref[...]
    kraw = xk * kk_ref[...]
    ss = _seg_dot(kraw * kraw, ones_bd)
    kk = kraw / jnp.maximum(jnp.sqrt(ss), 1e-12)

    lane = lax.broadcasted_iota(jnp.int32, xwa.shape, 1)
    lin = jnp.where(lane < LORA_W, jnp.tanh(xwa), xwa).astype(BF16)
    lo = _dot(lin, lw_ref[...])

    scr_ref[:, 2 * DIR_COLS:] = xv.astype(BF16)
    ksum = None
    for d in range(2):
        z = w0_ref[d:d + 1, :] + lo[:, d * D_RWKV:(d + 1) * D_RWKV]
        ld = -math.exp(-0.5) * jax.nn.sigmoid(z)
        ag = jax.nn.sigmoid(a0_ref[d:d + 1, :] + lo[:, (2 + d) * D_RWKV:(3 + d) * D_RWKV])
        kd = xk * (1.0 + (ag - 1.0) * ka_ref[...])
        bb = kk * ag
        ksum = kd if ksum is None else ksum + kd
        tri = tril_ref[...] if d == 0 else triu_ref[...]
        h3 = _split3(ld)
        cs = _dot(tri, h3[0]) + _dot(tri, h3[1]) + _dot(tri, h3[2])
        e_in = jnp.exp(cs)
        e_ex = jnp.exp(cs - ld)
        e_neg = jnp.exp(-cs)
        bt = bb * e_neg
        kt = kd * e_neg
        base = d * DIR_COLS
        scr_ref[:, base:base + D_RWKV] = (-kk * e_ex).astype(BF16)
        scr_ref[:, base + D_RWKV:base + 2 * D_RWKV] = (xr * e_in).astype(BF16)
        scr_ref[:, base + 2 * D_RWKV:base + 3 * D_RWKV] = bt.astype(BF16)
        scr_ref[:, base + 3 * D_RWKV:base + 4 * D_RWKV] = kt.astype(BF16)
        for c in range(tt // CHUNK):
            rows = slice(c * CHUNK, (c + 1) * CHUNK)
            edge = c * CHUNK + (CHUNK - 1 if d == 0 else 0)
            wrow = e_in[edge:edge + 1, :]
            wc_ref[0, c, :, d * D_RWKV:(d + 1) * D_RWKV] = wrow
            scr_ref[rows, base + 4 * D_RWKV:base + 5 * D_RWKV] = (bt[rows] * wrow).astype(BF16)
            scr_ref[rows, base + 5 * D_RWKV:base + 6 * D_RWKV] = (kt[rows] * wrow).astype(BF16)

    bonus_ref[0] = _seg_dot(xr * ksum * rk_ref[...], ones_bd) * xv
    g_ref[0] = _dot(jax.nn.sigmoid(xg).astype(BF16), gup_ref[...])

    def chunk_body(c, carry):
        r0 = pl.multiple_of(c * CHUNK, CHUNK)
        chains = []
        for d in range(2):
            for pp in range(N_PAIR):
                cols = [d * DIR_COLS + j * D_RWKV + pp * PAIR for j in range(6)] + [2 * DIR_COLS + pp * PAIR]
                chains.append((d == 1,) + tuple(scr_ref[pl.ds(r0, CHUNK), k:k + PAIR] for k in cols))
        for idx, (rbar, y0, mx, nn) in enumerate(_chunk_local(chains)):
            col = (idx // N_PAIR) * D_RWKV + (idx % N_PAIR) * PAIR
            rbar_ref[0, pl.ds(r0, CHUNK), col:col + PAIR] = rbar
            y0_ref[0, pl.ds(r0, CHUNK), col:col + PAIR] = y0
            mx_ref[0, c, :, col:col + PAIR] = mx
            nn_ref[0, c, :, col:col + PAIR] = nn
        return carry

    lax.fori_loop(0, tt // CHUNK, chunk_body, 0)


def _prep_latent_kernel(tt, p_ref, prev_ref, next_ref, *rest):
    ext_ref = rest[-1]
    i = pl.program_id(1)
    n = pl.num_programs(1)
    ext_ref[0:GRID_W] = jnp.where(i > 0, prev_ref[0], 0.0)
    ext_ref[GRID_W:GRID_W + tt] = p_ref[0]
    ext_ref[GRID_W + tt:] = jnp.where(i < n - 1, next_ref[0], 0.0)
    shape = (tt, RWKV_COLS)
    col = lax.broadcasted_iota(jnp.int32, shape, 0) & (GRID_W - 1)
    q = lax.broadcasted_iota(jnp.int32, shape, 1) & 3
    left = jnp.where(col == 0, 0.0, ext_ref[GRID_W - 1:GRID_W - 1 + tt])
    right = jnp.where(col == GRID_W - 1, 0.0, ext_ref[GRID_W + 1:GRID_W + 1 + tt])
    up = ext_ref[0:tt]
    down = ext_ref[2 * GRID_W:2 * GRID_W + tt]
    shifted = jnp.where(q == 0, left, jnp.where(q == 1, right, jnp.where(q == 2, up, down)))
    _prep_body(tt, p_ref[0], shifted, *rest[:-1])


def _prep_ctx_kernel(tt, p_ref, *rest):
    ext_ref = rest[-1]
    ext_ref[0:8] = jnp.zeros((8, RWKV_COLS), F32)
    ext_ref[8:8 + tt] = p_ref[0]
    ext_ref[8 + tt:] = jnp.zeros((8, RWKV_COLS), F32)
    q = lax.broadcasted_iota(jnp.int32, (tt, RWKV_COLS), 1) & 1
    shifted = jnp.where(q == 0, ext_ref[7:7 + tt], ext_ref[9:9 + tt])
    _prep_body(tt, p_ref[0], shifted, *rest[:-1])


def _prep(p_rw, consts, latent, tt):
    bsz, l, _ = p_rw.shape
    nc = l // CHUNK
    cpt = tt // CHUNK
    const_specs = [pl.BlockSpec(c.shape, lambda b, i, nd=c.ndim: (0,) * nd) for c in consts]
    if latent:
        nblk = l // GRID_W
        in_specs = [pl.BlockSpec((1, tt, RWKV_COLS), lambda b, i: (b, i, 0)),
                    pl.BlockSpec((1, GRID_W, RWKV_COLS),
                                 lambda b, i: (b, jnp.maximum(i * (tt // GRID_W) - 1, 0), 0)),
                    pl.BlockSpec((1, GRID_W, RWKV_COLS),
                                 lambda b, i: (b, jnp.minimum((i + 1) * (tt // GRID_W), nblk - 1), 0))]
        args = (p_rw, p_rw, p_rw)
        kern = functools.partial(_prep_latent_kernel, tt)
        ext_rows = tt + 2 * GRID_W
    else:
        assert tt == l
        in_specs = [pl.BlockSpec((1, tt, RWKV_COLS), lambda b, i: (b, i, 0))]
        args = (p_rw,)
        kern = functools.partial(_prep_ctx_kernel, tt)
        ext_rows = tt + 16
    row = lambda b, i: (b, i, 0)
    chunk = lambda b, i: (b, i, 0, 0)
    w2 = 2 * D_RWKV
    return pl.pallas_call(
        kern,
        grid=(bsz, l // tt),
        in_specs=in_specs + const_specs,
        out_specs=[pl.BlockSpec((1, tt, w2), row), pl.BlockSpec((1, tt, w2), row),
                   pl.BlockSpec((1, cpt, CHUNK, w2), chunk), pl.BlockSpec((1, cpt, CHUNK, w2), chunk),
                   pl.BlockSpec((1, cpt, 1, w2), chunk),
                   pl.BlockSpec((1, tt, D_RWKV), row), pl.BlockSpec((1, tt, D_RWKV), row)],
        out_shape=[jax.ShapeDtypeStruct((bsz, l, w2), BF16), jax.ShapeDtypeStruct((bsz, l, w2), F32),
                   jax.ShapeDtypeStruct((bsz, nc, CHUNK, w2), BF16), jax.ShapeDtypeStruct((bsz, nc, CHUNK, w2), F32),
                   jax.ShapeDtypeStruct((bsz, nc, 1, w2), F32),
                   jax.ShapeDtypeStruct((bsz, l, D_RWKV), F32), jax.ShapeDtypeStruct((bsz, l, D_RWKV), F32)],
        scratch_shapes=[pltpu.VMEM((tt, SCR_COLS), BF16), pltpu.VMEM((ext_rows, RWKV_COLS), F32)],
        compiler_params=_cparams(("parallel", "parallel")),
        name="prep_latent" if latent else "prep_ctx",
    )(*args, *consts)


def _scan_kernel(rbf_ref, rbb_ref, y0f_ref, y0b_ref, mxf_ref, mxb_ref, nnf_ref, nnb_ref, wcf_ref, wcb_ref,
                 s0_ref, yf_ref, yb_ref, s_ref):
    i = pl.program_id(1)

    @pl.when(i == 0)
    def _():
        s_ref[...] = s0_ref[...]

    c = CHUNK
    h0 = lax.broadcasted_iota(jnp.int32, (c, PAIR), 1) < HEAD
    row = lax.broadcasted_iota(jnp.int32, (PAIR, PAIR), 0)
    col = lax.broadcasted_iota(jnp.int32, (PAIR, PAIR), 1)
    same = (row >= c) == (col >= c)
    zb = jnp.zeros((c, PAIR), BF16)
    dirs = ((rbf_ref, y0f_ref, mxf_ref, nnf_ref, wcf_ref, yf_ref), (rbb_ref, y0b_ref, mxb_ref, nnb_ref, wcb_ref, yb_ref))
    idx = [(d, p) for d in range(2) for p in range(N_PAIR)]
    lanes = [slice(p * PAIR, (p + 1) * PAIR) for _, p in idx]
    s = [s_ref[0, d, p] for d, p in idx]
    sb = [_bf(x) for x in s]
    rbar = [dirs[d][0][0, :, lanes[k]] for k, (d, _) in enumerate(idx)]
    r_st = [jnp.concatenate([jnp.where(h0, x, zb), jnp.where(h0, zb, x)], axis=0) for x in rbar]
    y_st = [_dot_nt(r_st[k], sb[k]) for k in range(len(idx))]
    m_bd = [jnp.where(same, jnp.concatenate([x, x], axis=0), jnp.zeros((PAIR, PAIR), BF16))
            for x in (dirs[d][2][0, 0, :, lanes[k]] for k, (d, _) in enumerate(idx))]
    sm = [_dot(sb[k], m_bd[k]) for k in range(len(idx))]
    for k, (d, p) in enumerate(idx):
        nn = dirs[d][3][0, 0, :, lanes[k]]
        n_bd = jnp.where(same, jnp.concatenate([nn, nn], axis=0), 0.0)
        dirs[d][5][0, :, lanes[k]] = y_st[k][:c] + y_st[k][c:] + dirs[d][1][0, :, lanes[k]]
        s_ref[0, d, p] = s[k] * dirs[d][4][0, 0, :, lanes[k]] + sm[k] + n_bd


def _scan(rbar, y0, mx, nn, wc, s0):
    bsz, l, _ = rbar.shape
    nc = l // CHUNK
    row_f = pl.BlockSpec((1, CHUNK, D_RWKV), lambda b, i: (b, i, 0))
    row_b = pl.BlockSpec((1, CHUNK, D_RWKV), lambda b, i: (b, nc - 1 - i, 1))
    chk_f = pl.BlockSpec((1, 1, CHUNK, D_RWKV), lambda b, i: (b, i, 0, 0))
    chk_b = pl.BlockSpec((1, 1, CHUNK, D_RWKV), lambda b, i: (b, nc - 1 - i, 0, 1))
    wc_f = pl.BlockSpec((1, 1, 1, D_RWKV), lambda b, i: (b, i, 0, 0))
    wc_b = pl.BlockSpec((1, 1, 1, D_RWKV), lambda b, i: (b, nc - 1 - i, 0, 1))
    st_spec = pl.BlockSpec((1, 2, N_PAIR, PAIR, PAIR), lambda b, i: (b, 0, 0, 0, 0))
    return pl.pallas_call(
        _scan_kernel,
        grid=(bsz, nc),
        in_specs=[row_f, row_b, row_f, row_b, chk_f, chk_b, chk_f, chk_b, wc_f, wc_b, st_spec],
        out_specs=[row_f, pl.BlockSpec((1, CHUNK, D_RWKV), lambda b, i: (b, nc - 1 - i, 0)), st_spec],
        out_shape=[jax.ShapeDtypeStruct((bsz, l, D_RWKV), F32),
                   jax.ShapeDtypeStruct((bsz, l, D_RWKV), F32),
                   jax.ShapeDtypeStruct((bsz, 2, N_PAIR, PAIR, PAIR), F32)],
        compiler_params=_cparams(("parallel", "arbitrary")),
        name="scan",
    )(rbar, rbar, y0, y0, mx, mx, nn, nn, wc, wc, s0)


def _out_kernel(tt, yf_ref, yb_ref, bonus_ref, g_ref, cv_ref, cvp_ref, cvn_ref, x_ref, g1_ref, sh2_ref,
                sc2_ref, n2g_ref, gng_ref, gnb_ref, convw_ref, wout_ref, rwh_ref, rwl_ref, avg_ref,
                xm_ref, hx_ref, aff_ref):
    i = pl.program_id(1)
    n = pl.num_programs(1)
    avg = avg_ref[...]
    y = yf_ref[0] + yb_ref[0]
    mu = _seg_dot(y, avg)
    dlt = y - mu
    var = _seg_dot(dlt * dlt, avg)
    yn = dlt * lax.rsqrt(var + GN_EPS) * gng_ref[...] + gnb_ref[...]
    ax = ((yn + bonus_ref[0]) * g_ref[0]).astype(BF16)

    cv = cv_ref[0]
    b_gate = cv[:, 0:D_CONV]
    cu = cv[:, D_CONV:2 * D_CONV] * cv[:, 2 * D_CONV:]
    cvp = cvp_ref[0]
    cvn = cvn_ref[0]
    cu_prev = jnp.where(i > 0, cvp[7:8, D_CONV:2 * D_CONV] * cvp[7:8, 2 * D_CONV:], 0.0)
    cu_next = jnp.where(i < n - 1, cvn[0:1, D_CONV:2 * D_CONV] * cvn[0:1, 2 * D_CONV:], 0.0)
    ridx = lax.broadcasted_iota(jnp.int32, cu.shape, 0)
    cu_m1 = jnp.where(ridx == 0, cu_prev, pltpu.roll(cu, 1, 0))
    cu_p1 = jnp.where(ridx == tt - 1, cu_next, pltpu.roll(cu, tt - 1, 0))
    conv = convw_ref[0:1, :] * cu_m1 + convw_ref[1:2, :] * cu + convw_ref[2:3, :] * cu_p1
    bx = (b_gate * conv).astype(BF16)

    mix = _dot(ax, wout_ref[0:D_RWKV, :]) + _dot(bx, wout_ref[D_RWKV:, :])
    xm = x_ref[0] + g1_ref[0] * mix
    xm_ref[0] = xm
    ms = jnp.mean(xm * xm, axis=-1, keepdims=True)
    hx = xm * lax.rsqrt(ms + NORM_EPS) * n2g_ref[...]
    hx = hx * (1.0 + sc2_ref[0]) + sh2_ref[0]
    hi, lo = _split2(hx)
    hx_ref[0] = hi
    rwh = rwh_ref[...]
    logits = _dot_nt(rwh, hi) + _dot_nt(rwh, lo) + _dot_nt(rwl_ref[...], hi)
    m = jnp.max(logits, axis=0, keepdims=True)
    ex = jnp.exp(logits - m)
    aff_ref[0] = ex / jnp.sum(ex, axis=0, keepdims=True)


def _out(yf, yb, bonus, g, p_cv, x, g1, sh2, sc2, n2g, gng, gnb, convw, wout, rwh, rwl, avg, tt):
    bsz, t, d = x.shape
    ne = rwh.shape[0]
    nb8 = t // 8
    row = lambda b, i: (b, i, 0)
    per_b = lambda b, i: (b, 0, 0)
    const2 = lambda b, i: (0, 0)
    return pl.pallas_call(
        functools.partial(_out_kernel, tt),
        grid=(bsz, t // tt),
        in_specs=[pl.BlockSpec((1, tt, D_RWKV), row), pl.BlockSpec((1, tt, D_RWKV), row),
                  pl.BlockSpec((1, tt, D_RWKV), row), pl.BlockSpec((1, tt, D_RWKV), row),
                  pl.BlockSpec((1, tt, CONV_COLS), row),
                  pl.BlockSpec((1, 8, CONV_COLS), lambda b, i: (b, jnp.maximum(i * (tt // 8) - 1, 0), 0)),
                  pl.BlockSpec((1, 8, CONV_COLS), lambda b, i: (b, jnp.minimum((i + 1) * (tt // 8), nb8 - 1), 0)),
                  pl.BlockSpec((1, tt, d), row),
                  pl.BlockSpec((1, 1, d), per_b), pl.BlockSpec((1, 1, d), per_b), pl.BlockSpec((1, 1, d), per_b),
                  pl.BlockSpec((1, d), const2), pl.BlockSpec((1, D_RWKV), const2), pl.BlockSpec((1, D_RWKV), const2),
                  pl.BlockSpec((3, D_CONV), const2), pl.BlockSpec((D_RWKV + D_CONV, d), const2),
                  pl.BlockSpec((ne, d), const2), pl.BlockSpec((ne, d), const2),
                  pl.BlockSpec((D_RWKV, D_RWKV), const2)],
        out_specs=[pl.BlockSpec((1, tt, d), row), pl.BlockSpec((1, tt, d), row),
                   pl.BlockSpec((1, ne, tt), lambda b, i: (b, 0, i))],
        out_shape=[jax.ShapeDtypeStruct((bsz, t, d), F32), jax.ShapeDtypeStruct((bsz, t, d), BF16),
                   jax.ShapeDtypeStruct((bsz, ne, t), F32)],
        compiler_params=_cparams(("parallel", "parallel")),
        name="out",
    )(yf, yb, bonus, g, p_cv, p_cv, p_cv, x, g1, sh2, sc2, n2g, gng, gnb, convw, wout, rwh, rwl, avg)


def _prefix_blocks(mask_fn, t, tri, emit):
    carry = None
    for j in range(t // 128):
        m = mask_fn(j)
        inc = _dot(m.astype(BF16), tri)
        carry = jnp.zeros_like(inc[:, 0:1]) if carry is None else carry
        emit(j, m, inc - m + carry)
        carry = carry + inc[:, 127:128]


def _topk_kernel(cap, aff_ref, tri_ref, sel_ref, pos_ref):
    t = aff_ref.shape[2]
    aff = aff_ref[0]

    def body(k, bits):
        cand = bits | jnp.left_shift(jnp.int32(1), 30 - k)
        cnt = jnp.sum(jnp.where(aff >= pltpu.bitcast(cand, F32), 1, 0), axis=-1, keepdims=True)
        return jnp.where(cnt >= cap, cand, bits)

    bits = lax.fori_loop(0, 31, body, jnp.zeros((aff.shape[0], 1), jnp.int32))
    thr = pltpu.bitcast(bits, F32)
    above = pltpu.bitcast(bits + 1, F32)
    n_gt = jnp.sum(jnp.where(aff >= above, 1, 0), axis=-1, keepdims=True)
    need = (cap - n_gt).astype(F32)
    tri = tri_ref[...]

    def blk(j):
        return aff[:, j * 128:(j + 1) * 128]

    def emit_sel(j, eq, before):
        take = (blk(j) >= above) | ((eq > 0.5) & (before < need))
        sel_ref[0, :, j * 128:(j + 1) * 128] = jnp.where(take, 1, 0)

    _prefix_blocks(lambda j: jnp.where((blk(j) >= thr) & (blk(j) < above), 1.0, 0.0), t, tri, emit_sel)

    def emit_pos(j, m, before):
        pos_ref[0, :, j * 128:(j + 1) * 128] = before.astype(jnp.int32)

    _prefix_blocks(lambda j: sel_ref[0, :, j * 128:(j + 1) * 128].astype(F32), t, tri, emit_pos)


def _topk(aff_t, tri, cap):
    bsz, ne, t = aff_t.shape
    spec = pl.BlockSpec((1, ne, t), lambda b: (b, 0, 0))
    return pl.pallas_call(
        functools.partial(_topk_kernel, cap),
        grid=(bsz,),
        in_specs=[spec, pl.BlockSpec((128, 128), lambda b: (0, 0))],
        out_specs=[spec, spec],
        out_shape=[jax.ShapeDtypeStruct((bsz, ne, t), jnp.int32)] * 2,
        compiler_params=_cparams(("parallel",)),
        name="topk",
    )(aff_t, tri)


def _slot_block_range(lo, hi, sb):
    shift = sb.bit_length() - 1
    assert sb == 1 << shift
    first = lo >> shift
    return first, jnp.where(hi > lo, ((hi - 1) >> shift) + 1, first)


def _moe_kernel(tk, sb, tsp_ref, hx_ref, sel_ref, pos_ref, wg_ref, wu_ref, wd_ref, ye_ref, xs_ref):
    b = pl.program_id(0)
    e = pl.program_id(1)
    ne = pl.num_programs(1)
    nt = hx_ref.shape[1] // tk
    base = (b * ne + e) * (nt + 1)
    xs_ref[...] = jnp.zeros_like(xs_ref)

    def tile_body(j, carry):
        t0 = pl.multiple_of(j * tk, tk)
        hxt = hx_ref[0, pl.ds(t0, tk), :]
        posr = pos_ref[0, 0, pl.ds(j, 1), :]
        selr = sel_ref[0, 0, pl.ds(j, 1), :]

        def sb_body(s, c2):
            s0 = pl.multiple_of(s * sb, sb)
            slot = lax.broadcasted_iota(jnp.int32, (sb, tk), 0) + s0
            onehot = jnp.where((posr == slot) & (selr > 0), 1.0, 0.0).astype(BF16)
            xs_ref[pl.ds(s0, sb), :] += _dot(onehot, hxt)
            return c2

        first, last = _slot_block_range(tsp_ref[base + j], tsp_ref[base + j + 1], sb)
        lax.fori_loop(first, last, sb_body, 0)
        return carry

    lax.fori_loop(0, nt, tile_body, 0)
    xs = xs_ref[...].astype(BF16)
    h1 = _dot(xs, wg_ref[0])
    h2 = _dot(xs, wu_ref[0])
    hid = (h1 * jax.nn.sigmoid(h1) * h2).astype(BF16)
    ye_ref[0, 0] = _dot(hid, wd_ref[0]).astype(BF16)


def _moe(tsp, hx, sel4, pos4, wg, wu, wd, cap, tk, sb):
    bsz, t, d = hx.shape
    ne, _, f = wg.shape
    nt = t // tk
    grid_spec = pltpu.PrefetchScalarGridSpec(
        num_scalar_prefetch=1,
        grid=(bsz, ne),
        in_specs=[pl.BlockSpec((1, t, d), lambda b, e, s: (b, 0, 0)),
                  pl.BlockSpec((1, 1, nt, tk), lambda b, e, s: (b, e, 0, 0)),
                  pl.BlockSpec((1, 1, nt, tk), lambda b, e, s: (b, e, 0, 0)),
                  pl.BlockSpec((1, d, f), lambda b, e, s: (e, 0, 0)),
                  pl.BlockSpec((1, d, f), lambda b, e, s: (e, 0, 0)),
                  pl.BlockSpec((1, f, d), lambda b, e, s: (e, 0, 0))],
        out_specs=pl.BlockSpec((1, 1, cap, d), lambda b, e, s: (b, e, 0, 0)),
        scratch_shapes=[pltpu.VMEM((cap, d), F32)],
    )
    return pl.pallas_call(
        functools.partial(_moe_kernel, tk, sb),
        grid_spec=grid_spec,
        out_shape=jax.ShapeDtypeStruct((bsz, ne, cap, d), BF16),
        compiler_params=_cparams(("parallel", "arbitrary")),
        name="moe",
    )(tsp, hx, sel4, pos4, wg, wu, wd)


def _comb_kernel(tk, sb, tsp_ref, xm_ref, ye_ref, selc_ref, posc_ref, affc_ref, g2_ref, fg_ref, o_ref, acc_ref):
    b = pl.program_id(0)
    j = pl.program_id(1)
    nt = pl.num_programs(1)
    ne = ye_ref.shape[1]
    acc_ref[...] = jnp.zeros_like(acc_ref)
    selc = selc_ref[0]
    posc = posc_ref[0]
    affc = affc_ref[0]
    for e in range(ne):
        base = (b * ne + e) * (nt + 1)
        pos_e = posc[:, e:e + 1]
        sel_e = selc[:, e:e + 1]
        val_e = affc[:, e:e + 1]

        def sb_body(s, c2, e=e, pos_e=pos_e, sel_e=sel_e, val_e=val_e):
            s0 = pl.multiple_of(s * sb, sb)
            slot = lax.broadcasted_iota(jnp.int32, (tk, sb), 1) + s0
            onehot = jnp.where((pos_e == slot) & (sel_e > 0), 1.0, 0.0).astype(BF16)
            rows = ye_ref[0, e, pl.ds(s0, sb), :]
            acc_ref[...] += val_e * _dot(onehot, rows)
            return c2

        first, last = _slot_block_range(tsp_ref[base + j], tsp_ref[base + j + 1], sb)
        lax.fori_loop(first, last, sb_body, 0)

    xo = xm_ref[0] + g2_ref[0] * acc_ref[...]
    ms = jnp.mean(xo * xo, axis=-1, keepdims=True)
    o_ref[0] = xo * lax.rsqrt(ms + NORM_EPS) * fg_ref[...]


def _comb(tsp, xm, ye, selc, posc, affc, g2, fg, tk, sb):
    bsz, t, d = xm.shape
    ne, cap = ye.shape[1], ye.shape[2]
    grid_spec = pltpu.PrefetchScalarGridSpec(
        num_scalar_prefetch=1,
        grid=(bsz, t // tk),
        in_specs=[pl.BlockSpec((1, tk, d), lambda b, j, s: (b, j, 0)),
                  pl.BlockSpec((1, ne, cap, d), lambda b, j, s: (b, 0, 0, 0)),
                  pl.BlockSpec((1, tk, ne), lambda b, j, s: (b, j, 0)),
                  pl.BlockSpec((1, tk, ne), lambda b, j, s: (b, j, 0)),
                  pl.BlockSpec((1, tk, ne), lambda b, j, s: (b, j, 0)),
                  pl.BlockSpec((1, 1, d), lambda b, j, s: (b, 0, 0)),
                  pl.BlockSpec((1, d), lambda b, j, s: (0, 0))],
        out_specs=pl.BlockSpec((1, tk, d), lambda b, j, s: (b, j, 0)),
        scratch_shapes=[pltpu.VMEM((tk, d), F32)],
    )
    return pl.pallas_call(
        functools.partial(_comb_kernel, tk, sb),
        grid_spec=grid_spec,
        out_shape=jax.ShapeDtypeStruct((bsz, t, d), F32),
        compiler_params=_cparams(("parallel", "arbitrary")),
        name="comb",
    )(tsp, xm, ye, selc, posc, affc, g2, fg)


def _block_diag_ones(n, blk, value=1.0):
    r = jnp.arange(n)
    return jnp.where((r[:, None] // blk) == (r[None, :] // blk), value, 0.0)


def kernel(x, c, ctx, c_ctx, ada_w, ada_b, norm1_g, norm2_g, w_in, shift_mu, w0, w_lora_up, a0, a_lora_up, k_k, k_a,
           r_k, g_lora_up, gn_g, gn_b, conv_w, w_out, router_w, exp_w_gate, exp_w_up, exp_w_down, final_g):
    bsz, t, d = x.shape
    lc = ctx.shape[1]
    ne = router_w.shape[-1]
    cap = EC_CAPACITY * t // ne
    tt = 256
    tk = 256
    sb = min(128, cap)
    l = 0

    rows = ((bsz + 1 + 7) // 8) * 8
    cc = jnp.zeros((rows, d), F32).at[:bsz].set(c).at[bsz].set(c_ctx)
    mod = _mod(cc, ada_w[l], ada_b[l][None, :])
    sh1, sc1, g1, sh2, sc2, g2 = (m[:, None, :] for m in jnp.split(mod[:bsz], 6, axis=-1))
    csh1, csc1 = (jnp.broadcast_to(m[None, None, :], (bsz, 1, d)) for m in jnp.split(mod[bsz], 6)[:2])

    w_rw = w_in[l][:, :RWKV_COLS].astype(BF16)
    w_cv = w_in[l][:, RWKV_COLS:].astype(BF16)
    n1g = norm1_g[l][None, :]
    px_rw, px_cv = _in_proj(x, sh1, sc1, n1g, w_rw, w_cv, 256)
    pc_rw, _ = _in_proj(ctx, csh1, csc1, n1g, w_rw, w_cv, 256)

    zw = jnp.zeros((LORA_W, 2 * D_RWKV), F32)
    lora = jnp.concatenate([
        jnp.concatenate([w_lora_up[l, 0], w_lora_up[l, 1], zw], axis=1),
        jnp.concatenate([zw, a_lora_up[l, 0], a_lora_up[l, 1]], axis=1)], axis=0).astype(BF16)
    ridx = jnp.arange(tt)
    same_chunk = (ridx[:, None] // CHUNK) == (ridx[None, :] // CHUNK)
    tril = jnp.where(same_chunk & (ridx[None, :] <= ridx[:, None]), 1.0, 0.0).astype(BF16)
    triu = jnp.where(same_chunk & (ridx[None, :] >= ridx[:, None]), 1.0, 0.0).astype(BF16)
    ones_bd = _block_diag_ones(D_RWKV, HEAD).astype(BF16)
    consts = (shift_mu[l][None, :], lora, w0[l], a0[l], k_k[l][None, :], k_a[l][None, :],
              r_k[l].reshape(1, D_RWKV), g_lora_up[l].astype(BF16), ones_bd, tril, triu)

    chunks_c = _prep(pc_rw, consts, False, lc)[:5]
    *chunks_x, bonus, gate = _prep(px_rw, consts, True, tt)

    s_zero = jnp.zeros((bsz, 2, N_PAIR, PAIR, PAIR), F32)
    _, _, s_ctx = _scan(*chunks_c, s_zero)
    yf, yb, _ = _scan(*chunks_x, s_ctx)

    rw_t = router_w[l].T
    rwh = rw_t.astype(BF16)
    rwl = (rw_t - rwh.astype(F32)).astype(BF16)
    avg = _block_diag_ones(D_RWKV, HEAD, 1.0 / HEAD).astype(BF16)
    xm, hx, aff_t = _out(yf, yb, bonus, gate, px_cv, x, g1, sh2, sc2, norm2_g[l][None, :], gn_g[l][None, :],
                         gn_b[l][None, :], conv_w[l], w_out[l].astype(BF16), rwh, rwl, avg, tt)

    r128 = jnp.arange(128)
    tri128 = jnp.where(r128[:, None] <= r128[None, :], 1.0, 0.0).astype(BF16)
    sel, pos = _topk(aff_t, tri128, cap)

    nt = t // tk
    tsp = jnp.concatenate([pos[:, :, ::tk], jnp.full((bsz, ne, 1), cap, jnp.int32)], axis=-1).reshape(-1)
    ye = _moe(tsp, hx, sel.reshape(bsz, ne, nt, tk), pos.reshape(bsz, ne, nt, tk),
              exp_w_gate[l].astype(BF16), exp_w_up[l].astype(BF16), exp_w_down[l].astype(BF16), cap, tk, sb)
    tr = lambda a: jnp.transpose(a, (0, 2, 1))
    return _comb(tsp, xm, ye, tr(sel), tr(pos), tr(aff_t), g2, final_g[None, :], tk, sb)
```

```python
import functools
import math

import jax
import jax.numpy as jnp
from jax import lax
from jax.experimental import pallas as pl
from jax.experimental.pallas import tpu as pltpu

F32 = jnp.float32
BF16 = jnp.bfloat16
HIGHEST = lax.Precision.HIGHEST

GRID_W = 64
D_RWKV = 512
D_CONV = 512
HEAD = 64
LORA_W = 64
LORA_A = 64
LORA_G = 128
N_EXPERTS = 16
EC_CAPACITY = 2
NORM_EPS = 1e-6
GN_EPS = 64e-5
RWKV_COLS = 3 * D_RWKV + LORA_W + LORA_A + LORA_G
CONV_COLS = 3 * D_CONV

CHUNK = 64
PAIR = 2 * HEAD
N_PAIR = D_RWKV // PAIR
VMEM_LIMIT = 48 * 1024 * 1024


def _cparams(sem):
    return pltpu.CompilerParams(dimension_semantics=sem, vmem_limit_bytes=VMEM_LIMIT)


def _dot(a, b):
    return jnp.dot(a, b, preferred_element_type=F32)


def _dot_nt(a, b):
    return lax.dot_general(a, b, (((1,), (1,)), ((), ())), preferred_element_type=F32)


def _split2(x):
    hi = x.astype(BF16)
    lo = (x - hi.astype(F32)).astype(BF16)
    return hi, lo


def _split3(x):
    hi = x.astype(BF16)
    r = x - hi.astype(F32)
    mid = r.astype(BF16)
    lo = (r - mid.astype(F32)).astype(BF16)
    return hi, mid, lo


def _seg_dot(x, m):
    hi, lo = _split2(x)
    return _dot(hi, m) + _dot(lo, m)


def _mod_kernel(c_ref, w_ref, b_ref, o_ref):
    c = c_ref[...]
    s = c * jax.nn.sigmoid(c)
    o_ref[...] = jnp.dot(s, w_ref[...], precision=HIGHEST, preferred_element_type=F32) + b_ref[...]


def _mod(cc, w, b):
    rows, d = cc.shape
    n = w.shape[1]
    tn = 1024
    return pl.pallas_call(
        _mod_kernel,
        grid=(n // tn,),
        in_specs=[pl.BlockSpec((rows, d), lambda j: (0, 0)),
                  pl.BlockSpec((d, tn), lambda j: (0, j)),
                  pl.BlockSpec((1, tn), lambda j: (0, j))],
        out_specs=pl.BlockSpec((rows, tn), lambda j: (0, j)),
        out_shape=jax.ShapeDtypeStruct((rows, n), F32),
        compiler_params=_cparams(("parallel",)),
        name="mod",
    )(cc, w, b)


def _in_proj_kernel(x_ref, sh_ref, sc_ref, g_ref, wrw_ref, wcv_ref, orw_ref, ocv_ref):
    x = x_ref[0]
    ms = jnp.mean(x * x, axis=-1, keepdims=True)
    h = x * lax.rsqrt(ms + NORM_EPS) * g_ref[...]
    h = (h * (1.0 + sc_ref[0]) + sh_ref[0]).astype(BF16)
    orw_ref[0] = _dot(h, wrw_ref[...])
    ocv_ref[0] = _dot(h, wcv_ref[...])


def _in_proj(x, sh, sc, g, w_rw, w_cv, tm):
    bsz, l, d = x.shape
    return pl.pallas_call(
        _in_proj_kernel,
        grid=(bsz, l // tm),
        in_specs=[pl.BlockSpec((1, tm, d), lambda b, i: (b, i, 0)),
                  pl.BlockSpec((1, 1, d), lambda b, i: (b, 0, 0)),
                  pl.BlockSpec((1, 1, d), lambda b, i: (b, 0, 0)),
                  pl.BlockSpec((1, d), lambda b, i: (0, 0)),
                  pl.BlockSpec((d, RWKV_COLS), lambda b, i: (0, 0)),
                  pl.BlockSpec((d, CONV_COLS), lambda b, i: (0, 0))],
        out_specs=[pl.BlockSpec((1, tm, RWKV_COLS), lambda b, i: (b, i, 0)),
                   pl.BlockSpec((1, tm, CONV_COLS), lambda b, i: (b, i, 0))],
        out_shape=[jax.ShapeDtypeStruct((bsz, l, RWKV_COLS), F32),
                   jax.ShapeDtypeStruct((bsz, l, CONV_COLS), F32)],
        compiler_params=_cparams(("parallel", "parallel")),
        name="in_proj",
    )(x, sh, sc, g, w_rw, w_cv)


def _bf(x):
    return x.astype(BF16)


def _chunk_local(chains):
    c = CHUNK
    n = range(len(chains))
    lane = lax.broadcasted_iota(jnp.int32, (c, PAIR), 1)
    h0 = lane < HEAD
    row = lax.broadcasted_iota(jnp.int32, (PAIR, PAIR), 0)
    col = lax.broadcasted_iota(jnp.int32, (PAIR, PAIR), 1)
    same = (row >= c) == (col >= c)
    tr, tc = row & (c - 1), col & (c - 1)
    eye = jnp.where(row == col, 1.0, 0.0)
    masks = {rev: (same & ((tc > tr) if rev else (tc < tr)), same & ((tc >= tr) if rev else (tc <= tr)))
             for rev in (False, True)}
    zb = jnp.zeros((c, PAIR), BF16)
    zf = jnp.zeros((c, PAIR), F32)

    def stack(x):
        return jnp.concatenate([jnp.where(h0, x, zb), jnp.where(h0, zb, x)], axis=0)

    a_st = [stack(ch[1]) for ch in chains]
    r_st = [stack(ch[2]) for ch in chains]
    gram = [_dot_nt(jnp.concatenate([a_st[i], r_st[i]], axis=0),
                    jnp.concatenate([chains[i][3], chains[i][3], chains[i][4], chains[i][4]], axis=0)) for i in n]
    lab = [jnp.where(masks[chains[i][0]][0], gram[i][:PAIR, :PAIR], 0.0) for i in n]
    lak = [jnp.where(masks[chains[i][0]][0], gram[i][:PAIR, PAIR:], 0.0) for i in n]
    mrbk = [jnp.concatenate([jnp.where(masks[chains[i][0]][1], gram[i][PAIR:, :PAIR], 0.0),
                             jnp.where(masks[chains[i][0]][1], gram[i][PAIR:, PAIR:], 0.0)], axis=1) for i in n]

    labb = [_bf(x) for x in lab]
    pw = [_dot(labb[i], labb[i]) for i in n]
    tp = [eye + lab[i] for i in n]
    for _ in range(4):
        pwb = [_bf(x) for x in pw]
        both = [_dot(pwb[i], jnp.concatenate([pwb[i], _bf(tp[i])], axis=1)) for i in n]
        pw = [x[:, :PAIR] for x in both]
        tp = [tp[i] + both[i][:, PAIR:] for i in n]
    tinv = [tp[i] + _dot(_bf(pw[i]), _bf(tp[i])) for i in n]

    v_rep = [jnp.concatenate([ch[7], ch[7]], axis=0) for ch in chains]
    lakv = [_dot(_bf(lak[i]), v_rep[i]) for i in n]
    x = [_dot(_bf(tinv[i]), jnp.concatenate([a_st[i], _bf(lakv[i])], axis=1)) for i in n]
    abar = [xi[:, :PAIR] for xi in x]
    u0 = [jnp.where(same, xi[:, PAIR:], 0.0) for xi in x]
    zpad = jnp.zeros((PAIR, PAIR), BF16)
    z = [_dot(_bf(mrbk[i]),
              jnp.concatenate([jnp.concatenate([_bf(abar[i]), _bf(u0[i])], axis=1),
                               jnp.concatenate([zpad, v_rep[i]], axis=1)], axis=0)) for i in n]
    rbar = [r_st[i].astype(F32) + z[i][:, :PAIR] for i in n]
    rbar = [_bf(x[:c] + x[c:]) for x in rbar]
    y0 = [jnp.where(h0, zi[:c, PAIR:], zi[c:, PAIR:]) for zi in z]

    au = [jnp.concatenate([jnp.concatenate([abar[i], u0[i]], axis=1),
                           jnp.concatenate([zf, chains[i][7].astype(F32)], axis=1),
                           jnp.zeros((c, 2 * PAIR), F32)], axis=0) for i in n]
    mn = [_dot(_bf(au[i].T), jnp.concatenate([chains[i][5], chains[i][5], chains[i][6], zb], axis=0)) for i in n]
    mx = [_bf(jnp.where(h0, m[:c], m[c:PAIR])) for m in mn]
    nn = [jnp.where(h0, m[PAIR:PAIR + c], m[PAIR + c:]) for m in mn]
    return list(zip(rbar, y0, mx, nn))


DIR_COLS = 6 * D_RWKV
SCR_COLS = 2 * DIR_COLS + D_RWKV


def _prep_body(tt, p, shifted, mu_ref, lw_ref, w0_ref, a0_ref, kk_ref, ka_ref, rk_ref, gup_ref,
               ones_ref, tril_ref, triu_ref, rbar_ref, y0_ref, mx_ref, nn_ref, wc_ref, bonus_ref, g_ref, scr_ref):
    pm = p + mu_ref[...] * (shifted - p)
    xr = pm[:, 0:D_RWKV]
    xk = pm[:, D_RWKV:2 * D_RWKV]
    xv = pm[:, 2 * D_RWKV:3 * D_RWKV]
    xwa = pm[:, 3 * D_RWKV:3 * D_RWKV + LORA_W + LORA_A]
    xg = pm[:, 3 * D_RWKV + LORA_W + LORA_A:]

    ones_bd = ones_ref[...]
    kraw = xk * kk_ref[...]
    ss = _seg_dot(kraw * kraw, ones_bd)
    kk = kraw / jnp.maximum(jnp.sqrt(ss), 1e-12)

    lane = lax.broadcasted_iota(jnp.int32, xwa.shape, 1)
    lin = jnp.where(lane < LORA_W, jnp.tanh(xwa), xwa).astype(BF16)
    lo = _dot(lin, lw_ref[...])

    scr_ref[:, 2 * DIR_COLS:] = xv.astype(BF16)
    ksum = None
    for d in range(2):
        z = w0_ref[d:d + 1, :] + lo[:, d * D_RWKV:(d + 1) * D_RWKV]
        ld = -math.exp(-0.5) * jax.nn.sigmoid(z)
        ag = jax.nn.sigmoid(a0_ref[d:d + 1, :] + lo[:, (2 + d) * D_RWKV:(3 + d) * D_RWKV])
        kd = xk * (1.0 + (ag - 1.0) * ka_ref[...])
        bb = kk * ag
        ksum = kd if ksum is None else ksum + kd
        tri = tril_ref[...] if d == 0 else triu_ref[...]
        h3 = _split3(ld)
        cs = _dot(tri, h3[0]) + _dot(tri, h3[1]) + _dot(tri, h3[2])
        e_in = jnp.exp(cs)
        e_ex = jnp.exp(cs - ld)
        e_neg = jnp.exp(-cs)
        bt = bb * e_neg
        kt = kd * e_neg
        base = d * DIR_COLS
        scr_ref[:, base:base + D_RWKV] = (-kk * e_ex).astype(BF16)
        scr_ref[:, base + D_RWKV:base + 2 * D_RWKV] = (xr * e_in).astype(BF16)
        scr_ref[:, base + 2 * D_RWKV:base + 3 * D_RWKV] = bt.astype(BF16)
        scr_ref[:, base + 3 * D_RWKV:base + 4 * D_RWKV] = kt.astype(BF16)
        for c in range(tt // CHUNK):
            rows = slice(c * CHUNK, (c + 1) * CHUNK)
            edge = c * CHUNK + (CHUNK - 1 if d == 0 else 0)
            wrow = e_in[edge:edge + 1, :]
            wc_ref[0, c, :, d * D_RWKV:(d + 1) * D_RWKV] = wrow
            scr_ref[rows, base + 4 * D_RWKV:base + 5 * D_RWKV] = (bt[rows] * wrow).astype(BF16)
            scr_ref[rows, base + 5 * D_RWKV:base + 6 * D_RWKV] = (kt[rows] * wrow).astype(BF16)

    bonus_ref[0] = _seg_dot(xr * ksum * rk_ref[...], ones_bd) * xv
    g_ref[0] = _dot(jax.nn.sigmoid(xg).astype(BF16), gup_ref[...])

    def chunk_body(c, carry):
        r0 = pl.multiple_of(c * CHUNK, CHUNK)
        chains = []
        for d in range(2):
            for pp in range(N_PAIR):
                cols = [d * DIR_COLS + j * D_RWKV + pp * PAIR for j in range(6)] + [2 * DIR_COLS + pp * PAIR]
                chains.append((d == 1,) + tuple(scr_ref[pl.ds(r0, CHUNK), k:k + PAIR] for k in cols))
        for idx, (rbar, y0, mx, nn) in enumerate(_chunk_local(chains)):
            col = (idx // N_PAIR) * D_RWKV + (idx % N_PAIR) * PAIR
            rbar_ref[0, pl.ds(r0, CHUNK), col:col + PAIR] = rbar
            y0_ref[0, pl.ds(r0, CHUNK), col:col + PAIR] = y0
            mx_ref[0, c, :, col:col + PAIR] = mx
            nn_ref[0, c, :, col:col + PAIR] = nn
        return carry

    lax.fori_loop(0, tt // CHUNK, chunk_body, 0)


def _prep_latent_kernel(tt, p_ref, prev_ref, next_ref, *rest):
    ext_ref = rest[-1]
    i = pl.program_id(1)
    n = pl.num_programs(1)
    ext_ref[0:GRID_W] = jnp.where(i > 0, prev_ref[0], 0.0)
    ext_ref[GRID_W:GRID_W + tt] = p_ref[0]
    ext_ref[GRID_W + tt:] = jnp.where(i < n - 1, next_ref[0], 0.0)
    shape = (tt, RWKV_COLS)
    col = lax.broadcasted_iota(jnp.int32, shape, 0) & (GRID_W - 1)
    q = lax.broadcasted_iota(jnp.int32, shape, 1) & 3
    left = jnp.where(col == 0, 0.0, ext_ref[GRID_W - 1:GRID_W - 1 + tt])
    right = jnp.where(col == GRID_W - 1, 0.0, ext_ref[GRID_W + 1:GRID_W + 1 + tt])
    up = ext_ref[0:tt]
    down = ext_ref[2 * GRID_W:2 * GRID_W + tt]
    shifted = jnp.where(q == 0, left, jnp.where(q == 1, right, jnp.where(q == 2, up, down)))
    _prep_body(tt, p_ref[0], shifted, *rest[:-1])


def _prep_ctx_kernel(tt, p_ref, *rest):
    ext_ref = rest[-1]
    ext_ref[0:8] = jnp.zeros((8, RWKV_COLS), F32)
    ext_ref[8:8 + tt] = p_ref[0]
    ext_ref[8 + tt:] = jnp.zeros((8, RWKV_COLS), F32)
    q = lax.broadcasted_iota(jnp.int32, (tt, RWKV_COLS), 1) & 1
    shifted = jnp.where(q == 0, ext_ref[7:7 + tt], ext_ref[9:9 + tt])
    _prep_body(tt, p_ref[0], shifted, *rest[:-1])


def _prep(p_rw, consts, latent, tt):
    bsz, l, _ = p_rw.shape
    nc = l // CHUNK
    cpt = tt // CHUNK
    const_specs = [pl.BlockSpec(c.shape, lambda b, i, nd=c.ndim: (0,) * nd) for c in consts]
    if latent:
        nblk = l // GRID_W
        in_specs = [pl.BlockSpec((1, tt, RWKV_COLS), lambda b, i: (b, i, 0)),
                    pl.BlockSpec((1, GRID_W, RWKV_COLS),
                                 lambda b, i: (b, jnp.maximum(i * (tt // GRID_W) - 1, 0), 0)),
                    pl.BlockSpec((1, GRID_W, RWKV_COLS),
                                 lambda b, i: (b, jnp.minimum((i + 1) * (tt // GRID_W), nblk - 1), 0))]
        args = (p_rw, p_rw, p_rw)
        kern = functools.partial(_prep_latent_kernel, tt)
        ext_rows = tt + 2 * GRID_W
    else:
        assert tt == l
        in_specs = [pl.BlockSpec((1, tt, RWKV_COLS), lambda b, i: (b, i, 0))]
        args = (p_rw,)
        kern = functools.partial(_prep_ctx_kernel, tt)
        ext_rows = tt + 16
    row = lambda b, i: (b, i, 0)
    chunk = lambda b, i: (b, i, 0, 0)
    w2 = 2 * D_RWKV
    return pl.pallas_call(
        kern,
        grid=(bsz, l // tt),
        in_specs=in_specs + const_specs,
        out_specs=[pl.BlockSpec((1, tt, w2), row), pl.BlockSpec((1, tt, w2), row),
                   pl.BlockSpec((1, cpt, CHUNK, w2), chunk), pl.BlockSpec((1, cpt, CHUNK, w2), chunk),
                   pl.BlockSpec((1, cpt, 1, w2), chunk),
                   pl.BlockSpec((1, tt, D_RWKV), row), pl.BlockSpec((1, tt, D_RWKV), row)],
        out_shape=[jax.ShapeDtypeStruct((bsz, l, w2), BF16), jax.ShapeDtypeStruct((bsz, l, w2), F32),
                   jax.ShapeDtypeStruct((bsz, nc, CHUNK, w2), BF16), jax.ShapeDtypeStruct((bsz, nc, CHUNK, w2), F32),
                   jax.ShapeDtypeStruct((bsz, nc, 1, w2), F32),
                   jax.ShapeDtypeStruct((bsz, l, D_RWKV), F32), jax.ShapeDtypeStruct((bsz, l, D_RWKV), F32)],
        scratch_shapes=[pltpu.VMEM((tt, SCR_COLS), BF16), pltpu.VMEM((ext_rows, RWKV_COLS), F32)],
        compiler_params=_cparams(("parallel", "parallel")),
        name="prep_latent" if latent else "prep_ctx",
    )(*args, *consts)


def _scan_kernel(rbf_ref, rbb_ref, y0f_ref, y0b_ref, mxf_ref, mxb_ref, nnf_ref, nnb_ref, wcf_ref, wcb_ref,
                 s0_ref, yf_ref, yb_ref, s_ref):
    i = pl.program_id(1)

    @pl.when(i == 0)
    def _():
        s_ref[...] = s0_ref[...]

    c = CHUNK
    h0 = lax.broadcasted_iota(jnp.int32, (c, PAIR), 1) < HEAD
    row = lax.broadcasted_iota(jnp.int32, (PAIR, PAIR), 0)
    col = lax.broadcasted_iota(jnp.int32, (PAIR, PAIR), 1)
    same = (row >= c) == (col >= c)
    zb = jnp.zeros((c, PAIR), BF16)
    dirs = ((rbf_ref, y0f_ref, mxf_ref, nnf_ref, wcf_ref, yf_ref), (rbb_ref, y0b_ref, mxb_ref, nnb_ref, wcb_ref, yb_ref))
    idx = [(d, p) for d in range(2) for p in range(N_PAIR)]
    lanes = [slice(p * PAIR, (p + 1) * PAIR) for _, p in idx]
    s = [s_ref[0, d, p] for d, p in idx]
    sb = [_bf(x) for x in s]
    rbar = [dirs[d][0][0, :, lanes[k]] for k, (d, _) in enumerate(idx)]
    r_st = [jnp.concatenate([jnp.where(h0, x, zb), jnp.where(h0, zb, x)], axis=0) for x in rbar]
    y_st = [_dot_nt(r_st[k], sb[k]) for k in range(len(idx))]
    m_bd = [jnp.where(same, jnp.concatenate([x, x], axis=0), jnp.zeros((PAIR, PAIR), BF16))
            for x in (dirs[d][2][0, 0, :, lanes[k]] for k, (d, _) in enumerate(idx))]
    sm = [_dot(sb[k], m_bd[k]) for k in range(len(idx))]
    for k, (d, p) in enumerate(idx):
        nn = dirs[d][3][0, 0, :, lanes[k]]
        n_bd = jnp.where(same, jnp.concatenate([nn, nn], axis=0), 0.0)
        dirs[d][5][0, :, lanes[k]] = y_st[k][:c] + y_st[k][c:] + dirs[d][1][0, :, lanes[k]]
        s_ref[0, d, p] = s[k] * dirs[d][4][0, 0, :, lanes[k]] + sm[k] + n_bd


def _scan(rbar, y0, mx, nn, wc, s0):
    bsz, l, _ = rbar.shape
    nc = l // CHUNK
    row_f = pl.BlockSpec((1, CHUNK, D_RWKV), lambda b, i: (b, i, 0))
    row_b = pl.BlockSpec((1, CHUNK, D_RWKV), lambda b, i: (b, nc - 1 - i, 1))
    chk_f = pl.BlockSpec((1, 1, CHUNK, D_RWKV), lambda b, i: (b, i, 0, 0))
    chk_b = pl.BlockSpec((1, 1, CHUNK, D_RWKV), lambda b, i: (b, nc - 1 - i, 0, 1))
    wc_f = pl.BlockSpec((1, 1, 1, D_RWKV), lambda b, i: (b, i, 0, 0))
    wc_b = pl.BlockSpec((1, 1, 1, D_RWKV), lambda b, i: (b, nc - 1 - i, 0, 1))
    st_spec = pl.BlockSpec((1, 2, N_PAIR, PAIR, PAIR), lambda b, i: (b, 0, 0, 0, 0))
    return pl.pallas_call(
        _scan_kernel,
        grid=(bsz, nc),
        in_specs=[row_f, row_b, row_f, row_b, chk_f, chk_b, chk_f, chk_b, wc_f, wc_b, st_spec],
        out_specs=[row_f, pl.BlockSpec((1, CHUNK, D_RWKV), lambda b, i: (b, nc - 1 - i, 0)), st_spec],
        out_shape=[jax.ShapeDtypeStruct((bsz, l, D_RWKV), F32),
                   jax.ShapeDtypeStruct((bsz, l, D_RWKV), F32),
                   jax.ShapeDtypeStruct((bsz, 2, N_PAIR, PAIR, PAIR), F32)],
        compiler_params=_cparams(("parallel", "arbitrary")),
        name="scan",
    )(rbar, rbar, y0, y0, mx, mx, nn, nn, wc, wc, s0)


def _out_kernel(tt, yf_ref, yb_ref, bonus_ref, g_ref, cv_ref, cvp_ref, cvn_ref, x_ref, g1_ref, sh2_ref,
                sc2_ref, n2g_ref, gng_ref, gnb_ref, convw_ref, wout_ref, rwh_ref, rwl_ref, avg_ref,
                xm_ref, hx_ref, aff_ref):
    i = pl.program_id(1)
    n = pl.num_programs(1)
    avg = avg_ref[...]
    y = yf_ref[0] + yb_ref[0]
    mu = _seg_dot(y, avg)
    dlt = y - mu
    var = _seg_dot(dlt * dlt, avg)
    yn = dlt * lax.rsqrt(var + GN_EPS) * gng_ref[...] + gnb_ref[...]
    ax = ((yn + bonus_ref[0]) * g_ref[0]).astype(BF16)

    cv = cv_ref[0]
    b_gate = cv[:, 0:D_CONV]
    cu = cv[:, D_CONV:2 * D_CONV] * cv[:, 2 * D_CONV:]
    cvp = cvp_ref[0]
    cvn = cvn_ref[0]
    cu_prev = jnp.where(i > 0, cvp[7:8, D_CONV:2 * D_CONV] * cvp[7:8, 2 * D_CONV:], 0.0)
    cu_next = jnp.where(i < n - 1, cvn[0:1, D_CONV:2 * D_CONV] * cvn[0:1, 2 * D_CONV:], 0.0)
    ridx = lax.broadcasted_iota(jnp.int32, cu.shape, 0)
    cu_m1 = jnp.where(ridx == 0, cu_prev, pltpu.roll(cu, 1, 0))
    cu_p1 = jnp.where(ridx == tt - 1, cu_next, pltpu.roll(cu, tt - 1, 0))
    conv = convw_ref[0:1, :] * cu_m1 + convw_ref[1:2, :] * cu + convw_ref[2:3, :] * cu_p1
    bx = (b_gate * conv).astype(BF16)

    mix = _dot(ax, wout_ref[0:D_RWKV, :]) + _dot(bx, wout_ref[D_RWKV:, :])
    xm = x_ref[0] + g1_ref[0] * mix
    xm_ref[0] = xm
    ms = jnp.mean(xm * xm, axis=-1, keepdims=True)
    hx = xm * lax.rsqrt(ms + NORM_EPS) * n2g_ref[...]
    hx = hx * (1.0 + sc2_ref[0]) + sh2_ref[0]
    hi, lo = _split2(hx)
    hx_ref[0] = hi
    rwh = rwh_ref[...]
    logits = _dot_nt(rwh, hi) + _dot_nt(rwh, lo) + _dot_nt(rwl_ref[...], hi)
    m = jnp.max(logits, axis=0, keepdims=True)
    ex = jnp.exp(logits - m)
    aff_ref[0] = ex / jnp.sum(ex, axis=0, keepdims=True)


def _out(yf, yb, bonus, g, p_cv, x, g1, sh2, sc2, n2g, gng, gnb, convw, wout, rwh, rwl, avg, tt):
    bsz, t, d = x.shape
    ne = rwh.shape[0]
    nb8 = t // 8
    row = lambda b, i: (b, i, 0)
    per_b = lambda b, i: (b, 0, 0)
    const2 = lambda b, i: (0, 0)
    return pl.pallas_call(
        functools.partial(_out_kernel, tt),
        grid=(bsz, t // tt),
        in_specs=[pl.BlockSpec((1, tt, D_RWKV), row), pl.BlockSpec((1, tt, D_RWKV), row),
                  pl.BlockSpec((1, tt, D_RWKV), row), pl.BlockSpec((1, tt, D_RWKV), row),
                  pl.BlockSpec((1, tt, CONV_COLS), row),
                  pl.BlockSpec((1, 8, CONV_COLS), lambda b, i: (b, jnp.maximum(i * (tt // 8) - 1, 0), 0)),
                  pl.BlockSpec((1, 8, CONV_COLS), lambda b, i: (b, jnp.minimum((i + 1) * (tt // 8), nb8 - 1), 0)),
                  pl.BlockSpec((1, tt, d), row),
                  pl.BlockSpec((1, 1, d), per_b), pl.BlockSpec((1, 1, d), per_b), pl.BlockSpec((1, 1, d), per_b),
                  pl.BlockSpec((1, d), const2), pl.BlockSpec((1, D_RWKV), const2), pl.BlockSpec((1, D_RWKV), const2),
                  pl.BlockSpec((3, D_CONV), const2), pl.BlockSpec((D_RWKV + D_CONV, d), const2),
                  pl.BlockSpec((ne, d), const2), pl.BlockSpec((ne, d), const2),
                  pl.BlockSpec((D_RWKV, D_RWKV), const2)],
        out_specs=[pl.BlockSpec((1, tt, d), row), pl.BlockSpec((1, tt, d), row),
                   pl.BlockSpec((1, ne, tt), lambda b, i: (b, 0, i))],
        out_shape=[jax.ShapeDtypeStruct((bsz, t, d), F32), jax.ShapeDtypeStruct((bsz, t, d), BF16),
                   jax.ShapeDtypeStruct((bsz, ne, t), F32)],
        compiler_params=_cparams(("parallel", "parallel")),
        name="out",
    )(yf, yb, bonus, g, p_cv, p_cv, p_cv, x, g1, sh2, sc2, n2g, gng, gnb, convw, wout, rwh, rwl, avg)


def _prefix_blocks(mask_fn, t, tri, emit):
    carry = None
    for j in range(t // 128):
        m = mask_fn(j)
        inc = _dot(m.astype(BF16), tri)
        carry = jnp.zeros_like(inc[:, 0:1]) if carry is None else carry
        emit(j, m, inc - m + carry)
        carry = carry + inc[:, 127:128]


def _topk_kernel(cap, aff_ref, tri_ref, sel_ref, pos_ref):
    t = aff_ref.shape[2]
    aff = aff_ref[0]

    def body(k, bits):
        cand = bits | jnp.left_shift(jnp.int32(1), 30 - k)
        cnt = jnp.sum(jnp.where(aff >= pltpu.bitcast(cand, F32), 1, 0), axis=-1, keepdims=True)
        return jnp.where(cnt >= cap, cand, bits)

    bits = lax.fori_loop(0, 31, body, jnp.zeros((aff.shape[0], 1), jnp.int32))
    thr = pltpu.bitcast(bits, F32)
    above = pltpu.bitcast(bits + 1, F32)
    n_gt = jnp.sum(jnp.where(aff >= above, 1, 0), axis=-1, keepdims=True)
    need = (cap - n_gt).astype(F32)
    tri = tri_ref[...]

    def blk(j):
        return aff[:, j * 128:(j + 1) * 128]

    def emit_sel(j, eq, before):
        take = (blk(j) >= above) | ((eq > 0.5) & (before < need))
        sel_ref[0, :, j * 128:(j + 1) * 128] = jnp.where(take, 1, 0)

    _prefix_blocks(lambda j: jnp.where((blk(j) >= thr) & (blk(j) < above), 1.0, 0.0), t, tri, emit_sel)

    def emit_pos(j, m, before):
        pos_ref[0, :, j * 128:(j + 1) * 128] = before.astype(jnp.int32)

    _prefix_blocks(lambda j: sel_ref[0, :, j * 128:(j + 1) * 128].astype(F32), t, tri, emit_pos)


def _topk(aff_t, tri, cap):
    bsz, ne, t = aff_t.shape
    spec = pl.BlockSpec((1, ne, t), lambda b: (b, 0, 0))
    return pl.pallas_call(
        functools.partial(_topk_kernel, cap),
        grid=(bsz,),
        in_specs=[spec, pl.BlockSpec((128, 128), lambda b: (0, 0))],
        out_specs=[spec, spec],
        out_shape=[jax.ShapeDtypeStruct((bsz, ne, t), jnp.int32)] * 2,
        compiler_params=_cparams(("parallel",)),
        name="topk",
    )(aff_t, tri)


def _slot_block_range(lo, hi, sb):
    shift = sb.bit_length() - 1
    assert sb == 1 << shift
    first = lo >> shift
    return first, jnp.where(hi > lo, ((hi - 1) >> shift) + 1, first)


def _moe_kernel(tk, sb, tsp_ref, hx_ref, sel_ref, pos_ref, wg_ref, wu_ref, wd_ref, ye_ref, xs_ref):
    b = pl.program_id(0)
    e = pl.program_id(1)
    ne = pl.num_programs(1)
    nt = hx_ref.shape[1] // tk
    base = (b * ne + e) * (nt + 1)
    xs_ref[...] = jnp.zeros_like(xs_ref)

    def tile_body(j, carry):
        t0 = pl.multiple_of(j * tk, tk)
        hxt = hx_ref[0, pl.ds(t0, tk), :]
        posr = pos_ref[0, 0, pl.ds(j, 1), :]
        selr = sel_ref[0, 0, pl.ds(j, 1), :]

        def sb_body(s, c2):
            s0 = pl.multiple_of(s * sb, sb)
            slot = lax.broadcasted_iota(jnp.int32, (sb, tk), 0) + s0
            onehot = jnp.where((posr == slot) & (selr > 0), 1.0, 0.0).astype(BF16)
            xs_ref[pl.ds(s0, sb), :] += _dot(onehot, hxt)
            return c2

        first, last = _slot_block_range(tsp_ref[base + j], tsp_ref[base + j + 1], sb)
        lax.fori_loop(first, last, sb_body, 0)
        return carry

    lax.fori_loop(0, nt, tile_body, 0)
    xs = xs_ref[...].astype(BF16)
    h1 = _dot(xs, wg_ref[0])
    h2 = _dot(xs, wu_ref[0])
    hid = (h1 * jax.nn.sigmoid(h1) * h2).astype(BF16)
    ye_ref[0, 0] = _dot(hid, wd_ref[0]).astype(BF16)


def _moe(tsp, hx, sel4, pos4, wg, wu, wd, cap, tk, sb):
    bsz, t, d = hx.shape
    ne, _, f = wg.shape
    nt = t // tk
    grid_spec = pltpu.PrefetchScalarGridSpec(
        num_scalar_prefetch=1,
        grid=(bsz, ne),
        in_specs=[pl.BlockSpec((1, t, d), lambda b, e, s: (b, 0, 0)),
                  pl.BlockSpec((1, 1, nt, tk), lambda b, e, s: (b, e, 0, 0)),
                  pl.BlockSpec((1, 1, nt, tk), lambda b, e, s: (b, e, 0, 0)),
                  pl.BlockSpec((1, d, f), lambda b, e, s: (e, 0, 0)),
                  pl.BlockSpec((1, d, f), lambda b, e, s: (e, 0, 0)),
                  pl.BlockSpec((1, f, d), lambda b, e, s: (e, 0, 0))],
        out_specs=pl.BlockSpec((1, 1, cap, d), lambda b, e, s: (b, e, 0, 0)),
        scratch_shapes=[pltpu.VMEM((cap, d), F32)],
    )
    return pl.pallas_call(
        functools.partial(_moe_kernel, tk, sb),
        grid_spec=grid_spec,
        out_shape=jax.ShapeDtypeStruct((bsz, ne, cap, d), BF16),
        compiler_params=_cparams(("parallel", "arbitrary")),
        name="moe",
    )(tsp, hx, sel4, pos4, wg, wu, wd)


COMB_ROWS = 128
SLOT_ALIGN = 16


def _comb_kernel(tk, win, tsp_ref, xm_ref, ye_ref, selc_ref, posc_ref, affc_ref, g2_ref, fg_ref, o_ref):
    b = pl.program_id(0)
    j = pl.program_id(1)
    nh = tk // COMB_ROWS
    ntile = pl.num_programs(1) * nh
    ne, cap = ye_ref.shape[1], ye_ref.shape[2]
    selc = selc_ref[0]
    posc = posc_ref[0]
    affc = affc_ref[0]
    slot = lax.broadcasted_iota(jnp.int32, (COMB_ROWS, win), 1)
    for h in range(nh):
        rows = slice(h * COMB_ROWS, (h + 1) * COMB_ROWS)
        acc = None
        for e in range(ne):
            first = tsp_ref[(b * ne + e) * ntile + j * nh + h]
            start = jnp.minimum(first & ~(SLOT_ALIGN - 1), cap - win)
            start = pl.multiple_of(start, SLOT_ALIGN)
            hit = (posc[rows, e:e + 1] - start == slot) & (selc[rows, e:e + 1] > 0)
            onehot = jnp.where(hit, 1.0, 0.0).astype(BF16)
            part = affc[rows, e:e + 1] * _dot(onehot, ye_ref[0, e, pl.ds(start, win), :])
            acc = part if acc is None else acc + part
        xo = xm_ref[0, rows, :] + g2_ref[0] * acc
        ms = jnp.mean(xo * xo, axis=-1, keepdims=True)
        o_ref[0, rows, :] = xo * lax.rsqrt(ms + NORM_EPS) * fg_ref[...]


def _comb(tsp, xm, ye, selc, posc, affc, g2, fg, tk):
    bsz, t, d = xm.shape
    ne, cap = ye.shape[1], ye.shape[2]
    win = min(2 * COMB_ROWS, cap)
    assert win == cap or win >= COMB_ROWS + SLOT_ALIGN
    grid_spec = pltpu.PrefetchScalarGridSpec(
        num_scalar_prefetch=1,
        grid=(bsz, t // tk),
        in_specs=[pl.BlockSpec((1, tk, d), lambda b, j, s: (b, j, 0)),
                  pl.BlockSpec((1, ne, cap, d), lambda b, j, s: (b, 0, 0, 0)),
                  pl.BlockSpec((1, tk, ne), lambda b, j, s: (b, j, 0)),
                  pl.BlockSpec((1, tk, ne), lambda b, j, s: (b, j, 0)),
                  pl.BlockSpec((1, tk, ne), lambda b, j, s: (b, j, 0)),
                  pl.BlockSpec((1, 1, d), lambda b, j, s: (b, 0, 0)),
                  pl.BlockSpec((1, d), lambda b, j, s: (0, 0))],
        out_specs=pl.BlockSpec((1, tk, d), lambda b, j, s: (b, j, 0)),
    )
    return pl.pallas_call(
        functools.partial(_comb_kernel, tk, win),
        grid_spec=grid_spec,
        out_shape=jax.ShapeDtypeStruct((bsz, t, d), F32),
        compiler_params=_cparams(("parallel", "arbitrary")),
        name="comb",
    )(tsp, xm, ye, selc, posc, affc, g2, fg)


def _block_diag_ones(n, blk, value=1.0):
    r = jnp.arange(n)
    return jnp.where((r[:, None] // blk) == (r[None, :] // blk), value, 0.0)


def kernel(x, c, ctx, c_ctx, ada_w, ada_b, norm1_g, norm2_g, w_in, shift_mu, w0, w_lora_up, a0, a_lora_up, k_k, k_a,
           r_k, g_lora_up, gn_g, gn_b, conv_w, w_out, router_w, exp_w_gate, exp_w_up, exp_w_down, final_g):
    bsz, t, d = x.shape
    lc = ctx.shape[1]
    ne = router_w.shape[-1]
    cap = EC_CAPACITY * t // ne
    tt = 256
    tk = 512
    sb = min(128, cap)
    l = 0

    rows = ((bsz + 1 + 7) // 8) * 8
    cc = jnp.zeros((rows, d), F32).at[:bsz].set(c).at[bsz].set(c_ctx)
    mod = _mod(cc, ada_w[l], ada_b[l][None, :])
    sh1, sc1, g1, sh2, sc2, g2 = (m[:, None, :] for m in jnp.split(mod[:bsz], 6, axis=-1))
    csh1, csc1 = (jnp.broadcast_to(m[None, None, :], (bsz, 1, d)) for m in jnp.split(mod[bsz], 6)[:2])

    w_rw = w_in[l][:, :RWKV_COLS].astype(BF16)
    w_cv = w_in[l][:, RWKV_COLS:].astype(BF16)
    n1g = norm1_g[l][None, :]
    px_rw, px_cv = _in_proj(x, sh1, sc1, n1g, w_rw, w_cv, 256)
    pc_rw, _ = _in_proj(ctx, csh1, csc1, n1g, w_rw, w_cv, 256)

    zw = jnp.zeros((LORA_W, 2 * D_RWKV), F32)
    lora = jnp.concatenate([
        jnp.concatenate([w_lora_up[l, 0], w_lora_up[l, 1], zw], axis=1),
        jnp.concatenate([zw, a_lora_up[l, 0], a_lora_up[l, 1]], axis=1)], axis=0).astype(BF16)
    ridx = jnp.arange(tt)
    same_chunk = (ridx[:, None] // CHUNK) == (ridx[None, :] // CHUNK)
    tril = jnp.where(same_chunk & (ridx[None, :] <= ridx[:, None]), 1.0, 0.0).astype(BF16)
    triu = jnp.where(same_chunk & (ridx[None, :] >= ridx[:, None]), 1.0, 0.0).astype(BF16)
    ones_bd = _block_diag_ones(D_RWKV, HEAD).astype(BF16)
    consts = (shift_mu[l][None, :], lora, w0[l], a0[l], k_k[l][None, :], k_a[l][None, :],
              r_k[l].reshape(1, D_RWKV), g_lora_up[l].astype(BF16), ones_bd, tril, triu)

    chunks_c = _prep(pc_rw, consts, False, lc)[:5]
    *chunks_x, bonus, gate = _prep(px_rw, consts, True, tt)

    s_zero = jnp.zeros((bsz, 2, N_PAIR, PAIR, PAIR), F32)
    _, _, s_ctx = _scan(*chunks_c, s_zero)
    yf, yb, _ = _scan(*chunks_x, s_ctx)

    rw_t = router_w[l].T
    rwh = rw_t.astype(BF16)
    rwl = (rw_t - rwh.astype(F32)).astype(BF16)
    avg = _block_diag_ones(D_RWKV, HEAD, 1.0 / HEAD).astype(BF16)
    xm, hx, aff_t = _out(yf, yb, bonus, gate, px_cv, x, g1, sh2, sc2, norm2_g[l][None, :], gn_g[l][None, :],
                         gn_b[l][None, :], conv_w[l], w_out[l].astype(BF16), rwh, rwl, avg, tt)

    r128 = jnp.arange(128)
    tri128 = jnp.where(r128[:, None] <= r128[None, :], 1.0, 0.0).astype(BF16)
    sel, pos = _topk(aff_t, tri128, cap)

    nt = t // tk
    tsp = jnp.concatenate([pos[:, :, ::tk], jnp.full((bsz, ne, 1), cap, jnp.int32)], axis=-1).reshape(-1)
    ye = _moe(tsp, hx, sel.reshape(bsz, ne, nt, tk), pos.reshape(bsz, ne, nt, tk),
              exp_w_gate[l].astype(BF16), exp_w_up[l].astype(BF16), exp_w_down[l].astype(BF16), cap, tk, sb)
    tr = lambda a: jnp.transpose(a, (0, 2, 1))
    first_slot = pos[:, :, ::COMB_ROWS].reshape(-1)
    return _comb(first_slot, xm, ye, tr(sel), tr(pos), tr(aff_t), g2, final_g[None, :], tt)
```

```python
import functools
import math

import jax
import jax.numpy as jnp
from jax import lax
from jax.experimental import pallas as pl
from jax.experimental.pallas import tpu as pltpu

F32 = jnp.float32
BF16 = jnp.bfloat16
HIGHEST = lax.Precision.HIGHEST

GRID_W = 64
D_RWKV = 512
D_CONV = 512
HEAD = 64
LORA_W = 64
LORA_A = 64
LORA_G = 128
N_EXPERTS = 16
EC_CAPACITY = 2
NORM_EPS = 1e-6
GN_EPS = 64e-5
RWKV_COLS = 3 * D_RWKV + LORA_W + LORA_A + LORA_G
CONV_COLS = 3 * D_CONV

CHUNK = 64
PAIR = 2 * HEAD
N_PAIR = D_RWKV // PAIR
VMEM_LIMIT = 48 * 1024 * 1024


def _cparams(sem):
    return pltpu.CompilerParams(dimension_semantics=sem, vmem_limit_bytes=VMEM_LIMIT)


def _dot(a, b):
    return jnp.dot(a, b, preferred_element_type=F32)


def _dot_nt(a, b):
    return lax.dot_general(a, b, (((1,), (1,)), ((), ())), preferred_element_type=F32)


def _split2(x):
    hi = x.astype(BF16)
    lo = (x - hi.astype(F32)).astype(BF16)
    return hi, lo


def _split3(x):
    hi = x.astype(BF16)
    r = x - hi.astype(F32)
    mid = r.astype(BF16)
    lo = (r - mid.astype(F32)).astype(BF16)
    return hi, mid, lo


def _seg_dot(x, m):
    hi, lo = _split2(x)
    return _dot(hi, m) + _dot(lo, m)


def _mod_kernel(c_ref, w_ref, b_ref, o_ref):
    c = c_ref[...]
    s = c * jax.nn.sigmoid(c)
    o_ref[...] = jnp.dot(s, w_ref[...], precision=HIGHEST, preferred_element_type=F32) + b_ref[...]


def _mod(cc, w, b):
    rows, d = cc.shape
    n = w.shape[1]
    tn = 1024
    return pl.pallas_call(
        _mod_kernel,
        grid=(n // tn,),
        in_specs=[pl.BlockSpec((rows, d), lambda j: (0, 0)),
                  pl.BlockSpec((d, tn), lambda j: (0, j)),
                  pl.BlockSpec((1, tn), lambda j: (0, j))],
        out_specs=pl.BlockSpec((rows, tn), lambda j: (0, j)),
        out_shape=jax.ShapeDtypeStruct((rows, n), F32),
        compiler_params=_cparams(("parallel",)),
        name="mod",
    )(cc, w, b)


def _in_proj_kernel(x_ref, sh_ref, sc_ref, g_ref, wrw_ref, wcv_ref, orw_ref, ocv_ref):
    x = x_ref[0]
    ms = jnp.mean(x * x, axis=-1, keepdims=True)
    h = x * lax.rsqrt(ms + NORM_EPS) * g_ref[...]
    h = (h * (1.0 + sc_ref[0]) + sh_ref[0]).astype(BF16)
    orw_ref[0] = _dot(h, wrw_ref[...])
    ocv_ref[0] = _dot(h, wcv_ref[...])


def _in_proj(x, sh, sc, g, w_rw, w_cv, tm):
    bsz, l, d = x.shape
    return pl.pallas_call(
        _in_proj_kernel,
        grid=(bsz, l // tm),
        in_specs=[pl.BlockSpec((1, tm, d), lambda b, i: (b, i, 0)),
                  pl.BlockSpec((1, 1, d), lambda b, i: (b, 0, 0)),
                  pl.BlockSpec((1, 1, d), lambda b, i: (b, 0, 0)),
                  pl.BlockSpec((1, d), lambda b, i: (0, 0)),
                  pl.BlockSpec((d, RWKV_COLS), lambda b, i: (0, 0)),
                  pl.BlockSpec((d, CONV_COLS), lambda b, i: (0, 0))],
        out_specs=[pl.BlockSpec((1, tm, RWKV_COLS), lambda b, i: (b, i, 0)),
                   pl.BlockSpec((1, tm, CONV_COLS), lambda b, i: (b, i, 0))],
        out_shape=[jax.ShapeDtypeStruct((bsz, l, RWKV_COLS), F32),
                   jax.ShapeDtypeStruct((bsz, l, CONV_COLS), F32)],
        compiler_params=_cparams(("parallel", "parallel")),
        name="in_proj",
    )(x, sh, sc, g, w_rw, w_cv)


def _bf(x):
    return x.astype(BF16)


def _chunk_local(chains):
    c = CHUNK
    n = range(len(chains))
    lane = lax.broadcasted_iota(jnp.int32, (c, PAIR), 1)
    h0 = lane < HEAD
    row = lax.broadcasted_iota(jnp.int32, (PAIR, PAIR), 0)
    col = lax.broadcasted_iota(jnp.int32, (PAIR, PAIR), 1)
    same = (row >= c) == (col >= c)
    tr, tc = row & (c - 1), col & (c - 1)
    eye = jnp.where(row == col, 1.0, 0.0)
    masks = {rev: (same & ((tc > tr) if rev else (tc < tr)), same & ((tc >= tr) if rev else (tc <= tr)))
             for rev in (False, True)}
    zb = jnp.zeros((c, PAIR), BF16)

    def stack(x):
        return jnp.concatenate([jnp.where(h0, x, zb), jnp.where(h0, zb, x)], axis=0)

    a_st = [stack(ch[1]) for ch in chains]
    r_st = [stack(ch[2]) for ch in chains]
    gram = [_dot_nt(jnp.concatenate([a_st[i], r_st[i]], axis=0),
                    jnp.concatenate([chains[i][3], chains[i][3], chains[i][4], chains[i][4]], axis=0)) for i in n]
    lab = [jnp.where(masks[chains[i][0]][0], gram[i][:PAIR, :PAIR], 0.0) for i in n]
    lak = [jnp.where(masks[chains[i][0]][0], gram[i][:PAIR, PAIR:], 0.0) for i in n]
    mrb = [jnp.where(masks[chains[i][0]][1], gram[i][PAIR:, :PAIR], 0.0) for i in n]
    mrk = [jnp.where(masks[chains[i][0]][1], gram[i][PAIR:, PAIR:], 0.0) for i in n]

    labb = [_bf(x) for x in lab]
    pw = [_dot(labb[i], labb[i]) for i in n]
    tp = [eye + lab[i] for i in n]
    for _ in range(4):
        pwb = [_bf(x) for x in pw]
        both = [_dot(pwb[i], jnp.concatenate([pwb[i], _bf(tp[i])], axis=1)) for i in n]
        pw = [x[:, :PAIR] for x in both]
        tp = [tp[i] + both[i][:, PAIR:] for i in n]
    tinv = [tp[i] + _dot(_bf(pw[i]), _bf(tp[i])) for i in n]

    v_rep = [jnp.concatenate([ch[7], ch[7]], axis=0) for ch in chains]
    lakv = [_dot(_bf(lak[i]), v_rep[i]) for i in n]
    x = [_dot(_bf(tinv[i]), jnp.concatenate([a_st[i], _bf(lakv[i])], axis=1)) for i in n]
    abar = [xi[:, :PAIR] for xi in x]
    u0 = [jnp.where(same, xi[:, PAIR:], 0.0) for xi in x]
    au = [jnp.concatenate([abar[i], u0[i]], axis=1) for i in n]
    aub = [_bf(x) for x in au]
    z = [_dot(_bf(mrb[i]), aub[i]) for i in n]
    zv = [_dot(_bf(mrk[i]), v_rep[i]) for i in n]
    rbar = [r_st[i].astype(F32) + z[i][:, :PAIR] for i in n]
    rbar = [_bf(x[:c] + x[c:]) for x in rbar]
    y0 = [z[i][:, PAIR:] + zv[i] for i in n]
    y0 = [jnp.where(h0, x[:c], x[c:]) for x in y0]

    mn = [_dot(_bf(au[i].T), jnp.concatenate([chains[i][5], chains[i][5]], axis=0)) for i in n]
    vk = [_dot(_bf(chains[i][7].astype(F32).T), chains[i][6]) for i in n]
    mx = [_bf(jnp.where(h0, m[:c], m[c:PAIR])) for m in mn]
    nn = [mn[i][PAIR:] + vk[i] for i in n]
    nn = [jnp.where(h0, x[:c], x[c:]) for x in nn]
    return list(zip(rbar, y0, mx, nn))


DIR_COLS = 6 * D_RWKV
SCR_COLS = 2 * DIR_COLS + D_RWKV


def _prep_body(tt, p, shifted, mu_ref, lw_ref, w0_ref, a0_ref, kk_ref, ka_ref, rk_ref, gup_ref,
               ones_ref, tril_ref, triu_ref, rbar_ref, y0_ref, mx_ref, nn_ref, wc_ref, bonus_ref, g_ref, scr_ref):
    pm = p + mu_ref[...] * (shifted - p)
    xr = pm[:, 0:D_RWKV]
    xk = pm[:, D_RWKV:2 * D_RWKV]
    xv = pm[:, 2 * D_RWKV:3 * D_RWKV]
    xwa = pm[:, 3 * D_RWKV:3 * D_RWKV + LORA_W + LORA_A]
    xg = pm[:, 3 * D_RWKV + LORA_W + LORA_A:]

    ones_bd = ones_ref[...]
    kraw = xk * kk_ref[...]
    ss = _seg_dot(kraw * kraw, ones_bd)
    kk = kraw / jnp.maximum(jnp.sqrt(ss), 1e-12)

    lane = lax.broadcasted_iota(jnp.int32, xwa.shape, 1)
    lin = jnp.where(lane < LORA_W, jnp.tanh(xwa), xwa).astype(BF16)
    lo = _dot(lin, lw_ref[...])

    scr_ref[:, 2 * DIR_COLS:] = xv.astype(BF16)
    ksum = None
    for d in range(2):
        z = w0_ref[d:d + 1, :] + lo[:, d * D_RWKV:(d + 1) * D_RWKV]
        ld = -math.exp(-0.5) * jax.nn.sigmoid(z)
        ag = jax.nn.sigmoid(a0_ref[d:d + 1, :] + lo[:, (2 + d) * D_RWKV:(3 + d) * D_RWKV])
        kd = xk * (1.0 + (ag - 1.0) * ka_ref[...])
        bb = kk * ag
        ksum = kd if ksum is None else ksum + kd
        tri = tril_ref[...] if d == 0 else triu_ref[...]
        h3 = _split3(ld)
        cs = _dot(tri, h3[0]) + _dot(tri, h3[1]) + _dot(tri, h3[2])
        e_in = jnp.exp(cs)
        e_ex = jnp.exp(cs - ld)
        e_neg = jnp.exp(-cs)
        bt = bb * e_neg
        kt = kd * e_neg
        base = d * DIR_COLS
        scr_ref[:, base:base + D_RWKV] = (-kk * e_ex).astype(BF16)
        scr_ref[:, base + D_RWKV:base + 2 * D_RWKV] = (xr * e_in).astype(BF16)
        scr_ref[:, base + 2 * D_RWKV:base + 3 * D_RWKV] = bt.astype(BF16)
        scr_ref[:, base + 3 * D_RWKV:base + 4 * D_RWKV] = kt.astype(BF16)
        for c in range(tt // CHUNK):
            rows = slice(c * CHUNK, (c + 1) * CHUNK)
            edge = c * CHUNK + (CHUNK - 1 if d == 0 else 0)
            wrow = e_in[edge:edge + 1, :]
            wc_ref[0, c, :, d * D_RWKV:(d + 1) * D_RWKV] = wrow
            scr_ref[rows, base + 4 * D_RWKV:base + 5 * D_RWKV] = (bt[rows] * wrow).astype(BF16)
            scr_ref[rows, base + 5 * D_RWKV:base + 6 * D_RWKV] = (kt[rows] * wrow).astype(BF16)

    bonus_ref[0] = _seg_dot(xr * ksum * rk_ref[...], ones_bd) * xv
    g_ref[0] = _dot(jax.nn.sigmoid(xg).astype(BF16), gup_ref[...])

    def chunk_body(c, carry):
        r0 = pl.multiple_of(c * CHUNK, CHUNK)
        chains = []
        for d in range(2):
            for pp in range(N_PAIR):
                cols = [d * DIR_COLS + j * D_RWKV + pp * PAIR for j in range(6)] + [2 * DIR_COLS + pp * PAIR]
                chains.append((d == 1,) + tuple(scr_ref[pl.ds(r0, CHUNK), k:k + PAIR] for k in cols))
        for idx, (rbar, y0, mx, nn) in enumerate(_chunk_local(chains)):
            col = (idx // N_PAIR) * D_RWKV + (idx % N_PAIR) * PAIR
            rbar_ref[0, pl.ds(r0, CHUNK), col:col + PAIR] = rbar
            y0_ref[0, pl.ds(r0, CHUNK), col:col + PAIR] = y0
            mx_ref[0, c, :, col:col + PAIR] = mx
            nn_ref[0, c, :, col:col + PAIR] = nn
        return carry

    lax.fori_loop(0, tt // CHUNK, chunk_body, 0)


def _prep_latent_kernel(tt, p_ref, prev_ref, next_ref, *rest):
    ext_ref = rest[-1]
    i = pl.program_id(1)
    n = pl.num_programs(1)
    ext_ref[0:GRID_W] = jnp.where(i > 0, prev_ref[0], 0.0)
    ext_ref[GRID_W:GRID_W + tt] = p_ref[0]
    ext_ref[GRID_W + tt:] = jnp.where(i < n - 1, next_ref[0], 0.0)
    shape = (tt, RWKV_COLS)
    col = lax.broadcasted_iota(jnp.int32, shape, 0) & (GRID_W - 1)
    q = lax.broadcasted_iota(jnp.int32, shape, 1) & 3
    left = jnp.where(col == 0, 0.0, ext_ref[GRID_W - 1:GRID_W - 1 + tt])
    right = jnp.where(col == GRID_W - 1, 0.0, ext_ref[GRID_W + 1:GRID_W + 1 + tt])
    up = ext_ref[0:tt]
    down = ext_ref[2 * GRID_W:2 * GRID_W + tt]
    shifted = jnp.where(q == 0, left, jnp.where(q == 1, right, jnp.where(q == 2, up, down)))
    _prep_body(tt, p_ref[0], shifted, *rest[:-1])


def _prep_ctx_kernel(tt, p_ref, *rest):
    ext_ref = rest[-1]
    ext_ref[0:8] = jnp.zeros((8, RWKV_COLS), F32)
    ext_ref[8:8 + tt] = p_ref[0]
    ext_ref[8 + tt:] = jnp.zeros((8, RWKV_COLS), F32)
    q = lax.broadcasted_iota(jnp.int32, (tt, RWKV_COLS), 1) & 1
    shifted = jnp.where(q == 0, ext_ref[7:7 + tt], ext_ref[9:9 + tt])
    _prep_body(tt, p_ref[0], shifted, *rest[:-1])


def _prep(p_rw, consts, latent, tt):
    bsz, l, _ = p_rw.shape
    nc = l // CHUNK
    cpt = tt // CHUNK
    const_specs = [pl.BlockSpec(c.shape, lambda b, i, nd=c.ndim: (0,) * nd) for c in consts]
    if latent:
        nblk = l // GRID_W
        in_specs = [pl.BlockSpec((1, tt, RWKV_COLS), lambda b, i: (b, i, 0)),
                    pl.BlockSpec((1, GRID_W, RWKV_COLS),
                                 lambda b, i: (b, jnp.maximum(i * (tt // GRID_W) - 1, 0), 0)),
                    pl.BlockSpec((1, GRID_W, RWKV_COLS),
                                 lambda b, i: (b, jnp.minimum((i + 1) * (tt // GRID_W), nblk - 1), 0))]
        args = (p_rw, p_rw, p_rw)
        kern = functools.partial(_prep_latent_kernel, tt)
        ext_rows = tt + 2 * GRID_W
    else:
        assert tt == l
        in_specs = [pl.BlockSpec((1, tt, RWKV_COLS), lambda b, i: (b, i, 0))]
        args = (p_rw,)
        kern = functools.partial(_prep_ctx_kernel, tt)
        ext_rows = tt + 16
    row = lambda b, i: (b, i, 0)
    chunk = lambda b, i: (b, i, 0, 0)
    w2 = 2 * D_RWKV
    return pl.pallas_call(
        kern,
        grid=(bsz, l // tt),
        in_specs=in_specs + const_specs,
        out_specs=[pl.BlockSpec((1, tt, w2), row), pl.BlockSpec((1, tt, w2), row),
                   pl.BlockSpec((1, cpt, CHUNK, w2), chunk), pl.BlockSpec((1, cpt, CHUNK, w2), chunk),
                   pl.BlockSpec((1, cpt, 1, w2), chunk),
                   pl.BlockSpec((1, tt, D_RWKV), row), pl.BlockSpec((1, tt, D_RWKV), row)],
        out_shape=[jax.ShapeDtypeStruct((bsz, l, w2), BF16), jax.ShapeDtypeStruct((bsz, l, w2), F32),
                   jax.ShapeDtypeStruct((bsz, nc, CHUNK, w2), BF16), jax.ShapeDtypeStruct((bsz, nc, CHUNK, w2), F32),
                   jax.ShapeDtypeStruct((bsz, nc, 1, w2), F32),
                   jax.ShapeDtypeStruct((bsz, l, D_RWKV), F32), jax.ShapeDtypeStruct((bsz, l, D_RWKV), F32)],
        scratch_shapes=[pltpu.VMEM((tt, SCR_COLS), BF16), pltpu.VMEM((ext_rows, RWKV_COLS), F32)],
        compiler_params=_cparams(("parallel", "parallel")),
        name="prep_latent" if latent else "prep_ctx",
    )(*args, *consts)


SCAN_CHUNKS = 4


def _scan_kernel(rbf_ref, rbb_ref, y0f_ref, y0b_ref, mxf_ref, mxb_ref, nnf_ref, nnb_ref, wcf_ref, wcb_ref,
                 s0_ref, yf_ref, yb_ref, s_ref):
    i = pl.program_id(1)

    @pl.when(i == 0)
    def _():
        s_ref[...] = s0_ref[...]

    c = CHUNK
    h0 = lax.broadcasted_iota(jnp.int32, (c, PAIR), 1) < HEAD
    row = lax.broadcasted_iota(jnp.int32, (PAIR, PAIR), 0)
    col = lax.broadcasted_iota(jnp.int32, (PAIR, PAIR), 1)
    same = (row >= c) == (col >= c)
    zb = jnp.zeros((c, PAIR), BF16)
    dirs = ((rbf_ref, y0f_ref, mxf_ref, nnf_ref, wcf_ref, yf_ref), (rbb_ref, y0b_ref, mxb_ref, nnb_ref, wcb_ref, yb_ref))
    idx = [(d, p) for d in range(2) for p in range(N_PAIR)]
    lanes = [slice(p * PAIR, (p + 1) * PAIR) for _, p in idx]
    s = [s_ref[0, d, p] for d, p in idx]
    for q in range(SCAN_CHUNKS):
        cq = (q, SCAN_CHUNKS - 1 - q)
        rows = [slice(cq[d] * c, (cq[d] + 1) * c) for d, _ in idx]
        sb = [_bf(x) for x in s]
        rbar = [dirs[d][0][0, rows[k], lanes[k]] for k, (d, _) in enumerate(idx)]
        r_st = [jnp.concatenate([jnp.where(h0, x, zb), jnp.where(h0, zb, x)], axis=0) for x in rbar]
        y_st = [_dot_nt(r_st[k], sb[k]) for k in range(len(idx))]
        m_bd = [jnp.where(same, jnp.concatenate([x, x], axis=0), jnp.zeros((PAIR, PAIR), BF16))
                for x in (dirs[d][2][0, cq[d], :, lanes[k]] for k, (d, _) in enumerate(idx))]
        sm = [_dot(sb[k], m_bd[k]) for k in range(len(idx))]
        s_next = []
        for k, (d, p) in enumerate(idx):
            nn = dirs[d][3][0, cq[d], :, lanes[k]]
            n_bd = jnp.where(same, jnp.concatenate([nn, nn], axis=0), 0.0)
            dirs[d][5][0, rows[k], lanes[k]] = y_st[k][:c] + y_st[k][c:] + dirs[d][1][0, rows[k], lanes[k]]
            s_next.append(s[k] * dirs[d][4][0, cq[d], :, lanes[k]] + sm[k] + n_bd)
        s = s_next
    for k, (d, p) in enumerate(idx):
        s_ref[0, d, p] = s[k]


def _scan(rbar, y0, mx, nn, wc, s0):
    bsz, l, _ = rbar.shape
    sub = SCAN_CHUNKS
    nc = l // (CHUNK * sub)
    row_f = pl.BlockSpec((1, sub * CHUNK, D_RWKV), lambda b, i: (b, i, 0))
    row_b = pl.BlockSpec((1, sub * CHUNK, D_RWKV), lambda b, i: (b, nc - 1 - i, 1))
    chk_f = pl.BlockSpec((1, sub, CHUNK, D_RWKV), lambda b, i: (b, i, 0, 0))
    chk_b = pl.BlockSpec((1, sub, CHUNK, D_RWKV), lambda b, i: (b, nc - 1 - i, 0, 1))
    wc_f = pl.BlockSpec((1, sub, 1, D_RWKV), lambda b, i: (b, i, 0, 0))
    wc_b = pl.BlockSpec((1, sub, 1, D_RWKV), lambda b, i: (b, nc - 1 - i, 0, 1))
    st_spec = pl.BlockSpec((1, 2, N_PAIR, PAIR, PAIR), lambda b, i: (b, 0, 0, 0, 0))
    return pl.pallas_call(
        _scan_kernel,
        grid=(bsz, nc),
        in_specs=[row_f, row_b, row_f, row_b, chk_f, chk_b, chk_f, chk_b, wc_f, wc_b, st_spec],
        out_specs=[row_f, pl.BlockSpec((1, sub * CHUNK, D_RWKV), lambda b, i: (b, nc - 1 - i, 0)), st_spec],
        out_shape=[jax.ShapeDtypeStruct((bsz, l, D_RWKV), F32),
                   jax.ShapeDtypeStruct((bsz, l, D_RWKV), F32),
                   jax.ShapeDtypeStruct((bsz, 2, N_PAIR, PAIR, PAIR), F32)],
        compiler_params=_cparams(("parallel", "arbitrary")),
        name="scan",
    )(rbar, rbar, y0, y0, mx, mx, nn, nn, wc, wc, s0)


OUT_ROWS = 128


def _out_kernel(tt, yf_ref, yb_ref, bonus_ref, g_ref, cv_ref, cvp_ref, cvn_ref, x_ref, g1_ref, sh2_ref,
                sc2_ref, n2g_ref, gng_ref, gnb_ref, convw_ref, wout_ref, rwh_ref, rwl_ref, avg_ref,
                xm_ref, hx_ref, aff_ref):
    i = pl.program_id(1)
    n = pl.num_programs(1)

    cv = cv_ref[0]
    b_gate = cv[:, 0:D_CONV]
    cu = cv[:, D_CONV:2 * D_CONV] * cv[:, 2 * D_CONV:]
    cvp = cvp_ref[0]
    cvn = cvn_ref[0]
    cu_prev = jnp.where(i > 0, cvp[7:8, D_CONV:2 * D_CONV] * cvp[7:8, 2 * D_CONV:], 0.0)
    cu_next = jnp.where(i < n - 1, cvn[0:1, D_CONV:2 * D_CONV] * cvn[0:1, 2 * D_CONV:], 0.0)
    ridx = lax.broadcasted_iota(jnp.int32, cu.shape, 0)
    cu_m1 = jnp.where(ridx == 0, cu_prev, pltpu.roll(cu, 1, 0))
    cu_p1 = jnp.where(ridx == tt - 1, cu_next, pltpu.roll(cu, tt - 1, 0))
    conv = convw_ref[0:1, :] * cu_m1 + convw_ref[1:2, :] * cu + convw_ref[2:3, :] * cu_p1
    bx = (b_gate * conv).astype(BF16)

    parts = [slice(k * OUT_ROWS, (k + 1) * OUT_ROWS) for k in range(tt // OUT_ROWS)]
    avg = avg_ref[...]
    rwh = rwh_ref[...]
    rwl = rwl_ref[...]
    y = [yf_ref[0, r, :] + yb_ref[0, r, :] for r in parts]
    mu = [_seg_dot(v, avg) for v in y]
    dlt = [a - b for a, b in zip(y, mu)]
    var = [_seg_dot(v * v, avg) for v in dlt]
    yn = [a * lax.rsqrt(b + GN_EPS) * gng_ref[...] + gnb_ref[...] for a, b in zip(dlt, var)]
    ax = [((a + bonus_ref[0, r, :]) * g_ref[0, r, :]).astype(BF16) for a, r in zip(yn, parts)]
    mix = [_dot(a, wout_ref[0:D_RWKV, :]) + _dot(bx[r], wout_ref[D_RWKV:, :]) for a, r in zip(ax, parts)]
    xm = [x_ref[0, r, :] + g1_ref[0] * a for a, r in zip(mix, parts)]
    ms = [jnp.mean(v * v, axis=-1, keepdims=True) for v in xm]
    hx = [a * lax.rsqrt(b + NORM_EPS) * n2g_ref[...] for a, b in zip(xm, ms)]
    hx = [_split2(v * (1.0 + sc2_ref[0]) + sh2_ref[0]) for v in hx]
    logits = [_dot_nt(rwh, hi) + _dot_nt(rwh, lo) + _dot_nt(rwl, hi) for hi, lo in hx]
    for k, r in enumerate(parts):
        xm_ref[0, r, :] = xm[k]
        hx_ref[0, r, :] = hx[k][0]
        m = jnp.max(logits[k], axis=0, keepdims=True)
        ex = jnp.exp(logits[k] - m)
        aff_ref[0, :, r] = ex / jnp.sum(ex, axis=0, keepdims=True)


def _out(yf, yb, bonus, g, p_cv, x, g1, sh2, sc2, n2g, gng, gnb, convw, wout, rwh, rwl, avg, tt):
    bsz, t, d = x.shape
    ne = rwh.shape[0]
    nb8 = t // 8
    row = lambda b, i: (b, i, 0)
    per_b = lambda b, i: (b, 0, 0)
    const2 = lambda b, i: (0, 0)
    return pl.pallas_call(
        functools.partial(_out_kernel, tt),
        grid=(bsz, t // tt),
        in_specs=[pl.BlockSpec((1, tt, D_RWKV), row), pl.BlockSpec((1, tt, D_RWKV), row),
                  pl.BlockSpec((1, tt, D_RWKV), row), pl.BlockSpec((1, tt, D_RWKV), row),
                  pl.BlockSpec((1, tt, CONV_COLS), row),
                  pl.BlockSpec((1, 8, CONV_COLS), lambda b, i: (b, jnp.maximum(i * (tt // 8) - 1, 0), 0)),
                  pl.BlockSpec((1, 8, CONV_COLS), lambda b, i: (b, jnp.minimum((i + 1) * (tt // 8), nb8 - 1), 0)),
                  pl.BlockSpec((1, tt, d), row),
                  pl.BlockSpec((1, 1, d), per_b), pl.BlockSpec((1, 1, d), per_b), pl.BlockSpec((1, 1, d), per_b),
                  pl.BlockSpec((1, d), const2), pl.BlockSpec((1, D_RWKV), const2), pl.BlockSpec((1, D_RWKV), const2),
                  pl.BlockSpec((3, D_CONV), const2), pl.BlockSpec((D_RWKV + D_CONV, d), const2),
                  pl.BlockSpec((ne, d), const2), pl.BlockSpec((ne, d), const2),
                  pl.BlockSpec((D_RWKV, D_RWKV), const2)],
        out_specs=[pl.BlockSpec((1, tt, d), row), pl.BlockSpec((1, tt, d), row),
                   pl.BlockSpec((1, ne, tt), lambda b, i: (b, 0, i))],
        out_shape=[jax.ShapeDtypeStruct((bsz, t, d), F32), jax.ShapeDtypeStruct((bsz, t, d), BF16),
                   jax.ShapeDtypeStruct((bsz, ne, t), F32)],
        compiler_params=_cparams(("parallel", "parallel")),
        name="out",
    )(yf, yb, bonus, g, p_cv, p_cv, p_cv, x, g1, sh2, sc2, n2g, gng, gnb, convw, wout, rwh, rwl, avg)


def _prefix_blocks(mask_fn, t, tri, emit):
    carry = None
    for j in range(t // 128):
        m = mask_fn(j)
        inc = _dot(m.astype(BF16), tri)
        carry = jnp.zeros_like(inc[:, 0:1]) if carry is None else carry
        emit(j, m, inc - m + carry)
        carry = carry + inc[:, 127:128]


def _topk_kernel(cap, aff_ref, tri_ref, cnt_ref, slot_ref):
    t = aff_ref.shape[2]
    aff = aff_ref[0]

    def body(k, bits):
        cand = bits | jnp.left_shift(jnp.int32(1), 30 - k)
        cnt = jnp.sum(jnp.where(aff >= pltpu.bitcast(cand, F32), 1, 0), axis=-1, keepdims=True)
        return jnp.where(cnt >= cap, cand, bits)

    bits = lax.fori_loop(0, 31, body, jnp.zeros((aff.shape[0], 1), jnp.int32))
    thr = pltpu.bitcast(bits, F32)
    above = pltpu.bitcast(bits + 1, F32)
    n_gt = jnp.sum(jnp.where(aff >= above, 1, 0), axis=-1, keepdims=True)
    need = (cap - n_gt).astype(F32)
    tri = tri_ref[...]

    def blk(j):
        return aff[:, j * 128:(j + 1) * 128]

    def emit_sel(j, eq, before):
        take = (blk(j) >= above) | ((eq > 0.5) & (before < need))
        slot_ref[0, :, j * 128:(j + 1) * 128] = jnp.where(take, 1, 0)

    _prefix_blocks(lambda j: jnp.where((blk(j) >= thr) & (blk(j) < above), 1.0, 0.0), t, tri, emit_sel)

    def emit_slot(j, m, before):
        count = before.astype(jnp.int32)
        cnt_ref[0, :, j * 128:(j + 1) * 128] = count
        slot_ref[0, :, j * 128:(j + 1) * 128] = jnp.where(m > 0.5, count, -1)

    _prefix_blocks(lambda j: slot_ref[0, :, j * 128:(j + 1) * 128].astype(F32), t, tri, emit_slot)


def _topk(aff_t, tri, cap):
    bsz, ne, t = aff_t.shape
    spec = pl.BlockSpec((1, ne, t), lambda b: (b, 0, 0))
    return pl.pallas_call(
        functools.partial(_topk_kernel, cap),
        grid=(bsz,),
        in_specs=[spec, pl.BlockSpec((128, 128), lambda b: (0, 0))],
        out_specs=[spec, spec],
        out_shape=[jax.ShapeDtypeStruct((bsz, ne, t), jnp.int32)] * 2,
        compiler_params=_cparams(("parallel",)),
        name="topk",
    )(aff_t, tri)


def _slot_block_range(lo, hi, sb):
    shift = sb.bit_length() - 1
    assert sb == 1 << shift
    first = lo >> shift
    return first, jnp.where(hi > lo, ((hi - 1) >> shift) + 1, first)


def _moe_kernel(tk, sb, tsp_ref, hx_ref, slot_ref, wg_ref, wu_ref, wd_ref, ye_ref, xs_ref):
    b = pl.program_id(0)
    e = pl.program_id(1)
    ne = pl.num_programs(1)
    nt = hx_ref.shape[1] // tk
    base = (b * ne + e) * (nt + 1)
    xs_ref[...] = jnp.zeros_like(xs_ref)

    def tile_body(j, carry):
        t0 = pl.multiple_of(j * tk, tk)
        hxt = hx_ref[0, pl.ds(t0, tk), :]
        slot_t = slot_ref[0, 0, pl.ds(j, 1), :]

        def sb_body(s, c2):
            s0 = pl.multiple_of(s * sb, sb)
            slot = lax.broadcasted_iota(jnp.int32, (sb, tk), 0) + s0
            onehot = jnp.where(slot_t == slot, 1.0, 0.0).astype(BF16)
            xs_ref[pl.ds(s0, sb), :] += _dot(onehot, hxt)
            return c2

        first, last = _slot_block_range(tsp_ref[base + j], tsp_ref[base + j + 1], sb)
        lax.fori_loop(first, last, sb_body, 0)
        return carry

    lax.fori_loop(0, nt, tile_body, 0)
    xs = xs_ref[...].astype(BF16)
    h1 = _dot(xs, wg_ref[0])
    h2 = _dot(xs, wu_ref[0])
    hid = (h1 * jax.nn.sigmoid(h1) * h2).astype(BF16)
    ye_ref[0, 0] = _dot(hid, wd_ref[0]).astype(BF16)


def _moe(tsp, hx, slot4, wg, wu, wd, cap, tk, sb):
    bsz, t, d = hx.shape
    ne, _, f = wg.shape
    nt = t // tk
    grid_spec = pltpu.PrefetchScalarGridSpec(
        num_scalar_prefetch=1,
        grid=(bsz, ne),
        in_specs=[pl.BlockSpec((1, t, d), lambda b, e, s: (b, 0, 0)),
                  pl.BlockSpec((1, 1, nt, tk), lambda b, e, s: (b, e, 0, 0)),
                  pl.BlockSpec((1, d, f), lambda b, e, s: (e, 0, 0)),
                  pl.BlockSpec((1, d, f), lambda b, e, s: (e, 0, 0)),
                  pl.BlockSpec((1, f, d), lambda b, e, s: (e, 0, 0))],
        out_specs=pl.BlockSpec((1, 1, cap, d), lambda b, e, s: (b, e, 0, 0)),
        scratch_shapes=[pltpu.VMEM((cap, d), F32)],
    )
    return pl.pallas_call(
        functools.partial(_moe_kernel, tk, sb),
        grid_spec=grid_spec,
        out_shape=jax.ShapeDtypeStruct((bsz, ne, cap, d), BF16),
        compiler_params=_cparams(("parallel", "arbitrary")),
        name="moe",
    )(tsp, hx, slot4, wg, wu, wd)


COMB_ROWS = 128
COMB_COLS = 256
SLOT_ALIGN = 16


def _comb_kernel(tk, win, tsp_ref, xm_ref, ye_ref, slotc_ref, affc_ref, g2_ref, fg_ref, o_ref):
    b = pl.program_id(0)
    j = pl.program_id(1)
    nh = tk // COMB_ROWS
    ntile = pl.num_programs(1) * nh
    ne, cap, d = ye_ref.shape[1], ye_ref.shape[2], ye_ref.shape[3]
    slotc = slotc_ref[0]
    affc = affc_ref[0]
    lane_slot = lax.broadcasted_iota(jnp.int32, (COMB_ROWS, win), 1)
    for h in range(nh):
        rows = slice(h * COMB_ROWS, (h + 1) * COMB_ROWS)
        starts, onehots, vals = [], [], []
        for e in range(ne):
            first = tsp_ref[(b * ne + e) * ntile + j * nh + h]
            start = pl.multiple_of(jnp.minimum(first & ~(SLOT_ALIGN - 1), cap - win), SLOT_ALIGN)
            starts.append(start)
            onehots.append(jnp.where(slotc[rows, e:e + 1] - start == lane_slot, 1.0, 0.0).astype(BF16))
            vals.append(jnp.broadcast_to(affc[rows, e:e + 1], (COMB_ROWS, COMB_COLS)))
        xo, ssq = [], None
        for q in range(d // COMB_COLS):
            cols = slice(q * COMB_COLS, (q + 1) * COMB_COLS)
            acc = None
            for e in range(ne):
                part = vals[e] * _dot(onehots[e], ye_ref[0, e, pl.ds(starts[e], win), cols])
                acc = part if acc is None else acc + part
            x = xm_ref[0, rows, cols] + g2_ref[0, :, cols] * acc
            xo.append(x)
            sq = jnp.sum(x * x, axis=-1, keepdims=True)
            ssq = sq if ssq is None else ssq + sq
        scale = lax.rsqrt(ssq / d + NORM_EPS)
        for q in range(d // COMB_COLS):
            cols = slice(q * COMB_COLS, (q + 1) * COMB_COLS)
            o_ref[0, rows, cols] = xo[q] * scale * fg_ref[:, cols]


def _comb(tsp, xm, ye, slotc, affc, g2, fg, tk):
    bsz, t, d = xm.shape
    ne, cap = ye.shape[1], ye.shape[2]
    win = min(2 * COMB_ROWS, cap)
    assert win == cap or win >= COMB_ROWS + SLOT_ALIGN
    grid_spec = pltpu.PrefetchScalarGridSpec(
        num_scalar_prefetch=1,
        grid=(bsz, t // tk),
        in_specs=[pl.BlockSpec((1, tk, d), lambda b, j, s: (b, j, 0)),
                  pl.BlockSpec((1, ne, cap, d), lambda b, j, s: (b, 0, 0, 0)),
                  pl.BlockSpec((1, tk, ne), lambda b, j, s: (b, j, 0)),
                  pl.BlockSpec((1, tk, ne), lambda b, j, s: (b, j, 0)),
                  pl.BlockSpec((1, 1, d), lambda b, j, s: (b, 0, 0)),
                  pl.BlockSpec((1, d), lambda b, j, s: (0, 0))],
        out_specs=pl.BlockSpec((1, tk, d), lambda b, j, s: (b, j, 0)),
    )
    return pl.pallas_call(
        functools.partial(_comb_kernel, tk, win),
        grid_spec=grid_spec,
        out_shape=jax.ShapeDtypeStruct((bsz, t, d), F32),
        compiler_params=_cparams(("parallel", "arbitrary")),
        name="comb",
    )(tsp, xm, ye, slotc, affc, g2, fg)


def _block_diag_ones(n, blk, value=1.0):
    r = jnp.arange(n)
    return jnp.where((r[:, None] // blk) == (r[None, :] // blk), value, 0.0)


def kernel(x, c, ctx, c_ctx, ada_w, ada_b, norm1_g, norm2_g, w_in, shift_mu, w0, w_lora_up, a0, a_lora_up, k_k, k_a,
           r_k, g_lora_up, gn_g, gn_b, conv_w, w_out, router_w, exp_w_gate, exp_w_up, exp_w_down, final_g):
    bsz, t, d = x.shape
    lc = ctx.shape[1]
    ne = router_w.shape[-1]
    cap = EC_CAPACITY * t // ne
    tt = 256
    tk = 512
    sb = min(128, cap)
    l = 0

    rows = ((bsz + 1 + 7) // 8) * 8
    cc = jnp.zeros((rows, d), F32).at[:bsz].set(c).at[bsz].set(c_ctx)
    mod = _mod(cc, ada_w[l], ada_b[l][None, :])
    sh1, sc1, g1, sh2, sc2, g2 = (m[:, None, :] for m in jnp.split(mod[:bsz], 6, axis=-1))
    csh1, csc1 = (jnp.broadcast_to(m[None, None, :], (bsz, 1, d)) for m in jnp.split(mod[bsz], 6)[:2])

    w_rw = w_in[l][:, :RWKV_COLS].astype(BF16)
    w_cv = w_in[l][:, RWKV_COLS:].astype(BF16)
    n1g = norm1_g[l][None, :]
    px_rw, px_cv = _in_proj(x, sh1, sc1, n1g, w_rw, w_cv, 256)
    pc_rw, _ = _in_proj(ctx, csh1, csc1, n1g, w_rw, w_cv, 256)

    zw = jnp.zeros((LORA_W, 2 * D_RWKV), F32)
    lora = jnp.concatenate([
        jnp.concatenate([w_lora_up[l, 0], w_lora_up[l, 1], zw], axis=1),
        jnp.concatenate([zw, a_lora_up[l, 0], a_lora_up[l, 1]], axis=1)], axis=0).astype(BF16)
    ridx = jnp.arange(tt)
    same_chunk = (ridx[:, None] // CHUNK) == (ridx[None, :] // CHUNK)
    tril = jnp.where(same_chunk & (ridx[None, :] <= ridx[:, None]), 1.0, 0.0).astype(BF16)
    triu = jnp.where(same_chunk & (ridx[None, :] >= ridx[:, None]), 1.0, 0.0).astype(BF16)
    ones_bd = _block_diag_ones(D_RWKV, HEAD).astype(BF16)
    consts = (shift_mu[l][None, :], lora, w0[l], a0[l], k_k[l][None, :], k_a[l][None, :],
              r_k[l].reshape(1, D_RWKV), g_lora_up[l].astype(BF16), ones_bd, tril, triu)

    chunks_c = _prep(pc_rw, consts, False, lc)[:5]
    *chunks_x, bonus, gate = _prep(px_rw, consts, True, tt)

    s_zero = jnp.zeros((bsz, 2, N_PAIR, PAIR, PAIR), F32)
    _, _, s_ctx = _scan(*chunks_c, s_zero)
    yf, yb, _ = _scan(*chunks_x, s_ctx)

    rw_t = router_w[l].T
    rwh = rw_t.astype(BF16)
    rwl = (rw_t - rwh.astype(F32)).astype(BF16)
    avg = _block_diag_ones(D_RWKV, HEAD, 1.0 / HEAD).astype(BF16)
    xm, hx, aff_t = _out(yf, yb, bonus, gate, px_cv, x, g1, sh2, sc2, norm2_g[l][None, :], gn_g[l][None, :],
                         gn_b[l][None, :], conv_w[l], w_out[l].astype(BF16), rwh, rwl, avg, tt)

    r128 = jnp.arange(128)
    tri128 = jnp.where(r128[:, None] <= r128[None, :], 1.0, 0.0).astype(BF16)
    cnt, slot = _topk(aff_t, tri128, cap)

    nt = t // tk
    tsp = jnp.concatenate([cnt[:, :, ::tk], jnp.full((bsz, ne, 1), cap, jnp.int32)], axis=-1).reshape(-1)
    ye = _moe(tsp, hx, slot.reshape(bsz, ne, nt, tk),
              exp_w_gate[l].astype(BF16), exp_w_up[l].astype(BF16), exp_w_down[l].astype(BF16), cap, tk, sb)
    tr = lambda a: jnp.transpose(a, (0, 2, 1))
    first_slot = cnt[:, :, ::COMB_ROWS].reshape(-1)
    return _comb(first_slot, xm, ye, tr(slot), tr(aff_t), g2, final_g[None, :], tt)
```

```python
import functools
import math

import jax
import jax.numpy as jnp
from jax import lax
from jax.experimental import pallas as pl
from jax.experimental.pallas import tpu as pltpu

F32 = jnp.float32
BF16 = jnp.bfloat16
HIGHEST = lax.Precision.HIGHEST

GRID_W = 64
D_RWKV = 512
D_CONV = 512
HEAD = 64
LORA_W = 64
LORA_A = 64
LORA_G = 128
N_EXPERTS = 16
EC_CAPACITY = 2
NORM_EPS = 1e-6
GN_EPS = 64e-5
RWKV_COLS = 3 * D_RWKV + LORA_W + LORA_A + LORA_G
CONV_COLS = 3 * D_CONV

CHUNK = 64
PAIR = 2 * HEAD
N_PAIR = D_RWKV // PAIR
VMEM_LIMIT = 48 * 1024 * 1024


def _cparams(sem):
    return pltpu.CompilerParams(dimension_semantics=sem, vmem_limit_bytes=VMEM_LIMIT)


def _dot(a, b):
    return jnp.dot(a, b, preferred_element_type=F32)


def _dot_nt(a, b):
    return lax.dot_general(a, b, (((1,), (1,)), ((), ())), preferred_element_type=F32)


def _split2(x):
    hi = x.astype(BF16)
    lo = (x - hi.astype(F32)).astype(BF16)
    return hi, lo


def _split3(x):
    hi = x.astype(BF16)
    r = x - hi.astype(F32)
    mid = r.astype(BF16)
    lo = (r - mid.astype(F32)).astype(BF16)
    return hi, mid, lo


def _seg_dot(x, m):
    hi, lo = _split2(x)
    return _dot(hi, m) + _dot(lo, m)


def _mod_kernel(c_ref, w_ref, b_ref, o_ref):
    c = c_ref[...]
    s = c * jax.nn.sigmoid(c)
    o_ref[...] = jnp.dot(s, w_ref[...], precision=HIGHEST, preferred_element_type=F32) + b_ref[...]


def _mod(cc, w, b):
    rows, d = cc.shape
    n = w.shape[1]
    tn = 1024
    return pl.pallas_call(
        _mod_kernel,
        grid=(n // tn,),
        in_specs=[pl.BlockSpec((rows, d), lambda j: (0, 0)),
                  pl.BlockSpec((d, tn), lambda j: (0, j)),
                  pl.BlockSpec((1, tn), lambda j: (0, j))],
        out_specs=pl.BlockSpec((rows, tn), lambda j: (0, j)),
        out_shape=jax.ShapeDtypeStruct((rows, n), F32),
        compiler_params=_cparams(("parallel",)),
        name="mod",
    )(cc, w, b)


def _in_proj_kernel(x_ref, sh_ref, sc_ref, g_ref, wrw_ref, wcv_ref, orw_ref, ocv_ref):
    x = x_ref[0]
    ms = jnp.mean(x * x, axis=-1, keepdims=True)
    h = x * lax.rsqrt(ms + NORM_EPS) * g_ref[...]
    h = (h * (1.0 + sc_ref[0]) + sh_ref[0]).astype(BF16)
    orw_ref[0] = _dot(h, wrw_ref[...])
    ocv_ref[0] = _dot(h, wcv_ref[...])


def _in_proj(x, sh, sc, g, w_rw, w_cv, tm):
    bsz, l, d = x.shape
    return pl.pallas_call(
        _in_proj_kernel,
        grid=(bsz, l // tm),
        in_specs=[pl.BlockSpec((1, tm, d), lambda b, i: (b, i, 0)),
                  pl.BlockSpec((1, 1, d), lambda b, i: (b, 0, 0)),
                  pl.BlockSpec((1, 1, d), lambda b, i: (b, 0, 0)),
                  pl.BlockSpec((1, d), lambda b, i: (0, 0)),
                  pl.BlockSpec((d, RWKV_COLS), lambda b, i: (0, 0)),
                  pl.BlockSpec((d, CONV_COLS), lambda b, i: (0, 0))],
        out_specs=[pl.BlockSpec((1, tm, RWKV_COLS), lambda b, i: (b, i, 0)),
                   pl.BlockSpec((1, tm, CONV_COLS), lambda b, i: (b, i, 0))],
        out_shape=[jax.ShapeDtypeStruct((bsz, l, RWKV_COLS), F32),
                   jax.ShapeDtypeStruct((bsz, l, CONV_COLS), F32)],
        compiler_params=_cparams(("parallel", "parallel")),
        name="in_proj",
    )(x, sh, sc, g, w_rw, w_cv)


def _bf(x):
    return x.astype(BF16)


def _chunk_local(chains):
    c = CHUNK
    n = range(len(chains))
    lane = lax.broadcasted_iota(jnp.int32, (c, PAIR), 1)
    h0 = lane < HEAD
    row = lax.broadcasted_iota(jnp.int32, (PAIR, PAIR), 0)
    col = lax.broadcasted_iota(jnp.int32, (PAIR, PAIR), 1)
    same = (row >= c) == (col >= c)
    tr, tc = row & (c - 1), col & (c - 1)
    eye = jnp.where(row == col, 1.0, 0.0)
    masks = {rev: (same & ((tc > tr) if rev else (tc < tr)), same & ((tc >= tr) if rev else (tc <= tr)))
             for rev in (False, True)}
    zb = jnp.zeros((c, PAIR), BF16)

    def stack(x):
        return jnp.concatenate([jnp.where(h0, x, zb), jnp.where(h0, zb, x)], axis=0)

    a_st = [stack(ch[1]) for ch in chains]
    r_st = [stack(ch[2]) for ch in chains]
    gram = [_dot_nt(jnp.concatenate([a_st[i], r_st[i]], axis=0),
                    jnp.concatenate([chains[i][3], chains[i][3], chains[i][4], chains[i][4]], axis=0)) for i in n]
    lab = [jnp.where(masks[chains[i][0]][0], gram[i][:PAIR, :PAIR], 0.0) for i in n]
    lak = [jnp.where(masks[chains[i][0]][0], gram[i][:PAIR, PAIR:], 0.0) for i in n]
    mrb = [jnp.where(masks[chains[i][0]][1], gram[i][PAIR:, :PAIR], 0.0) for i in n]
    mrk = [jnp.where(masks[chains[i][0]][1], gram[i][PAIR:, PAIR:], 0.0) for i in n]

    labb = [_bf(x) for x in lab]
    pw = [_dot(labb[i], labb[i]) for i in n]
    tp = [eye + lab[i] for i in n]
    for _ in range(4):
        pwb = [_bf(x) for x in pw]
        both = [_dot(pwb[i], jnp.concatenate([pwb[i], _bf(tp[i])], axis=1)) for i in n]
        pw = [x[:, :PAIR] for x in both]
        tp = [tp[i] + both[i][:, PAIR:] for i in n]
    tinv = [tp[i] + _dot(_bf(pw[i]), _bf(tp[i])) for i in n]

    v_rep = [jnp.concatenate([ch[7], ch[7]], axis=0) for ch in chains]
    lakv = [_dot(_bf(lak[i]), v_rep[i]) for i in n]
    x = [_dot(_bf(tinv[i]), jnp.concatenate([a_st[i], _bf(lakv[i])], axis=1)) for i in n]
    abar = [xi[:, :PAIR] for xi in x]
    u0 = [jnp.where(same, xi[:, PAIR:], 0.0) for xi in x]
    au = [jnp.concatenate([abar[i], u0[i]], axis=1) for i in n]
    aub = [_bf(x) for x in au]
    z = [_dot(_bf(mrb[i]), aub[i]) for i in n]
    zv = [_dot(_bf(mrk[i]), v_rep[i]) for i in n]
    rbar = [r_st[i].astype(F32) + z[i][:, :PAIR] for i in n]
    rbar = [_bf(x[:c] + x[c:]) for x in rbar]
    y0 = [z[i][:, PAIR:] + zv[i] for i in n]
    y0 = [jnp.where(h0, x[:c], x[c:]) for x in y0]

    mn = [_dot(_bf(au[i].T), jnp.concatenate([chains[i][5], chains[i][5]], axis=0)) for i in n]
    vk = [_dot(_bf(chains[i][7].astype(F32).T), chains[i][6]) for i in n]
    mx = [_bf(jnp.where(h0, m[:c], m[c:PAIR])) for m in mn]
    nn = [mn[i][PAIR:] + vk[i] for i in n]
    nn = [jnp.where(h0, x[:c], x[c:]) for x in nn]
    return list(zip(rbar, y0, mx, nn))


DIR_COLS = 6 * D_RWKV
SCR_COLS = 2 * DIR_COLS + D_RWKV


def _head_sums(x, ones):
    return jnp.concatenate([_seg_dot(x[:, g * PAIR:(g + 1) * PAIR], ones) for g in range(N_PAIR)], axis=1)


def _prep_rows(c, p, shifted, lo, mu_ref, w0_ref, a0_ref, kk_ref, ka_ref, rk_ref, ones_ref, tril_ref, triu_ref,
               wc_ref, bonus_ref, scr_ref):
    rows = slice(c * CHUNK, (c + 1) * CHUNK)
    pm = p + mu_ref[:, 0:3 * D_RWKV] * (shifted - p)
    xr = pm[:, 0:D_RWKV]
    xk = pm[:, D_RWKV:2 * D_RWKV]
    xv = pm[:, 2 * D_RWKV:3 * D_RWKV]

    ones = ones_ref[...]
    kraw = xk * kk_ref[...]
    kk = kraw / jnp.maximum(jnp.sqrt(_head_sums(kraw * kraw, ones)), 1e-12)

    scr_ref[:, 2 * DIR_COLS:] = xv.astype(BF16)
    ksum = None
    for d in range(2):
        z = w0_ref[d:d + 1, :] + lo[:, d * D_RWKV:(d + 1) * D_RWKV]
        ld = -math.exp(-0.5) * jax.nn.sigmoid(z)
        ag = jax.nn.sigmoid(a0_ref[d:d + 1, :] + lo[:, (2 + d) * D_RWKV:(3 + d) * D_RWKV])
        kd = xk * (1.0 + (ag - 1.0) * ka_ref[...])
        bb = kk * ag
        ksum = kd if ksum is None else ksum + kd
        tri = tril_ref[...] if d == 0 else triu_ref[...]
        h3 = _split3(ld)
        cs = _dot(tri, h3[0]) + _dot(tri, h3[1]) + _dot(tri, h3[2])
        e_in = jnp.exp(cs)
        e_ex = jnp.exp(cs - ld)
        e_neg = jnp.exp(-cs)
        bt = bb * e_neg
        kt = kd * e_neg
        edge = CHUNK - 1 if d == 0 else 0
        wrow = e_in[edge:edge + 1, :]
        wc_ref[0, c, :, d * D_RWKV:(d + 1) * D_RWKV] = wrow
        base = d * DIR_COLS
        for j, val in enumerate((-kk * e_ex, xr * e_in, bt, kt, bt * wrow, kt * wrow)):
            scr_ref[:, base + j * D_RWKV:base + (j + 1) * D_RWKV] = val.astype(BF16)

    bonus_ref[0, rows, :] = _head_sums(xr * ksum * rk_ref[...], ones) * xv


def _prep_chunk(c, scr_ref, rbar_ref, y0_ref, mx_ref, nn_ref):
    rows = slice(c * CHUNK, (c + 1) * CHUNK)
    chains = []
    for d in range(2):
        for pp in range(N_PAIR):
            cols = [d * DIR_COLS + j * D_RWKV + pp * PAIR for j in range(6)] + [2 * DIR_COLS + pp * PAIR]
            chains.append((d == 1,) + tuple(scr_ref[:, k:k + PAIR] for k in cols))
    for idx, (rbar, y0, mx, nn) in enumerate(_chunk_local(chains)):
        col = (idx // N_PAIR) * D_RWKV + (idx % N_PAIR) * PAIR
        rbar_ref[0, rows, col:col + PAIR] = rbar
        y0_ref[0, rows, col:col + PAIR] = y0
        mx_ref[0, c, :, col:col + PAIR] = mx
        nn_ref[0, c, :, col:col + PAIR] = nn


N_PREP_CONSTS = 11


def _prep_kernel(latent, tt, p_ref, *rest):
    if latent:
        prev_ref, next_ref = rest[:2]
        rest = rest[2:]
    (mu_ref, lw_ref, w0_ref, a0_ref, kk_ref, ka_ref, rk_ref, gup_ref, ones_ref, tril_ref,
     triu_ref) = rest[:N_PREP_CONSTS]
    rbar_ref, y0_ref, mx_ref, nn_ref, wc_ref, bonus_ref, g_ref = rest[N_PREP_CONSTS:N_PREP_CONSTS + 7]
    scr_a, scr_b, lo_ref, ext_ref = rest[N_PREP_CONSTS + 7:]
    if latent:
        i = pl.program_id(1)
        n = pl.num_programs(1)
        ext_ref[0:GRID_W] = jnp.where(i > 0, prev_ref[0], 0.0)
        ext_ref[GRID_W:GRID_W + tt] = p_ref[0]
        ext_ref[GRID_W + tt:] = jnp.where(i < n - 1, next_ref[0], 0.0)

        def shifted(r0, nrows, cols):
            shape = (nrows, cols.stop - cols.start)
            trow = lax.broadcasted_iota(jnp.int32, shape, 0) & (GRID_W - 1)
            q = lax.broadcasted_iota(jnp.int32, shape, 1) & 3
            left = jnp.where(trow == 0, 0.0, ext_ref[GRID_W - 1 + r0:GRID_W - 1 + r0 + nrows, cols])
            right = jnp.where(trow == GRID_W - 1, 0.0, ext_ref[GRID_W + 1 + r0:GRID_W + 1 + r0 + nrows, cols])
            up = ext_ref[r0:r0 + nrows, cols]
            down = ext_ref[2 * GRID_W + r0:2 * GRID_W + r0 + nrows, cols]
            return jnp.where(q == 0, left, jnp.where(q == 1, right, jnp.where(q == 2, up, down)))
    else:
        ext_ref[0:8] = jnp.zeros((8, RWKV_COLS), F32)
        ext_ref[8:8 + tt] = p_ref[0]
        ext_ref[8 + tt:] = jnp.zeros((8, RWKV_COLS), F32)

        def shifted(r0, nrows, cols):
            q = lax.broadcasted_iota(jnp.int32, (nrows, cols.stop - cols.start), 1) & 1
            return jnp.where(q == 0, ext_ref[7 + r0:7 + r0 + nrows, cols], ext_ref[9 + r0:9 + r0 + nrows, cols])

    tail = slice(3 * D_RWKV, RWKV_COLS)
    p_t = p_ref[0, :, tail]
    pm_t = p_t + mu_ref[:, tail] * (shifted(0, tt, tail) - p_t)
    xwa = pm_t[:, 0:LORA_W + LORA_A]
    lane = lax.broadcasted_iota(jnp.int32, xwa.shape, 1)
    lin = jnp.where(lane < LORA_W, jnp.tanh(xwa), xwa).astype(BF16)
    lo_ref[...] = _dot(lin, lw_ref[...])
    g_ref[0] = _dot(jax.nn.sigmoid(pm_t[:, LORA_W + LORA_A:]).astype(BF16), gup_ref[...])

    scr = (scr_a, scr_b)
    head = slice(0, 3 * D_RWKV)

    def rows_part(c):
        rows = slice(c * CHUNK, (c + 1) * CHUNK)
        _prep_rows(c, p_ref[0, rows, head], shifted(c * CHUNK, CHUNK, head), lo_ref[rows, :], mu_ref, w0_ref,
                   a0_ref, kk_ref, ka_ref, rk_ref, ones_ref, tril_ref, triu_ref, wc_ref, bonus_ref, scr[c % 2])

    nchunk = tt // CHUNK
    rows_part(0)
    for c in range(nchunk):
        if c + 1 < nchunk:
            rows_part(c + 1)
        _prep_chunk(c, scr[c % 2], rbar_ref, y0_ref, mx_ref, nn_ref)


def _prep(p_rw, consts, latent, tt):
    bsz, l, _ = p_rw.shape
    nc = l // CHUNK
    cpt = tt // CHUNK
    assert len(consts) == N_PREP_CONSTS
    const_specs = [pl.BlockSpec(c.shape, lambda b, i, nd=c.ndim: (0,) * nd) for c in consts]
    kern = functools.partial(_prep_kernel, latent, tt)
    if latent:
        nblk = l // GRID_W
        in_specs = [pl.BlockSpec((1, tt, RWKV_COLS), lambda b, i: (b, i, 0)),
                    pl.BlockSpec((1, GRID_W, RWKV_COLS),
                                 lambda b, i: (b, jnp.maximum(i * (tt // GRID_W) - 1, 0), 0)),
                    pl.BlockSpec((1, GRID_W, RWKV_COLS),
                                 lambda b, i: (b, jnp.minimum((i + 1) * (tt // GRID_W), nblk - 1), 0))]
        args = (p_rw, p_rw, p_rw)
        ext_rows = tt + 2 * GRID_W
    else:
        assert tt == l
        in_specs = [pl.BlockSpec((1, tt, RWKV_COLS), lambda b, i: (b, i, 0))]
        args = (p_rw,)
        ext_rows = tt + 16
    row = lambda b, i: (b, i, 0)
    chunk = lambda b, i: (b, i, 0, 0)
    w2 = 2 * D_RWKV
    return pl.pallas_call(
        kern,
        grid=(bsz, l // tt),
        in_specs=in_specs + const_specs,
        out_specs=[pl.BlockSpec((1, tt, w2), row), pl.BlockSpec((1, tt, w2), row),
                   pl.BlockSpec((1, cpt, CHUNK, w2), chunk), pl.BlockSpec((1, cpt, CHUNK, w2), chunk),
                   pl.BlockSpec((1, cpt, 1, w2), chunk),
                   pl.BlockSpec((1, tt, D_RWKV), row), pl.BlockSpec((1, tt, D_RWKV), row)],
        out_shape=[jax.ShapeDtypeStruct((bsz, l, w2), BF16), jax.ShapeDtypeStruct((bsz, l, w2), F32),
                   jax.ShapeDtypeStruct((bsz, nc, CHUNK, w2), BF16), jax.ShapeDtypeStruct((bsz, nc, CHUNK, w2), F32),
                   jax.ShapeDtypeStruct((bsz, nc, 1, w2), F32),
                   jax.ShapeDtypeStruct((bsz, l, D_RWKV), F32), jax.ShapeDtypeStruct((bsz, l, D_RWKV), F32)],
        scratch_shapes=[pltpu.VMEM((CHUNK, SCR_COLS), BF16), pltpu.VMEM((CHUNK, SCR_COLS), BF16),
                        pltpu.VMEM((tt, 4 * D_RWKV), F32), pltpu.VMEM((ext_rows, RWKV_COLS), F32)],
        compiler_params=_cparams(("parallel", "parallel")),
        name="prep_latent" if latent else "prep_ctx",
    )(*args, *consts)


SCAN_CHUNKS = 4


def _scan_kernel(rbf_ref, rbb_ref, y0f_ref, y0b_ref, mxf_ref, mxb_ref, nnf_ref, nnb_ref, wcf_ref, wcb_ref,
                 s0_ref, yf_ref, yb_ref, s_ref):
    i = pl.program_id(1)

    @pl.when(i == 0)
    def _():
        s_ref[...] = s0_ref[...]

    c = CHUNK
    h0 = lax.broadcasted_iota(jnp.int32, (c, PAIR), 1) < HEAD
    row = lax.broadcasted_iota(jnp.int32, (PAIR, PAIR), 0)
    col = lax.broadcasted_iota(jnp.int32, (PAIR, PAIR), 1)
    same = (row >= c) == (col >= c)
    zb = jnp.zeros((c, PAIR), BF16)
    dirs = ((rbf_ref, y0f_ref, mxf_ref, nnf_ref, wcf_ref, yf_ref), (rbb_ref, y0b_ref, mxb_ref, nnb_ref, wcb_ref, yb_ref))
    idx = [(d, p) for d in range(2) for p in range(N_PAIR)]
    lanes = [slice(p * PAIR, (p + 1) * PAIR) for _, p in idx]
    s = [s_ref[0, d, p] for d, p in idx]
    for q in range(SCAN_CHUNKS):
        cq = (q, SCAN_CHUNKS - 1 - q)
        rows = [slice(cq[d] * c, (cq[d] + 1) * c) for d, _ in idx]
        sb = [_bf(x) for x in s]
        rbar = [dirs[d][0][0, rows[k], lanes[k]] for k, (d, _) in enumerate(idx)]
        r_st = [jnp.concatenate([jnp.where(h0, x, zb), jnp.where(h0, zb, x)], axis=0) for x in rbar]
        y_st = [_dot_nt(r_st[k], sb[k]) for k in range(len(idx))]
        m_bd = [jnp.where(same, jnp.concatenate([x, x], axis=0), jnp.zeros((PAIR, PAIR), BF16))
                for x in (dirs[d][2][0, cq[d], :, lanes[k]] for k, (d, _) in enumerate(idx))]
        sm = [_dot(sb[k], m_bd[k]) for k in range(len(idx))]
        s_next = []
        for k, (d, p) in enumerate(idx):
            nn = dirs[d][3][0, cq[d], :, lanes[k]]
            n_bd = jnp.where(same, jnp.concatenate([nn, nn], axis=0), 0.0)
            dirs[d][5][0, rows[k], lanes[k]] = y_st[k][:c] + y_st[k][c:] + dirs[d][1][0, rows[k], lanes[k]]
            s_next.append(s[k] * dirs[d][4][0, cq[d], :, lanes[k]] + sm[k] + n_bd)
        s = s_next
    for k, (d, p) in enumerate(idx):
        s_ref[0, d, p] = s[k]


def _scan(rbar, y0, mx, nn, wc, s0):
    bsz, l, _ = rbar.shape
    sub = SCAN_CHUNKS
    nc = l // (CHUNK * sub)
    row_f = pl.BlockSpec((1, sub * CHUNK, D_RWKV), lambda b, i: (b, i, 0))
    row_b = pl.BlockSpec((1, sub * CHUNK, D_RWKV), lambda b, i: (b, nc - 1 - i, 1))
    chk_f = pl.BlockSpec((1, sub, CHUNK, D_RWKV), lambda b, i: (b, i, 0, 0))
    chk_b = pl.BlockSpec((1, sub, CHUNK, D_RWKV), lambda b, i: (b, nc - 1 - i, 0, 1))
    wc_f = pl.BlockSpec((1, sub, 1, D_RWKV), lambda b, i: (b, i, 0, 0))
    wc_b = pl.BlockSpec((1, sub, 1, D_RWKV), lambda b, i: (b, nc - 1 - i, 0, 1))
    st_spec = pl.BlockSpec((1, 2, N_PAIR, PAIR, PAIR), lambda b, i: (b, 0, 0, 0, 0))
    return pl.pallas_call(
        _scan_kernel,
        grid=(bsz, nc),
        in_specs=[row_f, row_b, row_f, row_b, chk_f, chk_b, chk_f, chk_b, wc_f, wc_b, st_spec],
        out_specs=[row_f, pl.BlockSpec((1, sub * CHUNK, D_RWKV), lambda b, i: (b, nc - 1 - i, 0)), st_spec],
        out_shape=[jax.ShapeDtypeStruct((bsz, l, D_RWKV), F32),
                   jax.ShapeDtypeStruct((bsz, l, D_RWKV), F32),
                   jax.ShapeDtypeStruct((bsz, 2, N_PAIR, PAIR, PAIR), F32)],
        compiler_params=_cparams(("parallel", "arbitrary")),
        name="scan",
    )(rbar, rbar, y0, y0, mx, mx, nn, nn, wc, wc, s0)


OUT_ROWS = 128


def _out_kernel(tt, yf_ref, yb_ref, bonus_ref, g_ref, cv_ref, cvp_ref, cvn_ref, x_ref, g1_ref, sh2_ref,
                sc2_ref, n2g_ref, gng_ref, gnb_ref, convw_ref, wout_ref, rwh_ref, rwl_ref, avg_ref,
                xm_ref, hx_ref, aff_ref):
    i = pl.program_id(1)
    n = pl.num_programs(1)

    cv = cv_ref[0]
    b_gate = cv[:, 0:D_CONV]
    cu = cv[:, D_CONV:2 * D_CONV] * cv[:, 2 * D_CONV:]
    cvp = cvp_ref[0]
    cvn = cvn_ref[0]
    cu_prev = jnp.where(i > 0, cvp[7:8, D_CONV:2 * D_CONV] * cvp[7:8, 2 * D_CONV:], 0.0)
    cu_next = jnp.where(i < n - 1, cvn[0:1, D_CONV:2 * D_CONV] * cvn[0:1, 2 * D_CONV:], 0.0)
    ridx = lax.broadcasted_iota(jnp.int32, cu.shape, 0)
    cu_m1 = jnp.where(ridx == 0, cu_prev, pltpu.roll(cu, 1, 0))
    cu_p1 = jnp.where(ridx == tt - 1, cu_next, pltpu.roll(cu, tt - 1, 0))
    conv = convw_ref[0:1, :] * cu_m1 + convw_ref[1:2, :] * cu + convw_ref[2:3, :] * cu_p1
    bx = (b_gate * conv).astype(BF16)

    parts = [slice(k * OUT_ROWS, (k + 1) * OUT_ROWS) for k in range(tt // OUT_ROWS)]
    avg = avg_ref[...]
    rwh = rwh_ref[...]
    rwl = rwl_ref[...]
    y = [yf_ref[0, r, :] + yb_ref[0, r, :] for r in parts]
    mu = [_seg_dot(v, avg) for v in y]
    dlt = [a - b for a, b in zip(y, mu)]
    var = [_seg_dot(v * v, avg) for v in dlt]
    yn = [a * lax.rsqrt(b + GN_EPS) * gng_ref[...] + gnb_ref[...] for a, b in zip(dlt, var)]
    ax = [((a + bonus_ref[0, r, :]) * g_ref[0, r, :]).astype(BF16) for a, r in zip(yn, parts)]
    mix = [_dot(a, wout_ref[0:D_RWKV, :]) + _dot(bx[r], wout_ref[D_RWKV:, :]) for a, r in zip(ax, parts)]
    xm = [x_ref[0, r, :] + g1_ref[0] * a for a, r in zip(mix, parts)]
    ms = [jnp.mean(v * v, axis=-1, keepdims=True) for v in xm]
    hx = [a * lax.rsqrt(b + NORM_EPS) * n2g_ref[...] for a, b in zip(xm, ms)]
    hx = [_split2(v * (1.0 + sc2_ref[0]) + sh2_ref[0]) for v in hx]
    logits = [_dot_nt(rwh, hi) + _dot_nt(rwh, lo) + _dot_nt(rwl, hi) for hi, lo in hx]
    for k, r in enumerate(parts):
        xm_ref[0, r, :] = xm[k]
        hx_ref[0, r, :] = hx[k][0]
        m = jnp.max(logits[k], axis=0, keepdims=True)
        ex = jnp.exp(logits[k] - m)
        aff_ref[0, :, r] = ex / jnp.sum(ex, axis=0, keepdims=True)


def _out(yf, yb, bonus, g, p_cv, x, g1, sh2, sc2, n2g, gng, gnb, convw, wout, rwh, rwl, avg, tt):
    bsz, t, d = x.shape
    ne = rwh.shape[0]
    nb8 = t // 8
    row = lambda b, i: (b, i, 0)
    per_b = lambda b, i: (b, 0, 0)
    const2 = lambda b, i: (0, 0)
    return pl.pallas_call(
        functools.partial(_out_kernel, tt),
        grid=(bsz, t // tt),
        in_specs=[pl.BlockSpec((1, tt, D_RWKV), row), pl.BlockSpec((1, tt, D_RWKV), row),
                  pl.BlockSpec((1, tt, D_RWKV), row), pl.BlockSpec((1, tt, D_RWKV), row),
                  pl.BlockSpec((1, tt, CONV_COLS), row),
                  pl.BlockSpec((1, 8, CONV_COLS), lambda b, i: (b, jnp.maximum(i * (tt // 8) - 1, 0), 0)),
                  pl.BlockSpec((1, 8, CONV_COLS), lambda b, i: (b, jnp.minimum((i + 1) * (tt // 8), nb8 - 1), 0)),
                  pl.BlockSpec((1, tt, d), row),
                  pl.BlockSpec((1, 1, d), per_b), pl.BlockSpec((1, 1, d), per_b), pl.BlockSpec((1, 1, d), per_b),
                  pl.BlockSpec((1, d), const2), pl.BlockSpec((1, D_RWKV), const2), pl.BlockSpec((1, D_RWKV), const2),
                  pl.BlockSpec((3, D_CONV), const2), pl.BlockSpec((D_RWKV + D_CONV, d), const2),
                  pl.BlockSpec((ne, d), const2), pl.BlockSpec((ne, d), const2),
                  pl.BlockSpec((D_RWKV, D_RWKV), const2)],
        out_specs=[pl.BlockSpec((1, tt, d), row), pl.BlockSpec((1, tt, d), row),
                   pl.BlockSpec((1, ne, tt), lambda b, i: (b, 0, i))],
        out_shape=[jax.ShapeDtypeStruct((bsz, t, d), F32), jax.ShapeDtypeStruct((bsz, t, d), BF16),
                   jax.ShapeDtypeStruct((bsz, ne, t), F32)],
        compiler_params=_cparams(("parallel", "parallel")),
        name="out",
    )(yf, yb, bonus, g, p_cv, p_cv, p_cv, x, g1, sh2, sc2, n2g, gng, gnb, convw, wout, rwh, rwl, avg)


def _prefix_blocks(mask_fn, t, tri, emit):
    carry = None
    for j in range(t // 128):
        m = mask_fn(j)
        inc = _dot(m.astype(BF16), tri)
        carry = jnp.zeros_like(inc[:, 0:1]) if carry is None else carry
        emit(j, m, inc - m + carry)
        carry = carry + inc[:, 127:128]


def _topk_kernel(cap, aff_ref, tri_ref, cnt_ref, slot_ref):
    t = aff_ref.shape[2]
    aff = aff_ref[0]

    def body(k, bits):
        cand = bits | jnp.left_shift(jnp.int32(1), 30 - k)
        cnt = jnp.sum(jnp.where(aff >= pltpu.bitcast(cand, F32), 1, 0), axis=-1, keepdims=True)
        return jnp.where(cnt >= cap, cand, bits)

    bits = lax.fori_loop(0, 31, body, jnp.zeros((aff.shape[0], 1), jnp.int32))
    thr = pltpu.bitcast(bits, F32)
    above = pltpu.bitcast(bits + 1, F32)
    n_gt = jnp.sum(jnp.where(aff >= above, 1, 0), axis=-1, keepdims=True)
    need = (cap - n_gt).astype(F32)
    tri = tri_ref[...]

    def blk(j):
        return aff[:, j * 128:(j + 1) * 128]

    def emit_sel(j, eq, before):
        take = (blk(j) >= above) | ((eq > 0.5) & (before < need))
        slot_ref[0, :, j * 128:(j + 1) * 128] = jnp.where(take, 1, 0)

    _prefix_blocks(lambda j: jnp.where((blk(j) >= thr) & (blk(j) < above), 1.0, 0.0), t, tri, emit_sel)

    def emit_slot(j, m, before):
        count = before.astype(jnp.int32)
        cnt_ref[0, :, j * 128:(j + 1) * 128] = count
        slot_ref[0, :, j * 128:(j + 1) * 128] = jnp.where(m > 0.5, count, -1)

    _prefix_blocks(lambda j: slot_ref[0, :, j * 128:(j + 1) * 128].astype(F32), t, tri, emit_slot)


def _topk(aff_t, tri, cap):
    bsz, ne, t = aff_t.shape
    spec = pl.BlockSpec((1, ne, t), lambda b: (b, 0, 0))
    return pl.pallas_call(
        functools.partial(_topk_kernel, cap),
        grid=(bsz,),
        in_specs=[spec, pl.BlockSpec((128, 128), lambda b: (0, 0))],
        out_specs=[spec, spec],
        out_shape=[jax.ShapeDtypeStruct((bsz, ne, t), jnp.int32)] * 2,
        compiler_params=_cparams(("parallel",)),
        name="topk",
    )(aff_t, tri)


def _slot_block_range(lo, hi, sb):
    shift = sb.bit_length() - 1
    assert sb == 1 << shift
    first = lo >> shift
    return first, jnp.where(hi > lo, ((hi - 1) >> shift) + 1, first)


def _moe_kernel(tk, sb, tsp_ref, hx_ref, slot_ref, wg_ref, wu_ref, wd_ref, ye_ref, xs_ref):
    b = pl.program_id(0)
    e = pl.program_id(1)
    ne = pl.num_programs(1)
    nt = hx_ref.shape[1] // tk
    base = (b * ne + e) * (nt + 1)
    xs_ref[...] = jnp.zeros_like(xs_ref)

    def tile_body(j, carry):
        t0 = pl.multiple_of(j * tk, tk)
        hxt = hx_ref[0, pl.ds(t0, tk), :]
        slot_t = slot_ref[0, 0, pl.ds(j, 1), :]

        def sb_body(s, c2):
            s0 = pl.multiple_of(s * sb, sb)
            slot = lax.broadcasted_iota(jnp.int32, (sb, tk), 0) + s0
            onehot = jnp.where(slot_t == slot, 1.0, 0.0).astype(BF16)
            xs_ref[pl.ds(s0, sb), :] += _dot(onehot, hxt)
            return c2

        first, last = _slot_block_range(tsp_ref[base + j], tsp_ref[base + j + 1], sb)
        lax.fori_loop(first, last, sb_body, 0)
        return carry

    lax.fori_loop(0, nt, tile_body, 0)
    xs = xs_ref[...].astype(BF16)
    h1 = _dot(xs, wg_ref[0])
    h2 = _dot(xs, wu_ref[0])
    hid = (h1 * jax.nn.sigmoid(h1) * h2).astype(BF16)
    ye_ref[0, 0] = _dot(hid, wd_ref[0]).astype(BF16)


def _moe(tsp, hx, slot4, wg, wu, wd, cap, tk, sb):
    bsz, t, d = hx.shape
    ne, _, f = wg.shape
    nt = t // tk
    grid_spec = pltpu.PrefetchScalarGridSpec(
        num_scalar_prefetch=1,
        grid=(bsz, ne),
        in_specs=[pl.BlockSpec((1, t, d), lambda b, e, s: (b, 0, 0)),
                  pl.BlockSpec((1, 1, nt, tk), lambda b, e, s: (b, e, 0, 0)),
                  pl.BlockSpec((1, d, f), lambda b, e, s: (e, 0, 0)),
                  pl.BlockSpec((1, d, f), lambda b, e, s: (e, 0, 0)),
                  pl.BlockSpec((1, f, d), lambda b, e, s: (e, 0, 0))],
        out_specs=pl.BlockSpec((1, 1, cap, d), lambda b, e, s: (b, e, 0, 0)),
        scratch_shapes=[pltpu.VMEM((cap, d), F32)],
    )
    return pl.pallas_call(
        functools.partial(_moe_kernel, tk, sb),
        grid_spec=grid_spec,
        out_shape=jax.ShapeDtypeStruct((bsz, ne, cap, d), BF16),
        compiler_params=_cparams(("parallel", "arbitrary")),
        name="moe",
    )(tsp, hx, slot4, wg, wu, wd)


COMB_ROWS = 128
COMB_COLS = 256
SLOT_ALIGN = 16


def _comb_kernel(tk, win, tsp_ref, xm_ref, ye_ref, slotc_ref, affc_ref, g2_ref, fg_ref, o_ref):
    b = pl.program_id(0)
    j = pl.program_id(1)
    nh = tk // COMB_ROWS
    ntile = pl.num_programs(1) * nh
    ne, cap, d = ye_ref.shape[1], ye_ref.shape[2], ye_ref.shape[3]
    slotc = slotc_ref[0]
    affc = affc_ref[0]
    lane_slot = lax.broadcasted_iota(jnp.int32, (COMB_ROWS, win), 1)
    for h in range(nh):
        rows = slice(h * COMB_ROWS, (h + 1) * COMB_ROWS)
        starts, onehots, vals = [], [], []
        for e in range(ne):
            first = tsp_ref[(b * ne + e) * ntile + j * nh + h]
            start = pl.multiple_of(jnp.minimum(first & ~(SLOT_ALIGN - 1), cap - win), SLOT_ALIGN)
            starts.append(start)
            onehots.append(jnp.where(slotc[rows, e:e + 1] - start == lane_slot, 1.0, 0.0).astype(BF16))
            vals.append(jnp.broadcast_to(affc[rows, e:e + 1], (COMB_ROWS, COMB_COLS)))
        xo, ssq = [], None
        for q in range(d // COMB_COLS):
            cols = slice(q * COMB_COLS, (q + 1) * COMB_COLS)
            acc = None
            for e in range(ne):
                part = vals[e] * _dot(onehots[e], ye_ref[0, e, pl.ds(starts[e], win), cols])
                acc = part if acc is None else acc + part
            x = xm_ref[0, rows, cols] + g2_ref[0, :, cols] * acc
            xo.append(x)
            sq = jnp.sum(x * x, axis=-1, keepdims=True)
            ssq = sq if ssq is None else ssq + sq
        scale = lax.rsqrt(ssq / d + NORM_EPS)
        for q in range(d // COMB_COLS):
            cols = slice(q * COMB_COLS, (q + 1) * COMB_COLS)
            o_ref[0, rows, cols] = xo[q] * scale * fg_ref[:, cols]


def _comb(tsp, xm, ye, slotc, affc, g2, fg, tk):
    bsz, t, d = xm.shape
    ne, cap = ye.shape[1], ye.shape[2]
    win = min(2 * COMB_ROWS, cap)
    assert win == cap or win >= COMB_ROWS + SLOT_ALIGN
    grid_spec = pltpu.PrefetchScalarGridSpec(
        num_scalar_prefetch=1,
        grid=(bsz, t // tk),
        in_specs=[pl.BlockSpec((1, tk, d), lambda b, j, s: (b, j, 0)),
                  pl.BlockSpec((1, ne, cap, d), lambda b, j, s: (b, 0, 0, 0)),
                  pl.BlockSpec((1, tk, ne), lambda b, j, s: (b, j, 0)),
                  pl.BlockSpec((1, tk, ne), lambda b, j, s: (b, j, 0)),
                  pl.BlockSpec((1, 1, d), lambda b, j, s: (b, 0, 0)),
                  pl.BlockSpec((1, d), lambda b, j, s: (0, 0))],
        out_specs=pl.BlockSpec((1, tk, d), lambda b, j, s: (b, j, 0)),
    )
    return pl.pallas_call(
        functools.partial(_comb_kernel, tk, win),
        grid_spec=grid_spec,
        out_shape=jax.ShapeDtypeStruct((bsz, t, d), F32),
        compiler_params=_cparams(("parallel", "arbitrary")),
        name="comb",
    )(tsp, xm, ye, slotc, affc, g2, fg)


def _block_diag_ones(n, blk, value=1.0):
    r = jnp.arange(n)
    return jnp.where((r[:, None] // blk) == (r[None, :] // blk), value, 0.0)


def kernel(x, c, ctx, c_ctx, ada_w, ada_b, norm1_g, norm2_g, w_in, shift_mu, w0, w_lora_up, a0, a_lora_up, k_k, k_a,
           r_k, g_lora_up, gn_g, gn_b, conv_w, w_out, router_w, exp_w_gate, exp_w_up, exp_w_down, final_g):
    bsz, t, d = x.shape
    lc = ctx.shape[1]
    ne = router_w.shape[-1]
    cap = EC_CAPACITY * t // ne
    tt = 256
    tk = 512
    sb = min(128, cap)
    l = 0

    rows = ((bsz + 1 + 7) // 8) * 8
    cc = jnp.zeros((rows, d), F32).at[:bsz].set(c).at[bsz].set(c_ctx)
    mod = _mod(cc, ada_w[l], ada_b[l][None, :])
    sh1, sc1, g1, sh2, sc2, g2 = (m[:, None, :] for m in jnp.split(mod[:bsz], 6, axis=-1))
    csh1, csc1 = (jnp.broadcast_to(m[None, None, :], (bsz, 1, d)) for m in jnp.split(mod[bsz], 6)[:2])

    w_rw = w_in[l][:, :RWKV_COLS].astype(BF16)
    w_cv = w_in[l][:, RWKV_COLS:].astype(BF16)
    n1g = norm1_g[l][None, :]
    px_rw, px_cv = _in_proj(x, sh1, sc1, n1g, w_rw, w_cv, 256)
    pc_rw, _ = _in_proj(ctx, csh1, csc1, n1g, w_rw, w_cv, 256)

    zw = jnp.zeros((LORA_W, 2 * D_RWKV), F32)
    lora = jnp.concatenate([
        jnp.concatenate([w_lora_up[l, 0], w_lora_up[l, 1], zw], axis=1),
        jnp.concatenate([zw, a_lora_up[l, 0], a_lora_up[l, 1]], axis=1)], axis=0).astype(BF16)
    ridx = jnp.arange(CHUNK)
    tril = jnp.where(ridx[None, :] <= ridx[:, None], 1.0, 0.0).astype(BF16)
    triu = jnp.where(ridx[None, :] >= ridx[:, None], 1.0, 0.0).astype(BF16)
    ones_bd = _block_diag_ones(PAIR, HEAD).astype(BF16)
    consts = (shift_mu[l][None, :], lora, w0[l], a0[l], k_k[l][None, :], k_a[l][None, :],
              r_k[l].reshape(1, D_RWKV), g_lora_up[l].astype(BF16), ones_bd, tril, triu)

    chunks_c = _prep(pc_rw, consts, False, lc)[:5]
    *chunks_x, bonus, gate = _prep(px_rw, consts, True, tt)

    s_zero = jnp.zeros((bsz, 2, N_PAIR, PAIR, PAIR), F32)
    _, _, s_ctx = _scan(*chunks_c, s_zero)
    yf, yb, _ = _scan(*chunks_x, s_ctx)

    rw_t = router_w[l].T
    rwh = rw_t.astype(BF16)
    rwl = (rw_t - rwh.astype(F32)).astype(BF16)
    avg = _block_diag_ones(D_RWKV, HEAD, 1.0 / HEAD).astype(BF16)
    xm, hx, aff_t = _out(yf, yb, bonus, gate, px_cv, x, g1, sh2, sc2, norm2_g[l][None, :], gn_g[l][None, :],
                         gn_b[l][None, :], conv_w[l], w_out[l].astype(BF16), rwh, rwl, avg, tt)

    r128 = jnp.arange(128)
    tri128 = jnp.where(r128[:, None] <= r128[None, :], 1.0, 0.0).astype(BF16)
    cnt, slot = _topk(aff_t, tri128, cap)

    nt = t // tk
    tsp = jnp.concatenate([cnt[:, :, ::tk], jnp.full((bsz, ne, 1), cap, jnp.int32)], axis=-1).reshape(-1)
    ye = _moe(tsp, hx, slot.reshape(bsz, ne, nt, tk),
              exp_w_gate[l].astype(BF16), exp_w_up[l].astype(BF16), exp_w_down[l].astype(BF16), cap, tk, sb)
    tr = lambda a: jnp.transpose(a, (0, 2, 1))
    first_slot = cnt[:, :, ::COMB_ROWS].reshape(-1)
    return _comb(first_slot, xm, ye, tr(slot), tr(aff_t), g2, final_g[None, :], tt)
```

```python
import functools
import math

import jax
import jax.numpy as jnp
from jax import lax
from jax.experimental import pallas as pl
from jax.experimental.pallas import tpu as pltpu

F32 = jnp.float32
BF16 = jnp.bfloat16
HIGHEST = lax.Precision.HIGHEST

GRID_W = 64
D_RWKV = 512
D_CONV = 512
HEAD = 64
LORA_W = 64
LORA_A = 64
LORA_G = 128
N_EXPERTS = 16
EC_CAPACITY = 2
NORM_EPS = 1e-6
GN_EPS = 64e-5
RWKV_COLS = 3 * D_RWKV + LORA_W + LORA_A + LORA_G
CONV_COLS = 3 * D_CONV

CHUNK = 64
PAIR = 2 * HEAD
N_PAIR = D_RWKV // PAIR
VMEM_LIMIT = 48 * 1024 * 1024


def _cparams(sem):
    return pltpu.CompilerParams(dimension_semantics=sem, vmem_limit_bytes=VMEM_LIMIT)


def _dot(a, b):
    return jnp.dot(a, b, preferred_element_type=F32)


def _dot_nt(a, b):
    return lax.dot_general(a, b, (((1,), (1,)), ((), ())), preferred_element_type=F32)


def _split2(x):
    hi = x.astype(BF16)
    lo = (x - hi.astype(F32)).astype(BF16)
    return hi, lo


def _split3(x):
    hi = x.astype(BF16)
    r = x - hi.astype(F32)
    mid = r.astype(BF16)
    lo = (r - mid.astype(F32)).astype(BF16)
    return hi, mid, lo


def _seg_dot(x, m):
    hi, lo = _split2(x)
    return _dot(hi, m) + _dot(lo, m)


def _mod_kernel(c_ref, w_ref, b_ref, o_ref):
    c = c_ref[...]
    s = c * jax.nn.sigmoid(c)
    o_ref[...] = jnp.dot(s, w_ref[...], precision=HIGHEST, preferred_element_type=F32) + b_ref[...]


def _mod(cc, w, b):
    rows, d = cc.shape
    n = w.shape[1]
    tn = 1024
    return pl.pallas_call(
        _mod_kernel,
        grid=(n // tn,),
        in_specs=[pl.BlockSpec((rows, d), lambda j: (0, 0)),
                  pl.BlockSpec((d, tn), lambda j: (0, j)),
                  pl.BlockSpec((1, tn), lambda j: (0, j))],
        out_specs=pl.BlockSpec((rows, tn), lambda j: (0, j)),
        out_shape=jax.ShapeDtypeStruct((rows, n), F32),
        compiler_params=_cparams(("parallel",)),
        name="mod",
    )(cc, w, b)


def _in_proj_kernel(x_ref, sh_ref, sc_ref, g_ref, wrw_ref, wcv_ref, orw_ref, ocv_ref):
    x = x_ref[0]
    ms = jnp.mean(x * x, axis=-1, keepdims=True)
    h = x * lax.rsqrt(ms + NORM_EPS) * g_ref[...]
    h = (h * (1.0 + sc_ref[0]) + sh_ref[0]).astype(BF16)
    orw_ref[0] = _dot(h, wrw_ref[...])
    ocv_ref[0] = _dot(h, wcv_ref[...])


def _in_proj(x, sh, sc, g, w_rw, w_cv, tm):
    bsz, l, d = x.shape
    return pl.pallas_call(
        _in_proj_kernel,
        grid=(bsz, l // tm),
        in_specs=[pl.BlockSpec((1, tm, d), lambda b, i: (b, i, 0)),
                  pl.BlockSpec((1, 1, d), lambda b, i: (b, 0, 0)),
                  pl.BlockSpec((1, 1, d), lambda b, i: (b, 0, 0)),
                  pl.BlockSpec((1, d), lambda b, i: (0, 0)),
                  pl.BlockSpec((d, RWKV_COLS), lambda b, i: (0, 0)),
                  pl.BlockSpec((d, CONV_COLS), lambda b, i: (0, 0))],
        out_specs=[pl.BlockSpec((1, tm, RWKV_COLS), lambda b, i: (b, i, 0)),
                   pl.BlockSpec((1, tm, CONV_COLS), lambda b, i: (b, i, 0))],
        out_shape=[jax.ShapeDtypeStruct((bsz, l, RWKV_COLS), F32),
                   jax.ShapeDtypeStruct((bsz, l, CONV_COLS), F32)],
        compiler_params=_cparams(("parallel", "parallel")),
        name="in_proj",
    )(x, sh, sc, g, w_rw, w_cv)


def _bf(x):
    return x.astype(BF16)


def _chunk_local(chains):
    c = CHUNK
    n = range(len(chains))
    lane = lax.broadcasted_iota(jnp.int32, (c, PAIR), 1)
    h0 = lane < HEAD
    row = lax.broadcasted_iota(jnp.int32, (PAIR, PAIR), 0)
    col = lax.broadcasted_iota(jnp.int32, (PAIR, PAIR), 1)
    same = (row >= c) == (col >= c)
    tr, tc = row & (c - 1), col & (c - 1)
    eye = jnp.where(row == col, 1.0, 0.0)
    masks = {rev: (same & ((tc > tr) if rev else (tc < tr)), same & ((tc >= tr) if rev else (tc <= tr)))
             for rev in (False, True)}
    zb = jnp.zeros((c, PAIR), BF16)

    def stack(x):
        return jnp.concatenate([jnp.where(h0, x, zb), jnp.where(h0, zb, x)], axis=0)

    a_st = [stack(ch[1]) for ch in chains]
    r_st = [stack(ch[2]) for ch in chains]
    gram = [_dot_nt(jnp.concatenate([a_st[i], r_st[i]], axis=0),
                    jnp.concatenate([chains[i][3], chains[i][3], chains[i][4], chains[i][4]], axis=0)) for i in n]
    lab = [jnp.where(masks[chains[i][0]][0], gram[i][:PAIR, :PAIR], 0.0) for i in n]
    lak = [jnp.where(masks[chains[i][0]][0], gram[i][:PAIR, PAIR:], 0.0) for i in n]
    mrb = [jnp.where(masks[chains[i][0]][1], gram[i][PAIR:, :PAIR], 0.0) for i in n]
    mrk = [jnp.where(masks[chains[i][0]][1], gram[i][PAIR:, PAIR:], 0.0) for i in n]

    labb = [_bf(x) for x in lab]
    pw = [_dot(labb[i], labb[i]) for i in n]
    tp = [eye + lab[i] for i in n]
    for _ in range(4):
        pwb = [_bf(x) for x in pw]
        both = [_dot(pwb[i], jnp.concatenate([pwb[i], _bf(tp[i])], axis=1)) for i in n]
        pw = [x[:, :PAIR] for x in both]
        tp = [tp[i] + both[i][:, PAIR:] for i in n]
    tinv = [tp[i] + _dot(_bf(pw[i]), _bf(tp[i])) for i in n]

    v_rep = [jnp.concatenate([ch[7], ch[7]], axis=0) for ch in chains]
    lakv = [_dot(_bf(lak[i]), v_rep[i]) for i in n]
    x = [_dot(_bf(tinv[i]), jnp.concatenate([a_st[i], _bf(lakv[i])], axis=1)) for i in n]
    abar = [xi[:, :PAIR] for xi in x]
    u0 = [jnp.where(same, xi[:, PAIR:], 0.0) for xi in x]
    au = [jnp.concatenate([abar[i], u0[i]], axis=1) for i in n]
    aub = [_bf(x) for x in au]
    z = [_dot(_bf(mrb[i]), aub[i]) for i in n]
    zv = [_dot(_bf(mrk[i]), v_rep[i]) for i in n]
    rbar = [r_st[i].astype(F32) + z[i][:, :PAIR] for i in n]
    rbar = [_bf(x[:c] + x[c:]) for x in rbar]
    y0 = [z[i][:, PAIR:] + zv[i] for i in n]
    y0 = [jnp.where(h0, x[:c], x[c:]) for x in y0]

    mn = [_dot(_bf(au[i].T), jnp.concatenate([chains[i][5], chains[i][5]], axis=0)) for i in n]
    vk = [_dot(_bf(chains[i][7].astype(F32).T), chains[i][6]) for i in n]
    mx = [_bf(jnp.where(h0, m[:c], m[c:PAIR])) for m in mn]
    nn = [mn[i][PAIR:] + vk[i] for i in n]
    nn = [jnp.where(h0, x[:c], x[c:]) for x in nn]
    return list(zip(rbar, y0, mx, nn))


DIR_COLS = 6 * D_RWKV
SCR_COLS = 2 * DIR_COLS + D_RWKV


def _head_sums(x, ones):
    return jnp.concatenate([_seg_dot(x[:, g * PAIR:(g + 1) * PAIR], ones) for g in range(N_PAIR)], axis=1)


def _prep_rows(c, p, shifted, lo, mu_ref, w0_ref, a0_ref, kk_ref, ka_ref, rk_ref, ones_ref, tril_ref, triu_ref,
               wc_ref, bonus_ref, scr_ref):
    rows = slice(c * CHUNK, (c + 1) * CHUNK)
    pm = p + mu_ref[:, 0:3 * D_RWKV] * (shifted - p)
    xr = pm[:, 0:D_RWKV]
    xk = pm[:, D_RWKV:2 * D_RWKV]
    xv = pm[:, 2 * D_RWKV:3 * D_RWKV]

    ones = ones_ref[...]
    kraw = xk * kk_ref[...]
    kk = kraw / jnp.maximum(jnp.sqrt(_head_sums(kraw * kraw, ones)), 1e-12)

    scr_ref[:, 2 * DIR_COLS:] = xv.astype(BF16)
    ksum = None
    for d in range(2):
        z = w0_ref[d:d + 1, :] + lo[:, d * D_RWKV:(d + 1) * D_RWKV]
        ld = -math.exp(-0.5) * jax.nn.sigmoid(z)
        ag = jax.nn.sigmoid(a0_ref[d:d + 1, :] + lo[:, (2 + d) * D_RWKV:(3 + d) * D_RWKV])
        kd = xk * (1.0 + (ag - 1.0) * ka_ref[...])
        bb = kk * ag
        ksum = kd if ksum is None else ksum + kd
        tri = tril_ref[...] if d == 0 else triu_ref[...]
        h3 = _split3(ld)
        cs = _dot(tri, h3[0]) + _dot(tri, h3[1]) + _dot(tri, h3[2])
        e_in = jnp.exp(cs)
        e_ex = jnp.exp(cs - ld)
        e_neg = jnp.exp(-cs)
        bt = bb * e_neg
        kt = kd * e_neg
        edge = CHUNK - 1 if d == 0 else 0
        wrow = e_in[edge:edge + 1, :]
        wc_ref[0, c, :, d * D_RWKV:(d + 1) * D_RWKV] = wrow
        base = d * DIR_COLS
        for j, val in enumerate((-kk * e_ex, xr * e_in, bt, kt, bt * wrow, kt * wrow)):
            scr_ref[:, base + j * D_RWKV:base + (j + 1) * D_RWKV] = val.astype(BF16)

    bonus_ref[0, rows, :] = _head_sums(xr * ksum * rk_ref[...], ones) * xv


def _prep_chunk(c, scr_ref, rbar_ref, y0_ref, mx_ref, nn_ref):
    rows = slice(c * CHUNK, (c + 1) * CHUNK)
    chains = []
    for d in range(2):
        for pp in range(N_PAIR):
            cols = [d * DIR_COLS + j * D_RWKV + pp * PAIR for j in range(6)] + [2 * DIR_COLS + pp * PAIR]
            chains.append((d == 1,) + tuple(scr_ref[:, k:k + PAIR] for k in cols))
    for idx, (rbar, y0, mx, nn) in enumerate(_chunk_local(chains)):
        col = (idx // N_PAIR) * D_RWKV + (idx % N_PAIR) * PAIR
        rbar_ref[0, rows, col:col + PAIR] = rbar
        y0_ref[0, rows, col:col + PAIR] = y0
        mx_ref[0, c, :, col:col + PAIR] = mx
        nn_ref[0, c, :, col:col + PAIR] = nn


N_PREP_CONSTS = 11


def _prep_kernel(latent, tt, p_ref, *rest):
    if latent:
        prev_ref, next_ref = rest[:2]
        rest = rest[2:]
    (mu_ref, lw_ref, w0_ref, a0_ref, kk_ref, ka_ref, rk_ref, gup_ref, ones_ref, tril_ref,
     triu_ref) = rest[:N_PREP_CONSTS]
    rbar_ref, y0_ref, mx_ref, nn_ref, wc_ref, bonus_ref, g_ref = rest[N_PREP_CONSTS:N_PREP_CONSTS + 7]
    scr_a, scr_b, lo_ref, ext_ref = rest[N_PREP_CONSTS + 7:]
    if latent:
        i = pl.program_id(1)
        n = pl.num_programs(1)
        ext_ref[0:GRID_W] = jnp.where(i > 0, prev_ref[0], 0.0)
        ext_ref[GRID_W:GRID_W + tt] = p_ref[0]
        ext_ref[GRID_W + tt:] = jnp.where(i < n - 1, next_ref[0], 0.0)

        def shifted(r0, nrows, cols):
            shape = (nrows, cols.stop - cols.start)
            trow = lax.broadcasted_iota(jnp.int32, shape, 0) & (GRID_W - 1)
            q = lax.broadcasted_iota(jnp.int32, shape, 1) & 3
            left = jnp.where(trow == 0, 0.0, ext_ref[GRID_W - 1 + r0:GRID_W - 1 + r0 + nrows, cols])
            right = jnp.where(trow == GRID_W - 1, 0.0, ext_ref[GRID_W + 1 + r0:GRID_W + 1 + r0 + nrows, cols])
            up = ext_ref[r0:r0 + nrows, cols]
            down = ext_ref[2 * GRID_W + r0:2 * GRID_W + r0 + nrows, cols]
            return jnp.where(q == 0, left, jnp.where(q == 1, right, jnp.where(q == 2, up, down)))
    else:
        ext_ref[0:8] = jnp.zeros((8, RWKV_COLS), F32)
        ext_ref[8:8 + tt] = p_ref[0]
        ext_ref[8 + tt:] = jnp.zeros((8, RWKV_COLS), F32)

        def shifted(r0, nrows, cols):
            q = lax.broadcasted_iota(jnp.int32, (nrows, cols.stop - cols.start), 1) & 1
            return jnp.where(q == 0, ext_ref[7 + r0:7 + r0 + nrows, cols], ext_ref[9 + r0:9 + r0 + nrows, cols])

    tail = slice(3 * D_RWKV, RWKV_COLS)
    p_t = p_ref[0, :, tail]
    pm_t = p_t + mu_ref[:, tail] * (shifted(0, tt, tail) - p_t)
    xwa = pm_t[:, 0:LORA_W + LORA_A]
    lane = lax.broadcasted_iota(jnp.int32, xwa.shape, 1)
    lin = jnp.where(lane < LORA_W, jnp.tanh(xwa), xwa).astype(BF16)
    lo_ref[...] = _dot(lin, lw_ref[...])
    g_ref[0] = _dot(jax.nn.sigmoid(pm_t[:, LORA_W + LORA_A:]).astype(BF16), gup_ref[...])

    scr = (scr_a, scr_b)
    head = slice(0, 3 * D_RWKV)

    def rows_part(c):
        rows = slice(c * CHUNK, (c + 1) * CHUNK)
        _prep_rows(c, p_ref[0, rows, head], shifted(c * CHUNK, CHUNK, head), lo_ref[rows, :], mu_ref, w0_ref,
                   a0_ref, kk_ref, ka_ref, rk_ref, ones_ref, tril_ref, triu_ref, wc_ref, bonus_ref, scr[c % 2])

    nchunk = tt // CHUNK
    rows_part(0)
    for c in range(nchunk):
        if c + 1 < nchunk:
            rows_part(c + 1)
        _prep_chunk(c, scr[c % 2], rbar_ref, y0_ref, mx_ref, nn_ref)


def _prep(p_rw, consts, latent, tt):
    bsz, l, _ = p_rw.shape
    nc = l // CHUNK
    cpt = tt // CHUNK
    assert len(consts) == N_PREP_CONSTS
    const_specs = [pl.BlockSpec(c.shape, lambda b, i, nd=c.ndim: (0,) * nd) for c in consts]
    kern = functools.partial(_prep_kernel, latent, tt)
    if latent:
        nblk = l // GRID_W
        in_specs = [pl.BlockSpec((1, tt, RWKV_COLS), lambda b, i: (b, i, 0)),
                    pl.BlockSpec((1, GRID_W, RWKV_COLS),
                                 lambda b, i: (b, jnp.maximum(i * (tt // GRID_W) - 1, 0), 0)),
                    pl.BlockSpec((1, GRID_W, RWKV_COLS),
                                 lambda b, i: (b, jnp.minimum((i + 1) * (tt // GRID_W), nblk - 1), 0))]
        args = (p_rw, p_rw, p_rw)
        ext_rows = tt + 2 * GRID_W
    else:
        assert tt == l
        in_specs = [pl.BlockSpec((1, tt, RWKV_COLS), lambda b, i: (b, i, 0))]
        args = (p_rw,)
        ext_rows = tt + 16
    row = lambda b, i: (b, i, 0)
    chunk = lambda b, i: (b, i, 0, 0)
    w2 = 2 * D_RWKV
    return pl.pallas_call(
        kern,
        grid=(bsz, l // tt),
        in_specs=in_specs + const_specs,
        out_specs=[pl.BlockSpec((1, tt, w2), row), pl.BlockSpec((1, tt, w2), row),
                   pl.BlockSpec((1, cpt, CHUNK, w2), chunk), pl.BlockSpec((1, cpt, CHUNK, w2), chunk),
                   pl.BlockSpec((1, cpt, 1, w2), chunk),
                   pl.BlockSpec((1, tt, D_RWKV), row), pl.BlockSpec((1, tt, D_RWKV), row)],
        out_shape=[jax.ShapeDtypeStruct((bsz, l, w2), BF16), jax.ShapeDtypeStruct((bsz, l, w2), F32),
                   jax.ShapeDtypeStruct((bsz, nc, CHUNK, w2), BF16), jax.ShapeDtypeStruct((bsz, nc, CHUNK, w2), F32),
                   jax.ShapeDtypeStruct((bsz, nc, 1, w2), F32),
                   jax.ShapeDtypeStruct((bsz, l, D_RWKV), F32), jax.ShapeDtypeStruct((bsz, l, D_RWKV), F32)],
        scratch_shapes=[pltpu.VMEM((CHUNK, SCR_COLS), BF16), pltpu.VMEM((CHUNK, SCR_COLS), BF16),
                        pltpu.VMEM((tt, 4 * D_RWKV), F32), pltpu.VMEM((ext_rows, RWKV_COLS), F32)],
        compiler_params=_cparams(("parallel", "parallel")),
        name="prep_latent" if latent else "prep_ctx",
    )(*args, *consts)


SCAN_CHUNKS = 4


def _scan_kernel(rbf_ref, rbb_ref, y0f_ref, y0b_ref, mxf_ref, mxb_ref, nnf_ref, nnb_ref, wcf_ref, wcb_ref,
                 s0_ref, yf_ref, yb_ref, s_ref):
    i = pl.program_id(1)

    @pl.when(i == 0)
    def _():
        s_ref[...] = s0_ref[...]

    c = CHUNK
    h0 = lax.broadcasted_iota(jnp.int32, (c, PAIR), 1) < HEAD
    row = lax.broadcasted_iota(jnp.int32, (PAIR, PAIR), 0)
    col = lax.broadcasted_iota(jnp.int32, (PAIR, PAIR), 1)
    same = (row >= c) == (col >= c)
    zb = jnp.zeros((c, PAIR), BF16)
    dirs = ((rbf_ref, y0f_ref, mxf_ref, nnf_ref, wcf_ref, yf_ref), (rbb_ref, y0b_ref, mxb_ref, nnb_ref, wcb_ref, yb_ref))
    idx = [(d, p) for d in range(2) for p in range(N_PAIR)]
    lanes = [slice(p * PAIR, (p + 1) * PAIR) for _, p in idx]
    s = [s_ref[0, d, p] for d, p in idx]
    for q in range(SCAN_CHUNKS):
        cq = (q, SCAN_CHUNKS - 1 - q)
        rows = [slice(cq[d] * c, (cq[d] + 1) * c) for d, _ in idx]
        sb = [_bf(x) for x in s]
        rbar = [dirs[d][0][0, rows[k], lanes[k]] for k, (d, _) in enumerate(idx)]
        r_st = [jnp.concatenate([jnp.where(h0, x, zb), jnp.where(h0, zb, x)], axis=0) for x in rbar]
        y_st = [_dot_nt(r_st[k], sb[k]) for k in range(len(idx))]
        m_bd = [jnp.where(same, jnp.concatenate([x, x], axis=0), jnp.zeros((PAIR, PAIR), BF16))
                for x in (dirs[d][2][0, cq[d], :, lanes[k]] for k, (d, _) in enumerate(idx))]
        sm = [_dot(sb[k], m_bd[k]) for k in range(len(idx))]
        s_next = []
        for k, (d, p) in enumerate(idx):
            nn = dirs[d][3][0, cq[d], :, lanes[k]]
            n_bd = jnp.where(same, jnp.concatenate([nn, nn], axis=0), 0.0)
            dirs[d][5][0, rows[k], lanes[k]] = y_st[k][:c] + y_st[k][c:] + dirs[d][1][0, rows[k], lanes[k]]
            s_next.append(s[k] * dirs[d][4][0, cq[d], :, lanes[k]] + sm[k] + n_bd)
        s = s_next
    for k, (d, p) in enumerate(idx):
        s_ref[0, d, p] = s[k]


def _scan(rbar, y0, mx, nn, wc, s0):
    bsz, l, _ = rbar.shape
    sub = SCAN_CHUNKS
    nc = l // (CHUNK * sub)
    row_f = pl.BlockSpec((1, sub * CHUNK, D_RWKV), lambda b, i: (b, i, 0))
    row_b = pl.BlockSpec((1, sub * CHUNK, D_RWKV), lambda b, i: (b, nc - 1 - i, 1))
    chk_f = pl.BlockSpec((1, sub, CHUNK, D_RWKV), lambda b, i: (b, i, 0, 0))
    chk_b = pl.BlockSpec((1, sub, CHUNK, D_RWKV), lambda b, i: (b, nc - 1 - i, 0, 1))
    wc_f = pl.BlockSpec((1, sub, 1, D_RWKV), lambda b, i: (b, i, 0, 0))
    wc_b = pl.BlockSpec((1, sub, 1, D_RWKV), lambda b, i: (b, nc - 1 - i, 0, 1))
    st_spec = pl.BlockSpec((1, 2, N_PAIR, PAIR, PAIR), lambda b, i: (b, 0, 0, 0, 0))
    return pl.pallas_call(
        _scan_kernel,
        grid=(bsz, nc),
        in_specs=[row_f, row_b, row_f, row_b, chk_f, chk_b, chk_f, chk_b, wc_f, wc_b, st_spec],
        out_specs=[row_f, pl.BlockSpec((1, sub * CHUNK, D_RWKV), lambda b, i: (b, nc - 1 - i, 0)), st_spec],
        out_shape=[jax.ShapeDtypeStruct((bsz, l, D_RWKV), F32),
                   jax.ShapeDtypeStruct((bsz, l, D_RWKV), F32),
                   jax.ShapeDtypeStruct((bsz, 2, N_PAIR, PAIR, PAIR), F32)],
        compiler_params=_cparams(("parallel", "arbitrary")),
        name="scan",
    )(rbar, rbar, y0, y0, mx, mx, nn, nn, wc, wc, s0)


OUT_ROWS = 128


def _out_kernel(tt, yf_ref, yb_ref, bonus_ref, g_ref, cv_ref, cvp_ref, cvn_ref, x_ref, g1_ref, sh2_ref,
                sc2_ref, n2g_ref, gng_ref, gnb_ref, convw_ref, wout_ref, rwh_ref, rwl_ref, avg_ref,
                xm_ref, hx_ref, aff_ref):
    i = pl.program_id(1)
    n = pl.num_programs(1)

    cv = cv_ref[0]
    b_gate = cv[:, 0:D_CONV]
    cu = cv[:, D_CONV:2 * D_CONV] * cv[:, 2 * D_CONV:]
    cvp = cvp_ref[0]
    cvn = cvn_ref[0]
    cu_prev = jnp.where(i > 0, cvp[7:8, D_CONV:2 * D_CONV] * cvp[7:8, 2 * D_CONV:], 0.0)
    cu_next = jnp.where(i < n - 1, cvn[0:1, D_CONV:2 * D_CONV] * cvn[0:1, 2 * D_CONV:], 0.0)
    ridx = lax.broadcasted_iota(jnp.int32, cu.shape, 0)
    cu_m1 = jnp.where(ridx == 0, cu_prev, pltpu.roll(cu, 1, 0))
    cu_p1 = jnp.where(ridx == tt - 1, cu_next, pltpu.roll(cu, tt - 1, 0))
    conv = convw_ref[0:1, :] * cu_m1 + convw_ref[1:2, :] * cu + convw_ref[2:3, :] * cu_p1
    bx = (b_gate * conv).astype(BF16)

    parts = [slice(k * OUT_ROWS, (k + 1) * OUT_ROWS) for k in range(tt // OUT_ROWS)]
    avg = avg_ref[...]
    rwh = rwh_ref[...]
    rwl = rwl_ref[...]
    y = [yf_ref[0, r, :] + yb_ref[0, r, :] for r in parts]
    mu = [_seg_dot(v, avg) for v in y]
    dlt = [a - b for a, b in zip(y, mu)]
    var = [_seg_dot(v * v, avg) for v in dlt]
    yn = [a * lax.rsqrt(b + GN_EPS) * gng_ref[...] + gnb_ref[...] for a, b in zip(dlt, var)]
    ax = [((a + bonus_ref[0, r, :]) * g_ref[0, r, :]).astype(BF16) for a, r in zip(yn, parts)]
    mix = [_dot(a, wout_ref[0:D_RWKV, :]) + _dot(bx[r], wout_ref[D_RWKV:, :]) for a, r in zip(ax, parts)]
    xm = [x_ref[0, r, :] + g1_ref[0] * a for a, r in zip(mix, parts)]
    ms = [jnp.mean(v * v, axis=-1, keepdims=True) for v in xm]
    hx = [a * lax.rsqrt(b + NORM_EPS) * n2g_ref[...] for a, b in zip(xm, ms)]
    hx = [_split2(v * (1.0 + sc2_ref[0]) + sh2_ref[0]) for v in hx]
    logits = [_dot_nt(rwh, hi) + _dot_nt(rwh, lo) + _dot_nt(rwl, hi) for hi, lo in hx]
    for k, r in enumerate(parts):
        xm_ref[0, r, :] = xm[k]
        hx_ref[0, r, :] = hx[k][0]
        m = jnp.max(logits[k], axis=0, keepdims=True)
        ex = jnp.exp(logits[k] - m)
        aff_ref[0, :, r] = ex / jnp.sum(ex, axis=0, keepdims=True)


def _out(yf, yb, bonus, g, p_cv, x, g1, sh2, sc2, n2g, gng, gnb, convw, wout, rwh, rwl, avg, tt):
    bsz, t, d = x.shape
    ne = rwh.shape[0]
    nb8 = t // 8
    row = lambda b, i: (b, i, 0)
    per_b = lambda b, i: (b, 0, 0)
    const2 = lambda b, i: (0, 0)
    return pl.pallas_call(
        functools.partial(_out_kernel, tt),
        grid=(bsz, t // tt),
        in_specs=[pl.BlockSpec((1, tt, D_RWKV), row), pl.BlockSpec((1, tt, D_RWKV), row),
                  pl.BlockSpec((1, tt, D_RWKV), row), pl.BlockSpec((1, tt, D_RWKV), row),
                  pl.BlockSpec((1, tt, CONV_COLS), row),
                  pl.BlockSpec((1, 8, CONV_COLS), lambda b, i: (b, jnp.maximum(i * (tt // 8) - 1, 0), 0)),
                  pl.BlockSpec((1, 8, CONV_COLS), lambda b, i: (b, jnp.minimum((i + 1) * (tt // 8), nb8 - 1), 0)),
                  pl.BlockSpec((1, tt, d), row),
                  pl.BlockSpec((1, 1, d), per_b), pl.BlockSpec((1, 1, d), per_b), pl.BlockSpec((1, 1, d), per_b),
                  pl.BlockSpec((1, d), const2), pl.BlockSpec((1, D_RWKV), const2), pl.BlockSpec((1, D_RWKV), const2),
                  pl.BlockSpec((3, D_CONV), const2), pl.BlockSpec((D_RWKV + D_CONV, d), const2),
                  pl.BlockSpec((ne, d), const2), pl.BlockSpec((ne, d), const2),
                  pl.BlockSpec((D_RWKV, D_RWKV), const2)],
        out_specs=[pl.BlockSpec((1, tt, d), row), pl.BlockSpec((1, tt, d), row),
                   pl.BlockSpec((1, ne, tt), lambda b, i: (b, 0, i))],
        out_shape=[jax.ShapeDtypeStruct((bsz, t, d), F32), jax.ShapeDtypeStruct((bsz, t, d), BF16),
                   jax.ShapeDtypeStruct((bsz, ne, t), F32)],
        compiler_params=_cparams(("parallel", "parallel")),
        name="out",
    )(yf, yb, bonus, g, p_cv, p_cv, p_cv, x, g1, sh2, sc2, n2g, gng, gnb, convw, wout, rwh, rwl, avg)


def _prefix_blocks(mask_fn, t, tri, emit):
    carry = None
    for j in range(t // 128):
        m = mask_fn(j)
        inc = _dot(m.astype(BF16), tri)
        carry = jnp.zeros_like(inc[:, 0:1]) if carry is None else carry
        emit(j, m, inc - m + carry)
        carry = carry + inc[:, 127:128]


def _topk_kernel(cap, aff_ref, tri_ref, cnt_ref, slot_ref):
    t = aff_ref.shape[2]
    aff = aff_ref[0]

    def body(k, bits):
        cand = bits | jnp.left_shift(jnp.int32(1), 30 - k)
        cnt = jnp.sum(jnp.where(aff >= pltpu.bitcast(cand, F32), 1, 0), axis=-1, keepdims=True)
        return jnp.where(cnt >= cap, cand, bits)

    bits = lax.fori_loop(0, 31, body, jnp.zeros((aff.shape[0], 1), jnp.int32))
    thr = pltpu.bitcast(bits, F32)
    above = pltpu.bitcast(bits + 1, F32)
    n_gt = jnp.sum(jnp.where(aff >= above, 1, 0), axis=-1, keepdims=True)
    need = (cap - n_gt).astype(F32)
    tri = tri_ref[...]

    def blk(j):
        return aff[:, j * 128:(j + 1) * 128]

    def emit_sel(j, eq, before):
        take = (blk(j) >= above) | ((eq > 0.5) & (before < need))
        slot_ref[0, :, j * 128:(j + 1) * 128] = jnp.where(take, 1, 0)

    _prefix_blocks(lambda j: jnp.where((blk(j) >= thr) & (blk(j) < above), 1.0, 0.0), t, tri, emit_sel)

    def emit_slot(j, m, before):
        count = before.astype(jnp.int32)
        cnt_ref[0, :, j * 128:(j + 1) * 128] = count
        slot_ref[0, :, j * 128:(j + 1) * 128] = jnp.where(m > 0.5, count, -1)

    _prefix_blocks(lambda j: slot_ref[0, :, j * 128:(j + 1) * 128].astype(F32), t, tri, emit_slot)


def _topk(aff_t, tri, cap):
    bsz, ne, t = aff_t.shape
    spec = pl.BlockSpec((1, ne, t), lambda b: (b, 0, 0))
    return pl.pallas_call(
        functools.partial(_topk_kernel, cap),
        grid=(bsz,),
        in_specs=[spec, pl.BlockSpec((128, 128), lambda b: (0, 0))],
        out_specs=[spec, spec],
        out_shape=[jax.ShapeDtypeStruct((bsz, ne, t), jnp.int32)] * 2,
        compiler_params=_cparams(("parallel",)),
        name="topk",
    )(aff_t, tri)


def _slot_block_range(lo, hi, sb):
    shift = sb.bit_length() - 1
    assert sb == 1 << shift
    first = lo >> shift
    return first, jnp.where(hi > lo, ((hi - 1) >> shift) + 1, first)


def _moe_kernel(tk, sb, tsp_ref, hx_ref, slot_ref, wg_ref, wu_ref, wd_ref, ye_ref, xs_ref):
    b = pl.program_id(0)
    e = pl.program_id(1)
    ne = pl.num_programs(1)
    nt = hx_ref.shape[1] // tk
    base = (b * ne + e) * (nt + 1)
    xs_ref[...] = jnp.zeros_like(xs_ref)

    def tile_body(j, carry):
        t0 = pl.multiple_of(j * tk, tk)
        hxt = hx_ref[0, pl.ds(t0, tk), :]
        slot_t = slot_ref[0, 0, pl.ds(j, 1), :]

        def sb_body(s, c2):
            s0 = pl.multiple_of(s * sb, sb)
            slot = lax.broadcasted_iota(jnp.int32, (sb, tk), 0) + s0
            onehot = jnp.where(slot_t == slot, 1.0, 0.0).astype(BF16)
            xs_ref[pl.ds(s0, sb), :] += _dot(onehot, hxt)
            return c2

        first, last = _slot_block_range(tsp_ref[base + j], tsp_ref[base + j + 1], sb)
        lax.fori_loop(first, last, sb_body, 0)
        return carry

    lax.fori_loop(0, nt, tile_body, 0)
    xs = xs_ref[...].astype(BF16)
    h1 = _dot(xs, wg_ref[0].astype(BF16))
    h2 = _dot(xs, wu_ref[0].astype(BF16))
    hid = (h1 * jax.nn.sigmoid(h1) * h2).astype(BF16)
    ye_ref[0, 0] = _dot(hid, wd_ref[0].astype(BF16)).astype(BF16)


def _moe(tsp, hx, slot4, wg, wu, wd, cap, tk, sb):
    bsz, t, d = hx.shape
    ne, _, f = wg.shape
    nt = t // tk
    grid_spec = pltpu.PrefetchScalarGridSpec(
        num_scalar_prefetch=1,
        grid=(bsz, ne),
        in_specs=[pl.BlockSpec((1, t, d), lambda b, e, s: (b, 0, 0), pipeline_mode=pl.Buffered(1)),
                  pl.BlockSpec((1, 1, nt, tk), lambda b, e, s: (b, e, 0, 0)),
                  pl.BlockSpec((1, d, f), lambda b, e, s: (e, 0, 0)),
                  pl.BlockSpec((1, d, f), lambda b, e, s: (e, 0, 0)),
                  pl.BlockSpec((1, f, d), lambda b, e, s: (e, 0, 0))],
        out_specs=pl.BlockSpec((1, 1, cap, d), lambda b, e, s: (b, e, 0, 0)),
        scratch_shapes=[pltpu.VMEM((cap, d), F32)],
    )
    return pl.pallas_call(
        functools.partial(_moe_kernel, tk, sb),
        grid_spec=grid_spec,
        out_shape=jax.ShapeDtypeStruct((bsz, ne, cap, d), BF16),
        compiler_params=_cparams(("parallel", "arbitrary")),
        name="moe",
    )(tsp, hx, slot4, wg, wu, wd)


COMB_ROWS = 128
COMB_COLS = 256
SLOT_ALIGN = 16


def _comb_kernel(tk, win, tsp_ref, xm_ref, ye_ref, slotc_ref, affc_ref, g2_ref, fg_ref, o_ref):
    b = pl.program_id(0)
    j = pl.program_id(1)
    nh = tk // COMB_ROWS
    ntile = pl.num_programs(1) * nh
    ne, cap, d = ye_ref.shape[1], ye_ref.shape[2], ye_ref.shape[3]
    slotc = slotc_ref[0]
    affc = affc_ref[0]
    lane_slot = lax.broadcasted_iota(jnp.int32, (COMB_ROWS, win), 1)
    for h in range(nh):
        rows = slice(h * COMB_ROWS, (h + 1) * COMB_ROWS)
        starts, onehots, vals = [], [], []
        for e in range(ne):
            first = tsp_ref[(b * ne + e) * ntile + j * nh + h]
            start = pl.multiple_of(jnp.minimum(first & ~(SLOT_ALIGN - 1), cap - win), SLOT_ALIGN)
            starts.append(start)
            onehots.append(jnp.where(slotc[rows, e:e + 1] - start == lane_slot, 1.0, 0.0).astype(BF16))
            vals.append(jnp.broadcast_to(affc[rows, e:e + 1], (COMB_ROWS, COMB_COLS)))
        xo, ssq = [], None
        for q in range(d // COMB_COLS):
            cols = slice(q * COMB_COLS, (q + 1) * COMB_COLS)
            acc = None
            for e in range(ne):
                part = vals[e] * _dot(onehots[e], ye_ref[0, e, pl.ds(starts[e], win), cols])
                acc = part if acc is None else acc + part
            x = xm_ref[0, rows, cols] + g2_ref[0, :, cols] * acc
            xo.append(x)
            sq = jnp.sum(x * x, axis=-1, keepdims=True)
            ssq = sq if ssq is None else ssq + sq
        scale = lax.rsqrt(ssq / d + NORM_EPS)
        for q in range(d // COMB_COLS):
            cols = slice(q * COMB_COLS, (q + 1) * COMB_COLS)
            o_ref[0, rows, cols] = xo[q] * scale * fg_ref[:, cols]


def _comb(tsp, xm, ye, slotc, affc, g2, fg, tk):
    bsz, t, d = xm.shape
    ne, cap = ye.shape[1], ye.shape[2]
    win = min(2 * COMB_ROWS, cap)
    assert win == cap or win >= COMB_ROWS + SLOT_ALIGN
    grid_spec = pltpu.PrefetchScalarGridSpec(
        num_scalar_prefetch=1,
        grid=(bsz, t // tk),
        in_specs=[pl.BlockSpec((1, tk, d), lambda b, j, s: (b, j, 0)),
                  pl.BlockSpec((1, ne, cap, d), lambda b, j, s: (b, 0, 0, 0)),
                  pl.BlockSpec((1, tk, ne), lambda b, j, s: (b, j, 0)),
                  pl.BlockSpec((1, tk, ne), lambda b, j, s: (b, j, 0)),
                  pl.BlockSpec((1, 1, d), lambda b, j, s: (b, 0, 0)),
                  pl.BlockSpec((1, d), lambda b, j, s: (0, 0))],
        out_specs=pl.BlockSpec((1, tk, d), lambda b, j, s: (b, j, 0)),
    )
    return pl.pallas_call(
        functools.partial(_comb_kernel, tk, win),
        grid_spec=grid_spec,
        out_shape=jax.ShapeDtypeStruct((bsz, t, d), F32),
        compiler_params=_cparams(("parallel", "arbitrary")),
        name="comb",
    )(tsp, xm, ye, slotc, affc, g2, fg)


def _block_diag_ones(n, blk, value=1.0):
    r = jnp.arange(n)
    return jnp.where((r[:, None] // blk) == (r[None, :] // blk), value, 0.0)


def kernel(x, c, ctx, c_ctx, ada_w, ada_b, norm1_g, norm2_g, w_in, shift_mu, w0, w_lora_up, a0, a_lora_up, k_k, k_a,
           r_k, g_lora_up, gn_g, gn_b, conv_w, w_out, router_w, exp_w_gate, exp_w_up, exp_w_down, final_g):
    bsz, t, d = x.shape
    lc = ctx.shape[1]
    ne = router_w.shape[-1]
    cap = EC_CAPACITY * t // ne
    tt = 256
    tk = 1024
    sb = min(128, cap)
    l = 0

    rows = ((bsz + 1 + 7) // 8) * 8
    cc = jnp.zeros((rows, d), F32).at[:bsz].set(c).at[bsz].set(c_ctx)
    mod = _mod(cc, ada_w[l], ada_b[l][None, :])
    sh1, sc1, g1, sh2, sc2, g2 = (m[:, None, :] for m in jnp.split(mod[:bsz], 6, axis=-1))
    csh1, csc1 = (jnp.broadcast_to(m[None, None, :], (bsz, 1, d)) for m in jnp.split(mod[bsz], 6)[:2])

    w_rw = w_in[l][:, :RWKV_COLS].astype(BF16)
    w_cv = w_in[l][:, RWKV_COLS:].astype(BF16)
    n1g = norm1_g[l][None, :]
    px_rw, px_cv = _in_proj(x, sh1, sc1, n1g, w_rw, w_cv, 256)
    pc_rw, _ = _in_proj(ctx, csh1, csc1, n1g, w_rw, w_cv, 256)

    zw = jnp.zeros((LORA_W, 2 * D_RWKV), F32)
    lora = jnp.concatenate([
        jnp.concatenate([w_lora_up[l, 0], w_lora_up[l, 1], zw], axis=1),
        jnp.concatenate([zw, a_lora_up[l, 0], a_lora_up[l, 1]], axis=1)], axis=0).astype(BF16)
    ridx = jnp.arange(CHUNK)
    tril = jnp.where(ridx[None, :] <= ridx[:, None], 1.0, 0.0).astype(BF16)
    triu = jnp.where(ridx[None, :] >= ridx[:, None], 1.0, 0.0).astype(BF16)
    ones_bd = _block_diag_ones(PAIR, HEAD).astype(BF16)
    consts = (shift_mu[l][None, :], lora, w0[l], a0[l], k_k[l][None, :], k_a[l][None, :],
              r_k[l].reshape(1, D_RWKV), g_lora_up[l].astype(BF16), ones_bd, tril, triu)

    chunks_c = _prep(pc_rw, consts, False, lc)[:5]
    *chunks_x, bonus, gate = _prep(px_rw, consts, True, tt)

    s_zero = jnp.zeros((bsz, 2, N_PAIR, PAIR, PAIR), F32)
    _, _, s_ctx = _scan(*chunks_c, s_zero)
    yf, yb, _ = _scan(*chunks_x, s_ctx)

    rw_t = router_w[l].T
    rwh = rw_t.astype(BF16)
    rwl = (rw_t - rwh.astype(F32)).astype(BF16)
    avg = _block_diag_ones(D_RWKV, HEAD, 1.0 / HEAD).astype(BF16)
    xm, hx, aff_t = _out(yf, yb, bonus, gate, px_cv, x, g1, sh2, sc2, norm2_g[l][None, :], gn_g[l][None, :],
                         gn_b[l][None, :], conv_w[l], w_out[l].astype(BF16), rwh, rwl, avg, tt)

    r128 = jnp.arange(128)
    tri128 = jnp.where(r128[:, None] <= r128[None, :], 1.0, 0.0).astype(BF16)
    cnt, slot = _topk(aff_t, tri128, cap)

    nt = t // tk
    tsp = jnp.concatenate([cnt[:, :, ::tk], jnp.full((bsz, ne, 1), cap, jnp.int32)], axis=-1).reshape(-1)
    ye = _moe(tsp, hx, slot.reshape(bsz, ne, nt, tk),
              exp_w_gate[l], exp_w_up[l], exp_w_down[l], cap, tk, sb)
    tr = lambda a: jnp.transpose(a, (0, 2, 1))
    first_slot = cnt[:, :, ::COMB_ROWS].reshape(-1)
    return _comb(first_slot, xm, ye, tr(slot), tr(aff_t), g2, final_g[None, :], tt)
```

```python
import functools
import math

import jax
import jax.numpy as jnp
from jax import lax
from jax.experimental import pallas as pl
from jax.experimental.pallas import tpu as pltpu

F32 = jnp.float32
BF16 = jnp.bfloat16
HIGHEST = lax.Precision.HIGHEST

GRID_W = 64
D_RWKV = 512
D_CONV = 512
HEAD = 64
LORA_W = 64
LORA_A = 64
LORA_G = 128
N_EXPERTS = 16
EC_CAPACITY = 2
NORM_EPS = 1e-6
GN_EPS = 64e-5
RWKV_COLS = 3 * D_RWKV + LORA_W + LORA_A + LORA_G
CONV_COLS = 3 * D_CONV

CHUNK = 64
PAIR = 2 * HEAD
N_PAIR = D_RWKV // PAIR
VMEM_LIMIT = 48 * 1024 * 1024


def _cparams(sem):
    return pltpu.CompilerParams(dimension_semantics=sem, vmem_limit_bytes=VMEM_LIMIT)


def _dot(a, b):
    return jnp.dot(a, b, preferred_element_type=F32)


def _dot_nt(a, b):
    return lax.dot_general(a, b, (((1,), (1,)), ((), ())), preferred_element_type=F32)


def _split2(x):
    hi = x.astype(BF16)
    lo = (x - hi.astype(F32)).astype(BF16)
    return hi, lo


def _split3(x):
    hi = x.astype(BF16)
    r = x - hi.astype(F32)
    mid = r.astype(BF16)
    lo = (r - mid.astype(F32)).astype(BF16)
    return hi, mid, lo


def _seg_dot(x, m):
    hi, lo = _split2(x)
    return _dot(hi, m) + _dot(lo, m)


def _mod_kernel(c_ref, w_ref, b_ref, o_ref):
    c = c_ref[...]
    s = c * jax.nn.sigmoid(c)
    o_ref[...] = jnp.dot(s, w_ref[...], precision=HIGHEST, preferred_element_type=F32) + b_ref[...]


def _mod(cc, w, b):
    rows, d = cc.shape
    n = w.shape[1]
    tn = 1024
    return pl.pallas_call(
        _mod_kernel,
        grid=(n // tn,),
        in_specs=[pl.BlockSpec((rows, d), lambda j: (0, 0)),
                  pl.BlockSpec((d, tn), lambda j: (0, j)),
                  pl.BlockSpec((1, tn), lambda j: (0, j))],
        out_specs=pl.BlockSpec((rows, tn), lambda j: (0, j)),
        out_shape=jax.ShapeDtypeStruct((rows, n), F32),
        compiler_params=_cparams(("parallel",)),
        name="mod",
    )(cc, w, b)


def _in_proj_kernel(x_ref, sh_ref, sc_ref, g_ref, wrw_ref, wcv_ref, orw_ref, ocv_ref):
    x = x_ref[0]
    ms = jnp.mean(x * x, axis=-1, keepdims=True)
    h = x * lax.rsqrt(ms + NORM_EPS) * g_ref[...]
    h = (h * (1.0 + sc_ref[0]) + sh_ref[0]).astype(BF16)
    orw_ref[0] = _dot(h, wrw_ref[...])
    ocv_ref[0] = _dot(h, wcv_ref[...])


def _in_proj(x, sh, sc, g, w_rw, w_cv, tm):
    bsz, l, d = x.shape
    return pl.pallas_call(
        _in_proj_kernel,
        grid=(bsz, l // tm),
        in_specs=[pl.BlockSpec((1, tm, d), lambda b, i: (b, i, 0)),
                  pl.BlockSpec((1, 1, d), lambda b, i: (b, 0, 0)),
                  pl.BlockSpec((1, 1, d), lambda b, i: (b, 0, 0)),
                  pl.BlockSpec((1, d), lambda b, i: (0, 0)),
                  pl.BlockSpec((d, RWKV_COLS), lambda b, i: (0, 0)),
                  pl.BlockSpec((d, CONV_COLS), lambda b, i: (0, 0))],
        out_specs=[pl.BlockSpec((1, tm, RWKV_COLS), lambda b, i: (b, i, 0)),
                   pl.BlockSpec((1, tm, CONV_COLS), lambda b, i: (b, i, 0))],
        out_shape=[jax.ShapeDtypeStruct((bsz, l, RWKV_COLS), F32),
                   jax.ShapeDtypeStruct((bsz, l, CONV_COLS), F32)],
        compiler_params=_cparams(("parallel", "parallel")),
        name="in_proj",
    )(x, sh, sc, g, w_rw, w_cv)


def _bf(x):
    return x.astype(BF16)


def _chunk_local(chains):
    c = CHUNK
    n = range(len(chains))
    lane = lax.broadcasted_iota(jnp.int32, (c, PAIR), 1)
    h0 = lane < HEAD
    row = lax.broadcasted_iota(jnp.int32, (PAIR, PAIR), 0)
    col = lax.broadcasted_iota(jnp.int32, (PAIR, PAIR), 1)
    same = (row >= c) == (col >= c)
    tr, tc = row & (c - 1), col & (c - 1)
    eye = jnp.where(row == col, 1.0, 0.0)
    masks = {rev: (same & ((tc > tr) if rev else (tc < tr)), same & ((tc >= tr) if rev else (tc <= tr)))
             for rev in (False, True)}
    zb = jnp.zeros((c, PAIR), BF16)

    def stack(x):
        return jnp.concatenate([jnp.where(h0, x, zb), jnp.where(h0, zb, x)], axis=0)

    a_st = [stack(ch[1]) for ch in chains]
    r_st = [stack(ch[2]) for ch in chains]
    gram = [_dot_nt(jnp.concatenate([a_st[i], r_st[i]], axis=0),
                    jnp.concatenate([chains[i][3], chains[i][3], chains[i][4], chains[i][4]], axis=0)) for i in n]
    lab = [jnp.where(masks[chains[i][0]][0], gram[i][:PAIR, :PAIR], 0.0) for i in n]
    lak = [jnp.where(masks[chains[i][0]][0], gram[i][:PAIR, PAIR:], 0.0) for i in n]
    mrb = [jnp.where(masks[chains[i][0]][1], gram[i][PAIR:, :PAIR], 0.0) for i in n]
    mrk = [jnp.where(masks[chains[i][0]][1], gram[i][PAIR:, PAIR:], 0.0) for i in n]

    labb = [_bf(x) for x in lab]
    pw = [_dot(labb[i], labb[i]) for i in n]
    tp = [eye + lab[i] for i in n]
    for _ in range(4):
        pwb = [_bf(x) for x in pw]
        both = [_dot(pwb[i], jnp.concatenate([pwb[i], _bf(tp[i])], axis=1)) for i in n]
        pw = [x[:, :PAIR] for x in both]
        tp = [tp[i] + both[i][:, PAIR:] for i in n]
    tinv = [tp[i] + _dot(_bf(pw[i]), _bf(tp[i])) for i in n]

    v_rep = [jnp.concatenate([ch[7], ch[7]], axis=0) for ch in chains]
    lakv = [_dot(_bf(lak[i]), v_rep[i]) for i in n]
    x = [_dot(_bf(tinv[i]), jnp.concatenate([a_st[i], _bf(lakv[i])], axis=1)) for i in n]
    abar = [xi[:, :PAIR] for xi in x]
    u0 = [jnp.where(same, xi[:, PAIR:], 0.0) for xi in x]
    au = [jnp.concatenate([abar[i], u0[i]], axis=1) for i in n]
    aub = [_bf(x) for x in au]
    z = [_dot(_bf(mrb[i]), aub[i]) for i in n]
    zv = [_dot(_bf(mrk[i]), v_rep[i]) for i in n]
    rbar = [r_st[i].astype(F32) + z[i][:, :PAIR] for i in n]
    rbar = [_bf(x[:c] + x[c:]) for x in rbar]
    y0 = [z[i][:, PAIR:] + zv[i] for i in n]
    y0 = [jnp.where(h0, x[:c], x[c:]) for x in y0]

    mn = [_dot(_bf(au[i].T), jnp.concatenate([chains[i][5], chains[i][5]], axis=0)) for i in n]
    vk = [_dot(_bf(chains[i][7].astype(F32).T), chains[i][6]) for i in n]
    mx = [_bf(jnp.where(h0, m[:c], m[c:PAIR])) for m in mn]
    nn = [mn[i][PAIR:] + vk[i] for i in n]
    nn = [jnp.where(h0, x[:c], x[c:]) for x in nn]
    return list(zip(rbar, y0, mx, nn))


DIR_COLS = 6 * D_RWKV
SCR_COLS = 2 * DIR_COLS + D_RWKV


def _head_sums(x, ones):
    return jnp.concatenate([_seg_dot(x[:, g * PAIR:(g + 1) * PAIR], ones) for g in range(N_PAIR)], axis=1)


def _prep_rows(c, p, shifted, lo, mu_ref, w0_ref, a0_ref, kk_ref, ka_ref, rk_ref, ones_ref, tril_ref, triu_ref,
               wc_ref, bonus_ref, scr_ref):
    rows = slice(c * CHUNK, (c + 1) * CHUNK)
    pm = p + mu_ref[:, 0:3 * D_RWKV] * (shifted - p)
    xr = pm[:, 0:D_RWKV]
    xk = pm[:, D_RWKV:2 * D_RWKV]
    xv = pm[:, 2 * D_RWKV:3 * D_RWKV]

    ones = ones_ref[...]
    kraw = xk * kk_ref[...]
    kk = kraw / jnp.maximum(jnp.sqrt(_head_sums(kraw * kraw, ones)), 1e-12)

    scr_ref[:, 2 * DIR_COLS:] = xv.astype(BF16)
    ksum = None
    for d in range(2):
        z = w0_ref[d:d + 1, :] + lo[:, d * D_RWKV:(d + 1) * D_RWKV]
        ld = -math.exp(-0.5) * jax.nn.sigmoid(z)
        ag = jax.nn.sigmoid(a0_ref[d:d + 1, :] + lo[:, (2 + d) * D_RWKV:(3 + d) * D_RWKV])
        kd = xk * (1.0 + (ag - 1.0) * ka_ref[...])
        bb = kk * ag
        ksum = kd if ksum is None else ksum + kd
        tri = tril_ref[...] if d == 0 else triu_ref[...]
        h3 = _split3(ld)
        cs = _dot(tri, h3[0]) + _dot(tri, h3[1]) + _dot(tri, h3[2])
        e_in = jnp.exp(cs)
        e_ex = jnp.exp(cs - ld)
        e_neg = jnp.exp(-cs)
        bt = bb * e_neg
        kt = kd * e_neg
        edge = CHUNK - 1 if d == 0 else 0
        wrow = e_in[edge:edge + 1, :]
        wc_ref[0, c, :, d * D_RWKV:(d + 1) * D_RWKV] = wrow
        base = d * DIR_COLS
        for j, val in enumerate((-kk * e_ex, xr * e_in, bt, kt, bt * wrow, kt * wrow)):
            scr_ref[:, base + j * D_RWKV:base + (j + 1) * D_RWKV] = val.astype(BF16)

    bonus_ref[0, rows, :] = _head_sums(xr * ksum * rk_ref[...], ones) * xv


def _prep_chunk(c, scr_ref, rbar_ref, y0_ref, mx_ref, nn_ref):
    rows = slice(c * CHUNK, (c + 1) * CHUNK)
    chains = []
    for d in range(2):
        for pp in range(N_PAIR):
            cols = [d * DIR_COLS + j * D_RWKV + pp * PAIR for j in range(6)] + [2 * DIR_COLS + pp * PAIR]
            chains.append((d == 1,) + tuple(scr_ref[:, k:k + PAIR] for k in cols))
    for idx, (rbar, y0, mx, nn) in enumerate(_chunk_local(chains)):
        col = (idx // N_PAIR) * D_RWKV + (idx % N_PAIR) * PAIR
        rbar_ref[0, rows, col:col + PAIR] = rbar
        y0_ref[0, rows, col:col + PAIR] = y0
        mx_ref[0, c, :, col:col + PAIR] = mx
        nn_ref[0, c, :, col:col + PAIR] = nn


N_PREP_CONSTS = 11


def _prep_kernel(latent, tt, p_ref, *rest):
    if latent:
        prev_ref, next_ref = rest[:2]
        rest = rest[2:]
    (mu_ref, lw_ref, w0_ref, a0_ref, kk_ref, ka_ref, rk_ref, gup_ref, ones_ref, tril_ref,
     triu_ref) = rest[:N_PREP_CONSTS]
    rbar_ref, y0_ref, mx_ref, nn_ref, wc_ref, bonus_ref, g_ref = rest[N_PREP_CONSTS:N_PREP_CONSTS + 7]
    scr_a, scr_b, lo_ref, ext_ref = rest[N_PREP_CONSTS + 7:]
    if latent:
        i = pl.program_id(1)
        n = pl.num_programs(1)
        ext_ref[0:GRID_W] = jnp.where(i > 0, prev_ref[0], 0.0)
        ext_ref[GRID_W:GRID_W + tt] = p_ref[0]
        ext_ref[GRID_W + tt:] = jnp.where(i < n - 1, next_ref[0], 0.0)

        def shifted(r0, nrows, cols):
            shape = (nrows, cols.stop - cols.start)
            trow = lax.broadcasted_iota(jnp.int32, shape, 0) & (GRID_W - 1)
            q = lax.broadcasted_iota(jnp.int32, shape, 1) & 3
            left = jnp.where(trow == 0, 0.0, ext_ref[GRID_W - 1 + r0:GRID_W - 1 + r0 + nrows, cols])
            right = jnp.where(trow == GRID_W - 1, 0.0, ext_ref[GRID_W + 1 + r0:GRID_W + 1 + r0 + nrows, cols])
            up = ext_ref[r0:r0 + nrows, cols]
            down = ext_ref[2 * GRID_W + r0:2 * GRID_W + r0 + nrows, cols]
            return jnp.where(q == 0, left, jnp.where(q == 1, right, jnp.where(q == 2, up, down)))
    else:
        ext_ref[0:8] = jnp.zeros((8, RWKV_COLS), F32)
        ext_ref[8:8 + tt] = p_ref[0]
        ext_ref[8 + tt:] = jnp.zeros((8, RWKV_COLS), F32)

        def shifted(r0, nrows, cols):
            q = lax.broadcasted_iota(jnp.int32, (nrows, cols.stop - cols.start), 1) & 1
            return jnp.where(q == 0, ext_ref[7 + r0:7 + r0 + nrows, cols], ext_ref[9 + r0:9 + r0 + nrows, cols])

    tail = slice(3 * D_RWKV, RWKV_COLS)
    p_t = p_ref[0, :, tail]
    pm_t = p_t + mu_ref[:, tail] * (shifted(0, tt, tail) - p_t)
    xwa = pm_t[:, 0:LORA_W + LORA_A]
    lane = lax.broadcasted_iota(jnp.int32, xwa.shape, 1)
    lin = jnp.where(lane < LORA_W, jnp.tanh(xwa), xwa).astype(BF16)
    lo_ref[...] = _dot(lin, lw_ref[...])
    g_ref[0] = _dot(jax.nn.sigmoid(pm_t[:, LORA_W + LORA_A:]).astype(BF16), gup_ref[...])

    scr = (scr_a, scr_b)
    head = slice(0, 3 * D_RWKV)

    def rows_part(c):
        rows = slice(c * CHUNK, (c + 1) * CHUNK)
        _prep_rows(c, p_ref[0, rows, head], shifted(c * CHUNK, CHUNK, head), lo_ref[rows, :], mu_ref, w0_ref,
                   a0_ref, kk_ref, ka_ref, rk_ref, ones_ref, tril_ref, triu_ref, wc_ref, bonus_ref, scr[c % 2])

    nchunk = tt // CHUNK
    rows_part(0)
    for c in range(nchunk):
        if c + 1 < nchunk:
            rows_part(c + 1)
        _prep_chunk(c, scr[c % 2], rbar_ref, y0_ref, mx_ref, nn_ref)


def _prep(p_rw, consts, latent, tt):
    bsz, l, _ = p_rw.shape
    nc = l // CHUNK
    cpt = tt // CHUNK
    assert len(consts) == N_PREP_CONSTS
    const_specs = [pl.BlockSpec(c.shape, lambda b, i, nd=c.ndim: (0,) * nd) for c in consts]
    kern = functools.partial(_prep_kernel, latent, tt)
    if latent:
        nblk = l // GRID_W
        in_specs = [pl.BlockSpec((1, tt, RWKV_COLS), lambda b, i: (b, i, 0)),
                    pl.BlockSpec((1, GRID_W, RWKV_COLS),
                                 lambda b, i: (b, jnp.maximum(i * (tt // GRID_W) - 1, 0), 0)),
                    pl.BlockSpec((1, GRID_W, RWKV_COLS),
                                 lambda b, i: (b, jnp.minimum((i + 1) * (tt // GRID_W), nblk - 1), 0))]
        args = (p_rw, p_rw, p_rw)
        ext_rows = tt + 2 * GRID_W
    else:
        assert tt == l
        in_specs = [pl.BlockSpec((1, tt, RWKV_COLS), lambda b, i: (b, i, 0))]
        args = (p_rw,)
        ext_rows = tt + 16
    row = lambda b, i: (b, i, 0)
    chunk = lambda b, i: (b, i, 0, 0)
    w2 = 2 * D_RWKV
    return pl.pallas_call(
        kern,
        grid=(bsz, l // tt),
        in_specs=in_specs + const_specs,
        out_specs=[pl.BlockSpec((1, tt, w2), row), pl.BlockSpec((1, tt, w2), row),
                   pl.BlockSpec((1, cpt, CHUNK, w2), chunk), pl.BlockSpec((1, cpt, CHUNK, w2), chunk),
                   pl.BlockSpec((1, cpt, 1, w2), chunk),
                   pl.BlockSpec((1, tt, D_RWKV), row), pl.BlockSpec((1, tt, D_RWKV), row)],
        out_shape=[jax.ShapeDtypeStruct((bsz, l, w2), BF16), jax.ShapeDtypeStruct((bsz, l, w2), F32),
                   jax.ShapeDtypeStruct((bsz, nc, CHUNK, w2), BF16), jax.ShapeDtypeStruct((bsz, nc, CHUNK, w2), F32),
                   jax.ShapeDtypeStruct((bsz, nc, 1, w2), F32),
                   jax.ShapeDtypeStruct((bsz, l, D_RWKV), F32), jax.ShapeDtypeStruct((bsz, l, D_RWKV), F32)],
        scratch_shapes=[pltpu.VMEM((CHUNK, SCR_COLS), BF16), pltpu.VMEM((CHUNK, SCR_COLS), BF16),
                        pltpu.VMEM((tt, 4 * D_RWKV), F32), pltpu.VMEM((ext_rows, RWKV_COLS), F32)],
        compiler_params=_cparams(("parallel", "parallel")),
        name="prep_latent" if latent else "prep_ctx",
    )(*args, *consts)


SCAN_CHUNKS = 4
SCAN_BATCH = 2


def _scan_kernel(rbf_ref, rbb_ref, y0f_ref, y0b_ref, mxf_ref, mxb_ref, nnf_ref, nnb_ref, wcf_ref, wcb_ref,
                 s0_ref, yf_ref, yb_ref, s_ref):
    i = pl.program_id(1)

    @pl.when(i == 0)
    def _():
        s_ref[...] = s0_ref[...]

    c = CHUNK
    h0 = lax.broadcasted_iota(jnp.int32, (c, PAIR), 1) < HEAD
    row = lax.broadcasted_iota(jnp.int32, (PAIR, PAIR), 0)
    col = lax.broadcasted_iota(jnp.int32, (PAIR, PAIR), 1)
    same = (row >= c) == (col >= c)
    zb = jnp.zeros((c, PAIR), BF16)
    dirs = ((rbf_ref, y0f_ref, mxf_ref, nnf_ref, wcf_ref, yf_ref), (rbb_ref, y0b_ref, mxb_ref, nnb_ref, wcb_ref, yb_ref))
    idx = [(n, d, p) for n in range(SCAN_BATCH) for d in range(2) for p in range(N_PAIR)]
    lanes = [slice(p * PAIR, (p + 1) * PAIR) for _, _, p in idx]
    s = [s_ref[n, d, p] for n, d, p in idx]
    for q in range(SCAN_CHUNKS):
        cq = (q, SCAN_CHUNKS - 1 - q)
        rows = [slice(cq[d] * c, (cq[d] + 1) * c) for _, d, _ in idx]
        sb = [_bf(x) for x in s]
        rbar = [dirs[d][0][n, rows[k], lanes[k]] for k, (n, d, _) in enumerate(idx)]
        r_st = [jnp.concatenate([jnp.where(h0, x, zb), jnp.where(h0, zb, x)], axis=0) for x in rbar]
        y_st = [_dot_nt(r_st[k], sb[k]) for k in range(len(idx))]
        m_bd = [jnp.where(same, jnp.concatenate([x, x], axis=0), jnp.zeros((PAIR, PAIR), BF16))
                for x in (dirs[d][2][n, cq[d], :, lanes[k]] for k, (n, d, _) in enumerate(idx))]
        sm = [_dot(sb[k], m_bd[k]) for k in range(len(idx))]
        s_next = []
        for k, (n, d, p) in enumerate(idx):
            nn = dirs[d][3][n, cq[d], :, lanes[k]]
            n_bd = jnp.where(same, jnp.concatenate([nn, nn], axis=0), 0.0)
            dirs[d][5][n, rows[k], lanes[k]] = y_st[k][:c] + y_st[k][c:] + dirs[d][1][n, rows[k], lanes[k]]
            s_next.append(s[k] * dirs[d][4][n, cq[d], :, lanes[k]] + sm[k] + n_bd)
        s = s_next
    for k, (n, d, p) in enumerate(idx):
        s_ref[n, d, p] = s[k]


def _scan(rbar, y0, mx, nn, wc, s0):
    bsz, l, _ = rbar.shape
    sub, nb = SCAN_CHUNKS, SCAN_BATCH
    nc = l // (CHUNK * sub)
    row_f = pl.BlockSpec((nb, sub * CHUNK, D_RWKV), lambda b, i: (b, i, 0))
    row_b = pl.BlockSpec((nb, sub * CHUNK, D_RWKV), lambda b, i: (b, nc - 1 - i, 1))
    chk_f = pl.BlockSpec((nb, sub, CHUNK, D_RWKV), lambda b, i: (b, i, 0, 0))
    chk_b = pl.BlockSpec((nb, sub, CHUNK, D_RWKV), lambda b, i: (b, nc - 1 - i, 0, 1))
    wc_f = pl.BlockSpec((nb, sub, 1, D_RWKV), lambda b, i: (b, i, 0, 0))
    wc_b = pl.BlockSpec((nb, sub, 1, D_RWKV), lambda b, i: (b, nc - 1 - i, 0, 1))
    st_spec = pl.BlockSpec((nb, 2, N_PAIR, PAIR, PAIR), lambda b, i: (b, 0, 0, 0, 0))
    return pl.pallas_call(
        _scan_kernel,
        grid=(bsz // nb, nc),
        in_specs=[row_f, row_b, row_f, row_b, chk_f, chk_b, chk_f, chk_b, wc_f, wc_b, st_spec],
        out_specs=[row_f, pl.BlockSpec((nb, sub * CHUNK, D_RWKV), lambda b, i: (b, nc - 1 - i, 0)), st_spec],
        out_shape=[jax.ShapeDtypeStruct((bsz, l, D_RWKV), F32),
                   jax.ShapeDtypeStruct((bsz, l, D_RWKV), F32),
                   jax.ShapeDtypeStruct((bsz, 2, N_PAIR, PAIR, PAIR), F32)],
        compiler_params=_cparams(("parallel", "arbitrary")),
        name="scan",
    )(rbar, rbar, y0, y0, mx, mx, nn, nn, wc, wc, s0)


OUT_ROWS = 128


def _out_kernel(tt, yf_ref, yb_ref, bonus_ref, g_ref, cv_ref, cvp_ref, cvn_ref, x_ref, g1_ref, sh2_ref,
                sc2_ref, n2g_ref, gng_ref, gnb_ref, convw_ref, wout_ref, rwh_ref, rwl_ref, avg_ref,
                xm_ref, hx_ref, aff_ref):
    i = pl.program_id(1)
    n = pl.num_programs(1)

    cv = cv_ref[0]
    b_gate = cv[:, 0:D_CONV]
    cu = cv[:, D_CONV:2 * D_CONV] * cv[:, 2 * D_CONV:]
    cvp = cvp_ref[0]
    cvn = cvn_ref[0]
    cu_prev = jnp.where(i > 0, cvp[7:8, D_CONV:2 * D_CONV] * cvp[7:8, 2 * D_CONV:], 0.0)
    cu_next = jnp.where(i < n - 1, cvn[0:1, D_CONV:2 * D_CONV] * cvn[0:1, 2 * D_CONV:], 0.0)
    ridx = lax.broadcasted_iota(jnp.int32, cu.shape, 0)
    cu_m1 = jnp.where(ridx == 0, cu_prev, pltpu.roll(cu, 1, 0))
    cu_p1 = jnp.where(ridx == tt - 1, cu_next, pltpu.roll(cu, tt - 1, 0))
    conv = convw_ref[0:1, :] * cu_m1 + convw_ref[1:2, :] * cu + convw_ref[2:3, :] * cu_p1
    bx = (b_gate * conv).astype(BF16)

    parts = [slice(k * OUT_ROWS, (k + 1) * OUT_ROWS) for k in range(tt // OUT_ROWS)]
    avg = avg_ref[...]
    rwh = rwh_ref[...]
    rwl = rwl_ref[...]
    y = [yf_ref[0, r, :] + yb_ref[0, r, :] for r in parts]
    mu = [_seg_dot(v, avg) for v in y]
    dlt = [a - b for a, b in zip(y, mu)]
    var = [_seg_dot(v * v, avg) for v in dlt]
    yn = [a * lax.rsqrt(b + GN_EPS) * gng_ref[...] + gnb_ref[...] for a, b in zip(dlt, var)]
    ax = [((a + bonus_ref[0, r, :]) * g_ref[0, r, :]).astype(BF16) for a, r in zip(yn, parts)]
    mix = [_dot(a, wout_ref[0:D_RWKV, :]) + _dot(bx[r], wout_ref[D_RWKV:, :]) for a, r in zip(ax, parts)]
    xm = [x_ref[0, r, :] + g1_ref[0] * a for a, r in zip(mix, parts)]
    ms = [jnp.mean(v * v, axis=-1, keepdims=True) for v in xm]
    hx = [a * lax.rsqrt(b + NORM_EPS) * n2g_ref[...] for a, b in zip(xm, ms)]
    hx = [_split2(v * (1.0 + sc2_ref[0]) + sh2_ref[0]) for v in hx]
    logits = [_dot_nt(rwh, hi) + _dot_nt(rwh, lo) + _dot_nt(rwl, hi) for hi, lo in hx]
    for k, r in enumerate(parts):
        xm_ref[0, r, :] = xm[k]
        hx_ref[0, r, :] = hx[k][0]
        m = jnp.max(logits[k], axis=0, keepdims=True)
        ex = jnp.exp(logits[k] - m)
        aff_ref[0, :, r] = ex / jnp.sum(ex, axis=0, keepdims=True)


def _out(yf, yb, bonus, g, p_cv, x, g1, sh2, sc2, n2g, gng, gnb, convw, wout, rwh, rwl, avg, tt):
    bsz, t, d = x.shape
    ne = rwh.shape[0]
    nb8 = t // 8
    row = lambda b, i: (b, i, 0)
    per_b = lambda b, i: (b, 0, 0)
    const2 = lambda b, i: (0, 0)
    return pl.pallas_call(
        functools.partial(_out_kernel, tt),
        grid=(bsz, t // tt),
        in_specs=[pl.BlockSpec((1, tt, D_RWKV), row), pl.BlockSpec((1, tt, D_RWKV), row),
                  pl.BlockSpec((1, tt, D_RWKV), row), pl.BlockSpec((1, tt, D_RWKV), row),
                  pl.BlockSpec((1, tt, CONV_COLS), row),
                  pl.BlockSpec((1, 8, CONV_COLS), lambda b, i: (b, jnp.maximum(i * (tt // 8) - 1, 0), 0)),
                  pl.BlockSpec((1, 8, CONV_COLS), lambda b, i: (b, jnp.minimum((i + 1) * (tt // 8), nb8 - 1), 0)),
                  pl.BlockSpec((1, tt, d), row),
                  pl.BlockSpec((1, 1, d), per_b), pl.BlockSpec((1, 1, d), per_b), pl.BlockSpec((1, 1, d), per_b),
                  pl.BlockSpec((1, d), const2), pl.BlockSpec((1, D_RWKV), const2), pl.BlockSpec((1, D_RWKV), const2),
                  pl.BlockSpec((3, D_CONV), const2), pl.BlockSpec((D_RWKV + D_CONV, d), const2),
                  pl.BlockSpec((ne, d), const2), pl.BlockSpec((ne, d), const2),
                  pl.BlockSpec((D_RWKV, D_RWKV), const2)],
        out_specs=[pl.BlockSpec((1, tt, d), row), pl.BlockSpec((1, tt, d), row),
                   pl.BlockSpec((1, ne, tt), lambda b, i: (b, 0, i))],
        out_shape=[jax.ShapeDtypeStruct((bsz, t, d), F32), jax.ShapeDtypeStruct((bsz, t, d), BF16),
                   jax.ShapeDtypeStruct((bsz, ne, t), F32)],
        compiler_params=_cparams(("parallel", "parallel")),
        name="out",
    )(yf, yb, bonus, g, p_cv, p_cv, p_cv, x, g1, sh2, sc2, n2g, gng, gnb, convw, wout, rwh, rwl, avg)


def _prefix_blocks(mask_fn, t, tri, emit):
    carry = None
    for j in range(t // 128):
        m = mask_fn(j)
        inc = _dot(m.astype(BF16), tri)
        carry = jnp.zeros_like(inc[:, 0:1]) if carry is None else carry
        emit(j, m, inc - m + carry)
        carry = carry + inc[:, 127:128]


def _topk_kernel(cap, aff_ref, tri_ref, cnt_ref, slot_ref):
    t = aff_ref.shape[2]
    aff = aff_ref[0]

    def body(k, bits):
        cand = bits | jnp.left_shift(jnp.int32(1), 30 - k)
        cnt = jnp.sum(jnp.where(aff >= pltpu.bitcast(cand, F32), 1, 0), axis=-1, keepdims=True)
        return jnp.where(cnt >= cap, cand, bits)

    bits = lax.fori_loop(0, 31, body, jnp.zeros((aff.shape[0], 1), jnp.int32))
    thr = pltpu.bitcast(bits, F32)
    above = pltpu.bitcast(bits + 1, F32)
    n_gt = jnp.sum(jnp.where(aff >= above, 1, 0), axis=-1, keepdims=True)
    need = (cap - n_gt).astype(F32)
    tri = tri_ref[...]

    def blk(j):
        return aff[:, j * 128:(j + 1) * 128]

    def emit_sel(j, eq, before):
        take = (blk(j) >= above) | ((eq > 0.5) & (before < need))
        slot_ref[0, :, j * 128:(j + 1) * 128] = jnp.where(take, 1, 0)

    _prefix_blocks(lambda j: jnp.where((blk(j) >= thr) & (blk(j) < above), 1.0, 0.0), t, tri, emit_sel)

    def emit_slot(j, m, before):
        count = before.astype(jnp.int32)
        cnt_ref[0, :, j * 128:(j + 1) * 128] = count
        slot_ref[0, :, j * 128:(j + 1) * 128] = jnp.where(m > 0.5, count, -1)

    _prefix_blocks(lambda j: slot_ref[0, :, j * 128:(j + 1) * 128].astype(F32), t, tri, emit_slot)


def _topk(aff_t, tri, cap):
    bsz, ne, t = aff_t.shape
    spec = pl.BlockSpec((1, ne, t), lambda b: (b, 0, 0))
    return pl.pallas_call(
        functools.partial(_topk_kernel, cap),
        grid=(bsz,),
        in_specs=[spec, pl.BlockSpec((128, 128), lambda b: (0, 0))],
        out_specs=[spec, spec],
        out_shape=[jax.ShapeDtypeStruct((bsz, ne, t), jnp.int32)] * 2,
        compiler_params=_cparams(("parallel",)),
        name="topk",
    )(aff_t, tri)


def _slot_block_range(lo, hi, sb):
    shift = sb.bit_length() - 1
    assert sb == 1 << shift
    first = lo >> shift
    return first, jnp.where(hi > lo, ((hi - 1) >> shift) + 1, first)


def _moe_kernel(tk, sb, tsp_ref, hx_ref, slot_ref, wg_ref, wu_ref, wd_ref, ye_ref, xs_ref):
    b = pl.program_id(0)
    e = pl.program_id(1)
    ne = pl.num_programs(1)
    nt = hx_ref.shape[1] // tk
    base = (b * ne + e) * (nt + 1)
    xs_ref[...] = jnp.zeros_like(xs_ref)

    def tile_body(j, carry):
        t0 = pl.multiple_of(j * tk, tk)
        hxt = hx_ref[0, pl.ds(t0, tk), :]
        slot_t = slot_ref[0, 0, pl.ds(j, 1), :]

        def sb_body(s, c2):
            s0 = pl.multiple_of(s * sb, sb)
            slot = lax.broadcasted_iota(jnp.int32, (sb, tk), 0) + s0
            onehot = jnp.where(slot_t == slot, 1.0, 0.0).astype(BF16)
            xs_ref[pl.ds(s0, sb), :] += _dot(onehot, hxt)
            return c2

        first, last = _slot_block_range(tsp_ref[base + j], tsp_ref[base + j + 1], sb)
        lax.fori_loop(first, last, sb_body, 0)
        return carry

    lax.fori_loop(0, nt, tile_body, 0)
    xs = xs_ref[...].astype(BF16)
    h1 = _dot(xs, wg_ref[0].astype(BF16))
    h2 = _dot(xs, wu_ref[0].astype(BF16))
    hid = (h1 * jax.nn.sigmoid(h1) * h2).astype(BF16)
    ye_ref[0, 0] = _dot(hid, wd_ref[0].astype(BF16)).astype(BF16)


def _moe(tsp, hx, slot4, wg, wu, wd, cap, tk, sb):
    bsz, t, d = hx.shape
    ne, _, f = wg.shape
    nt = t // tk
    grid_spec = pltpu.PrefetchScalarGridSpec(
        num_scalar_prefetch=1,
        grid=(bsz, ne),
        in_specs=[pl.BlockSpec((1, t, d), lambda b, e, s: (b, 0, 0), pipeline_mode=pl.Buffered(1)),
                  pl.BlockSpec((1, 1, nt, tk), lambda b, e, s: (b, e, 0, 0)),
                  pl.BlockSpec((1, d, f), lambda b, e, s: (e, 0, 0)),
                  pl.BlockSpec((1, d, f), lambda b, e, s: (e, 0, 0)),
                  pl.BlockSpec((1, f, d), lambda b, e, s: (e, 0, 0))],
        out_specs=pl.BlockSpec((1, 1, cap, d), lambda b, e, s: (b, e, 0, 0)),
        scratch_shapes=[pltpu.VMEM((cap, d), F32)],
    )
    return pl.pallas_call(
        functools.partial(_moe_kernel, tk, sb),
        grid_spec=grid_spec,
        out_shape=jax.ShapeDtypeStruct((bsz, ne, cap, d), BF16),
        compiler_params=_cparams(("parallel", "arbitrary")),
        name="moe",
    )(tsp, hx, slot4, wg, wu, wd)


COMB_ROWS = 128
COMB_COLS = 256
SLOT_ALIGN = 16


def _comb_kernel(tk, win, tsp_ref, xm_ref, ye_ref, slotc_ref, affc_ref, g2_ref, fg_ref, o_ref):
    b = pl.program_id(0)
    j = pl.program_id(1)
    nh = tk // COMB_ROWS
    ntile = pl.num_programs(1) * nh
    ne, cap, d = ye_ref.shape[1], ye_ref.shape[2], ye_ref.shape[3]
    slotc = slotc_ref[0]
    affc = affc_ref[0]
    lane_slot = lax.broadcasted_iota(jnp.int32, (COMB_ROWS, win), 1)
    for h in range(nh):
        rows = slice(h * COMB_ROWS, (h + 1) * COMB_ROWS)
        starts, onehots, vals = [], [], []
        for e in range(ne):
            first = tsp_ref[(b * ne + e) * ntile + j * nh + h]
            start = pl.multiple_of(jnp.minimum(first & ~(SLOT_ALIGN - 1), cap - win), SLOT_ALIGN)
            starts.append(start)
            onehots.append(jnp.where(slotc[rows, e:e + 1] - start == lane_slot, 1.0, 0.0).astype(BF16))
            vals.append(jnp.broadcast_to(affc[rows, e:e + 1], (COMB_ROWS, COMB_COLS)))
        xo, ssq = [], None
        for q in range(d // COMB_COLS):
            cols = slice(q * COMB_COLS, (q + 1) * COMB_COLS)
            acc = None
            for e in range(ne):
                part = vals[e] * _dot(onehots[e], ye_ref[0, e, pl.ds(starts[e], win), cols])
                acc = part if acc is None else acc + part
            x = xm_ref[0, rows, cols] + g2_ref[0, :, cols] * acc
            xo.append(x)
            sq = jnp.sum(x * x, axis=-1, keepdims=True)
            ssq = sq if ssq is None else ssq + sq
        scale = lax.rsqrt(ssq / d + NORM_EPS)
        for q in range(d // COMB_COLS):
            cols = slice(q * COMB_COLS, (q + 1) * COMB_COLS)
            o_ref[0, rows, cols] = xo[q] * scale * fg_ref[:, cols]


def _comb(tsp, xm, ye, slotc, affc, g2, fg, tk):
    bsz, t, d = xm.shape
    ne, cap = ye.shape[1], ye.shape[2]
    win = min(2 * COMB_ROWS, cap)
    assert win == cap or win >= COMB_ROWS + SLOT_ALIGN
    grid_spec = pltpu.PrefetchScalarGridSpec(
        num_scalar_prefetch=1,
        grid=(bsz, t // tk),
        in_specs=[pl.BlockSpec((1, tk, d), lambda b, j, s: (b, j, 0)),
                  pl.BlockSpec((1, ne, cap, d), lambda b, j, s: (b, 0, 0, 0)),
                  pl.BlockSpec((1, tk, ne), lambda b, j, s: (b, j, 0)),
                  pl.BlockSpec((1, tk, ne), lambda b, j, s: (b, j, 0)),
                  pl.BlockSpec((1, 1, d), lambda b, j, s: (b, 0, 0)),
                  pl.BlockSpec((1, d), lambda b, j, s: (0, 0))],
        out_specs=pl.BlockSpec((1, tk, d), lambda b, j, s: (b, j, 0)),
    )
    return pl.pallas_call(
        functools.partial(_comb_kernel, tk, win),
        grid_spec=grid_spec,
        out_shape=jax.ShapeDtypeStruct((bsz, t, d), F32),
        compiler_params=_cparams(("parallel", "arbitrary")),
        name="comb",
    )(tsp, xm, ye, slotc, affc, g2, fg)


def _block_diag_ones(n, blk, value=1.0):
    r = jnp.arange(n)
    return jnp.where((r[:, None] // blk) == (r[None, :] // blk), value, 0.0)


def kernel(x, c, ctx, c_ctx, ada_w, ada_b, norm1_g, norm2_g, w_in, shift_mu, w0, w_lora_up, a0, a_lora_up, k_k, k_a,
           r_k, g_lora_up, gn_g, gn_b, conv_w, w_out, router_w, exp_w_gate, exp_w_up, exp_w_down, final_g):
    bsz, t, d = x.shape
    lc = ctx.shape[1]
    ne = router_w.shape[-1]
    cap = EC_CAPACITY * t // ne
    tt = 256
    tk = 1024
    sb = min(128, cap)
    l = 0

    rows = ((bsz + 1 + 7) // 8) * 8
    cc = jnp.zeros((rows, d), F32).at[:bsz].set(c).at[bsz].set(c_ctx)
    mod = _mod(cc, ada_w[l], ada_b[l][None, :])
    sh1, sc1, g1, sh2, sc2, g2 = (m[:, None, :] for m in jnp.split(mod[:bsz], 6, axis=-1))
    csh1, csc1 = (jnp.broadcast_to(m[None, None, :], (bsz, 1, d)) for m in jnp.split(mod[bsz], 6)[:2])

    w_rw = w_in[l][:, :RWKV_COLS].astype(BF16)
    w_cv = w_in[l][:, RWKV_COLS:].astype(BF16)
    n1g = norm1_g[l][None, :]
    px_rw, px_cv = _in_proj(x, sh1, sc1, n1g, w_rw, w_cv, 512)
    pc_rw, _ = _in_proj(ctx, csh1, csc1, n1g, w_rw, w_cv, 256)

    zw = jnp.zeros((LORA_W, 2 * D_RWKV), F32)
    lora = jnp.concatenate([
        jnp.concatenate([w_lora_up[l, 0], w_lora_up[l, 1], zw], axis=1),
        jnp.concatenate([zw, a_lora_up[l, 0], a_lora_up[l, 1]], axis=1)], axis=0).astype(BF16)
    ridx = jnp.arange(CHUNK)
    tril = jnp.where(ridx[None, :] <= ridx[:, None], 1.0, 0.0).astype(BF16)
    triu = jnp.where(ridx[None, :] >= ridx[:, None], 1.0, 0.0).astype(BF16)
    ones_bd = _block_diag_ones(PAIR, HEAD).astype(BF16)
    consts = (shift_mu[l][None, :], lora, w0[l], a0[l], k_k[l][None, :], k_a[l][None, :],
              r_k[l].reshape(1, D_RWKV), g_lora_up[l].astype(BF16), ones_bd, tril, triu)

    chunks_c = _prep(pc_rw, consts, False, lc)[:5]
    *chunks_x, bonus, gate = _prep(px_rw, consts, True, tt)

    s_zero = jnp.zeros((bsz, 2, N_PAIR, PAIR, PAIR), F32)
    _, _, s_ctx = _scan(*chunks_c, s_zero)
    yf, yb, _ = _scan(*chunks_x, s_ctx)

    rw_t = router_w[l].T
    rwh = rw_t.astype(BF16)
    rwl = (rw_t - rwh.astype(F32)).astype(BF16)
    avg = _block_diag_ones(D_RWKV, HEAD, 1.0 / HEAD).astype(BF16)
    xm, hx, aff_t = _out(yf, yb, bonus, gate, px_cv, x, g1, sh2, sc2, norm2_g[l][None, :], gn_g[l][None, :],
                         gn_b[l][None, :], conv_w[l], w_out[l].astype(BF16), rwh, rwl, avg, 2 * tt)

    r128 = jnp.arange(128)
    tri128 = jnp.where(r128[:, None] <= r128[None, :], 1.0, 0.0).astype(BF16)
    cnt, slot = _topk(aff_t, tri128, cap)

    nt = t // tk
    tsp = jnp.concatenate([cnt[:, :, ::tk], jnp.full((bsz, ne, 1), cap, jnp.int32)], axis=-1).reshape(-1)
    ye = _moe(tsp, hx, slot.reshape(bsz, ne, nt, tk),
              exp_w_gate[l], exp_w_up[l], exp_w_down[l], cap, tk, sb)
    tr = lambda a: jnp.transpose(a, (0, 2, 1))
    first_slot = cnt[:, :, ::COMB_ROWS].reshape(-1)
    return _comb(first_slot, xm, ye, tr(slot), tr(aff_t), g2, final_g[None, :], tt)
```

```python
import functools
import math

import jax
import jax.numpy as jnp
from jax import lax
from jax.experimental import pallas as pl
from jax.experimental.pallas import tpu as pltpu

F32 = jnp.float32
BF16 = jnp.bfloat16
HIGHEST = lax.Precision.HIGHEST

GRID_W = 64
D_RWKV = 512
D_CONV = 512
HEAD = 64
LORA_W = 64
LORA_A = 64
LORA_G = 128
N_EXPERTS = 16
EC_CAPACITY = 2
NORM_EPS = 1e-6
GN_EPS = 64e-5
RWKV_COLS = 3 * D_RWKV + LORA_W + LORA_A + LORA_G
CONV_COLS = 3 * D_CONV

CHUNK = 64
PAIR = 2 * HEAD
N_PAIR = D_RWKV // PAIR
VMEM_LIMIT = 48 * 1024 * 1024


def _cparams(sem):
    return pltpu.CompilerParams(dimension_semantics=sem, vmem_limit_bytes=VMEM_LIMIT)


def _dot(a, b):
    return jnp.dot(a, b, preferred_element_type=F32)


def _dot_nt(a, b):
    return lax.dot_general(a, b, (((1,), (1,)), ((), ())), preferred_element_type=F32)


def _split2(x):
    hi = x.astype(BF16)
    lo = (x - hi.astype(F32)).astype(BF16)
    return hi, lo


def _split3(x):
    hi = x.astype(BF16)
    r = x - hi.astype(F32)
    mid = r.astype(BF16)
    lo = (r - mid.astype(F32)).astype(BF16)
    return hi, mid, lo


def _seg_dot(x, m):
    hi, lo = _split2(x)
    return _dot(hi, m) + _dot(lo, m)


def _mod_kernel(c_ref, w_ref, b_ref, o_ref):
    c = c_ref[...]
    s = c * jax.nn.sigmoid(c)
    o_ref[...] = jnp.dot(s, w_ref[...], precision=HIGHEST, preferred_element_type=F32) + b_ref[...]


def _mod(cc, w, b):
    rows, d = cc.shape
    n = w.shape[1]
    tn = 1024
    return pl.pallas_call(
        _mod_kernel,
        grid=(n // tn,),
        in_specs=[pl.BlockSpec((rows, d), lambda j: (0, 0)),
                  pl.BlockSpec((d, tn), lambda j: (0, j)),
                  pl.BlockSpec((1, tn), lambda j: (0, j))],
        out_specs=pl.BlockSpec((rows, tn), lambda j: (0, j)),
        out_shape=jax.ShapeDtypeStruct((rows, n), F32),
        compiler_params=_cparams(("parallel",)),
        name="mod",
    )(cc, w, b)


def _in_proj_kernel(x_ref, sh_ref, sc_ref, g_ref, wrw_ref, wcv_ref, orw_ref, ocv_ref):
    x = x_ref[0]
    ms = jnp.mean(x * x, axis=-1, keepdims=True)
    h = x * lax.rsqrt(ms + NORM_EPS) * g_ref[...]
    h = (h * (1.0 + sc_ref[0]) + sh_ref[0]).astype(BF16)
    orw_ref[0] = _dot(h, wrw_ref[...])
    ocv_ref[0] = _dot(h, wcv_ref[...])


def _in_proj(x, sh, sc, g, w_rw, w_cv, tm):
    bsz, l, d = x.shape
    return pl.pallas_call(
        _in_proj_kernel,
        grid=(bsz, l // tm),
        in_specs=[pl.BlockSpec((1, tm, d), lambda b, i: (b, i, 0)),
                  pl.BlockSpec((1, 1, d), lambda b, i: (b, 0, 0)),
                  pl.BlockSpec((1, 1, d), lambda b, i: (b, 0, 0)),
                  pl.BlockSpec((1, d), lambda b, i: (0, 0)),
                  pl.BlockSpec((d, RWKV_COLS), lambda b, i: (0, 0)),
                  pl.BlockSpec((d, CONV_COLS), lambda b, i: (0, 0))],
        out_specs=[pl.BlockSpec((1, tm, RWKV_COLS), lambda b, i: (b, i, 0)),
                   pl.BlockSpec((1, tm, CONV_COLS), lambda b, i: (b, i, 0))],
        out_shape=[jax.ShapeDtypeStruct((bsz, l, RWKV_COLS), F32),
                   jax.ShapeDtypeStruct((bsz, l, CONV_COLS), F32)],
        compiler_params=_cparams(("parallel", "parallel")),
        name="in_proj",
    )(x, sh, sc, g, w_rw, w_cv)


def _bf(x):
    return x.astype(BF16)


def _chunk_local(chains):
    c = CHUNK
    n = range(len(chains))
    lane = lax.broadcasted_iota(jnp.int32, (c, PAIR), 1)
    h0 = lane < HEAD
    row = lax.broadcasted_iota(jnp.int32, (PAIR, PAIR), 0)
    col = lax.broadcasted_iota(jnp.int32, (PAIR, PAIR), 1)
    same = (row >= c) == (col >= c)
    tr, tc = row & (c - 1), col & (c - 1)
    eye = jnp.where(row == col, 1.0, 0.0)
    masks = {rev: (same & ((tc > tr) if rev else (tc < tr)), same & ((tc >= tr) if rev else (tc <= tr)))
             for rev in (False, True)}
    zb = jnp.zeros((c, PAIR), BF16)

    def stack(x):
        return jnp.concatenate([jnp.where(h0, x, zb), jnp.where(h0, zb, x)], axis=0)

    a_st = [stack(ch[1]) for ch in chains]
    r_st = [stack(ch[2]) for ch in chains]
    gram = [_dot_nt(jnp.concatenate([a_st[i], r_st[i]], axis=0),
                    jnp.concatenate([chains[i][3], chains[i][3], chains[i][4], chains[i][4]], axis=0)) for i in n]
    lab = [jnp.where(masks[chains[i][0]][0], gram[i][:PAIR, :PAIR], 0.0) for i in n]
    lak = [jnp.where(masks[chains[i][0]][0], gram[i][:PAIR, PAIR:], 0.0) for i in n]
    mrb = [jnp.where(masks[chains[i][0]][1], gram[i][PAIR:, :PAIR], 0.0) for i in n]
    mrk = [jnp.where(masks[chains[i][0]][1], gram[i][PAIR:, PAIR:], 0.0) for i in n]

    labb = [_bf(x) for x in lab]
    pw = [_dot(labb[i], labb[i]) for i in n]
    tp = [eye + lab[i] for i in n]
    for _ in range(4):
        pwb = [_bf(x) for x in pw]
        both = [_dot(pwb[i], jnp.concatenate([pwb[i], _bf(tp[i])], axis=1)) for i in n]
        pw = [x[:, :PAIR] for x in both]
        tp = [tp[i] + both[i][:, PAIR:] for i in n]
    tinv = [tp[i] + _dot(_bf(pw[i]), _bf(tp[i])) for i in n]

    v_rep = [jnp.concatenate([ch[7], ch[7]], axis=0) for ch in chains]
    lakv = [_dot(_bf(lak[i]), v_rep[i]) for i in n]
    x = [_dot(_bf(tinv[i]), jnp.concatenate([a_st[i], _bf(lakv[i])], axis=1)) for i in n]
    abar = [xi[:, :PAIR] for xi in x]
    u0 = [jnp.where(same, xi[:, PAIR:], 0.0) for xi in x]
    au = [jnp.concatenate([abar[i], u0[i]], axis=1) for i in n]
    aub = [_bf(x) for x in au]
    z = [_dot(_bf(mrb[i]), aub[i]) for i in n]
    zv = [_dot(_bf(mrk[i]), v_rep[i]) for i in n]
    rbar = [r_st[i].astype(F32) + z[i][:, :PAIR] for i in n]
    rbar = [_bf(x[:c] + x[c:]) for x in rbar]
    y0 = [z[i][:, PAIR:] + zv[i] for i in n]
    y0 = [jnp.where(h0, x[:c], x[c:]) for x in y0]

    mn = [_dot(_bf(au[i].T), jnp.concatenate([chains[i][5], chains[i][5]], axis=0)) for i in n]
    vk = [_dot(_bf(chains[i][7].astype(F32).T), chains[i][6]) for i in n]
    mx = [_bf(jnp.where(h0, m[:c], m[c:PAIR])) for m in mn]
    nn = [mn[i][PAIR:] + vk[i] for i in n]
    nn = [jnp.where(h0, x[:c], x[c:]) for x in nn]
    return list(zip(rbar, y0, mx, nn))


DIR_COLS = 6 * D_RWKV
SCR_COLS = 2 * DIR_COLS + D_RWKV


def _head_sums(x, ones):
    return jnp.concatenate([_seg_dot(x[:, g * PAIR:(g + 1) * PAIR], ones) for g in range(N_PAIR)], axis=1)


def _prep_rows(c, p, shifted, lo, mu_ref, w0_ref, a0_ref, kk_ref, ka_ref, rk_ref, ones_ref, tril_ref, triu_ref,
               wc_ref, bonus_ref, scr_ref):
    rows = slice(c * CHUNK, (c + 1) * CHUNK)
    pm = p + mu_ref[:, 0:3 * D_RWKV] * (shifted - p)
    xr = pm[:, 0:D_RWKV]
    xk = pm[:, D_RWKV:2 * D_RWKV]
    xv = pm[:, 2 * D_RWKV:3 * D_RWKV]

    ones = ones_ref[...]
    kraw = xk * kk_ref[...]
    kk = kraw / jnp.maximum(jnp.sqrt(_head_sums(kraw * kraw, ones)), 1e-12)

    scr_ref[:, 2 * DIR_COLS:] = xv.astype(BF16)
    ksum = None
    for d in range(2):
        z = w0_ref[d:d + 1, :] + lo[:, d * D_RWKV:(d + 1) * D_RWKV]
        ld = -math.exp(-0.5) * jax.nn.sigmoid(z)
        ag = jax.nn.sigmoid(a0_ref[d:d + 1, :] + lo[:, (2 + d) * D_RWKV:(3 + d) * D_RWKV])
        kd = xk * (1.0 + (ag - 1.0) * ka_ref[...])
        bb = kk * ag
        ksum = kd if ksum is None else ksum + kd
        tri = tril_ref[...] if d == 0 else triu_ref[...]
        h3 = _split3(ld)
        cs = _dot(tri, h3[0]) + _dot(tri, h3[1]) + _dot(tri, h3[2])
        e_in = jnp.exp(cs)
        e_ex = jnp.exp(cs - ld)
        e_neg = jnp.exp(-cs)
        bt = bb * e_neg
        kt = kd * e_neg
        edge = CHUNK - 1 if d == 0 else 0
        wrow = e_in[edge:edge + 1, :]
        wc_ref[0, c, :, d * D_RWKV:(d + 1) * D_RWKV] = wrow
        base = d * DIR_COLS
        for j, val in enumerate((-kk * e_ex, xr * e_in, bt, kt, bt * wrow, kt * wrow)):
            scr_ref[:, base + j * D_RWKV:base + (j + 1) * D_RWKV] = val.astype(BF16)

    bonus_ref[0, rows, :] = _head_sums(xr * ksum * rk_ref[...], ones) * xv


def _prep_chunk(c, scr_ref, rbar_ref, y0_ref, mx_ref, nn_ref):
    rows = slice(c * CHUNK, (c + 1) * CHUNK)
    chains = []
    for d in range(2):
        for pp in range(N_PAIR):
            cols = [d * DIR_COLS + j * D_RWKV + pp * PAIR for j in range(6)] + [2 * DIR_COLS + pp * PAIR]
            chains.append((d == 1,) + tuple(scr_ref[:, k:k + PAIR] for k in cols))
    for idx, (rbar, y0, mx, nn) in enumerate(_chunk_local(chains)):
        col = (idx // N_PAIR) * D_RWKV + (idx % N_PAIR) * PAIR
        rbar_ref[0, rows, col:col + PAIR] = rbar
        y0_ref[0, rows, col:col + PAIR] = y0
        mx_ref[0, c, :, col:col + PAIR] = mx
        nn_ref[0, c, :, col:col + PAIR] = nn


N_PREP_CONSTS = 11


def _prep_kernel(latent, tt, p_ref, *rest):
    if latent:
        prev_ref, next_ref = rest[:2]
        rest = rest[2:]
    (mu_ref, lw_ref, w0_ref, a0_ref, kk_ref, ka_ref, rk_ref, gup_ref, ones_ref, tril_ref,
     triu_ref) = rest[:N_PREP_CONSTS]
    rbar_ref, y0_ref, mx_ref, nn_ref, wc_ref, bonus_ref, g_ref = rest[N_PREP_CONSTS:N_PREP_CONSTS + 7]
    scr_a, scr_b, lo_ref, ext_ref = rest[N_PREP_CONSTS + 7:]
    if latent:
        i = pl.program_id(1)
        n = pl.num_programs(1)
        ext_ref[0:GRID_W] = jnp.where(i > 0, prev_ref[0], 0.0)
        ext_ref[GRID_W:GRID_W + tt] = p_ref[0]
        ext_ref[GRID_W + tt:] = jnp.where(i < n - 1, next_ref[0], 0.0)

        def shifted(r0, nrows, cols):
            shape = (nrows, cols.stop - cols.start)
            trow = lax.broadcasted_iota(jnp.int32, shape, 0) & (GRID_W - 1)
            q = lax.broadcasted_iota(jnp.int32, shape, 1) & 3
            left = jnp.where(trow == 0, 0.0, ext_ref[GRID_W - 1 + r0:GRID_W - 1 + r0 + nrows, cols])
            right = jnp.where(trow == GRID_W - 1, 0.0, ext_ref[GRID_W + 1 + r0:GRID_W + 1 + r0 + nrows, cols])
            up = ext_ref[r0:r0 + nrows, cols]
            down = ext_ref[2 * GRID_W + r0:2 * GRID_W + r0 + nrows, cols]
            return jnp.where(q == 0, left, jnp.where(q == 1, right, jnp.where(q == 2, up, down)))
    else:
        ext_ref[0:8] = jnp.zeros((8, RWKV_COLS), F32)
        ext_ref[8:8 + tt] = p_ref[0]
        ext_ref[8 + tt:] = jnp.zeros((8, RWKV_COLS), F32)

        def shifted(r0, nrows, cols):
            q = lax.broadcasted_iota(jnp.int32, (nrows, cols.stop - cols.start), 1) & 1
            return jnp.where(q == 0, ext_ref[7 + r0:7 + r0 + nrows, cols], ext_ref[9 + r0:9 + r0 + nrows, cols])

    tail = slice(3 * D_RWKV, RWKV_COLS)
    p_t = p_ref[0, :, tail]
    pm_t = p_t + mu_ref[:, tail] * (shifted(0, tt, tail) - p_t)
    xwa = pm_t[:, 0:LORA_W + LORA_A]
    lane = lax.broadcasted_iota(jnp.int32, xwa.shape, 1)
    lin = jnp.where(lane < LORA_W, jnp.tanh(xwa), xwa).astype(BF16)
    lo_ref[...] = _dot(lin, lw_ref[...])
    g_ref[0] = _dot(jax.nn.sigmoid(pm_t[:, LORA_W + LORA_A:]).astype(BF16), gup_ref[...])

    scr = (scr_a, scr_b)
    head = slice(0, 3 * D_RWKV)

    def rows_part(c):
        rows = slice(c * CHUNK, (c + 1) * CHUNK)
        _prep_rows(c, p_ref[0, rows, head], shifted(c * CHUNK, CHUNK, head), lo_ref[rows, :], mu_ref, w0_ref,
                   a0_ref, kk_ref, ka_ref, rk_ref, ones_ref, tril_ref, triu_ref, wc_ref, bonus_ref, scr[c % 2])

    nchunk = tt // CHUNK
    rows_part(0)
    for c in range(nchunk):
        if c + 1 < nchunk:
            rows_part(c + 1)
        _prep_chunk(c, scr[c % 2], rbar_ref, y0_ref, mx_ref, nn_ref)


def _prep(p_rw, consts, latent, tt):
    bsz, l, _ = p_rw.shape
    nc = l // CHUNK
    cpt = tt // CHUNK
    assert len(consts) == N_PREP_CONSTS
    const_specs = [pl.BlockSpec(c.shape, lambda b, i, nd=c.ndim: (0,) * nd) for c in consts]
    kern = functools.partial(_prep_kernel, latent, tt)
    if latent:
        nblk = l // GRID_W
        in_specs = [pl.BlockSpec((1, tt, RWKV_COLS), lambda b, i: (b, i, 0)),
                    pl.BlockSpec((1, GRID_W, RWKV_COLS),
                                 lambda b, i: (b, jnp.maximum(i * (tt // GRID_W) - 1, 0), 0)),
                    pl.BlockSpec((1, GRID_W, RWKV_COLS),
                                 lambda b, i: (b, jnp.minimum((i + 1) * (tt // GRID_W), nblk - 1), 0))]
        args = (p_rw, p_rw, p_rw)
        ext_rows = tt + 2 * GRID_W
    else:
        assert tt == l
        in_specs = [pl.BlockSpec((1, tt, RWKV_COLS), lambda b, i: (b, i, 0))]
        args = (p_rw,)
        ext_rows = tt + 16
    row = lambda b, i: (b, i, 0)
    chunk = lambda b, i: (b, i, 0, 0)
    w2 = 2 * D_RWKV
    return pl.pallas_call(
        kern,
        grid=(bsz, l // tt),
        in_specs=in_specs + const_specs,
        out_specs=[pl.BlockSpec((1, tt, w2), row), pl.BlockSpec((1, tt, w2), row),
                   pl.BlockSpec((1, cpt, CHUNK, w2), chunk), pl.BlockSpec((1, cpt, CHUNK, w2), chunk),
                   pl.BlockSpec((1, cpt, 1, w2), chunk),
                   pl.BlockSpec((1, tt, D_RWKV), row), pl.BlockSpec((1, tt, D_RWKV), row)],
        out_shape=[jax.ShapeDtypeStruct((bsz, l, w2), BF16), jax.ShapeDtypeStruct((bsz, l, w2), F32),
                   jax.ShapeDtypeStruct((bsz, nc, CHUNK, w2), BF16), jax.ShapeDtypeStruct((bsz, nc, CHUNK, w2), F32),
                   jax.ShapeDtypeStruct((bsz, nc, 1, w2), F32),
                   jax.ShapeDtypeStruct((bsz, l, D_RWKV), F32), jax.ShapeDtypeStruct((bsz, l, D_RWKV), F32)],
        scratch_shapes=[pltpu.VMEM((CHUNK, SCR_COLS), BF16), pltpu.VMEM((CHUNK, SCR_COLS), BF16),
                        pltpu.VMEM((tt, 4 * D_RWKV), F32), pltpu.VMEM((ext_rows, RWKV_COLS), F32)],
        compiler_params=_cparams(("parallel", "parallel")),
        name="prep_latent" if latent else "prep_ctx",
    )(*args, *consts)


SCAN_CHUNKS = 4
SCAN_BATCH = 2


def _scan_kernel(rbf_ref, rbb_ref, y0f_ref, y0b_ref, mxf_ref, mxb_ref, nnf_ref, nnb_ref, wcf_ref, wcb_ref,
                 s0_ref, yf_ref, yb_ref, s_ref):
    i = pl.program_id(1)

    @pl.when(i == 0)
    def _():
        s_ref[...] = s0_ref[...]

    c = CHUNK
    h0 = lax.broadcasted_iota(jnp.int32, (c, PAIR), 1) < HEAD
    row = lax.broadcasted_iota(jnp.int32, (PAIR, PAIR), 0)
    col = lax.broadcasted_iota(jnp.int32, (PAIR, PAIR), 1)
    same = (row >= c) == (col >= c)
    zb = jnp.zeros((c, PAIR), BF16)
    dirs = ((rbf_ref, y0f_ref, mxf_ref, nnf_ref, wcf_ref, yf_ref), (rbb_ref, y0b_ref, mxb_ref, nnb_ref, wcb_ref, yb_ref))
    idx = [(n, d, p) for n in range(SCAN_BATCH) for d in range(2) for p in range(N_PAIR)]
    lanes = [slice(p * PAIR, (p + 1) * PAIR) for _, _, p in idx]
    s = [s_ref[n, d, p] for n, d, p in idx]
    for q in range(SCAN_CHUNKS):
        cq = (q, SCAN_CHUNKS - 1 - q)
        rows = [slice(cq[d] * c, (cq[d] + 1) * c) for _, d, _ in idx]
        sb = [_bf(x) for x in s]
        rbar = [dirs[d][0][n, rows[k], lanes[k]] for k, (n, d, _) in enumerate(idx)]
        r_st = [jnp.concatenate([jnp.where(h0, x, zb), jnp.where(h0, zb, x)], axis=0) for x in rbar]
        y_st = [_dot_nt(r_st[k], sb[k]) for k in range(len(idx))]
        m_bd = [jnp.where(same, jnp.concatenate([x, x], axis=0), jnp.zeros((PAIR, PAIR), BF16))
                for x in (dirs[d][2][n, cq[d], :, lanes[k]] for k, (n, d, _) in enumerate(idx))]
        sm = [_dot(sb[k], m_bd[k]) for k in range(len(idx))]
        s_next = []
        for k, (n, d, p) in enumerate(idx):
            nn = dirs[d][3][n, cq[d], :, lanes[k]]
            n_bd = jnp.where(same, jnp.concatenate([nn, nn], axis=0), 0.0)
            dirs[d][5][n, rows[k], lanes[k]] = y_st[k][:c] + y_st[k][c:] + dirs[d][1][n, rows[k], lanes[k]]
            s_next.append(s[k] * dirs[d][4][n, cq[d], :, lanes[k]] + sm[k] + n_bd)
        s = s_next
    for k, (n, d, p) in enumerate(idx):
        s_ref[n, d, p] = s[k]


def _scan(rbar, y0, mx, nn, wc, s0):
    bsz, l, _ = rbar.shape
    sub, nb = SCAN_CHUNKS, SCAN_BATCH
    nc = l // (CHUNK * sub)
    row_f = pl.BlockSpec((nb, sub * CHUNK, D_RWKV), lambda b, i: (b, i, 0))
    row_b = pl.BlockSpec((nb, sub * CHUNK, D_RWKV), lambda b, i: (b, nc - 1 - i, 1))
    chk_f = pl.BlockSpec((nb, sub, CHUNK, D_RWKV), lambda b, i: (b, i, 0, 0))
    chk_b = pl.BlockSpec((nb, sub, CHUNK, D_RWKV), lambda b, i: (b, nc - 1 - i, 0, 1))
    wc_f = pl.BlockSpec((nb, sub, 1, D_RWKV), lambda b, i: (b, i, 0, 0))
    wc_b = pl.BlockSpec((nb, sub, 1, D_RWKV), lambda b, i: (b, nc - 1 - i, 0, 1))
    st_spec = pl.BlockSpec((nb, 2, N_PAIR, PAIR, PAIR), lambda b, i: (b, 0, 0, 0, 0))
    return pl.pallas_call(
        _scan_kernel,
        grid=(bsz // nb, nc),
        in_specs=[row_f, row_b, row_f, row_b, chk_f, chk_b, chk_f, chk_b, wc_f, wc_b, st_spec],
        out_specs=[row_f, pl.BlockSpec((nb, sub * CHUNK, D_RWKV), lambda b, i: (b, nc - 1 - i, 0)), st_spec],
        out_shape=[jax.ShapeDtypeStruct((bsz, l, D_RWKV), F32),
                   jax.ShapeDtypeStruct((bsz, l, D_RWKV), F32),
                   jax.ShapeDtypeStruct((bsz, 2, N_PAIR, PAIR, PAIR), F32)],
        compiler_params=_cparams(("parallel", "arbitrary")),
        name="scan",
    )(rbar, rbar, y0, y0, mx, mx, nn, nn, wc, wc, s0)


OUT_ROWS = 128


def _out_kernel(tt, yf_ref, yb_ref, bonus_ref, g_ref, cv_ref, cvp_ref, cvn_ref, x_ref, g1_ref, sh2_ref,
                sc2_ref, n2g_ref, gng_ref, gnb_ref, convw_ref, wout_ref, rwh_ref, rwl_ref, avg_ref,
                xm_ref, hx_ref, aff_ref):
    i = pl.program_id(1)
    n = pl.num_programs(1)

    cv = cv_ref[0]
    b_gate = cv[:, 0:D_CONV]
    cu = cv[:, D_CONV:2 * D_CONV] * cv[:, 2 * D_CONV:]
    cvp = cvp_ref[0]
    cvn = cvn_ref[0]
    cu_prev = jnp.where(i > 0, cvp[7:8, D_CONV:2 * D_CONV] * cvp[7:8, 2 * D_CONV:], 0.0)
    cu_next = jnp.where(i < n - 1, cvn[0:1, D_CONV:2 * D_CONV] * cvn[0:1, 2 * D_CONV:], 0.0)
    ridx = lax.broadcasted_iota(jnp.int32, cu.shape, 0)
    cu_m1 = jnp.where(ridx == 0, cu_prev, pltpu.roll(cu, 1, 0))
    cu_p1 = jnp.where(ridx == tt - 1, cu_next, pltpu.roll(cu, tt - 1, 0))
    conv = convw_ref[0:1, :] * cu_m1 + convw_ref[1:2, :] * cu + convw_ref[2:3, :] * cu_p1
    bx = (b_gate * conv).astype(BF16)

    parts = [slice(k * OUT_ROWS, (k + 1) * OUT_ROWS) for k in range(tt // OUT_ROWS)]
    avg = avg_ref[...]
    rwh = rwh_ref[...]
    rwl = rwl_ref[...]
    y = [yf_ref[0, r, :] + yb_ref[0, r, :] for r in parts]
    mu = [_seg_dot(v, avg) for v in y]
    dlt = [a - b for a, b in zip(y, mu)]
    var = [_seg_dot(v * v, avg) for v in dlt]
    yn = [a * lax.rsqrt(b + GN_EPS) * gng_ref[...] + gnb_ref[...] for a, b in zip(dlt, var)]
    ax = [((a + bonus_ref[0, r, :]) * g_ref[0, r, :]).astype(BF16) for a, r in zip(yn, parts)]
    mix = [_dot(a, wout_ref[0:D_RWKV, :]) + _dot(bx[r], wout_ref[D_RWKV:, :]) for a, r in zip(ax, parts)]
    xm = [x_ref[0, r, :] + g1_ref[0] * a for a, r in zip(mix, parts)]
    ms = [jnp.mean(v * v, axis=-1, keepdims=True) for v in xm]
    hx = [a * lax.rsqrt(b + NORM_EPS) * n2g_ref[...] for a, b in zip(xm, ms)]
    hx = [_split2(v * (1.0 + sc2_ref[0]) + sh2_ref[0]) for v in hx]
    logits = [_dot_nt(rwh, hi) + _dot_nt(rwh, lo) + _dot_nt(rwl, hi) for hi, lo in hx]
    for k, r in enumerate(parts):
        xm_ref[0, r, :] = xm[k]
        hx_ref[0, r, :] = hx[k][0]
        m = jnp.max(logits[k], axis=0, keepdims=True)
        ex = jnp.exp(logits[k] - m)
        aff_ref[0, :, r] = ex / jnp.sum(ex, axis=0, keepdims=True)


def _out(yf, yb, bonus, g, p_cv, x, g1, sh2, sc2, n2g, gng, gnb, convw, wout, rwh, rwl, avg, tt):
    bsz, t, d = x.shape
    ne = rwh.shape[0]
    nb8 = t // 8
    row = lambda b, i: (b, i, 0)
    per_b = lambda b, i: (b, 0, 0)
    const2 = lambda b, i: (0, 0)
    return pl.pallas_call(
        functools.partial(_out_kernel, tt),
        grid=(bsz, t // tt),
        in_specs=[pl.BlockSpec((1, tt, D_RWKV), row), pl.BlockSpec((1, tt, D_RWKV), row),
                  pl.BlockSpec((1, tt, D_RWKV), row), pl.BlockSpec((1, tt, D_RWKV), row),
                  pl.BlockSpec((1, tt, CONV_COLS), row),
                  pl.BlockSpec((1, 8, CONV_COLS), lambda b, i: (b, jnp.maximum(i * (tt // 8) - 1, 0), 0)),
                  pl.BlockSpec((1, 8, CONV_COLS), lambda b, i: (b, jnp.minimum((i + 1) * (tt // 8), nb8 - 1), 0)),
                  pl.BlockSpec((1, tt, d), row),
                  pl.BlockSpec((1, 1, d), per_b), pl.BlockSpec((1, 1, d), per_b), pl.BlockSpec((1, 1, d), per_b),
                  pl.BlockSpec((1, d), const2), pl.BlockSpec((1, D_RWKV), const2), pl.BlockSpec((1, D_RWKV), const2),
                  pl.BlockSpec((3, D_CONV), const2), pl.BlockSpec((D_RWKV + D_CONV, d), const2),
                  pl.BlockSpec((ne, d), const2), pl.BlockSpec((ne, d), const2),
                  pl.BlockSpec((D_RWKV, D_RWKV), const2)],
        out_specs=[pl.BlockSpec((1, tt, d), row), pl.BlockSpec((1, tt, d), row),
                   pl.BlockSpec((1, ne, tt), lambda b, i: (b, 0, i))],
        out_shape=[jax.ShapeDtypeStruct((bsz, t, d), F32), jax.ShapeDtypeStruct((bsz, t, d), BF16),
                   jax.ShapeDtypeStruct((bsz, ne, t), F32)],
        compiler_params=_cparams(("parallel", "parallel")),
        name="out",
    )(yf, yb, bonus, g, p_cv, p_cv, p_cv, x, g1, sh2, sc2, n2g, gng, gnb, convw, wout, rwh, rwl, avg)


def _prefix_blocks(mask_fn, t, tri, emit):
    carry = None
    for j in range(t // 128):
        m = mask_fn(j)
        inc = _dot(m.astype(BF16), tri)
        carry = jnp.zeros_like(inc[:, 0:1]) if carry is None else carry
        emit(j, m, inc - m + carry)
        carry = carry + inc[:, 127:128]


def _topk_kernel(cap, sb, aff_ref, tri_ref, cnt_ref, slot_ref, edge_ref):
    t = aff_ref.shape[2]
    aff = aff_ref[0]

    def body(k, bits):
        cand = bits | jnp.left_shift(jnp.int32(1), 30 - k)
        cnt = jnp.sum(jnp.where(aff >= pltpu.bitcast(cand, F32), 1, 0), axis=-1, keepdims=True)
        return jnp.where(cnt >= cap, cand, bits)

    bits = lax.fori_loop(0, 31, body, jnp.zeros((aff.shape[0], 1), jnp.int32))
    thr = pltpu.bitcast(bits, F32)
    above = pltpu.bitcast(bits + 1, F32)
    n_gt = jnp.sum(jnp.where(aff >= above, 1, 0), axis=-1, keepdims=True)
    need = (cap - n_gt).astype(F32)
    tri = tri_ref[...]

    def blk(j):
        return aff[:, j * 128:(j + 1) * 128]

    def emit_sel(j, eq, before):
        take = (blk(j) >= above) | ((eq > 0.5) & (before < need))
        slot_ref[0, :, j * 128:(j + 1) * 128] = jnp.where(take, 1, 0)

    _prefix_blocks(lambda j: jnp.where((blk(j) >= thr) & (blk(j) < above), 1.0, 0.0), t, tri, emit_sel)

    def emit_slot(j, m, before):
        count = before.astype(jnp.int32)
        cnt_ref[0, :, j * 128:(j + 1) * 128] = count
        slot_ref[0, :, j * 128:(j + 1) * 128] = jnp.where(m > 0.5, count, -1)

    _prefix_blocks(lambda j: slot_ref[0, :, j * 128:(j + 1) * 128].astype(F32), t, tri, emit_slot)

    cnt = cnt_ref[0]
    lane = lax.broadcasted_iota(jnp.int32, (cnt.shape[0], 128), 1)
    edges = jnp.zeros((cnt.shape[0], 128), jnp.int32)
    for s in range(1, cap // sb + 1):
        below = jnp.sum(jnp.where(cnt < s * sb, 1, 0), axis=-1, keepdims=True)
        edges = jnp.where(lane == s, below, edges)
    edge_ref[0] = edges


def _topk(aff_t, tri, cap, sb):
    bsz, ne, t = aff_t.shape
    spec = pl.BlockSpec((1, ne, t), lambda b: (b, 0, 0))
    return pl.pallas_call(
        functools.partial(_topk_kernel, cap, sb),
        grid=(bsz,),
        in_specs=[spec, pl.BlockSpec((128, 128), lambda b: (0, 0))],
        out_specs=[spec, spec, pl.BlockSpec((1, ne, 128), lambda b: (b, 0, 0))],
        out_shape=[jax.ShapeDtypeStruct((bsz, ne, t), jnp.int32)] * 2 + [jax.ShapeDtypeStruct((bsz, ne, 128), jnp.int32)],
        compiler_params=_cparams(("parallel",)),
        name="topk",
    )(aff_t, tri)


TOKEN_ROW = 128


def _moe_kernel(win, sb, edge_ref, hx_ref, slot_ref, wg_ref, wu_ref, wd_ref, ye_ref, xs_ref):
    b = pl.program_id(0)
    e = pl.program_id(1)
    ne = pl.num_programs(1)
    cap = xs_ref.shape[0]
    nblk = cap // sb
    nrow = hx_ref.shape[1] // TOKEN_ROW
    wrows = win // TOKEN_ROW
    base = (b * ne + e) * (nblk + 1)
    for s in range(nblk):
        blk = slice(s * sb, (s + 1) * sb)
        target = lax.broadcasted_iota(jnp.int32, (sb, TOKEN_ROW), 0) + s * sb
        r0 = jnp.minimum(edge_ref[base + s] // TOKEN_ROW, nrow - wrows)
        rows = slot_ref[0, 0, pl.ds(r0, wrows), :]
        onehot = jnp.concatenate([jnp.where(rows[k:k + 1, :] == target, 1.0, 0.0).astype(BF16)
                                  for k in range(wrows)], axis=1)
        t0 = pl.multiple_of(r0 * TOKEN_ROW, TOKEN_ROW)
        xs_ref[blk, :] = _dot(onehot, hx_ref[0, pl.ds(t0, win), :])

        def extra_row(r, carry, blk=blk, target=target):
            hit = jnp.where(slot_ref[0, 0, pl.ds(r, 1), :] == target, 1.0, 0.0).astype(BF16)
            tr = pl.multiple_of(r * TOKEN_ROW, TOKEN_ROW)
            xs_ref[blk, :] += _dot(hit, hx_ref[0, pl.ds(tr, TOKEN_ROW), :])
            return carry

        r_end = (edge_ref[base + s + 1] + TOKEN_ROW - 1) // TOKEN_ROW
        lax.fori_loop(r0 + wrows, r_end, extra_row, 0)

    xs = xs_ref[...].astype(BF16)
    h1 = _dot(xs, wg_ref[0].astype(BF16))
    h2 = _dot(xs, wu_ref[0].astype(BF16))
    hid = (h1 * jax.nn.sigmoid(h1) * h2).astype(BF16)
    ye_ref[0, 0] = _dot(hid, wd_ref[0].astype(BF16)).astype(BF16)


def _moe(edges, hx, slot4, wg, wu, wd, cap, win, sb):
    bsz, t, d = hx.shape
    ne, _, f = wg.shape
    nrow = t // TOKEN_ROW
    grid_spec = pltpu.PrefetchScalarGridSpec(
        num_scalar_prefetch=1,
        grid=(bsz, ne),
        in_specs=[pl.BlockSpec((1, t, d), lambda b, e, s: (b, 0, 0), pipeline_mode=pl.Buffered(1)),
                  pl.BlockSpec((1, 1, nrow, TOKEN_ROW), lambda b, e, s: (b, e, 0, 0)),
                  pl.BlockSpec((1, d, f), lambda b, e, s: (e, 0, 0)),
                  pl.BlockSpec((1, d, f), lambda b, e, s: (e, 0, 0)),
                  pl.BlockSpec((1, f, d), lambda b, e, s: (e, 0, 0))],
        out_specs=pl.BlockSpec((1, 1, cap, d), lambda b, e, s: (b, e, 0, 0)),
        scratch_shapes=[pltpu.VMEM((cap, d), F32)],
    )
    return pl.pallas_call(
        functools.partial(_moe_kernel, win, sb),
        grid_spec=grid_spec,
        out_shape=jax.ShapeDtypeStruct((bsz, ne, cap, d), BF16),
        compiler_params=_cparams(("parallel", "arbitrary")),
        name="moe",
    )(edges, hx, slot4, wg, wu, wd)


COMB_ROWS = 128
COMB_COLS = 256
SLOT_ALIGN = 16


def _comb_kernel(tk, win, tsp_ref, xm_ref, ye_ref, slotc_ref, affc_ref, g2_ref, fg_ref, o_ref):
    b = pl.program_id(0)
    j = pl.program_id(1)
    nh = tk // COMB_ROWS
    ntile = pl.num_programs(1) * nh
    ne, cap, d = ye_ref.shape[1], ye_ref.shape[2], ye_ref.shape[3]
    slotc = slotc_ref[0]
    affc = affc_ref[0]
    lane_slot = lax.broadcasted_iota(jnp.int32, (COMB_ROWS, win), 1)
    for h in range(nh):
        rows = slice(h * COMB_ROWS, (h + 1) * COMB_ROWS)
        starts, onehots, vals = [], [], []
        for e in range(ne):
            first = tsp_ref[(b * ne + e) * ntile + j * nh + h]
            start = pl.multiple_of(jnp.minimum(first & ~(SLOT_ALIGN - 1), cap - win), SLOT_ALIGN)
            starts.append(start)
            onehots.append(jnp.where(slotc[rows, e:e + 1] - start == lane_slot, 1.0, 0.0).astype(BF16))
            vals.append(jnp.broadcast_to(affc[rows, e:e + 1], (COMB_ROWS, COMB_COLS)))
        xo, ssq = [], None
        for q in range(d // COMB_COLS):
            cols = slice(q * COMB_COLS, (q + 1) * COMB_COLS)
            acc = None
            for e in range(ne):
                part = vals[e] * _dot(onehots[e], ye_ref[0, e, pl.ds(starts[e], win), cols])
                acc = part if acc is None else acc + part
            x = xm_ref[0, rows, cols] + g2_ref[0, :, cols] * acc
            xo.append(x)
            sq = jnp.sum(x * x, axis=-1, keepdims=True)
            ssq = sq if ssq is None else ssq + sq
        scale = lax.rsqrt(ssq / d + NORM_EPS)
        for q in range(d // COMB_COLS):
            cols = slice(q * COMB_COLS, (q + 1) * COMB_COLS)
            o_ref[0, rows, cols] = xo[q] * scale * fg_ref[:, cols]


def _comb(tsp, xm, ye, slotc, affc, g2, fg, tk):
    bsz, t, d = xm.shape
    ne, cap = ye.shape[1], ye.shape[2]
    win = min(2 * COMB_ROWS, cap)
    assert win == cap or win >= COMB_ROWS + SLOT_ALIGN
    grid_spec = pltpu.PrefetchScalarGridSpec(
        num_scalar_prefetch=1,
        grid=(bsz, t // tk),
        in_specs=[pl.BlockSpec((1, tk, d), lambda b, j, s: (b, j, 0)),
                  pl.BlockSpec((1, ne, cap, d), lambda b, j, s: (b, 0, 0, 0)),
                  pl.BlockSpec((1, tk, ne), lambda b, j, s: (b, j, 0)),
                  pl.BlockSpec((1, tk, ne), lambda b, j, s: (b, j, 0)),
                  pl.BlockSpec((1, 1, d), lambda b, j, s: (b, 0, 0)),
                  pl.BlockSpec((1, d), lambda b, j, s: (0, 0))],
        out_specs=pl.BlockSpec((1, tk, d), lambda b, j, s: (b, j, 0)),
    )
    return pl.pallas_call(
        functools.partial(_comb_kernel, tk, win),
        grid_spec=grid_spec,
        out_shape=jax.ShapeDtypeStruct((bsz, t, d), F32),
        compiler_params=_cparams(("parallel", "arbitrary")),
        name="comb",
    )(tsp, xm, ye, slotc, affc, g2, fg)


def _block_diag_ones(n, blk, value=1.0):
    r = jnp.arange(n)
    return jnp.where((r[:, None] // blk) == (r[None, :] // blk), value, 0.0)


def kernel(x, c, ctx, c_ctx, ada_w, ada_b, norm1_g, norm2_g, w_in, shift_mu, w0, w_lora_up, a0, a_lora_up, k_k, k_a,
           r_k, g_lora_up, gn_g, gn_b, conv_w, w_out, router_w, exp_w_gate, exp_w_up, exp_w_down, final_g):
    bsz, t, d = x.shape
    lc = ctx.shape[1]
    ne = router_w.shape[-1]
    cap = EC_CAPACITY * t // ne
    tt = 256
    sb = min(128, cap)
    win = min(1536, t)
    l = 0

    rows = ((bsz + 1 + 7) // 8) * 8
    cc = jnp.zeros((rows, d), F32).at[:bsz].set(c).at[bsz].set(c_ctx)
    mod = _mod(cc, ada_w[l], ada_b[l][None, :])
    sh1, sc1, g1, sh2, sc2, g2 = (m[:, None, :] for m in jnp.split(mod[:bsz], 6, axis=-1))
    csh1, csc1 = (jnp.broadcast_to(m[None, None, :], (bsz, 1, d)) for m in jnp.split(mod[bsz], 6)[:2])

    w_rw = w_in[l][:, :RWKV_COLS].astype(BF16)
    w_cv = w_in[l][:, RWKV_COLS:].astype(BF16)
    n1g = norm1_g[l][None, :]
    px_rw, px_cv = _in_proj(x, sh1, sc1, n1g, w_rw, w_cv, 512)
    pc_rw, _ = _in_proj(ctx, csh1, csc1, n1g, w_rw, w_cv, 256)

    zw = jnp.zeros((LORA_W, 2 * D_RWKV), F32)
    lora = jnp.concatenate([
        jnp.concatenate([w_lora_up[l, 0], w_lora_up[l, 1], zw], axis=1),
        jnp.concatenate([zw, a_lora_up[l, 0], a_lora_up[l, 1]], axis=1)], axis=0).astype(BF16)
    ridx = jnp.arange(CHUNK)
    tril = jnp.where(ridx[None, :] <= ridx[:, None], 1.0, 0.0).astype(BF16)
    triu = jnp.where(ridx[None, :] >= ridx[:, None], 1.0, 0.0).astype(BF16)
    ones_bd = _block_diag_ones(PAIR, HEAD).astype(BF16)
    consts = (shift_mu[l][None, :], lora, w0[l], a0[l], k_k[l][None, :], k_a[l][None, :],
              r_k[l].reshape(1, D_RWKV), g_lora_up[l].astype(BF16), ones_bd, tril, triu)

    chunks_c = _prep(pc_rw, consts, False, lc)[:5]
    *chunks_x, bonus, gate = _prep(px_rw, consts, True, tt)

    s_zero = jnp.zeros((bsz, 2, N_PAIR, PAIR, PAIR), F32)
    _, _, s_ctx = _scan(*chunks_c, s_zero)
    yf, yb, _ = _scan(*chunks_x, s_ctx)

    rw_t = router_w[l].T
    rwh = rw_t.astype(BF16)
    rwl = (rw_t - rwh.astype(F32)).astype(BF16)
    avg = _block_diag_ones(D_RWKV, HEAD, 1.0 / HEAD).astype(BF16)
    xm, hx, aff_t = _out(yf, yb, bonus, gate, px_cv, x, g1, sh2, sc2, norm2_g[l][None, :], gn_g[l][None, :],
                         gn_b[l][None, :], conv_w[l], w_out[l].astype(BF16), rwh, rwl, avg, 2 * tt)

    r128 = jnp.arange(128)
    tri128 = jnp.where(r128[:, None] <= r128[None, :], 1.0, 0.0).astype(BF16)
    cnt, slot, edges = _topk(aff_t, tri128, cap, sb)

    ye = _moe(edges[:, :, :cap // sb + 1].reshape(-1), hx, slot.reshape(bsz, ne, t // TOKEN_ROW, TOKEN_ROW),
              exp_w_gate[l], exp_w_up[l], exp_w_down[l], cap, win, sb)
    tr = lambda a: jnp.transpose(a, (0, 2, 1))
    first_slot = cnt[:, :, ::COMB_ROWS].reshape(-1)
    return _comb(first_slot, xm, ye, tr(slot), tr(aff_t), g2, final_g[None, :], tt)
```

```python
import functools
import math

import jax
import jax.numpy as jnp
from jax import lax
from jax.experimental import pallas as pl
from jax.experimental.pallas import tpu as pltpu

F32 = jnp.float32
BF16 = jnp.bfloat16
HIGHEST = lax.Precision.HIGHEST

GRID_W = 64
D_RWKV = 512
D_CONV = 512
HEAD = 64
LORA_W = 64
LORA_A = 64
LORA_G = 128
N_EXPERTS = 16
EC_CAPACITY = 2
NORM_EPS = 1e-6
GN_EPS = 64e-5
RWKV_COLS = 3 * D_RWKV + LORA_W + LORA_A + LORA_G
CONV_COLS = 3 * D_CONV

CHUNK = 64
PAIR = 2 * HEAD
N_PAIR = D_RWKV // PAIR
VMEM_LIMIT = 48 * 1024 * 1024


def _cparams(sem):
    return pltpu.CompilerParams(dimension_semantics=sem, vmem_limit_bytes=VMEM_LIMIT)


def _dot(a, b):
    return jnp.dot(a, b, preferred_element_type=F32)


def _dot_nt(a, b):
    return lax.dot_general(a, b, (((1,), (1,)), ((), ())), preferred_element_type=F32)


def _split2(x):
    hi = x.astype(BF16)
    lo = (x - hi.astype(F32)).astype(BF16)
    return hi, lo


def _split3(x):
    hi = x.astype(BF16)
    r = x - hi.astype(F32)
    mid = r.astype(BF16)
    lo = (r - mid.astype(F32)).astype(BF16)
    return hi, mid, lo


def _seg_dot(x, m):
    hi, lo = _split2(x)
    return _dot(hi, m) + _dot(lo, m)


def _mod_kernel(c_ref, w_ref, b_ref, o_ref):
    c = c_ref[...]
    s = c * jax.nn.sigmoid(c)
    o_ref[...] = jnp.dot(s, w_ref[...], precision=HIGHEST, preferred_element_type=F32) + b_ref[...]


def _mod(cc, w, b):
    rows, d = cc.shape
    n = w.shape[1]
    tn = 1024
    return pl.pallas_call(
        _mod_kernel,
        grid=(n // tn,),
        in_specs=[pl.BlockSpec((rows, d), lambda j: (0, 0)),
                  pl.BlockSpec((d, tn), lambda j: (0, j)),
                  pl.BlockSpec((1, tn), lambda j: (0, j))],
        out_specs=pl.BlockSpec((rows, tn), lambda j: (0, j)),
        out_shape=jax.ShapeDtypeStruct((rows, n), F32),
        compiler_params=_cparams(("parallel",)),
        name="mod",
    )(cc, w, b)


def _in_proj_kernel(x_ref, sh_ref, sc_ref, g_ref, wrw_ref, wcv_ref, orw_ref, ocv_ref):
    x = x_ref[0]
    ms = jnp.mean(x * x, axis=-1, keepdims=True)
    h = x * lax.rsqrt(ms + NORM_EPS) * g_ref[...]
    h = (h * (1.0 + sc_ref[0]) + sh_ref[0]).astype(BF16)
    orw_ref[0] = _dot(h, wrw_ref[...])
    ocv_ref[0] = _dot(h, wcv_ref[...])


def _in_proj(x, sh, sc, g, w_rw, w_cv, tm):
    bsz, l, d = x.shape
    return pl.pallas_call(
        _in_proj_kernel,
        grid=(bsz, l // tm),
        in_specs=[pl.BlockSpec((1, tm, d), lambda b, i: (b, i, 0)),
                  pl.BlockSpec((1, 1, d), lambda b, i: (b, 0, 0)),
                  pl.BlockSpec((1, 1, d), lambda b, i: (b, 0, 0)),
                  pl.BlockSpec((1, d), lambda b, i: (0, 0)),
                  pl.BlockSpec((d, RWKV_COLS), lambda b, i: (0, 0)),
                  pl.BlockSpec((d, CONV_COLS), lambda b, i: (0, 0))],
        out_specs=[pl.BlockSpec((1, tm, RWKV_COLS), lambda b, i: (b, i, 0)),
                   pl.BlockSpec((1, tm, CONV_COLS), lambda b, i: (b, i, 0))],
        out_shape=[jax.ShapeDtypeStruct((bsz, l, RWKV_COLS), F32),
                   jax.ShapeDtypeStruct((bsz, l, CONV_COLS), F32)],
        compiler_params=_cparams(("parallel", "parallel")),
        name="in_proj",
    )(x, sh, sc, g, w_rw, w_cv)


def _bf(x):
    return x.astype(BF16)


def _chunk_local(chains):
    c = CHUNK
    n = range(len(chains))
    lane = lax.broadcasted_iota(jnp.int32, (c, PAIR), 1)
    h0 = lane < HEAD
    row = lax.broadcasted_iota(jnp.int32, (PAIR, PAIR), 0)
    col = lax.broadcasted_iota(jnp.int32, (PAIR, PAIR), 1)
    same = (row >= c) == (col >= c)
    tr, tc = row & (c - 1), col & (c - 1)
    eye = jnp.where(row == col, 1.0, 0.0)
    masks = {rev: (same & ((tc > tr) if rev else (tc < tr)), same & ((tc >= tr) if rev else (tc <= tr)))
             for rev in (False, True)}
    zb = jnp.zeros((c, PAIR), BF16)

    def stack(x):
        return jnp.concatenate([jnp.where(h0, x, zb), jnp.where(h0, zb, x)], axis=0)

    a_st = [stack(ch[1]) for ch in chains]
    r_st = [stack(ch[2]) for ch in chains]
    gram = [_dot_nt(jnp.concatenate([a_st[i], r_st[i]], axis=0),
                    jnp.concatenate([chains[i][3], chains[i][3], chains[i][4], chains[i][4]], axis=0)) for i in n]
    lab = [jnp.where(masks[chains[i][0]][0], gram[i][:PAIR, :PAIR], 0.0) for i in n]
    lak = [jnp.where(masks[chains[i][0]][0], gram[i][:PAIR, PAIR:], 0.0) for i in n]
    mrb = [jnp.where(masks[chains[i][0]][1], gram[i][PAIR:, :PAIR], 0.0) for i in n]
    mrk = [jnp.where(masks[chains[i][0]][1], gram[i][PAIR:, PAIR:], 0.0) for i in n]

    labb = [_bf(x) for x in lab]
    pw = [_dot(labb[i], labb[i]) for i in n]
    tp = [eye + lab[i] for i in n]
    for _ in range(4):
        pwb = [_bf(x) for x in pw]
        both = [_dot(pwb[i], jnp.concatenate([pwb[i], _bf(tp[i])], axis=1)) for i in n]
        pw = [x[:, :PAIR] for x in both]
        tp = [tp[i] + both[i][:, PAIR:] for i in n]
    tinv = [tp[i] + _dot(_bf(pw[i]), _bf(tp[i])) for i in n]

    v_rep = [jnp.concatenate([ch[7], ch[7]], axis=0) for ch in chains]
    lakv = [_dot(_bf(lak[i]), v_rep[i]) for i in n]
    x = [_dot(_bf(tinv[i]), jnp.concatenate([a_st[i], _bf(lakv[i])], axis=1)) for i in n]
    abar = [xi[:, :PAIR] for xi in x]
    u0 = [jnp.where(same, xi[:, PAIR:], 0.0) for xi in x]
    au = [jnp.concatenate([abar[i], u0[i]], axis=1) for i in n]
    aub = [_bf(x) for x in au]
    z = [_dot(_bf(mrb[i]), aub[i]) for i in n]
    zv = [_dot(_bf(mrk[i]), v_rep[i]) for i in n]
    rbar = [r_st[i].astype(F32) + z[i][:, :PAIR] for i in n]
    rbar = [_bf(x[:c] + x[c:]) for x in rbar]
    y0 = [z[i][:, PAIR:] + zv[i] for i in n]
    y0 = [jnp.where(h0, x[:c], x[c:]) for x in y0]

    mn = [_dot(_bf(au[i].T), jnp.concatenate([chains[i][5], chains[i][5]], axis=0)) for i in n]
    vk = [_dot(_bf(chains[i][7].astype(F32).T), chains[i][6]) for i in n]
    mx = [_bf(jnp.where(h0, m[:c], m[c:PAIR])) for m in mn]
    nn = [mn[i][PAIR:] + vk[i] for i in n]
    nn = [jnp.where(h0, x[:c], x[c:]) for x in nn]
    return list(zip(rbar, y0, mx, nn))


DIR_COLS = 6 * D_RWKV
SCR_COLS = 2 * DIR_COLS + D_RWKV


def _head_sums(x, ones):
    return jnp.concatenate([_seg_dot(x[:, g * PAIR:(g + 1) * PAIR], ones) for g in range(N_PAIR)], axis=1)


def _prep_rows(c, p, shifted, lo, mu_ref, w0_ref, a0_ref, kk_ref, ka_ref, rk_ref, ones_ref, tril_ref, triu_ref,
               wc_ref, bonus_ref, scr_ref):
    rows = slice(c * CHUNK, (c + 1) * CHUNK)
    pm = p + mu_ref[:, 0:3 * D_RWKV] * (shifted - p)
    xr = pm[:, 0:D_RWKV]
    xk = pm[:, D_RWKV:2 * D_RWKV]
    xv = pm[:, 2 * D_RWKV:3 * D_RWKV]

    ones = ones_ref[...]
    kraw = xk * kk_ref[...]
    kk = kraw / jnp.maximum(jnp.sqrt(_head_sums(kraw * kraw, ones)), 1e-12)

    scr_ref[:, 2 * DIR_COLS:] = xv.astype(BF16)
    ksum = None
    for d in range(2):
        z = w0_ref[d:d + 1, :] + lo[:, d * D_RWKV:(d + 1) * D_RWKV]
        ld = -math.exp(-0.5) * jax.nn.sigmoid(z)
        ag = jax.nn.sigmoid(a0_ref[d:d + 1, :] + lo[:, (2 + d) * D_RWKV:(3 + d) * D_RWKV])
        kd = xk * (1.0 + (ag - 1.0) * ka_ref[...])
        bb = kk * ag
        ksum = kd if ksum is None else ksum + kd
        tri = tril_ref[...] if d == 0 else triu_ref[...]
        h3 = _split3(ld)
        cs = _dot(tri, h3[0]) + _dot(tri, h3[1]) + _dot(tri, h3[2])
        e_in = jnp.exp(cs)
        e_ex = jnp.exp(cs - ld)
        e_neg = jnp.exp(-cs)
        bt = bb * e_neg
        kt = kd * e_neg
        edge = CHUNK - 1 if d == 0 else 0
        wrow = e_in[edge:edge + 1, :]
        wc_ref[0, c, :, d * D_RWKV:(d + 1) * D_RWKV] = wrow
        base = d * DIR_COLS
        for j, val in enumerate((-kk * e_ex, xr * e_in, bt, kt, bt * wrow, kt * wrow)):
            scr_ref[:, base + j * D_RWKV:base + (j + 1) * D_RWKV] = val.astype(BF16)

    bonus_ref[0, rows, :] = _head_sums(xr * ksum * rk_ref[...], ones) * xv


PREP_GROUP = 2


def _prep_chunks(cs, scr_refs, rbar_ref, y0_ref, mx_ref, nn_ref):
    chains = []
    for scr_ref in scr_refs:
        for d in range(2):
            for pp in range(N_PAIR):
                cols = [d * DIR_COLS + j * D_RWKV + pp * PAIR for j in range(6)] + [2 * DIR_COLS + pp * PAIR]
                chains.append((d == 1,) + tuple(scr_ref[:, k:k + PAIR] for k in cols))
    per_chunk = 2 * N_PAIR
    for idx, (rbar, y0, mx, nn) in enumerate(_chunk_local(chains)):
        c = cs[idx // per_chunk]
        rows = slice(c * CHUNK, (c + 1) * CHUNK)
        col = (idx % per_chunk // N_PAIR) * D_RWKV + (idx % N_PAIR) * PAIR
        rbar_ref[0, rows, col:col + PAIR] = rbar
        y0_ref[0, rows, col:col + PAIR] = y0
        mx_ref[0, c, :, col:col + PAIR] = mx
        nn_ref[0, c, :, col:col + PAIR] = nn


N_PREP_CONSTS = 11


def _prep_kernel(latent, tt, p_ref, *rest):
    if latent:
        prev_ref, next_ref = rest[:2]
        rest = rest[2:]
    (mu_ref, lw_ref, w0_ref, a0_ref, kk_ref, ka_ref, rk_ref, gup_ref, ones_ref, tril_ref,
     triu_ref) = rest[:N_PREP_CONSTS]
    rbar_ref, y0_ref, mx_ref, nn_ref, wc_ref, bonus_ref, g_ref = rest[N_PREP_CONSTS:N_PREP_CONSTS + 7]
    *scr, lo_ref, ext_ref = rest[N_PREP_CONSTS + 7:]
    if latent:
        i = pl.program_id(1)
        n = pl.num_programs(1)
        ext_ref[0:GRID_W] = jnp.where(i > 0, prev_ref[0], 0.0)
        ext_ref[GRID_W:GRID_W + tt] = p_ref[0]
        ext_ref[GRID_W + tt:] = jnp.where(i < n - 1, next_ref[0], 0.0)

        def shifted(r0, nrows, cols):
            shape = (nrows, cols.stop - cols.start)
            trow = lax.broadcasted_iota(jnp.int32, shape, 0) & (GRID_W - 1)
            q = lax.broadcasted_iota(jnp.int32, shape, 1) & 3
            left = jnp.where(trow == 0, 0.0, ext_ref[GRID_W - 1 + r0:GRID_W - 1 + r0 + nrows, cols])
            right = jnp.where(trow == GRID_W - 1, 0.0, ext_ref[GRID_W + 1 + r0:GRID_W + 1 + r0 + nrows, cols])
            up = ext_ref[r0:r0 + nrows, cols]
            down = ext_ref[2 * GRID_W + r0:2 * GRID_W + r0 + nrows, cols]
            return jnp.where(q == 0, left, jnp.where(q == 1, right, jnp.where(q == 2, up, down)))
    else:
        ext_ref[0:8] = jnp.zeros((8, RWKV_COLS), F32)
        ext_ref[8:8 + tt] = p_ref[0]
        ext_ref[8 + tt:] = jnp.zeros((8, RWKV_COLS), F32)

        def shifted(r0, nrows, cols):
            q = lax.broadcasted_iota(jnp.int32, (nrows, cols.stop - cols.start), 1) & 1
            return jnp.where(q == 0, ext_ref[7 + r0:7 + r0 + nrows, cols], ext_ref[9 + r0:9 + r0 + nrows, cols])

    tail = slice(3 * D_RWKV, RWKV_COLS)
    p_t = p_ref[0, :, tail]
    pm_t = p_t + mu_ref[:, tail] * (shifted(0, tt, tail) - p_t)
    xwa = pm_t[:, 0:LORA_W + LORA_A]
    lane = lax.broadcasted_iota(jnp.int32, xwa.shape, 1)
    lin = jnp.where(lane < LORA_W, jnp.tanh(xwa), xwa).astype(BF16)
    lo_ref[...] = _dot(lin, lw_ref[...])
    g_ref[0] = _dot(jax.nn.sigmoid(pm_t[:, LORA_W + LORA_A:]).astype(BF16), gup_ref[...])

    head = slice(0, 3 * D_RWKV)

    def rows_part(c):
        rows = slice(c * CHUNK, (c + 1) * CHUNK)
        _prep_rows(c, p_ref[0, rows, head], shifted(c * CHUNK, CHUNK, head), lo_ref[rows, :], mu_ref, w0_ref,
                   a0_ref, kk_ref, ka_ref, rk_ref, ones_ref, tril_ref, triu_ref, wc_ref, bonus_ref,
                   scr[c % (2 * PREP_GROUP)])

    groups = [list(range(g, g + PREP_GROUP)) for g in range(0, tt // CHUNK, PREP_GROUP)]
    for c in groups[0]:
        rows_part(c)
    for k, cs in enumerate(groups):
        if k + 1 < len(groups):
            for c in groups[k + 1]:
                rows_part(c)
        _prep_chunks(cs, [scr[c % (2 * PREP_GROUP)] for c in cs], rbar_ref, y0_ref, mx_ref, nn_ref)


def _prep(p_rw, consts, latent, tt):
    bsz, l, _ = p_rw.shape
    nc = l // CHUNK
    cpt = tt // CHUNK
    assert len(consts) == N_PREP_CONSTS
    const_specs = [pl.BlockSpec(c.shape, lambda b, i, nd=c.ndim: (0,) * nd) for c in consts]
    kern = functools.partial(_prep_kernel, latent, tt)
    if latent:
        nblk = l // GRID_W
        in_specs = [pl.BlockSpec((1, tt, RWKV_COLS), lambda b, i: (b, i, 0)),
                    pl.BlockSpec((1, GRID_W, RWKV_COLS),
                                 lambda b, i: (b, jnp.maximum(i * (tt // GRID_W) - 1, 0), 0)),
                    pl.BlockSpec((1, GRID_W, RWKV_COLS),
                                 lambda b, i: (b, jnp.minimum((i + 1) * (tt // GRID_W), nblk - 1), 0))]
        args = (p_rw, p_rw, p_rw)
        ext_rows = tt + 2 * GRID_W
    else:
        assert tt == l
        in_specs = [pl.BlockSpec((1, tt, RWKV_COLS), lambda b, i: (b, i, 0))]
        args = (p_rw,)
        ext_rows = tt + 16
    row = lambda b, i: (b, i, 0)
    chunk = lambda b, i: (b, i, 0, 0)
    w2 = 2 * D_RWKV
    return pl.pallas_call(
        kern,
        grid=(bsz, l // tt),
        in_specs=in_specs + const_specs,
        out_specs=[pl.BlockSpec((1, tt, w2), row), pl.BlockSpec((1, tt, w2), row),
                   pl.BlockSpec((1, cpt, CHUNK, w2), chunk), pl.BlockSpec((1, cpt, CHUNK, w2), chunk),
                   pl.BlockSpec((1, cpt, 1, w2), chunk),
                   pl.BlockSpec((1, tt, D_RWKV), row), pl.BlockSpec((1, tt, D_RWKV), row)],
        out_shape=[jax.ShapeDtypeStruct((bsz, l, w2), BF16), jax.ShapeDtypeStruct((bsz, l, w2), F32),
                   jax.ShapeDtypeStruct((bsz, nc, CHUNK, w2), BF16), jax.ShapeDtypeStruct((bsz, nc, CHUNK, w2), F32),
                   jax.ShapeDtypeStruct((bsz, nc, 1, w2), F32),
                   jax.ShapeDtypeStruct((bsz, l, D_RWKV), F32), jax.ShapeDtypeStruct((bsz, l, D_RWKV), F32)],
        scratch_shapes=[pltpu.VMEM((CHUNK, SCR_COLS), BF16)] * (2 * PREP_GROUP)
        + [pltpu.VMEM((tt, 4 * D_RWKV), F32), pltpu.VMEM((ext_rows, RWKV_COLS), F32)],
        compiler_params=_cparams(("parallel", "parallel")),
        name="prep_latent" if latent else "prep_ctx",
    )(*args, *consts)


SCAN_CHUNKS = 4
SCAN_BATCH = 2


def _scan_kernel(rbf_ref, rbb_ref, y0f_ref, y0b_ref, mxf_ref, mxb_ref, nnf_ref, nnb_ref, wcf_ref, wcb_ref,
                 s0_ref, yf_ref, yb_ref, s_ref):
    i = pl.program_id(1)

    @pl.when(i == 0)
    def _():
        s_ref[...] = s0_ref[...]

    c = CHUNK
    h0 = lax.broadcasted_iota(jnp.int32, (c, PAIR), 1) < HEAD
    row = lax.broadcasted_iota(jnp.int32, (PAIR, PAIR), 0)
    col = lax.broadcasted_iota(jnp.int32, (PAIR, PAIR), 1)
    same = (row >= c) == (col >= c)
    zb = jnp.zeros((c, PAIR), BF16)
    dirs = ((rbf_ref, y0f_ref, mxf_ref, nnf_ref, wcf_ref, yf_ref), (rbb_ref, y0b_ref, mxb_ref, nnb_ref, wcb_ref, yb_ref))
    idx = [(n, d, p) for n in range(SCAN_BATCH) for d in range(2) for p in range(N_PAIR)]
    lanes = [slice(p * PAIR, (p + 1) * PAIR) for _, _, p in idx]
    s = [s_ref[n, d, p] for n, d, p in idx]
    for q in range(SCAN_CHUNKS):
        cq = (q, SCAN_CHUNKS - 1 - q)
        rows = [slice(cq[d] * c, (cq[d] + 1) * c) for _, d, _ in idx]
        sb = [_bf(x) for x in s]
        rbar = [dirs[d][0][n, rows[k], lanes[k]] for k, (n, d, _) in enumerate(idx)]
        r_st = [jnp.concatenate([jnp.where(h0, x, zb), jnp.where(h0, zb, x)], axis=0) for x in rbar]
        y_st = [_dot_nt(r_st[k], sb[k]) for k in range(len(idx))]
        m_bd = [jnp.where(same, jnp.concatenate([x, x], axis=0), jnp.zeros((PAIR, PAIR), BF16))
                for x in (dirs[d][2][n, cq[d], :, lanes[k]] for k, (n, d, _) in enumerate(idx))]
        sm = [_dot(sb[k], m_bd[k]) for k in range(len(idx))]
        s_next = []
        for k, (n, d, p) in enumerate(idx):
            nn = dirs[d][3][n, cq[d], :, lanes[k]]
            n_bd = jnp.where(same, jnp.concatenate([nn, nn], axis=0), 0.0)
            dirs[d][5][n, rows[k], lanes[k]] = y_st[k][:c] + y_st[k][c:] + dirs[d][1][n, rows[k], lanes[k]]
            s_next.append(s[k] * dirs[d][4][n, cq[d], :, lanes[k]] + sm[k] + n_bd)
        s = s_next
    for k, (n, d, p) in enumerate(idx):
        s_ref[n, d, p] = s[k]


def _scan(rbar, y0, mx, nn, wc, s0):
    bsz, l, _ = rbar.shape
    sub, nb = SCAN_CHUNKS, SCAN_BATCH
    nc = l // (CHUNK * sub)
    row_f = pl.BlockSpec((nb, sub * CHUNK, D_RWKV), lambda b, i: (b, i, 0))
    row_b = pl.BlockSpec((nb, sub * CHUNK, D_RWKV), lambda b, i: (b, nc - 1 - i, 1))
    chk_f = pl.BlockSpec((nb, sub, CHUNK, D_RWKV), lambda b, i: (b, i, 0, 0))
    chk_b = pl.BlockSpec((nb, sub, CHUNK, D_RWKV), lambda b, i: (b, nc - 1 - i, 0, 1))
    wc_f = pl.BlockSpec((nb, sub, 1, D_RWKV), lambda b, i: (b, i, 0, 0))
    wc_b = pl.BlockSpec((nb, sub, 1, D_RWKV), lambda b, i: (b, nc - 1 - i, 0, 1))
    st_spec = pl.BlockSpec((nb, 2, N_PAIR, PAIR, PAIR), lambda b, i: (b, 0, 0, 0, 0))
    return pl.pallas_call(
        _scan_kernel,
        grid=(bsz // nb, nc),
        in_specs=[row_f, row_b, row_f, row_b, chk_f, chk_b, chk_f, chk_b, wc_f, wc_b, st_spec],
        out_specs=[row_f, pl.BlockSpec((nb, sub * CHUNK, D_RWKV), lambda b, i: (b, nc - 1 - i, 0)), st_spec],
        out_shape=[jax.ShapeDtypeStruct((bsz, l, D_RWKV), F32),
                   jax.ShapeDtypeStruct((bsz, l, D_RWKV), F32),
                   jax.ShapeDtypeStruct((bsz, 2, N_PAIR, PAIR, PAIR), F32)],
        compiler_params=_cparams(("parallel", "arbitrary")),
        name="scan",
    )(rbar, rbar, y0, y0, mx, mx, nn, nn, wc, wc, s0)


OUT_ROWS = 128


def _out_kernel(tt, yf_ref, yb_ref, bonus_ref, g_ref, cv_ref, cvp_ref, cvn_ref, x_ref, g1_ref, sh2_ref,
                sc2_ref, n2g_ref, gng_ref, gnb_ref, convw_ref, wout_ref, rwh_ref, rwl_ref, avg_ref,
                xm_ref, hx_ref, aff_ref):
    i = pl.program_id(1)
    n = pl.num_programs(1)

    cv = cv_ref[0]
    b_gate = cv[:, 0:D_CONV]
    cu = cv[:, D_CONV:2 * D_CONV] * cv[:, 2 * D_CONV:]
    cvp = cvp_ref[0]
    cvn = cvn_ref[0]
    cu_prev = jnp.where(i > 0, cvp[7:8, D_CONV:2 * D_CONV] * cvp[7:8, 2 * D_CONV:], 0.0)
    cu_next = jnp.where(i < n - 1, cvn[0:1, D_CONV:2 * D_CONV] * cvn[0:1, 2 * D_CONV:], 0.0)
    ridx = lax.broadcasted_iota(jnp.int32, cu.shape, 0)
    cu_m1 = jnp.where(ridx == 0, cu_prev, pltpu.roll(cu, 1, 0))
    cu_p1 = jnp.where(ridx == tt - 1, cu_next, pltpu.roll(cu, tt - 1, 0))
    conv = convw_ref[0:1, :] * cu_m1 + convw_ref[1:2, :] * cu + convw_ref[2:3, :] * cu_p1
    bx = (b_gate * conv).astype(BF16)

    parts = [slice(k * OUT_ROWS, (k + 1) * OUT_ROWS) for k in range(tt // OUT_ROWS)]
    avg = avg_ref[...]
    rwh = rwh_ref[...]
    rwl = rwl_ref[...]
    y = [yf_ref[0, r, :] + yb_ref[0, r, :] for r in parts]
    mu = [_seg_dot(v, avg) for v in y]
    dlt = [a - b for a, b in zip(y, mu)]
    var = [_seg_dot(v * v, avg) for v in dlt]
    yn = [a * lax.rsqrt(b + GN_EPS) * gng_ref[...] + gnb_ref[...] for a, b in zip(dlt, var)]
    ax = [((a + bonus_ref[0, r, :]) * g_ref[0, r, :]).astype(BF16) for a, r in zip(yn, parts)]
    mix = [_dot(a, wout_ref[0:D_RWKV, :]) + _dot(bx[r], wout_ref[D_RWKV:, :]) for a, r in zip(ax, parts)]
    xm = [x_ref[0, r, :] + g1_ref[0] * a for a, r in zip(mix, parts)]
    ms = [jnp.mean(v * v, axis=-1, keepdims=True) for v in xm]
    hx = [a * lax.rsqrt(b + NORM_EPS) * n2g_ref[...] for a, b in zip(xm, ms)]
    hx = [_split2(v * (1.0 + sc2_ref[0]) + sh2_ref[0]) for v in hx]
    logits = [_dot_nt(rwh, hi) + _dot_nt(rwh, lo) + _dot_nt(rwl, hi) for hi, lo in hx]
    for k, r in enumerate(parts):
        xm_ref[0, r, :] = xm[k]
        hx_ref[0, r, :] = hx[k][0]
        m = jnp.max(logits[k], axis=0, keepdims=True)
        ex = jnp.exp(logits[k] - m)
        aff_ref[0, :, r] = ex / jnp.sum(ex, axis=0, keepdims=True)


def _out(yf, yb, bonus, g, p_cv, x, g1, sh2, sc2, n2g, gng, gnb, convw, wout, rwh, rwl, avg, tt):
    bsz, t, d = x.shape
    ne = rwh.shape[0]
    nb8 = t // 8
    row = lambda b, i: (b, i, 0)
    per_b = lambda b, i: (b, 0, 0)
    const2 = lambda b, i: (0, 0)
    return pl.pallas_call(
        functools.partial(_out_kernel, tt),
        grid=(bsz, t // tt),
        in_specs=[pl.BlockSpec((1, tt, D_RWKV), row), pl.BlockSpec((1, tt, D_RWKV), row),
                  pl.BlockSpec((1, tt, D_RWKV), row), pl.BlockSpec((1, tt, D_RWKV), row),
                  pl.BlockSpec((1, tt, CONV_COLS), row),
                  pl.BlockSpec((1, 8, CONV_COLS), lambda b, i: (b, jnp.maximum(i * (tt // 8) - 1, 0), 0)),
                  pl.BlockSpec((1, 8, CONV_COLS), lambda b, i: (b, jnp.minimum((i + 1) * (tt // 8), nb8 - 1), 0)),
                  pl.BlockSpec((1, tt, d), row),
                  pl.BlockSpec((1, 1, d), per_b), pl.BlockSpec((1, 1, d), per_b), pl.BlockSpec((1, 1, d), per_b),
                  pl.BlockSpec((1, d), const2), pl.BlockSpec((1, D_RWKV), const2), pl.BlockSpec((1, D_RWKV), const2),
                  pl.BlockSpec((3, D_CONV), const2), pl.BlockSpec((D_RWKV + D_CONV, d), const2),
                  pl.BlockSpec((ne, d), const2), pl.BlockSpec((ne, d), const2),
                  pl.BlockSpec((D_RWKV, D_RWKV), const2)],
        out_specs=[pl.BlockSpec((1, tt, d), row), pl.BlockSpec((1, tt, d), row),
                   pl.BlockSpec((1, ne, tt), lambda b, i: (b, 0, i))],
        out_shape=[jax.ShapeDtypeStruct((bsz, t, d), F32), jax.ShapeDtypeStruct((bsz, t, d), BF16),
                   jax.ShapeDtypeStruct((bsz, ne, t), F32)],
        compiler_params=_cparams(("parallel", "parallel")),
        name="out",
    )(yf, yb, bonus, g, p_cv, p_cv, p_cv, x, g1, sh2, sc2, n2g, gng, gnb, convw, wout, rwh, rwl, avg)


def _prefix_blocks(mask_fn, t, tri, emit):
    carry = None
    for j in range(t // 128):
        m = mask_fn(j)
        inc = _dot(m.astype(BF16), tri)
        carry = jnp.zeros_like(inc[:, 0:1]) if carry is None else carry
        emit(j, m, inc - m + carry)
        carry = carry + inc[:, 127:128]


def _topk_kernel(cap, sb, aff_ref, tri_ref, cnt_ref, slot_ref, edge_ref):
    t = aff_ref.shape[2]
    aff = aff_ref[0]

    def body(k, bits):
        cand = bits | jnp.left_shift(jnp.int32(1), 30 - k)
        cnt = jnp.sum(jnp.where(aff >= pltpu.bitcast(cand, F32), 1, 0), axis=-1, keepdims=True)
        return jnp.where(cnt >= cap, cand, bits)

    bits = lax.fori_loop(0, 31, body, jnp.zeros((aff.shape[0], 1), jnp.int32))
    thr = pltpu.bitcast(bits, F32)
    above = pltpu.bitcast(bits + 1, F32)
    n_gt = jnp.sum(jnp.where(aff >= above, 1, 0), axis=-1, keepdims=True)
    need = (cap - n_gt).astype(F32)
    tri = tri_ref[...]

    def blk(j):
        return aff[:, j * 128:(j + 1) * 128]

    def emit_sel(j, eq, before):
        take = (blk(j) >= above) | ((eq > 0.5) & (before < need))
        slot_ref[0, :, j * 128:(j + 1) * 128] = jnp.where(take, 1, 0)

    _prefix_blocks(lambda j: jnp.where((blk(j) >= thr) & (blk(j) < above), 1.0, 0.0), t, tri, emit_sel)

    def emit_slot(j, m, before):
        count = before.astype(jnp.int32)
        cnt_ref[0, :, j * 128:(j + 1) * 128] = count
        slot_ref[0, :, j * 128:(j + 1) * 128] = jnp.where(m > 0.5, count, -1)

    _prefix_blocks(lambda j: slot_ref[0, :, j * 128:(j + 1) * 128].astype(F32), t, tri, emit_slot)

    cnt = cnt_ref[0]
    lane = lax.broadcasted_iota(jnp.int32, (cnt.shape[0], 128), 1)
    edges = jnp.zeros((cnt.shape[0], 128), jnp.int32)
    for s in range(1, cap // sb + 1):
        below = jnp.sum(jnp.where(cnt < s * sb, 1, 0), axis=-1, keepdims=True)
        edges = jnp.where(lane == s, below, edges)
    edge_ref[0] = edges


def _topk(aff_t, tri, cap, sb):
    bsz, ne, t = aff_t.shape
    spec = pl.BlockSpec((1, ne, t), lambda b: (b, 0, 0))
    return pl.pallas_call(
        functools.partial(_topk_kernel, cap, sb),
        grid=(bsz,),
        in_specs=[spec, pl.BlockSpec((128, 128), lambda b: (0, 0))],
        out_specs=[spec, spec, pl.BlockSpec((1, ne, 128), lambda b: (b, 0, 0))],
        out_shape=[jax.ShapeDtypeStruct((bsz, ne, t), jnp.int32)] * 2 + [jax.ShapeDtypeStruct((bsz, ne, 128), jnp.int32)],
        compiler_params=_cparams(("parallel",)),
        name="topk",
    )(aff_t, tri)


TOKEN_ROW = 128


def _moe_kernel(win, sb, edge_ref, hx_ref, slot_ref, wg_ref, wu_ref, wd_ref, ye_ref, xs_ref):
    b = pl.program_id(0)
    e = pl.program_id(1)
    ne = pl.num_programs(1)
    cap = xs_ref.shape[0]
    nblk = cap // sb
    nrow = hx_ref.shape[1] // TOKEN_ROW
    wrows = win // TOKEN_ROW
    base = (b * ne + e) * (nblk + 1)
    for s in range(nblk):
        blk = slice(s * sb, (s + 1) * sb)
        target = lax.broadcasted_iota(jnp.int32, (sb, TOKEN_ROW), 0) + s * sb
        r0 = jnp.minimum(edge_ref[base + s] // TOKEN_ROW, nrow - wrows)
        rows = slot_ref[0, 0, pl.ds(r0, wrows), :]
        onehot = jnp.concatenate([jnp.where(rows[k:k + 1, :] == target, 1.0, 0.0).astype(BF16)
                                  for k in range(wrows)], axis=1)
        t0 = pl.multiple_of(r0 * TOKEN_ROW, TOKEN_ROW)
        xs_ref[blk, :] = _dot(onehot, hx_ref[0, pl.ds(t0, win), :])

        def extra_row(r, carry, blk=blk, target=target):
            hit = jnp.where(slot_ref[0, 0, pl.ds(r, 1), :] == target, 1.0, 0.0).astype(BF16)
            tr = pl.multiple_of(r * TOKEN_ROW, TOKEN_ROW)
            xs_ref[blk, :] += _dot(hit, hx_ref[0, pl.ds(tr, TOKEN_ROW), :])
            return carry

        r_end = (edge_ref[base + s + 1] + TOKEN_ROW - 1) // TOKEN_ROW
        lax.fori_loop(r0 + wrows, r_end, extra_row, 0)

    xs = xs_ref[...].astype(BF16)
    h1 = _dot(xs, wg_ref[0].astype(BF16))
    h2 = _dot(xs, wu_ref[0].astype(BF16))
    hid = (h1 * jax.nn.sigmoid(h1) * h2).astype(BF16)
    ye_ref[0, 0] = _dot(hid, wd_ref[0].astype(BF16)).astype(BF16)


def _moe(edges, hx, slot4, wg, wu, wd, cap, win, sb):
    bsz, t, d = hx.shape
    ne, _, f = wg.shape
    nrow = t // TOKEN_ROW
    grid_spec = pltpu.PrefetchScalarGridSpec(
        num_scalar_prefetch=1,
        grid=(bsz, ne),
        in_specs=[pl.BlockSpec((1, t, d), lambda b, e, s: (b, 0, 0), pipeline_mode=pl.Buffered(1)),
                  pl.BlockSpec((1, 1, nrow, TOKEN_ROW), lambda b, e, s: (b, e, 0, 0)),
                  pl.BlockSpec((1, d, f), lambda b, e, s: (e, 0, 0)),
                  pl.BlockSpec((1, d, f), lambda b, e, s: (e, 0, 0)),
                  pl.BlockSpec((1, f, d), lambda b, e, s: (e, 0, 0))],
        out_specs=pl.BlockSpec((1, 1, cap, d), lambda b, e, s: (b, e, 0, 0)),
        scratch_shapes=[pltpu.VMEM((cap, d), F32)],
    )
    return pl.pallas_call(
        functools.partial(_moe_kernel, win, sb),
        grid_spec=grid_spec,
        out_shape=jax.ShapeDtypeStruct((bsz, ne, cap, d), BF16),
        compiler_params=_cparams(("parallel", "arbitrary")),
        name="moe",
    )(edges, hx, slot4, wg, wu, wd)


COMB_ROWS = 128
COMB_COLS = 256
SLOT_ALIGN = 16


def _comb_kernel(tk, win, tsp_ref, xm_ref, ye_ref, slotc_ref, affc_ref, g2_ref, fg_ref, o_ref):
    b = pl.program_id(0)
    j = pl.program_id(1)
    nh = tk // COMB_ROWS
    ntile = pl.num_programs(1) * nh
    ne, cap, d = ye_ref.shape[1], ye_ref.shape[2], ye_ref.shape[3]
    slotc = slotc_ref[0]
    affc = affc_ref[0]
    lane_slot = lax.broadcasted_iota(jnp.int32, (COMB_ROWS, win), 1)
    for h in range(nh):
        rows = slice(h * COMB_ROWS, (h + 1) * COMB_ROWS)
        starts, onehots, vals = [], [], []
        for e in range(ne):
            first = tsp_ref[(b * ne + e) * ntile + j * nh + h]
            start = pl.multiple_of(jnp.minimum(first & ~(SLOT_ALIGN - 1), cap - win), SLOT_ALIGN)
            starts.append(start)
            onehots.append(jnp.where(slotc[rows, e:e + 1] - start == lane_slot, 1.0, 0.0).astype(BF16))
            vals.append(jnp.broadcast_to(affc[rows, e:e + 1], (COMB_ROWS, COMB_COLS)))
        xo, ssq = [], None
        for q in range(d // COMB_COLS):
            cols = slice(q * COMB_COLS, (q + 1) * COMB_COLS)
            acc = None
            for e in range(ne):
                part = vals[e] * _dot(onehots[e], ye_ref[0, e, pl.ds(starts[e], win), cols])
                acc = part if acc is None else acc + part
            x = xm_ref[0, rows, cols] + g2_ref[0, :, cols] * acc
            xo.append(x)
            sq = jnp.sum(x * x, axis=-1, keepdims=True)
            ssq = sq if ssq is None else ssq + sq
        scale = lax.rsqrt(ssq / d + NORM_EPS)
        for q in range(d // COMB_COLS):
            cols = slice(q * COMB_COLS, (q + 1) * COMB_COLS)
            o_ref[0, rows, cols] = xo[q] * scale * fg_ref[:, cols]


def _comb(tsp, xm, ye, slotc, affc, g2, fg, tk):
    bsz, t, d = xm.shape
    ne, cap = ye.shape[1], ye.shape[2]
    win = min(2 * COMB_ROWS, cap)
    assert win == cap or win >= COMB_ROWS + SLOT_ALIGN
    grid_spec = pltpu.PrefetchScalarGridSpec(
        num_scalar_prefetch=1,
        grid=(bsz, t // tk),
        in_specs=[pl.BlockSpec((1, tk, d), lambda b, j, s: (b, j, 0)),
                  pl.BlockSpec((1, ne, cap, d), lambda b, j, s: (b, 0, 0, 0)),
                  pl.BlockSpec((1, tk, ne), lambda b, j, s: (b, j, 0)),
                  pl.BlockSpec((1, tk, ne), lambda b, j, s: (b, j, 0)),
                  pl.BlockSpec((1, 1, d), lambda b, j, s: (b, 0, 0)),
                  pl.BlockSpec((1, d), lambda b, j, s: (0, 0))],
        out_specs=pl.BlockSpec((1, tk, d), lambda b, j, s: (b, j, 0)),
    )
    return pl.pallas_call(
        functools.partial(_comb_kernel, tk, win),
        grid_spec=grid_spec,
        out_shape=jax.ShapeDtypeStruct((bsz, t, d), F32),
        compiler_params=_cparams(("parallel", "arbitrary")),
        name="comb",
    )(tsp, xm, ye, slotc, affc, g2, fg)


def _block_diag_ones(n, blk, value=1.0):
    r = jnp.arange(n)
    return jnp.where((r[:, None] // blk) == (r[None, :] // blk), value, 0.0)


def kernel(x, c, ctx, c_ctx, ada_w, ada_b, norm1_g, norm2_g, w_in, shift_mu, w0, w_lora_up, a0, a_lora_up, k_k, k_a,
           r_k, g_lora_up, gn_g, gn_b, conv_w, w_out, router_w, exp_w_gate, exp_w_up, exp_w_down, final_g):
    bsz, t, d = x.shape
    lc = ctx.shape[1]
    ne = router_w.shape[-1]
    cap = EC_CAPACITY * t // ne
    tt = 256
    sb = min(128, cap)
    win = min(1536, t)
    l = 0

    rows = ((bsz + 1 + 7) // 8) * 8
    cc = jnp.zeros((rows, d), F32).at[:bsz].set(c).at[bsz].set(c_ctx)
    mod = _mod(cc, ada_w[l], ada_b[l][None, :])
    sh1, sc1, g1, sh2, sc2, g2 = (m[:, None, :] for m in jnp.split(mod[:bsz], 6, axis=-1))
    csh1, csc1 = (jnp.broadcast_to(m[None, None, :], (bsz, 1, d)) for m in jnp.split(mod[bsz], 6)[:2])

    w_rw = w_in[l][:, :RWKV_COLS].astype(BF16)
    w_cv = w_in[l][:, RWKV_COLS:].astype(BF16)
    n1g = norm1_g[l][None, :]
    px_rw, px_cv = _in_proj(x, sh1, sc1, n1g, w_rw, w_cv, 512)
    pc_rw, _ = _in_proj(ctx, csh1, csc1, n1g, w_rw, w_cv, 256)

    zw = jnp.zeros((LORA_W, 2 * D_RWKV), F32)
    lora = jnp.concatenate([
        jnp.concatenate([w_lora_up[l, 0], w_lora_up[l, 1], zw], axis=1),
        jnp.concatenate([zw, a_lora_up[l, 0], a_lora_up[l, 1]], axis=1)], axis=0).astype(BF16)
    ridx = jnp.arange(CHUNK)
    tril = jnp.where(ridx[None, :] <= ridx[:, None], 1.0, 0.0).astype(BF16)
    triu = jnp.where(ridx[None, :] >= ridx[:, None], 1.0, 0.0).astype(BF16)
    ones_bd = _block_diag_ones(PAIR, HEAD).astype(BF16)
    consts = (shift_mu[l][None, :], lora, w0[l], a0[l], k_k[l][None, :], k_a[l][None, :],
              r_k[l].reshape(1, D_RWKV), g_lora_up[l].astype(BF16), ones_bd, tril, triu)

    chunks_c = _prep(pc_rw, consts, False, lc)[:5]
    *chunks_x, bonus, gate = _prep(px_rw, consts, True, tt)

    s_zero = jnp.zeros((bsz, 2, N_PAIR, PAIR, PAIR), F32)
    _, _, s_ctx = _scan(*chunks_c, s_zero)
    yf, yb, _ = _scan(*chunks_x, s_ctx)

    rw_t = router_w[l].T
    rwh = rw_t.astype(BF16)
    rwl = (rw_t - rwh.astype(F32)).astype(BF16)
    avg = _block_diag_ones(D_RWKV, HEAD, 1.0 / HEAD).astype(BF16)
    xm, hx, aff_t = _out(yf, yb, bonus, gate, px_cv, x, g1, sh2, sc2, norm2_g[l][None, :], gn_g[l][None, :],
                         gn_b[l][None, :], conv_w[l], w_out[l].astype(BF16), rwh, rwl, avg, 2 * tt)

    r128 = jnp.arange(128)
    tri128 = jnp.where(r128[:, None] <= r128[None, :], 1.0, 0.0).astype(BF16)
    cnt, slot, edges = _topk(aff_t, tri128, cap, sb)

    ye = _moe(edges[:, :, :cap // sb + 1].reshape(-1), hx, slot.reshape(bsz, ne, t // TOKEN_ROW, TOKEN_ROW),
              exp_w_gate[l], exp_w_up[l], exp_w_down[l], cap, win, sb)
    tr = lambda a: jnp.transpose(a, (0, 2, 1))
    first_slot = cnt[:, :, ::COMB_ROWS].reshape(-1)
    return _comb(first_slot, xm, ye, tr(slot), tr(aff_t), g2, final_g[None, :], tt)
```

```python
import functools
import math

import jax
import jax.numpy as jnp
from jax import lax
from jax.experimental import pallas as pl
from jax.experimental.pallas import tpu as pltpu

F32 = jnp.float32
BF16 = jnp.bfloat16
HIGHEST = lax.Precision.HIGHEST

GRID_W = 64
D_RWKV = 512
D_CONV = 512
HEAD = 64
LORA_W = 64
LORA_A = 64
LORA_G = 128
N_EXPERTS = 16
EC_CAPACITY = 2
NORM_EPS = 1e-6
GN_EPS = 64e-5
RWKV_COLS = 3 * D_RWKV + LORA_W + LORA_A + LORA_G
CONV_COLS = 3 * D_CONV

CHUNK = 64
PAIR = 2 * HEAD
N_PAIR = D_RWKV // PAIR
VMEM_LIMIT = 48 * 1024 * 1024


def _cparams(sem):
    return pltpu.CompilerParams(dimension_semantics=sem, vmem_limit_bytes=VMEM_LIMIT)


def _dot(a, b):
    return jnp.dot(a, b, preferred_element_type=F32)


def _dot_nt(a, b):
    return lax.dot_general(a, b, (((1,), (1,)), ((), ())), preferred_element_type=F32)


def _split2(x):
    hi = x.astype(BF16)
    lo = (x - hi.astype(F32)).astype(BF16)
    return hi, lo


def _split3(x):
    hi = x.astype(BF16)
    r = x - hi.astype(F32)
    mid = r.astype(BF16)
    lo = (r - mid.astype(F32)).astype(BF16)
    return hi, mid, lo


def _seg_dot(x, m):
    hi, lo = _split2(x)
    return _dot(hi, m) + _dot(lo, m)


def _mod_kernel(c_ref, w_ref, b_ref, o_ref):
    c = c_ref[...]
    s = c * jax.nn.sigmoid(c)
    o_ref[...] = jnp.dot(s, w_ref[...], precision=HIGHEST, preferred_element_type=F32) + b_ref[...]


def _mod(cc, w, b):
    rows, d = cc.shape
    n = w.shape[1]
    tn = 1024
    return pl.pallas_call(
        _mod_kernel,
        grid=(n // tn,),
        in_specs=[pl.BlockSpec((rows, d), lambda j: (0, 0)),
                  pl.BlockSpec((d, tn), lambda j: (0, j)),
                  pl.BlockSpec((1, tn), lambda j: (0, j))],
        out_specs=pl.BlockSpec((rows, tn), lambda j: (0, j)),
        out_shape=jax.ShapeDtypeStruct((rows, n), F32),
        compiler_params=_cparams(("parallel",)),
        name="mod",
    )(cc, w, b)


def _in_proj_kernel(x_ref, sh_ref, sc_ref, g_ref, wrw_ref, wcv_ref, orw_ref, ocv_ref):
    x = x_ref[0]
    ms = jnp.mean(x * x, axis=-1, keepdims=True)
    h = x * lax.rsqrt(ms + NORM_EPS) * g_ref[...]
    h = (h * (1.0 + sc_ref[0]) + sh_ref[0]).astype(BF16)
    orw_ref[0] = _dot(h, wrw_ref[...])
    ocv_ref[0] = _dot(h, wcv_ref[...])


def _in_proj(x, sh, sc, g, w_rw, w_cv, tm):
    bsz, l, d = x.shape
    return pl.pallas_call(
        _in_proj_kernel,
        grid=(bsz, l // tm),
        in_specs=[pl.BlockSpec((1, tm, d), lambda b, i: (b, i, 0)),
                  pl.BlockSpec((1, 1, d), lambda b, i: (b, 0, 0)),
                  pl.BlockSpec((1, 1, d), lambda b, i: (b, 0, 0)),
                  pl.BlockSpec((1, d), lambda b, i: (0, 0)),
                  pl.BlockSpec((d, RWKV_COLS), lambda b, i: (0, 0)),
                  pl.BlockSpec((d, CONV_COLS), lambda b, i: (0, 0))],
        out_specs=[pl.BlockSpec((1, tm, RWKV_COLS), lambda b, i: (b, i, 0)),
                   pl.BlockSpec((1, tm, CONV_COLS), lambda b, i: (b, i, 0))],
        out_shape=[jax.ShapeDtypeStruct((bsz, l, RWKV_COLS), F32),
                   jax.ShapeDtypeStruct((bsz, l, CONV_COLS), F32)],
        compiler_params=_cparams(("parallel", "parallel")),
        name="in_proj",
    )(x, sh, sc, g, w_rw, w_cv)


def _bf(x):
    return x.astype(BF16)


def _chunk_local(chains):
    c = CHUNK
    n = range(len(chains))
    lane = lax.broadcasted_iota(jnp.int32, (c, PAIR), 1)
    h0 = lane < HEAD
    row = lax.broadcasted_iota(jnp.int32, (PAIR, PAIR), 0)
    col = lax.broadcasted_iota(jnp.int32, (PAIR, PAIR), 1)
    same = (row >= c) == (col >= c)
    tr, tc = row & (c - 1), col & (c - 1)
    eye = jnp.where(row == col, 1.0, 0.0)
    masks = {rev: (same & ((tc > tr) if rev else (tc < tr)), same & ((tc >= tr) if rev else (tc <= tr)))
             for rev in (False, True)}
    zb = jnp.zeros((c, PAIR), BF16)

    def stack(x):
        return jnp.concatenate([jnp.where(h0, x, zb), jnp.where(h0, zb, x)], axis=0)

    a_st = [stack(ch[1]) for ch in chains]
    r_st = [stack(ch[2]) for ch in chains]
    gram = [_dot_nt(jnp.concatenate([a_st[i], r_st[i]], axis=0),
                    jnp.concatenate([chains[i][3], chains[i][3], chains[i][4], chains[i][4]], axis=0)) for i in n]
    lab = [jnp.where(masks[chains[i][0]][0], gram[i][:PAIR, :PAIR], 0.0) for i in n]
    lak = [jnp.where(masks[chains[i][0]][0], gram[i][:PAIR, PAIR:], 0.0) for i in n]
    mrb = [jnp.where(masks[chains[i][0]][1], gram[i][PAIR:, :PAIR], 0.0) for i in n]
    mrk = [jnp.where(masks[chains[i][0]][1], gram[i][PAIR:, PAIR:], 0.0) for i in n]

    labb = [_bf(x) for x in lab]
    pw = [_dot(labb[i], labb[i]) for i in n]
    tp = [eye + lab[i] for i in n]
    for _ in range(4):
        pwb = [_bf(x) for x in pw]
        both = [_dot(pwb[i], jnp.concatenate([pwb[i], _bf(tp[i])], axis=1)) for i in n]
        pw = [x[:, :PAIR] for x in both]
        tp = [tp[i] + both[i][:, PAIR:] for i in n]
    tinv = [tp[i] + _dot(_bf(pw[i]), _bf(tp[i])) for i in n]

    v_rep = [jnp.concatenate([ch[7], ch[7]], axis=0) for ch in chains]
    lakv = [_dot(_bf(lak[i]), v_rep[i]) for i in n]
    x = [_dot(_bf(tinv[i]), jnp.concatenate([a_st[i], _bf(lakv[i])], axis=1)) for i in n]
    abar = [xi[:, :PAIR] for xi in x]
    u0 = [jnp.where(same, xi[:, PAIR:], 0.0) for xi in x]
    au = [jnp.concatenate([abar[i], u0[i]], axis=1) for i in n]
    aub = [_bf(x) for x in au]
    z = [_dot(_bf(mrb[i]), aub[i]) for i in n]
    zv = [_dot(_bf(mrk[i]), v_rep[i]) for i in n]
    rbar = [r_st[i].astype(F32) + z[i][:, :PAIR] for i in n]
    rbar = [_bf(x[:c] + x[c:]) for x in rbar]
    y0 = [z[i][:, PAIR:] + zv[i] for i in n]
    y0 = [jnp.where(h0, x[:c], x[c:]) for x in y0]

    mn = [_dot(_bf(au[i].T), jnp.concatenate([chains[i][5], chains[i][5]], axis=0)) for i in n]
    vk = [_dot(_bf(chains[i][7].astype(F32).T), chains[i][6]) for i in n]
    mx = [_bf(jnp.where(h0, m[:c], m[c:PAIR])) for m in mn]
    nn = [mn[i][PAIR:] + vk[i] for i in n]
    nn = [jnp.where(h0, x[:c], x[c:]) for x in nn]
    return list(zip(rbar, y0, mx, nn))


DIR_COLS = 6 * D_RWKV
SCR_COLS = 2 * DIR_COLS + D_RWKV


def _head_sums(x, ones):
    return jnp.concatenate([_seg_dot(x[:, g * PAIR:(g + 1) * PAIR], ones) for g in range(N_PAIR)], axis=1)


def _prep_rows(c, p, shifted, lo, mu_ref, w0_ref, a0_ref, kk_ref, ka_ref, rk_ref, ones_ref, tril_ref, triu_ref,
               wc_ref, bonus_ref, scr_ref):
    rows = slice(c * CHUNK, (c + 1) * CHUNK)
    pm = p + mu_ref[:, 0:3 * D_RWKV] * (shifted - p)
    xr = pm[:, 0:D_RWKV]
    xk = pm[:, D_RWKV:2 * D_RWKV]
    xv = pm[:, 2 * D_RWKV:3 * D_RWKV]

    ones = ones_ref[...]
    kraw = xk * kk_ref[...]
    kk = kraw / jnp.maximum(jnp.sqrt(_head_sums(kraw * kraw, ones)), 1e-12)

    scr_ref[:, 2 * DIR_COLS:] = xv.astype(BF16)
    ksum = None
    for d in range(2):
        z = w0_ref[d:d + 1, :] + lo[:, d * D_RWKV:(d + 1) * D_RWKV]
        ld = -math.exp(-0.5) * jax.nn.sigmoid(z)
        ag = jax.nn.sigmoid(a0_ref[d:d + 1, :] + lo[:, (2 + d) * D_RWKV:(3 + d) * D_RWKV])
        kd = xk * (1.0 + (ag - 1.0) * ka_ref[...])
        bb = kk * ag
        ksum = kd if ksum is None else ksum + kd
        tri = tril_ref[...] if d == 0 else triu_ref[...]
        h3 = _split3(ld)
        cs = _dot(tri, h3[0]) + _dot(tri, h3[1]) + _dot(tri, h3[2])
        e_in = jnp.exp(cs)
        e_ex = jnp.exp(cs - ld)
        e_neg = jnp.exp(-cs)
        bt = bb * e_neg
        kt = kd * e_neg
        edge = CHUNK - 1 if d == 0 else 0
        wrow = e_in[edge:edge + 1, :]
        wc_ref[0, c, :, d * D_RWKV:(d + 1) * D_RWKV] = wrow
        base = d * DIR_COLS
        for j, val in enumerate((-kk * e_ex, xr * e_in, bt, kt, bt * wrow, kt * wrow)):
            scr_ref[:, base + j * D_RWKV:base + (j + 1) * D_RWKV] = val.astype(BF16)

    bonus_ref[0, rows, :] = _head_sums(xr * ksum * rk_ref[...], ones) * xv


PREP_GROUP = 2


def _prep_chunks(cs, scr_refs, rbar_ref, y0_ref, mx_ref, nn_ref):
    chains = []
    for scr_ref in scr_refs:
        for d in range(2):
            for pp in range(N_PAIR):
                cols = [d * DIR_COLS + j * D_RWKV + pp * PAIR for j in range(6)] + [2 * DIR_COLS + pp * PAIR]
                chains.append((d == 1,) + tuple(scr_ref[:, k:k + PAIR] for k in cols))
    per_chunk = 2 * N_PAIR
    for idx, (rbar, y0, mx, nn) in enumerate(_chunk_local(chains)):
        c = cs[idx // per_chunk]
        rows = slice(c * CHUNK, (c + 1) * CHUNK)
        col = (idx % per_chunk // N_PAIR) * D_RWKV + (idx % N_PAIR) * PAIR
        rbar_ref[0, rows, col:col + PAIR] = rbar
        y0_ref[0, rows, col:col + PAIR] = y0
        mx_ref[0, c, :, col:col + PAIR] = mx
        nn_ref[0, c, :, col:col + PAIR] = nn


N_PREP_CONSTS = 11


def _prep_kernel(latent, tt, p_ref, *rest):
    if latent:
        prev_ref, next_ref = rest[:2]
        rest = rest[2:]
    (mu_ref, lw_ref, w0_ref, a0_ref, kk_ref, ka_ref, rk_ref, gup_ref, ones_ref, tril_ref,
     triu_ref) = rest[:N_PREP_CONSTS]
    rbar_ref, y0_ref, mx_ref, nn_ref, wc_ref, bonus_ref, g_ref = rest[N_PREP_CONSTS:N_PREP_CONSTS + 7]
    *scr, lo_ref, ext_ref = rest[N_PREP_CONSTS + 7:]
    if latent:
        i = pl.program_id(1)
        n = pl.num_programs(1)
        ext_ref[0:GRID_W] = jnp.where(i > 0, prev_ref[0], 0.0)
        ext_ref[GRID_W:GRID_W + tt] = p_ref[0]
        ext_ref[GRID_W + tt:] = jnp.where(i < n - 1, next_ref[0], 0.0)

        def shifted(r0, nrows, cols):
            shape = (nrows, cols.stop - cols.start)
            trow = lax.broadcasted_iota(jnp.int32, shape, 0) & (GRID_W - 1)
            q = lax.broadcasted_iota(jnp.int32, shape, 1) & 3
            left = jnp.where(trow == 0, 0.0, ext_ref[GRID_W - 1 + r0:GRID_W - 1 + r0 + nrows, cols])
            right = jnp.where(trow == GRID_W - 1, 0.0, ext_ref[GRID_W + 1 + r0:GRID_W + 1 + r0 + nrows, cols])
            up = ext_ref[r0:r0 + nrows, cols]
            down = ext_ref[2 * GRID_W + r0:2 * GRID_W + r0 + nrows, cols]
            return jnp.where(q == 0, left, jnp.where(q == 1, right, jnp.where(q == 2, up, down)))
    else:
        ext_ref[0:8] = jnp.zeros((8, RWKV_COLS), F32)
        ext_ref[8:8 + tt] = p_ref[0]
        ext_ref[8 + tt:] = jnp.zeros((8, RWKV_COLS), F32)

        def shifted(r0, nrows, cols):
            q = lax.broadcasted_iota(jnp.int32, (nrows, cols.stop - cols.start), 1) & 1
            return jnp.where(q == 0, ext_ref[7 + r0:7 + r0 + nrows, cols], ext_ref[9 + r0:9 + r0 + nrows, cols])

    tail = slice(3 * D_RWKV, RWKV_COLS)
    p_t = p_ref[0, :, tail]
    pm_t = p_t + mu_ref[:, tail] * (shifted(0, tt, tail) - p_t)
    xwa = pm_t[:, 0:LORA_W + LORA_A]
    lane = lax.broadcasted_iota(jnp.int32, xwa.shape, 1)
    lin = jnp.where(lane < LORA_W, jnp.tanh(xwa), xwa).astype(BF16)
    lo_ref[...] = _dot(lin, lw_ref[...])
    g_ref[0] = _dot(jax.nn.sigmoid(pm_t[:, LORA_W + LORA_A:]).astype(BF16), gup_ref[...])

    head = slice(0, 3 * D_RWKV)

    def rows_part(c):
        rows = slice(c * CHUNK, (c + 1) * CHUNK)
        _prep_rows(c, p_ref[0, rows, head], shifted(c * CHUNK, CHUNK, head), lo_ref[rows, :], mu_ref, w0_ref,
                   a0_ref, kk_ref, ka_ref, rk_ref, ones_ref, tril_ref, triu_ref, wc_ref, bonus_ref,
                   scr[c % (2 * PREP_GROUP)])

    groups = [list(range(g, g + PREP_GROUP)) for g in range(0, tt // CHUNK, PREP_GROUP)]
    for c in groups[0]:
        rows_part(c)
    for k, cs in enumerate(groups):
        if k + 1 < len(groups):
            for c in groups[k + 1]:
                rows_part(c)
        _prep_chunks(cs, [scr[c % (2 * PREP_GROUP)] for c in cs], rbar_ref, y0_ref, mx_ref, nn_ref)


def _prep(p_rw, consts, latent, tt):
    bsz, l, _ = p_rw.shape
    nc = l // CHUNK
    cpt = tt // CHUNK
    assert len(consts) == N_PREP_CONSTS
    const_specs = [pl.BlockSpec(c.shape, lambda b, i, nd=c.ndim: (0,) * nd) for c in consts]
    kern = functools.partial(_prep_kernel, latent, tt)
    if latent:
        nblk = l // GRID_W
        in_specs = [pl.BlockSpec((1, tt, RWKV_COLS), lambda b, i: (b, i, 0)),
                    pl.BlockSpec((1, GRID_W, RWKV_COLS),
                                 lambda b, i: (b, jnp.maximum(i * (tt // GRID_W) - 1, 0), 0)),
                    pl.BlockSpec((1, GRID_W, RWKV_COLS),
                                 lambda b, i: (b, jnp.minimum((i + 1) * (tt // GRID_W), nblk - 1), 0))]
        args = (p_rw, p_rw, p_rw)
        ext_rows = tt + 2 * GRID_W
    else:
        assert tt == l
        in_specs = [pl.BlockSpec((1, tt, RWKV_COLS), lambda b, i: (b, i, 0))]
        args = (p_rw,)
        ext_rows = tt + 16
    row = lambda b, i: (b, i, 0)
    chunk = lambda b, i: (b, i, 0, 0)
    w2 = 2 * D_RWKV
    return pl.pallas_call(
        kern,
        grid=(bsz, l // tt),
        in_specs=in_specs + const_specs,
        out_specs=[pl.BlockSpec((1, tt, w2), row), pl.BlockSpec((1, tt, w2), row),
                   pl.BlockSpec((1, cpt, CHUNK, w2), chunk), pl.BlockSpec((1, cpt, CHUNK, w2), chunk),
                   pl.BlockSpec((1, cpt, 1, w2), chunk),
                   pl.BlockSpec((1, tt, D_RWKV), row), pl.BlockSpec((1, tt, D_RWKV), row)],
        out_shape=[jax.ShapeDtypeStruct((bsz, l, w2), BF16), jax.ShapeDtypeStruct((bsz, l, w2), F32),
                   jax.ShapeDtypeStruct((bsz, nc, CHUNK, w2), BF16), jax.ShapeDtypeStruct((bsz, nc, CHUNK, w2), F32),
                   jax.ShapeDtypeStruct((bsz, nc, 1, w2), F32),
                   jax.ShapeDtypeStruct((bsz, l, D_RWKV), F32), jax.ShapeDtypeStruct((bsz, l, D_RWKV), F32)],
        scratch_shapes=[pltpu.VMEM((CHUNK, SCR_COLS), BF16)] * (2 * PREP_GROUP)
        + [pltpu.VMEM((tt, 4 * D_RWKV), F32), pltpu.VMEM((ext_rows, RWKV_COLS), F32)],
        compiler_params=_cparams(("parallel", "parallel")),
        name="prep_latent" if latent else "prep_ctx",
    )(*args, *consts)


SCAN_CHUNKS = 4
SCAN_BATCH = 2


def _scan_kernel(rbf_ref, rbb_ref, y0f_ref, y0b_ref, mxf_ref, mxb_ref, nnf_ref, nnb_ref, wcf_ref, wcb_ref,
                 s0_ref, yf_ref, yb_ref, s_ref):
    i = pl.program_id(1)

    @pl.when(i == 0)
    def _():
        s_ref[...] = s0_ref[...]

    c = CHUNK
    h0 = lax.broadcasted_iota(jnp.int32, (c, PAIR), 1) < HEAD
    row = lax.broadcasted_iota(jnp.int32, (PAIR, PAIR), 0)
    col = lax.broadcasted_iota(jnp.int32, (PAIR, PAIR), 1)
    same = (row >= c) == (col >= c)
    zb = jnp.zeros((c, PAIR), BF16)
    dirs = ((rbf_ref, y0f_ref, mxf_ref, nnf_ref, wcf_ref, yf_ref), (rbb_ref, y0b_ref, mxb_ref, nnb_ref, wcb_ref, yb_ref))
    idx = [(n, d, p) for n in range(SCAN_BATCH) for d in range(2) for p in range(N_PAIR)]
    lanes = [slice(p * PAIR, (p + 1) * PAIR) for _, _, p in idx]
    s = [s_ref[n, d, p] for n, d, p in idx]
    for q in range(SCAN_CHUNKS):
        cq = (q, SCAN_CHUNKS - 1 - q)
        rows = [slice(cq[d] * c, (cq[d] + 1) * c) for _, d, _ in idx]
        sb = [_bf(x) for x in s]
        rbar = [dirs[d][0][n, rows[k], lanes[k]] for k, (n, d, _) in enumerate(idx)]
        r_st = [jnp.concatenate([jnp.where(h0, x, zb), jnp.where(h0, zb, x)], axis=0) for x in rbar]
        y_st = [_dot_nt(r_st[k], sb[k]) for k in range(len(idx))]
        m_bd = [jnp.where(same, jnp.concatenate([x, x], axis=0), jnp.zeros((PAIR, PAIR), BF16))
                for x in (dirs[d][2][n, cq[d], :, lanes[k]] for k, (n, d, _) in enumerate(idx))]
        sm = [_dot(sb[k], m_bd[k]) for k in range(len(idx))]
        s_next = []
        for k, (n, d, p) in enumerate(idx):
            nn = dirs[d][3][n, cq[d], :, lanes[k]]
            n_bd = jnp.where(same, jnp.concatenate([nn, nn], axis=0), 0.0)
            dirs[d][5][n, rows[k], lanes[k]] = y_st[k][:c] + y_st[k][c:] + dirs[d][1][n, rows[k], lanes[k]]
            s_next.append(s[k] * dirs[d][4][n, cq[d], :, lanes[k]] + sm[k] + n_bd)
        s = s_next
    for k, (n, d, p) in enumerate(idx):
        s_ref[n, d, p] = s[k]


def _scan(rbar, y0, mx, nn, wc, s0):
    bsz, l, _ = rbar.shape
    sub, nb = SCAN_CHUNKS, SCAN_BATCH
    nc = l // (CHUNK * sub)
    row_f = pl.BlockSpec((nb, sub * CHUNK, D_RWKV), lambda b, i: (b, i, 0))
    row_b = pl.BlockSpec((nb, sub * CHUNK, D_RWKV), lambda b, i: (b, nc - 1 - i, 1))
    chk_f = pl.BlockSpec((nb, sub, CHUNK, D_RWKV), lambda b, i: (b, i, 0, 0))
    chk_b = pl.BlockSpec((nb, sub, CHUNK, D_RWKV), lambda b, i: (b, nc - 1 - i, 0, 1))
    wc_f = pl.BlockSpec((nb, sub, 1, D_RWKV), lambda b, i: (b, i, 0, 0))
    wc_b = pl.BlockSpec((nb, sub, 1, D_RWKV), lambda b, i: (b, nc - 1 - i, 0, 1))
    st_spec = pl.BlockSpec((nb, 2, N_PAIR, PAIR, PAIR), lambda b, i: (b, 0, 0, 0, 0))
    return pl.pallas_call(
        _scan_kernel,
        grid=(bsz // nb, nc),
        in_specs=[row_f, row_b, row_f, row_b, chk_f, chk_b, chk_f, chk_b, wc_f, wc_b, st_spec],
        out_specs=[row_f, pl.BlockSpec((nb, sub * CHUNK, D_RWKV), lambda b, i: (b, nc - 1 - i, 0)), st_spec],
        out_shape=[jax.ShapeDtypeStruct((bsz, l, D_RWKV), F32),
                   jax.ShapeDtypeStruct((bsz, l, D_RWKV), F32),
                   jax.ShapeDtypeStruct((bsz, 2, N_PAIR, PAIR, PAIR), F32)],
        compiler_params=_cparams(("parallel", "arbitrary")),
        name="scan",
    )(rbar, rbar, y0, y0, mx, mx, nn, nn, wc, wc, s0)


OUT_ROWS = 128


def _out_kernel(tt, yf_ref, yb_ref, bonus_ref, g_ref, cv_ref, cvp_ref, cvn_ref, x_ref, g1_ref, sh2_ref,
                sc2_ref, n2g_ref, gng_ref, gnb_ref, convw_ref, wout_ref, rwh_ref, rwl_ref, avg_ref,
                xm_ref, hx_ref, aff_ref):
    i = pl.program_id(1)
    n = pl.num_programs(1)

    cv = cv_ref[0]
    b_gate = cv[:, 0:D_CONV]
    cu = cv[:, D_CONV:2 * D_CONV] * cv[:, 2 * D_CONV:]
    cvp = cvp_ref[0]
    cvn = cvn_ref[0]
    cu_prev = jnp.where(i > 0, cvp[7:8, D_CONV:2 * D_CONV] * cvp[7:8, 2 * D_CONV:], 0.0)
    cu_next = jnp.where(i < n - 1, cvn[0:1, D_CONV:2 * D_CONV] * cvn[0:1, 2 * D_CONV:], 0.0)
    ridx = lax.broadcasted_iota(jnp.int32, cu.shape, 0)
    cu_m1 = jnp.where(ridx == 0, cu_prev, pltpu.roll(cu, 1, 0))
    cu_p1 = jnp.where(ridx == tt - 1, cu_next, pltpu.roll(cu, tt - 1, 0))
    conv = convw_ref[0:1, :] * cu_m1 + convw_ref[1:2, :] * cu + convw_ref[2:3, :] * cu_p1
    bx = (b_gate * conv).astype(BF16)

    parts = [slice(k * OUT_ROWS, (k + 1) * OUT_ROWS) for k in range(tt // OUT_ROWS)]
    avg = avg_ref[...]
    rwh = rwh_ref[...]
    rwl = rwl_ref[...]
    y = [yf_ref[0, r, :] + yb_ref[0, r, :] for r in parts]
    mu = [_seg_dot(v, avg) for v in y]
    dlt = [a - b for a, b in zip(y, mu)]
    var = [_seg_dot(v * v, avg) for v in dlt]
    yn = [a * lax.rsqrt(b + GN_EPS) * gng_ref[...] + gnb_ref[...] for a, b in zip(dlt, var)]
    ax = [((a + bonus_ref[0, r, :]) * g_ref[0, r, :]).astype(BF16) for a, r in zip(yn, parts)]
    mix = [_dot(a, wout_ref[0:D_RWKV, :]) + _dot(bx[r], wout_ref[D_RWKV:, :]) for a, r in zip(ax, parts)]
    xm = [x_ref[0, r, :] + g1_ref[0] * a for a, r in zip(mix, parts)]
    ms = [jnp.mean(v * v, axis=-1, keepdims=True) for v in xm]
    hx = [a * lax.rsqrt(b + NORM_EPS) * n2g_ref[...] for a, b in zip(xm, ms)]
    hx = [_split2(v * (1.0 + sc2_ref[0]) + sh2_ref[0]) for v in hx]
    logits = [_dot_nt(rwh, hi) + _dot_nt(rwh, lo) + _dot_nt(rwl, hi) for hi, lo in hx]
    for k, r in enumerate(parts):
        xm_ref[0, r, :] = xm[k]
        hx_ref[0, r, :] = hx[k][0]
        m = jnp.max(logits[k], axis=0, keepdims=True)
        ex = jnp.exp(logits[k] - m)
        aff_ref[0, :, r] = ex / jnp.sum(ex, axis=0, keepdims=True)


def _out(yf, yb, bonus, g, p_cv, x, g1, sh2, sc2, n2g, gng, gnb, convw, wout, rwh, rwl, avg, tt):
    bsz, t, d = x.shape
    ne = rwh.shape[0]
    nb8 = t // 8
    row = lambda b, i: (b, i, 0)
    per_b = lambda b, i: (b, 0, 0)
    const2 = lambda b, i: (0, 0)
    return pl.pallas_call(
        functools.partial(_out_kernel, tt),
        grid=(bsz, t // tt),
        in_specs=[pl.BlockSpec((1, tt, D_RWKV), row), pl.BlockSpec((1, tt, D_RWKV), row),
                  pl.BlockSpec((1, tt, D_RWKV), row), pl.BlockSpec((1, tt, D_RWKV), row),
                  pl.BlockSpec((1, tt, CONV_COLS), row),
                  pl.BlockSpec((1, 8, CONV_COLS), lambda b, i: (b, jnp.maximum(i * (tt // 8) - 1, 0), 0)),
                  pl.BlockSpec((1, 8, CONV_COLS), lambda b, i: (b, jnp.minimum((i + 1) * (tt // 8), nb8 - 1), 0)),
                  pl.BlockSpec((1, tt, d), row),
                  pl.BlockSpec((1, 1, d), per_b), pl.BlockSpec((1, 1, d), per_b), pl.BlockSpec((1, 1, d), per_b),
                  pl.BlockSpec((1, d), const2), pl.BlockSpec((1, D_RWKV), const2), pl.BlockSpec((1, D_RWKV), const2),
                  pl.BlockSpec((3, D_CONV), const2), pl.BlockSpec((D_RWKV + D_CONV, d), const2),
                  pl.BlockSpec((ne, d), const2), pl.BlockSpec((ne, d), const2),
                  pl.BlockSpec((D_RWKV, D_RWKV), const2)],
        out_specs=[pl.BlockSpec((1, tt, d), row), pl.BlockSpec((1, tt, d), row),
                   pl.BlockSpec((1, ne, tt), lambda b, i: (b, 0, i))],
        out_shape=[jax.ShapeDtypeStruct((bsz, t, d), F32), jax.ShapeDtypeStruct((bsz, t, d), BF16),
                   jax.ShapeDtypeStruct((bsz, ne, t), F32)],
        compiler_params=_cparams(("parallel", "parallel")),
        name="out",
    )(yf, yb, bonus, g, p_cv, p_cv, p_cv, x, g1, sh2, sc2, n2g, gng, gnb, convw, wout, rwh, rwl, avg)


def _prefix_blocks(mask_fn, t, tri, emit):
    carry = None
    for j in range(t // 128):
        m = mask_fn(j)
        inc = _dot(m.astype(BF16), tri)
        carry = jnp.zeros_like(inc[:, 0:1]) if carry is None else carry
        emit(j, m, inc - m + carry)
        carry = carry + inc[:, 127:128]


def _topk_kernel(cap, sb, aff_ref, tri_ref, cnt_ref, slot_ref, edge_ref):
    t = aff_ref.shape[2]
    aff = aff_ref[0]

    def body(k, bits):
        cand = bits | jnp.left_shift(jnp.int32(1), 30 - k)
        cnt = jnp.sum(jnp.where(aff >= pltpu.bitcast(cand, F32), 1, 0), axis=-1, keepdims=True)
        return jnp.where(cnt >= cap, cand, bits)

    bits = lax.fori_loop(0, 31, body, jnp.zeros((aff.shape[0], 1), jnp.int32))
    thr = pltpu.bitcast(bits, F32)
    above = pltpu.bitcast(bits + 1, F32)
    n_gt = jnp.sum(jnp.where(aff >= above, 1, 0), axis=-1, keepdims=True)
    need = (cap - n_gt).astype(F32)
    tri = tri_ref[...]

    def blk(j):
        return aff[:, j * 128:(j + 1) * 128]

    def emit_sel(j, eq, before):
        take = (blk(j) >= above) | ((eq > 0.5) & (before < need))
        slot_ref[0, :, j * 128:(j + 1) * 128] = jnp.where(take, 1, 0)

    _prefix_blocks(lambda j: jnp.where((blk(j) >= thr) & (blk(j) < above), 1.0, 0.0), t, tri, emit_sel)

    def emit_slot(j, m, before):
        count = before.astype(jnp.int32)
        cnt_ref[0, :, j * 128:(j + 1) * 128] = count
        slot_ref[0, :, j * 128:(j + 1) * 128] = jnp.where(m > 0.5, count, -1)

    _prefix_blocks(lambda j: slot_ref[0, :, j * 128:(j + 1) * 128].astype(F32), t, tri, emit_slot)

    cnt = cnt_ref[0]
    lane = lax.broadcasted_iota(jnp.int32, (cnt.shape[0], 128), 1)
    edges = jnp.zeros((cnt.shape[0], 128), jnp.int32)
    for s in range(1, cap // sb + 1):
        below = jnp.sum(jnp.where(cnt < s * sb, 1, 0), axis=-1, keepdims=True)
        edges = jnp.where(lane == s, below, edges)
    edge_ref[0] = edges


def _topk(aff_t, tri, cap, sb):
    bsz, ne, t = aff_t.shape
    spec = pl.BlockSpec((1, ne, t), lambda b: (b, 0, 0))
    return pl.pallas_call(
        functools.partial(_topk_kernel, cap, sb),
        grid=(bsz,),
        in_specs=[spec, pl.BlockSpec((128, 128), lambda b: (0, 0))],
        out_specs=[spec, spec, pl.BlockSpec((1, ne, 128), lambda b: (b, 0, 0))],
        out_shape=[jax.ShapeDtypeStruct((bsz, ne, t), jnp.int32)] * 2 + [jax.ShapeDtypeStruct((bsz, ne, 128), jnp.int32)],
        compiler_params=_cparams(("parallel",)),
        name="topk",
    )(aff_t, tri)


TOKEN_ROW = 128


def _moe_kernel(win, sb, edge_ref, hx_ref, slot_ref, wg_ref, wu_ref, wd_ref, ye_ref, xs_ref):
    b = pl.program_id(0)
    e = pl.program_id(1)
    ne = pl.num_programs(1)
    cap = xs_ref.shape[0]
    nblk = cap // sb
    nrow = hx_ref.shape[1] // TOKEN_ROW
    wrows = win // TOKEN_ROW
    base = (b * ne + e) * (nblk + 1)
    for s in range(nblk):
        blk = slice(s * sb, (s + 1) * sb)
        target = lax.broadcasted_iota(jnp.int32, (sb, TOKEN_ROW), 0) + s * sb
        r0 = jnp.minimum(edge_ref[base + s] // TOKEN_ROW, nrow - wrows)
        rows = slot_ref[0, 0, pl.ds(r0, wrows), :]
        onehot = jnp.concatenate([jnp.where(rows[k:k + 1, :] == target, 1.0, 0.0).astype(BF16)
                                  for k in range(wrows)], axis=1)
        t0 = pl.multiple_of(r0 * TOKEN_ROW, TOKEN_ROW)
        xs_ref[blk, :] = _dot(onehot, hx_ref[0, pl.ds(t0, win), :])

        def extra_row(r, carry, blk=blk, target=target):
            hit = jnp.where(slot_ref[0, 0, pl.ds(r, 1), :] == target, 1.0, 0.0).astype(BF16)
            tr = pl.multiple_of(r * TOKEN_ROW, TOKEN_ROW)
            xs_ref[blk, :] += _dot(hit, hx_ref[0, pl.ds(tr, TOKEN_ROW), :])
            return carry

        r_end = (edge_ref[base + s + 1] + TOKEN_ROW - 1) // TOKEN_ROW
        lax.fori_loop(r0 + wrows, r_end, extra_row, 0)

    xs = xs_ref[...].astype(BF16)
    h1 = _dot(xs, wg_ref[0].astype(BF16))
    h2 = _dot(xs, wu_ref[0].astype(BF16))
    hid = (h1 * jax.nn.sigmoid(h1) * h2).astype(BF16)
    ye_ref[0, 0] = _dot(hid, wd_ref[0].astype(BF16)).astype(BF16)


def _moe(edges, hx, slot4, wg, wu, wd, cap, win, sb):
    bsz, t, d = hx.shape
    ne, _, f = wg.shape
    nrow = t // TOKEN_ROW
    grid_spec = pltpu.PrefetchScalarGridSpec(
        num_scalar_prefetch=1,
        grid=(bsz, ne),
        in_specs=[pl.BlockSpec((1, t, d), lambda b, e, s: (b, 0, 0), pipeline_mode=pl.Buffered(1)),
                  pl.BlockSpec((1, 1, nrow, TOKEN_ROW), lambda b, e, s: (b, e, 0, 0)),
                  pl.BlockSpec((1, d, f), lambda b, e, s: (e, 0, 0)),
                  pl.BlockSpec((1, d, f), lambda b, e, s: (e, 0, 0)),
                  pl.BlockSpec((1, f, d), lambda b, e, s: (e, 0, 0))],
        out_specs=pl.BlockSpec((1, 1, cap, d), lambda b, e, s: (b, e, 0, 0)),
        scratch_shapes=[pltpu.VMEM((cap, d), F32)],
    )
    return pl.pallas_call(
        functools.partial(_moe_kernel, win, sb),
        grid_spec=grid_spec,
        out_shape=jax.ShapeDtypeStruct((bsz, ne, cap, d), BF16),
        compiler_params=_cparams(("parallel", "arbitrary")),
        name="moe",
    )(edges, hx, slot4, wg, wu, wd)


COMB_ROWS = 128
COMB_WIN = 64
SLOT_ALIGN = 16


def _comb_rows(win, groups, xm_ref, ye_ref, slotc, affc, g2_ref, fg_ref, o_ref):
    ne, cap = ye_ref.shape[1], ye_ref.shape[2]
    lane_slot = lax.broadcasted_iota(jnp.int32, (COMB_ROWS, win), 1)
    lhs, rhs = [], []
    for h, first in groups:
        rows = slice(h * COMB_ROWS, (h + 1) * COMB_ROWS)
        hi, lo, ywin = [], [], []
        for e in range(ne):
            start = pl.multiple_of(jnp.minimum(first[e] & ~(SLOT_ALIGN - 1), cap - win), SLOT_ALIGN)
            hit = slotc[rows, e:e + 1] - start == lane_slot
            val = affc[rows, e:e + 1]
            v_hi = val.astype(BF16).astype(F32)
            hi.append(jnp.where(hit, v_hi, 0.0).astype(BF16))
            lo.append(jnp.where(hit, val - v_hi, 0.0).astype(BF16))
            ywin.append(ye_ref[0, e, pl.ds(start, win), :])
        lhs.append(jnp.concatenate([jnp.concatenate(hi, axis=1), jnp.concatenate(lo, axis=1)], axis=0))
        rhs.append(jnp.concatenate(ywin, axis=0))
    both = [_dot(a, y) for a, y in zip(lhs, rhs)]
    for (h, _), bt in zip(groups, both):
        rows = slice(h * COMB_ROWS, (h + 1) * COMB_ROWS)
        x = xm_ref[0, rows, :] + g2_ref[0] * (bt[:COMB_ROWS] + bt[COMB_ROWS:])
        ms = jnp.mean(x * x, axis=-1, keepdims=True)
        o_ref[0, rows, :] = x * lax.rsqrt(ms + NORM_EPS) * fg_ref[...]


def _comb_kernel(tk, win, full, tsp_ref, xm_ref, ye_ref, slotc_ref, affc_ref, g2_ref, fg_ref, o_ref):
    b = pl.program_id(0)
    j = pl.program_id(1)
    nh = tk // COMB_ROWS
    ntile = pl.num_programs(1) * nh + 1
    ne, cap = ye_ref.shape[1], ye_ref.shape[2]
    slotc = slotc_ref[0]
    affc = affc_ref[0]
    refs = (xm_ref, ye_ref, slotc, affc, g2_ref, fg_ref, o_ref)
    groups = [(h, [tsp_ref[(b * ne + e) * ntile + j * nh + h] for e in range(ne)]) for h in range(nh)]
    _comb_rows(win, groups, *refs)
    if full != win:
        for h, first in groups:
            over = None
            for e in range(ne):
                start = jnp.minimum(first[e] & ~(SLOT_ALIGN - 1), cap - win)
                miss = tsp_ref[(b * ne + e) * ntile + j * nh + h + 1] > start + win
                over = miss if over is None else over | miss

            @pl.when(over)
            def _(h=h, first=first):
                _comb_rows(full, [(h, first)], *refs)


def _comb(tsp, xm, ye, slotc, affc, g2, fg, tk):
    bsz, t, d = xm.shape
    ne, cap = ye.shape[1], ye.shape[2]
    full = min(2 * COMB_ROWS, cap)
    assert full == cap or full >= COMB_ROWS + SLOT_ALIGN
    win = min(COMB_WIN, full)
    grid_spec = pltpu.PrefetchScalarGridSpec(
        num_scalar_prefetch=1,
        grid=(bsz, t // tk),
        in_specs=[pl.BlockSpec((1, tk, d), lambda b, j, s: (b, j, 0)),
                  pl.BlockSpec((1, ne, cap, d), lambda b, j, s: (b, 0, 0, 0)),
                  pl.BlockSpec((1, tk, ne), lambda b, j, s: (b, j, 0)),
                  pl.BlockSpec((1, tk, ne), lambda b, j, s: (b, j, 0)),
                  pl.BlockSpec((1, 1, d), lambda b, j, s: (b, 0, 0)),
                  pl.BlockSpec((1, d), lambda b, j, s: (0, 0))],
        out_specs=pl.BlockSpec((1, tk, d), lambda b, j, s: (b, j, 0)),
    )
    return pl.pallas_call(
        functools.partial(_comb_kernel, tk, win, full),
        grid_spec=grid_spec,
        out_shape=jax.ShapeDtypeStruct((bsz, t, d), F32),
        compiler_params=_cparams(("parallel", "arbitrary")),
        name="comb",
    )(tsp, xm, ye, slotc, affc, g2, fg)


def _block_diag_ones(n, blk, value=1.0):
    r = jnp.arange(n)
    return jnp.where((r[:, None] // blk) == (r[None, :] // blk), value, 0.0)


def kernel(x, c, ctx, c_ctx, ada_w, ada_b, norm1_g, norm2_g, w_in, shift_mu, w0, w_lora_up, a0, a_lora_up, k_k, k_a,
           r_k, g_lora_up, gn_g, gn_b, conv_w, w_out, router_w, exp_w_gate, exp_w_up, exp_w_down, final_g):
    bsz, t, d = x.shape
    lc = ctx.shape[1]
    ne = router_w.shape[-1]
    cap = EC_CAPACITY * t // ne
    tt = 256
    sb = min(128, cap)
    win = min(1536, t)
    l = 0

    rows = ((bsz + 1 + 7) // 8) * 8
    cc = jnp.zeros((rows, d), F32).at[:bsz].set(c).at[bsz].set(c_ctx)
    mod = _mod(cc, ada_w[l], ada_b[l][None, :])
    sh1, sc1, g1, sh2, sc2, g2 = (m[:, None, :] for m in jnp.split(mod[:bsz], 6, axis=-1))
    csh1, csc1 = (jnp.broadcast_to(m[None, None, :], (bsz, 1, d)) for m in jnp.split(mod[bsz], 6)[:2])

    w_rw = w_in[l][:, :RWKV_COLS].astype(BF16)
    w_cv = w_in[l][:, RWKV_COLS:].astype(BF16)
    n1g = norm1_g[l][None, :]
    px_rw, px_cv = _in_proj(x, sh1, sc1, n1g, w_rw, w_cv, 512)
    pc_rw, _ = _in_proj(ctx, csh1, csc1, n1g, w_rw, w_cv, 256)

    zw = jnp.zeros((LORA_W, 2 * D_RWKV), F32)
    lora = jnp.concatenate([
        jnp.concatenate([w_lora_up[l, 0], w_lora_up[l, 1], zw], axis=1),
        jnp.concatenate([zw, a_lora_up[l, 0], a_lora_up[l, 1]], axis=1)], axis=0).astype(BF16)
    ridx = jnp.arange(CHUNK)
    tril = jnp.where(ridx[None, :] <= ridx[:, None], 1.0, 0.0).astype(BF16)
    triu = jnp.where(ridx[None, :] >= ridx[:, None], 1.0, 0.0).astype(BF16)
    ones_bd = _block_diag_ones(PAIR, HEAD).astype(BF16)
    consts = (shift_mu[l][None, :], lora, w0[l], a0[l], k_k[l][None, :], k_a[l][None, :],
              r_k[l].reshape(1, D_RWKV), g_lora_up[l].astype(BF16), ones_bd, tril, triu)

    chunks_c = _prep(pc_rw, consts, False, lc)[:5]
    *chunks_x, bonus, gate = _prep(px_rw, consts, True, tt)

    s_zero = jnp.zeros((bsz, 2, N_PAIR, PAIR, PAIR), F32)
    _, _, s_ctx = _scan(*chunks_c, s_zero)
    yf, yb, _ = _scan(*chunks_x, s_ctx)

    rw_t = router_w[l].T
    rwh = rw_t.astype(BF16)
    rwl = (rw_t - rwh.astype(F32)).astype(BF16)
    avg = _block_diag_ones(D_RWKV, HEAD, 1.0 / HEAD).astype(BF16)
    xm, hx, aff_t = _out(yf, yb, bonus, gate, px_cv, x, g1, sh2, sc2, norm2_g[l][None, :], gn_g[l][None, :],
                         gn_b[l][None, :], conv_w[l], w_out[l].astype(BF16), rwh, rwl, avg, 2 * tt)

    r128 = jnp.arange(128)
    tri128 = jnp.where(r128[:, None] <= r128[None, :], 1.0, 0.0).astype(BF16)
    cnt, slot, edges = _topk(aff_t, tri128, cap, sb)

    ye = _moe(edges[:, :, :cap // sb + 1].reshape(-1), hx, slot.reshape(bsz, ne, t // TOKEN_ROW, TOKEN_ROW),
              exp_w_gate[l], exp_w_up[l], exp_w_down[l], cap, win, sb)
    tr = lambda a: jnp.transpose(a, (0, 2, 1))
    first_slot = jnp.concatenate([cnt[:, :, ::COMB_ROWS], jnp.full((bsz, ne, 1), cap, jnp.int32)], axis=-1).reshape(-1)
    return _comb(first_slot, xm, ye, tr(slot), tr(aff_t), g2, final_g[None, :], tt)
```

```python
import functools
import math

import jax
import jax.numpy as jnp
from jax import lax
from jax.experimental import pallas as pl
from jax.experimental.pallas import tpu as pltpu

F32 = jnp.float32
BF16 = jnp.bfloat16
HIGHEST = lax.Precision.HIGHEST

GRID_W = 64
D_RWKV = 512
D_CONV = 512
HEAD = 64
LORA_W = 64
LORA_A = 64
LORA_G = 128
N_EXPERTS = 16
EC_CAPACITY = 2
NORM_EPS = 1e-6
GN_EPS = 64e-5
RWKV_COLS = 3 * D_RWKV + LORA_W + LORA_A + LORA_G
CONV_COLS = 3 * D_CONV

CHUNK = 64
PAIR = 2 * HEAD
N_PAIR = D_RWKV // PAIR
VMEM_LIMIT = 48 * 1024 * 1024


def _cparams(sem):
    return pltpu.CompilerParams(dimension_semantics=sem, vmem_limit_bytes=VMEM_LIMIT)


def _dot(a, b):
    return jnp.dot(a, b, preferred_element_type=F32)


def _dot_nt(a, b):
    return lax.dot_general(a, b, (((1,), (1,)), ((), ())), preferred_element_type=F32)


def _split2(x):
    hi = x.astype(BF16)
    lo = (x - hi.astype(F32)).astype(BF16)
    return hi, lo


def _split3(x):
    hi = x.astype(BF16)
    r = x - hi.astype(F32)
    mid = r.astype(BF16)
    lo = (r - mid.astype(F32)).astype(BF16)
    return hi, mid, lo


def _seg_dot(x, m):
    hi, lo = _split2(x)
    return _dot(hi, m) + _dot(lo, m)


def _mod_kernel(c_ref, w_ref, b_ref, o_ref):
    c = c_ref[...]
    s = c * jax.nn.sigmoid(c)
    o_ref[...] = jnp.dot(s, w_ref[...], precision=HIGHEST, preferred_element_type=F32) + b_ref[...]


def _mod(cc, w, b):
    rows, d = cc.shape
    n = w.shape[1]
    tn = 1024
    return pl.pallas_call(
        _mod_kernel,
        grid=(n // tn,),
        in_specs=[pl.BlockSpec((rows, d), lambda j: (0, 0)),
                  pl.BlockSpec((d, tn), lambda j: (0, j)),
                  pl.BlockSpec((1, tn), lambda j: (0, j))],
        out_specs=pl.BlockSpec((rows, tn), lambda j: (0, j)),
        out_shape=jax.ShapeDtypeStruct((rows, n), F32),
        compiler_params=_cparams(("parallel",)),
        name="mod",
    )(cc, w, b)


def _in_proj_kernel(x_ref, sh_ref, sc_ref, g_ref, wrw_ref, wcv_ref, orw_ref, ocv_ref):
    x = x_ref[0]
    ms = jnp.mean(x * x, axis=-1, keepdims=True)
    h = x * lax.rsqrt(ms + NORM_EPS) * g_ref[...]
    h = (h * (1.0 + sc_ref[0]) + sh_ref[0]).astype(BF16)
    orw_ref[0] = _dot(h, wrw_ref[...])
    ocv_ref[0] = _dot(h, wcv_ref[...])


def _in_proj(x, sh, sc, g, w_rw, w_cv, tm):
    bsz, l, d = x.shape
    return pl.pallas_call(
        _in_proj_kernel,
        grid=(bsz, l // tm),
        in_specs=[pl.BlockSpec((1, tm, d), lambda b, i: (b, i, 0)),
                  pl.BlockSpec((1, 1, d), lambda b, i: (b, 0, 0)),
                  pl.BlockSpec((1, 1, d), lambda b, i: (b, 0, 0)),
                  pl.BlockSpec((1, d), lambda b, i: (0, 0)),
                  pl.BlockSpec((d, RWKV_COLS), lambda b, i: (0, 0)),
                  pl.BlockSpec((d, CONV_COLS), lambda b, i: (0, 0))],
        out_specs=[pl.BlockSpec((1, tm, RWKV_COLS), lambda b, i: (b, i, 0)),
                   pl.BlockSpec((1, tm, CONV_COLS), lambda b, i: (b, i, 0))],
        out_shape=[jax.ShapeDtypeStruct((bsz, l, RWKV_COLS), F32),
                   jax.ShapeDtypeStruct((bsz, l, CONV_COLS), F32)],
        compiler_params=_cparams(("parallel", "parallel")),
        name="in_proj",
    )(x, sh, sc, g, w_rw, w_cv)


def _bf(x):
    return x.astype(BF16)


def _chunk_local(chains):
    c = CHUNK
    n = range(len(chains))
    lane = lax.broadcasted_iota(jnp.int32, (c, PAIR), 1)
    h0 = lane < HEAD
    tcol = lane & (c - 1)
    trow = lax.broadcasted_iota(jnp.int32, (c, PAIR), 0)
    eye = jnp.where(tcol == trow, 1.0, 0.0)
    masks = {rev: ((tcol > trow) if rev else (tcol < trow), (tcol >= trow) if rev else (tcol <= trow))
             for rev in (False, True)}
    zb = jnp.zeros((c, PAIR), BF16)

    def stack(x):
        return jnp.concatenate([jnp.where(h0, x, zb), jnp.where(h0, zb, x)], axis=0)

    gram = [_dot_nt(jnp.concatenate([ch[1], ch[2]], axis=0), jnp.concatenate([stack(ch[3]), stack(ch[4])], axis=0))
            for ch in chains]
    lab = [jnp.where(masks[chains[i][0]][0], gram[i][:c, :PAIR], 0.0) for i in n]
    lak = [jnp.where(masks[chains[i][0]][0], gram[i][:c, PAIR:], 0.0) for i in n]
    mrb = [jnp.where(masks[chains[i][0]][1], gram[i][c:, :PAIR], 0.0) for i in n]
    mrk = [jnp.where(masks[chains[i][0]][1], gram[i][c:, PAIR:], 0.0) for i in n]

    labb = [_bf(x) for x in lab]
    pw = [_dot(labb[i], stack(labb[i])) for i in n]
    tp = [eye + lab[i] for i in n]
    for _ in range(4):
        pwb = [_bf(x) for x in pw]
        both = [_dot(pwb[i], jnp.concatenate([stack(pwb[i]), stack(_bf(tp[i]))], axis=1)) for i in n]
        pw = [x[:, :PAIR] for x in both]
        tp = [tp[i] + both[i][:, PAIR:] for i in n]
    tinv = [tp[i] + _dot(_bf(pw[i]), stack(_bf(tp[i]))) for i in n]

    lmv = [_dot(_bf(jnp.concatenate([lak[i], mrk[i]], axis=0)), stack(chains[i][7])) for i in n]
    x = [_dot(_bf(tinv[i]), jnp.concatenate([stack(chains[i][1]), stack(_bf(lmv[i][:c]))], axis=1))
         for i in n]
    z = [_dot(_bf(mrb[i]), jnp.concatenate([stack(_bf(x[i][:, :PAIR])), stack(_bf(x[i][:, PAIR:]))], axis=1))
         for i in n]
    rbar = [_bf(chains[i][2].astype(F32) + z[i][:, :PAIR]) for i in n]
    y0 = [z[i][:, PAIR:] + lmv[i][c:] for i in n]

    uv = [jnp.concatenate([x[i][:, PAIR:], chains[i][7].astype(F32)], axis=0) for i in n]
    mxf = [_dot(_bf(x[i][:, :PAIR].T), chains[i][5]) for i in n]
    nnf = [_dot(_bf(uv[i].T), jnp.concatenate([chains[i][5], chains[i][6]], axis=0)) for i in n]
    mx = [_bf(jnp.where(h0, m[:c], m[c:])) for m in mxf]
    nn = [jnp.where(h0, m[:c], m[c:]) for m in nnf]
    return list(zip(rbar, y0, mx, nn))


DIR_COLS = 6 * D_RWKV
SCR_COLS = 2 * DIR_COLS + D_RWKV


def _head_sums(x, ones):
    return jnp.concatenate([_seg_dot(x[:, g * PAIR:(g + 1) * PAIR], ones) for g in range(N_PAIR)], axis=1)


def _prep_rows(c, p, shifted, lo, mu_ref, w0_ref, a0_ref, kk_ref, ka_ref, rk_ref, ones_ref, tril_ref, triu_ref,
               wc_ref, bonus_ref, scr_ref):
    rows = slice(c * CHUNK, (c + 1) * CHUNK)
    pm = p + mu_ref[:, 0:3 * D_RWKV] * (shifted - p)
    xr = pm[:, 0:D_RWKV]
    xk = pm[:, D_RWKV:2 * D_RWKV]
    xv = pm[:, 2 * D_RWKV:3 * D_RWKV]

    ones = ones_ref[...]
    kraw = xk * kk_ref[...]
    kk = kraw / jnp.maximum(jnp.sqrt(_head_sums(kraw * kraw, ones)), 1e-12)

    scr_ref[:, 2 * DIR_COLS:] = xv.astype(BF16)
    ksum = None
    for d in range(2):
        z = w0_ref[d:d + 1, :] + lo[:, d * D_RWKV:(d + 1) * D_RWKV]
        ld = -math.exp(-0.5) * jax.nn.sigmoid(z)
        ag = jax.nn.sigmoid(a0_ref[d:d + 1, :] + lo[:, (2 + d) * D_RWKV:(3 + d) * D_RWKV])
        kd = xk * (1.0 + (ag - 1.0) * ka_ref[...])
        bb = kk * ag
        ksum = kd if ksum is None else ksum + kd
        tri = tril_ref[...] if d == 0 else triu_ref[...]
        h3 = _split3(ld)
        cs = _dot(tri, h3[0]) + _dot(tri, h3[1]) + _dot(tri, h3[2])
        e_in = jnp.exp(cs)
        e_ex = jnp.exp(cs - ld)
        e_neg = jnp.exp(-cs)
        bt = bb * e_neg
        kt = kd * e_neg
        edge = CHUNK - 1 if d == 0 else 0
        wrow = e_in[edge:edge + 1, :]
        wc_ref[0, c, :, d * D_RWKV:(d + 1) * D_RWKV] = wrow
        base = d * DIR_COLS
        for j, val in enumerate((-kk * e_ex, xr * e_in, bt, kt, bt * wrow, kt * wrow)):
            scr_ref[:, base + j * D_RWKV:base + (j + 1) * D_RWKV] = val.astype(BF16)

    bonus_ref[0, rows, :] = _head_sums(xr * ksum * rk_ref[...], ones) * xv


PREP_GROUP = 2


def _prep_chunks(cs, scr_refs, rbar_ref, y0_ref, mx_ref, nn_ref):
    chains = []
    for scr_ref in scr_refs:
        for d in range(2):
            for pp in range(N_PAIR):
                cols = [d * DIR_COLS + j * D_RWKV + pp * PAIR for j in range(6)] + [2 * DIR_COLS + pp * PAIR]
                chains.append((d == 1,) + tuple(scr_ref[:, k:k + PAIR] for k in cols))
    per_chunk = 2 * N_PAIR
    for idx, (rbar, y0, mx, nn) in enumerate(_chunk_local(chains)):
        c = cs[idx // per_chunk]
        rows = slice(c * CHUNK, (c + 1) * CHUNK)
        col = (idx % per_chunk // N_PAIR) * D_RWKV + (idx % N_PAIR) * PAIR
        rbar_ref[0, rows, col:col + PAIR] = rbar
        y0_ref[0, rows, col:col + PAIR] = y0
        mx_ref[0, c, :, col:col + PAIR] = mx
        nn_ref[0, c, :, col:col + PAIR] = nn


N_PREP_CONSTS = 11


def _prep_kernel(latent, tt, p_ref, *rest):
    if latent:
        prev_ref, next_ref = rest[:2]
        rest = rest[2:]
    (mu_ref, lw_ref, w0_ref, a0_ref, kk_ref, ka_ref, rk_ref, gup_ref, ones_ref, tril_ref,
     triu_ref) = rest[:N_PREP_CONSTS]
    rbar_ref, y0_ref, mx_ref, nn_ref, wc_ref, bonus_ref, g_ref = rest[N_PREP_CONSTS:N_PREP_CONSTS + 7]
    *scr, lo_ref, ext_ref = rest[N_PREP_CONSTS + 7:]
    if latent:
        i = pl.program_id(1)
        n = pl.num_programs(1)
        ext_ref[0:GRID_W] = jnp.where(i > 0, prev_ref[0], 0.0)
        ext_ref[GRID_W:GRID_W + tt] = p_ref[0]
        ext_ref[GRID_W + tt:] = jnp.where(i < n - 1, next_ref[0], 0.0)

        def shifted(r0, nrows, cols):
            shape = (nrows, cols.stop - cols.start)
            trow = lax.broadcasted_iota(jnp.int32, shape, 0) & (GRID_W - 1)
            q = lax.broadcasted_iota(jnp.int32, shape, 1) & 3
            left = jnp.where(trow == 0, 0.0, ext_ref[GRID_W - 1 + r0:GRID_W - 1 + r0 + nrows, cols])
            right = jnp.where(trow == GRID_W - 1, 0.0, ext_ref[GRID_W + 1 + r0:GRID_W + 1 + r0 + nrows, cols])
            up = ext_ref[r0:r0 + nrows, cols]
            down = ext_ref[2 * GRID_W + r0:2 * GRID_W + r0 + nrows, cols]
            return jnp.where(q == 0, left, jnp.where(q == 1, right, jnp.where(q == 2, up, down)))
    else:
        ext_ref[0:8] = jnp.zeros((8, RWKV_COLS), F32)
        ext_ref[8:8 + tt] = p_ref[0]
        ext_ref[8 + tt:] = jnp.zeros((8, RWKV_COLS), F32)

        def shifted(r0, nrows, cols):
            q = lax.broadcasted_iota(jnp.int32, (nrows, cols.stop - cols.start), 1) & 1
            return jnp.where(q == 0, ext_ref[7 + r0:7 + r0 + nrows, cols], ext_ref[9 + r0:9 + r0 + nrows, cols])

    tail = slice(3 * D_RWKV, RWKV_COLS)
    p_t = p_ref[0, :, tail]
    pm_t = p_t + mu_ref[:, tail] * (shifted(0, tt, tail) - p_t)
    xwa = pm_t[:, 0:LORA_W + LORA_A]
    lane = lax.broadcasted_iota(jnp.int32, xwa.shape, 1)
    lin = jnp.where(lane < LORA_W, jnp.tanh(xwa), xwa).astype(BF16)
    lo_ref[...] = _dot(lin, lw_ref[...])
    g_ref[0] = _dot(jax.nn.sigmoid(pm_t[:, LORA_W + LORA_A:]).astype(BF16), gup_ref[...])

    head = slice(0, 3 * D_RWKV)

    def rows_part(c):
        rows = slice(c * CHUNK, (c + 1) * CHUNK)
        _prep_rows(c, p_ref[0, rows, head], shifted(c * CHUNK, CHUNK, head), lo_ref[rows, :], mu_ref, w0_ref,
                   a0_ref, kk_ref, ka_ref, rk_ref, ones_ref, tril_ref, triu_ref, wc_ref, bonus_ref,
                   scr[c % (2 * PREP_GROUP)])

    groups = [list(range(g, g + PREP_GROUP)) for g in range(0, tt // CHUNK, PREP_GROUP)]
    for c in groups[0]:
        rows_part(c)
    for k, cs in enumerate(groups):
        if k + 1 < len(groups):
            for c in groups[k + 1]:
                rows_part(c)
        _prep_chunks(cs, [scr[c % (2 * PREP_GROUP)] for c in cs], rbar_ref, y0_ref, mx_ref, nn_ref)


def _prep(p_rw, consts, latent, tt):
    bsz, l, _ = p_rw.shape
    nc = l // CHUNK
    cpt = tt // CHUNK
    assert len(consts) == N_PREP_CONSTS
    const_specs = [pl.BlockSpec(c.shape, lambda b, i, nd=c.ndim: (0,) * nd) for c in consts]
    kern = functools.partial(_prep_kernel, latent, tt)
    if latent:
        nblk = l // GRID_W
        in_specs = [pl.BlockSpec((1, tt, RWKV_COLS), lambda b, i: (b, i, 0)),
                    pl.BlockSpec((1, GRID_W, RWKV_COLS),
                                 lambda b, i: (b, jnp.maximum(i * (tt // GRID_W) - 1, 0), 0)),
                    pl.BlockSpec((1, GRID_W, RWKV_COLS),
                                 lambda b, i: (b, jnp.minimum((i + 1) * (tt // GRID_W), nblk - 1), 0))]
        args = (p_rw, p_rw, p_rw)
        ext_rows = tt + 2 * GRID_W
    else:
        assert tt == l
        in_specs = [pl.BlockSpec((1, tt, RWKV_COLS), lambda b, i: (b, i, 0))]
        args = (p_rw,)
        ext_rows = tt + 16
    row = lambda b, i: (b, i, 0)
    chunk = lambda b, i: (b, i, 0, 0)
    w2 = 2 * D_RWKV
    return pl.pallas_call(
        kern,
        grid=(bsz, l // tt),
        in_specs=in_specs + const_specs,
        out_specs=[pl.BlockSpec((1, tt, w2), row), pl.BlockSpec((1, tt, w2), row),
                   pl.BlockSpec((1, cpt, CHUNK, w2), chunk), pl.BlockSpec((1, cpt, CHUNK, w2), chunk),
                   pl.BlockSpec((1, cpt, 1, w2), chunk),
                   pl.BlockSpec((1, tt, D_RWKV), row), pl.BlockSpec((1, tt, D_RWKV), row)],
        out_shape=[jax.ShapeDtypeStruct((bsz, l, w2), BF16), jax.ShapeDtypeStruct((bsz, l, w2), F32),
                   jax.ShapeDtypeStruct((bsz, nc, CHUNK, w2), BF16), jax.ShapeDtypeStruct((bsz, nc, CHUNK, w2), F32),
                   jax.ShapeDtypeStruct((bsz, nc, 1, w2), F32),
                   jax.ShapeDtypeStruct((bsz, l, D_RWKV), F32), jax.ShapeDtypeStruct((bsz, l, D_RWKV), F32)],
        scratch_shapes=[pltpu.VMEM((CHUNK, SCR_COLS), BF16)] * (2 * PREP_GROUP)
        + [pltpu.VMEM((tt, 4 * D_RWKV), F32), pltpu.VMEM((ext_rows, RWKV_COLS), F32)],
        compiler_params=_cparams(("parallel", "parallel")),
        name="prep_latent" if latent else "prep_ctx",
    )(*args, *consts)


SCAN_CHUNKS = 4
SCAN_BATCH = 2


def _scan_kernel(rbf_ref, rbb_ref, y0f_ref, y0b_ref, mxf_ref, mxb_ref, nnf_ref, nnb_ref, wcf_ref, wcb_ref,
                 s0_ref, yf_ref, yb_ref, s_ref):
    i = pl.program_id(1)

    @pl.when(i == 0)
    def _():
        s_ref[...] = s0_ref[...]

    c = CHUNK
    h0 = lax.broadcasted_iota(jnp.int32, (c, PAIR), 1) < HEAD
    zb = jnp.zeros((c, PAIR), BF16)

    def stack(x):
        return jnp.concatenate([jnp.where(h0, x, zb), jnp.where(h0, zb, x)], axis=0)

    dirs = ((rbf_ref, y0f_ref, mxf_ref, nnf_ref, wcf_ref, yf_ref), (rbb_ref, y0b_ref, mxb_ref, nnb_ref, wcb_ref, yb_ref))
    idx = [(n, d, p) for n in range(SCAN_BATCH) for d in range(2) for p in range(N_PAIR)]
    lanes = [slice(p * PAIR, (p + 1) * PAIR) for _, _, p in idx]
    s = [s_ref[n, d, p] for n, d, p in idx]
    for q in range(SCAN_CHUNKS):
        cq = (q, SCAN_CHUNKS - 1 - q)
        rows = [slice(cq[d] * c, (cq[d] + 1) * c) for _, d, _ in idx]
        sb = [_bf(x) for x in s]
        y = [_dot_nt(dirs[d][0][n, rows[k], lanes[k]], stack(sb[k])) for k, (n, d, _) in enumerate(idx)]
        sm = [_dot(sb[k], stack(dirs[d][2][n, cq[d], :, lanes[k]])) for k, (n, d, _) in enumerate(idx)]
        s_next = []
        for k, (n, d, p) in enumerate(idx):
            dirs[d][5][n, rows[k], lanes[k]] = y[k] + dirs[d][1][n, rows[k], lanes[k]]
            s_next.append(s[k] * dirs[d][4][n, cq[d], :, lanes[k]] + sm[k] + dirs[d][3][n, cq[d], :, lanes[k]])
        s = s_next
    for k, (n, d, p) in enumerate(idx):
        s_ref[n, d, p] = s[k]


def _scan(rbar, y0, mx, nn, wc, s0):
    bsz, l, _ = rbar.shape
    sub, nb = SCAN_CHUNKS, SCAN_BATCH
    nc = l // (CHUNK * sub)
    row_f = pl.BlockSpec((nb, sub * CHUNK, D_RWKV), lambda b, i: (b, i, 0))
    row_b = pl.BlockSpec((nb, sub * CHUNK, D_RWKV), lambda b, i: (b, nc - 1 - i, 1))
    chk_f = pl.BlockSpec((nb, sub, CHUNK, D_RWKV), lambda b, i: (b, i, 0, 0))
    chk_b = pl.BlockSpec((nb, sub, CHUNK, D_RWKV), lambda b, i: (b, nc - 1 - i, 0, 1))
    wc_f = pl.BlockSpec((nb, sub, 1, D_RWKV), lambda b, i: (b, i, 0, 0))
    wc_b = pl.BlockSpec((nb, sub, 1, D_RWKV), lambda b, i: (b, nc - 1 - i, 0, 1))
    st_spec = pl.BlockSpec((nb, 2, N_PAIR, HEAD, PAIR), lambda b, i: (b, 0, 0, 0, 0))
    return pl.pallas_call(
        _scan_kernel,
        grid=(bsz // nb, nc),
        in_specs=[row_f, row_b, row_f, row_b, chk_f, chk_b, chk_f, chk_b, wc_f, wc_b, st_spec],
        out_specs=[row_f, pl.BlockSpec((nb, sub * CHUNK, D_RWKV), lambda b, i: (b, nc - 1 - i, 0)), st_spec],
        out_shape=[jax.ShapeDtypeStruct((bsz, l, D_RWKV), F32),
                   jax.ShapeDtypeStruct((bsz, l, D_RWKV), F32),
                   jax.ShapeDtypeStruct((bsz, 2, N_PAIR, HEAD, PAIR), F32)],
        compiler_params=_cparams(("parallel", "arbitrary")),
        name="scan",
    )(rbar, rbar, y0, y0, mx, mx, nn, nn, wc, wc, s0)


OUT_ROWS = 128


def _out_kernel(tt, yf_ref, yb_ref, bonus_ref, g_ref, cv_ref, cvp_ref, cvn_ref, x_ref, g1_ref, sh2_ref,
                sc2_ref, n2g_ref, gng_ref, gnb_ref, convw_ref, wout_ref, rwh_ref, rwl_ref, avg_ref,
                xm_ref, hx_ref, aff_ref):
    i = pl.program_id(1)
    n = pl.num_programs(1)

    cv = cv_ref[0]
    b_gate = cv[:, 0:D_CONV]
    cu = cv[:, D_CONV:2 * D_CONV] * cv[:, 2 * D_CONV:]
    cvp = cvp_ref[0]
    cvn = cvn_ref[0]
    cu_prev = jnp.where(i > 0, cvp[7:8, D_CONV:2 * D_CONV] * cvp[7:8, 2 * D_CONV:], 0.0)
    cu_next = jnp.where(i < n - 1, cvn[0:1, D_CONV:2 * D_CONV] * cvn[0:1, 2 * D_CONV:], 0.0)
    ridx = lax.broadcasted_iota(jnp.int32, cu.shape, 0)
    cu_m1 = jnp.where(ridx == 0, cu_prev, pltpu.roll(cu, 1, 0))
    cu_p1 = jnp.where(ridx == tt - 1, cu_next, pltpu.roll(cu, tt - 1, 0))
    conv = convw_ref[0:1, :] * cu_m1 + convw_ref[1:2, :] * cu + convw_ref[2:3, :] * cu_p1
    bx = (b_gate * conv).astype(BF16)

    parts = [slice(k * OUT_ROWS, (k + 1) * OUT_ROWS) for k in range(tt // OUT_ROWS)]
    avg = avg_ref[...]
    rwh = rwh_ref[...]
    rwl = rwl_ref[...]
    y = [yf_ref[0, r, :] + yb_ref[0, r, :] for r in parts]
    mu = [_seg_dot(v, avg) for v in y]
    dlt = [a - b for a, b in zip(y, mu)]
    var = [_seg_dot(v * v, avg) for v in dlt]
    yn = [a * lax.rsqrt(b + GN_EPS) * gng_ref[...] + gnb_ref[...] for a, b in zip(dlt, var)]
    ax = [((a + bonus_ref[0, r, :]) * g_ref[0, r, :]).astype(BF16) for a, r in zip(yn, parts)]
    mix = [_dot(a, wout_ref[0:D_RWKV, :]) + _dot(bx[r], wout_ref[D_RWKV:, :]) for a, r in zip(ax, parts)]
    xm = [x_ref[0, r, :] + g1_ref[0] * a for a, r in zip(mix, parts)]
    ms = [jnp.mean(v * v, axis=-1, keepdims=True) for v in xm]
    hx = [a * lax.rsqrt(b + NORM_EPS) * n2g_ref[...] for a, b in zip(xm, ms)]
    hx = [_split2(v * (1.0 + sc2_ref[0]) + sh2_ref[0]) for v in hx]
    logits = [_dot_nt(rwh, hi) + _dot_nt(rwh, lo) + _dot_nt(rwl, hi) for hi, lo in hx]
    for k, r in enumerate(parts):
        xm_ref[0, r, :] = xm[k]
        hx_ref[0, r, :] = hx[k][0]
        m = jnp.max(logits[k], axis=0, keepdims=True)
        ex = jnp.exp(logits[k] - m)
        aff_ref[0, :, r] = ex / jnp.sum(ex, axis=0, keepdims=True)


def _out(yf, yb, bonus, g, p_cv, x, g1, sh2, sc2, n2g, gng, gnb, convw, wout, rwh, rwl, avg, tt):
    bsz, t, d = x.shape
    ne = rwh.shape[0]
    nb8 = t // 8
    row = lambda b, i: (b, i, 0)
    per_b = lambda b, i: (b, 0, 0)
    const2 = lambda b, i: (0, 0)
    return pl.pallas_call(
        functools.partial(_out_kernel, tt),
        grid=(bsz, t // tt),
        in_specs=[pl.BlockSpec((1, tt, D_RWKV), row), pl.BlockSpec((1, tt, D_RWKV), row),
                  pl.BlockSpec((1, tt, D_RWKV), row), pl.BlockSpec((1, tt, D_RWKV), row),
                  pl.BlockSpec((1, tt, CONV_COLS), row),
                  pl.BlockSpec((1, 8, CONV_COLS), lambda b, i: (b, jnp.maximum(i * (tt // 8) - 1, 0), 0)),
                  pl.BlockSpec((1, 8, CONV_COLS), lambda b, i: (b, jnp.minimum((i + 1) * (tt // 8), nb8 - 1), 0)),
                  pl.BlockSpec((1, tt, d), row),
                  pl.BlockSpec((1, 1, d), per_b), pl.BlockSpec((1, 1, d), per_b), pl.BlockSpec((1, 1, d), per_b),
                  pl.BlockSpec((1, d), const2), pl.BlockSpec((1, D_RWKV), const2), pl.BlockSpec((1, D_RWKV), const2),
                  pl.BlockSpec((3, D_CONV), const2), pl.BlockSpec((D_RWKV + D_CONV, d), const2),
                  pl.BlockSpec((ne, d), const2), pl.BlockSpec((ne, d), const2),
                  pl.BlockSpec((D_RWKV, D_RWKV), const2)],
        out_specs=[pl.BlockSpec((1, tt, d), row), pl.BlockSpec((1, tt, d), row),
                   pl.BlockSpec((1, ne, tt), lambda b, i: (b, 0, i))],
        out_shape=[jax.ShapeDtypeStruct((bsz, t, d), F32), jax.ShapeDtypeStruct((bsz, t, d), BF16),
                   jax.ShapeDtypeStruct((bsz, ne, t), F32)],
        compiler_params=_cparams(("parallel", "parallel")),
        name="out",
    )(yf, yb, bonus, g, p_cv, p_cv, p_cv, x, g1, sh2, sc2, n2g, gng, gnb, convw, wout, rwh, rwl, avg)


def _prefix_blocks(mask_fn, t, tri, emit):
    carry = None
    for j in range(t // 128):
        m = mask_fn(j)
        inc = _dot(m.astype(BF16), tri)
        carry = jnp.zeros_like(inc[:, 0:1]) if carry is None else carry
        emit(j, m, inc - m + carry)
        carry = carry + inc[:, 127:128]


def _topk_kernel(cap, sb, aff_ref, tri_ref, cnt_ref, slot_ref, edge_ref):
    t = aff_ref.shape[2]
    aff = aff_ref[0]

    def body(k, bits):
        cand = bits | jnp.left_shift(jnp.int32(1), 30 - k)
        cnt = jnp.sum(jnp.where(aff >= pltpu.bitcast(cand, F32), 1, 0), axis=-1, keepdims=True)
        return jnp.where(cnt >= cap, cand, bits)

    bits = lax.fori_loop(0, 31, body, jnp.zeros((aff.shape[0], 1), jnp.int32))
    thr = pltpu.bitcast(bits, F32)
    above = pltpu.bitcast(bits + 1, F32)
    n_gt = jnp.sum(jnp.where(aff >= above, 1, 0), axis=-1, keepdims=True)
    need = (cap - n_gt).astype(F32)
    tri = tri_ref[...]

    def blk(j):
        return aff[:, j * 128:(j + 1) * 128]

    def emit_sel(j, eq, before):
        take = (blk(j) >= above) | ((eq > 0.5) & (before < need))
        slot_ref[0, :, j * 128:(j + 1) * 128] = jnp.where(take, 1, 0)

    _prefix_blocks(lambda j: jnp.where((blk(j) >= thr) & (blk(j) < above), 1.0, 0.0), t, tri, emit_sel)

    def emit_slot(j, m, before):
        count = before.astype(jnp.int32)
        cnt_ref[0, :, j * 128:(j + 1) * 128] = count
        slot_ref[0, :, j * 128:(j + 1) * 128] = jnp.where(m > 0.5, count, -1)

    _prefix_blocks(lambda j: slot_ref[0, :, j * 128:(j + 1) * 128].astype(F32), t, tri, emit_slot)

    cnt = cnt_ref[0]
    lane = lax.broadcasted_iota(jnp.int32, (cnt.shape[0], 128), 1)
    edges = jnp.zeros((cnt.shape[0], 128), jnp.int32)
    for s in range(1, cap // sb + 1):
        below = jnp.sum(jnp.where(cnt < s * sb, 1, 0), axis=-1, keepdims=True)
        edges = jnp.where(lane == s, below, edges)
    edge_ref[0] = edges


def _topk(aff_t, tri, cap, sb):
    bsz, ne, t = aff_t.shape
    spec = pl.BlockSpec((1, ne, t), lambda b: (b, 0, 0))
    return pl.pallas_call(
        functools.partial(_topk_kernel, cap, sb),
        grid=(bsz,),
        in_specs=[spec, pl.BlockSpec((128, 128), lambda b: (0, 0))],
        out_specs=[spec, spec, pl.BlockSpec((1, ne, 128), lambda b: (b, 0, 0))],
        out_shape=[jax.ShapeDtypeStruct((bsz, ne, t), jnp.int32)] * 2 + [jax.ShapeDtypeStruct((bsz, ne, 128), jnp.int32)],
        compiler_params=_cparams(("parallel",)),
        name="topk",
    )(aff_t, tri)


TOKEN_ROW = 128


def _moe_kernel(win, sb, edge_ref, hx_ref, slot_ref, wg_ref, wu_ref, wd_ref, ye_ref, xs_ref):
    b = pl.program_id(0)
    e = pl.program_id(1)
    ne = pl.num_programs(1)
    cap = xs_ref.shape[0]
    nblk = cap // sb
    nrow = hx_ref.shape[1] // TOKEN_ROW
    wrows = win // TOKEN_ROW
    base = (b * ne + e) * (nblk + 1)
    for s in range(nblk):
        blk = slice(s * sb, (s + 1) * sb)
        target = lax.broadcasted_iota(jnp.int32, (sb, TOKEN_ROW), 0) + s * sb
        r0 = jnp.minimum(edge_ref[base + s] // TOKEN_ROW, nrow - wrows)
        rows = slot_ref[0, 0, pl.ds(r0, wrows), :]
        onehot = jnp.concatenate([jnp.where(rows[k:k + 1, :] == target, 1.0, 0.0).astype(BF16)
                                  for k in range(wrows)], axis=1)
        t0 = pl.multiple_of(r0 * TOKEN_ROW, TOKEN_ROW)
        xs_ref[blk, :] = _dot(onehot, hx_ref[0, pl.ds(t0, win), :])

        def extra_row(r, carry, blk=blk, target=target):
            hit = jnp.where(slot_ref[0, 0, pl.ds(r, 1), :] == target, 1.0, 0.0).astype(BF16)
            tr = pl.multiple_of(r * TOKEN_ROW, TOKEN_ROW)
            xs_ref[blk, :] += _dot(hit, hx_ref[0, pl.ds(tr, TOKEN_ROW), :])
            return carry

        r_end = (edge_ref[base + s + 1] + TOKEN_ROW - 1) // TOKEN_ROW
        lax.fori_loop(r0 + wrows, r_end, extra_row, 0)

    xs = xs_ref[...].astype(BF16)
    h1 = _dot(xs, wg_ref[0].astype(BF16))
    h2 = _dot(xs, wu_ref[0].astype(BF16))
    hid = (h1 * jax.nn.sigmoid(h1) * h2).astype(BF16)
    ye_ref[0, 0] = _dot(hid, wd_ref[0].astype(BF16)).astype(BF16)


def _moe(edges, hx, slot4, wg, wu, wd, cap, win, sb):
    bsz, t, d = hx.shape
    ne, _, f = wg.shape
    nrow = t // TOKEN_ROW
    grid_spec = pltpu.PrefetchScalarGridSpec(
        num_scalar_prefetch=1,
        grid=(bsz, ne),
        in_specs=[pl.BlockSpec((1, t, d), lambda b, e, s: (b, 0, 0), pipeline_mode=pl.Buffered(1)),
                  pl.BlockSpec((1, 1, nrow, TOKEN_ROW), lambda b, e, s: (b, e, 0, 0)),
                  pl.BlockSpec((1, d, f), lambda b, e, s: (e, 0, 0)),
                  pl.BlockSpec((1, d, f), lambda b, e, s: (e, 0, 0)),
                  pl.BlockSpec((1, f, d), lambda b, e, s: (e, 0, 0))],
        out_specs=pl.BlockSpec((1, 1, cap, d), lambda b, e, s: (b, e, 0, 0)),
        scratch_shapes=[pltpu.VMEM((cap, d), F32)],
    )
    return pl.pallas_call(
        functools.partial(_moe_kernel, win, sb),
        grid_spec=grid_spec,
        out_shape=jax.ShapeDtypeStruct((bsz, ne, cap, d), BF16),
        compiler_params=_cparams(("parallel", "arbitrary")),
        name="moe",
    )(edges, hx, slot4, wg, wu, wd)


COMB_ROWS = 128
COMB_WIN = 64
SLOT_ALIGN = 16


def _comb_rows(win, groups, xm_ref, ye_ref, slotc, affc, g2_ref, fg_ref, o_ref):
    ne, cap = ye_ref.shape[1], ye_ref.shape[2]
    lane_slot = lax.broadcasted_iota(jnp.int32, (COMB_ROWS, win), 1)
    lhs, rhs = [], []
    for h, first in groups:
        rows = slice(h * COMB_ROWS, (h + 1) * COMB_ROWS)
        hi, lo, ywin = [], [], []
        for e in range(ne):
            start = pl.multiple_of(jnp.minimum(first[e] & ~(SLOT_ALIGN - 1), cap - win), SLOT_ALIGN)
            hit = slotc[rows, e:e + 1] - start == lane_slot
            val = affc[rows, e:e + 1]
            v_hi = val.astype(BF16).astype(F32)
            hi.append(jnp.where(hit, v_hi, 0.0).astype(BF16))
            lo.append(jnp.where(hit, val - v_hi, 0.0).astype(BF16))
            ywin.append(ye_ref[0, e, pl.ds(start, win), :])
        lhs.append(jnp.concatenate([jnp.concatenate(hi, axis=1), jnp.concatenate(lo, axis=1)], axis=0))
        rhs.append(jnp.concatenate(ywin, axis=0))
    both = [_dot(a, y) for a, y in zip(lhs, rhs)]
    for (h, _), bt in zip(groups, both):
        rows = slice(h * COMB_ROWS, (h + 1) * COMB_ROWS)
        x = xm_ref[0, rows, :] + g2_ref[0] * (bt[:COMB_ROWS] + bt[COMB_ROWS:])
        ms = jnp.mean(x * x, axis=-1, keepdims=True)
        o_ref[0, rows, :] = x * lax.rsqrt(ms + NORM_EPS) * fg_ref[...]


def _comb_kernel(tk, win, full, tsp_ref, xm_ref, ye_ref, slotc_ref, affc_ref, g2_ref, fg_ref, o_ref):
    b = pl.program_id(0)
    j = pl.program_id(1)
    nh = tk // COMB_ROWS
    ntile = pl.num_programs(1) * nh + 1
    ne, cap = ye_ref.shape[1], ye_ref.shape[2]
    slotc = slotc_ref[0]
    affc = affc_ref[0]
    refs = (xm_ref, ye_ref, slotc, affc, g2_ref, fg_ref, o_ref)
    groups = [(h, [tsp_ref[(b * ne + e) * ntile + j * nh + h] for e in range(ne)]) for h in range(nh)]
    _comb_rows(win, groups, *refs)
    if full != win:
        for h, first in groups:
            over = None
            for e in range(ne):
                start = jnp.minimum(first[e] & ~(SLOT_ALIGN - 1), cap - win)
                miss = tsp_ref[(b * ne + e) * ntile + j * nh + h + 1] > start + win
                over = miss if over is None else over | miss

            @pl.when(over)
            def _(h=h, first=first):
                _comb_rows(full, [(h, first)], *refs)


def _comb(tsp, xm, ye, slotc, affc, g2, fg, tk):
    bsz, t, d = xm.shape
    ne, cap = ye.shape[1], ye.shape[2]
    full = min(2 * COMB_ROWS, cap)
    assert full == cap or full >= COMB_ROWS + SLOT_ALIGN
    win = min(COMB_WIN, full)
    grid_spec = pltpu.PrefetchScalarGridSpec(
        num_scalar_prefetch=1,
        grid=(bsz, t // tk),
        in_specs=[pl.BlockSpec((1, tk, d), lambda b, j, s: (b, j, 0)),
                  pl.BlockSpec((1, ne, cap, d), lambda b, j, s: (b, 0, 0, 0)),
                  pl.BlockSpec((1, tk, ne), lambda b, j, s: (b, j, 0)),
                  pl.BlockSpec((1, tk, ne), lambda b, j, s: (b, j, 0)),
                  pl.BlockSpec((1, 1, d), lambda b, j, s: (b, 0, 0)),
                  pl.BlockSpec((1, d), lambda b, j, s: (0, 0))],
        out_specs=pl.BlockSpec((1, tk, d), lambda b, j, s: (b, j, 0)),
    )
    return pl.pallas_call(
        functools.partial(_comb_kernel, tk, win, full),
        grid_spec=grid_spec,
        out_shape=jax.ShapeDtypeStruct((bsz, t, d), F32),
        compiler_params=_cparams(("parallel", "arbitrary")),
        name="comb",
    )(tsp, xm, ye, slotc, affc, g2, fg)


def _block_diag_ones(n, blk, value=1.0):
    r = jnp.arange(n)
    return jnp.where((r[:, None] // blk) == (r[None, :] // blk), value, 0.0)


def kernel(x, c, ctx, c_ctx, ada_w, ada_b, norm1_g, norm2_g, w_in, shift_mu, w0, w_lora_up, a0, a_lora_up, k_k, k_a,
           r_k, g_lora_up, gn_g, gn_b, conv_w, w_out, router_w, exp_w_gate, exp_w_up, exp_w_down, final_g):
    bsz, t, d = x.shape
    lc = ctx.shape[1]
    ne = router_w.shape[-1]
    cap = EC_CAPACITY * t // ne
    tt = 256
    sb = min(128, cap)
    win = min(1536, t)
    l = 0

    rows = ((bsz + 1 + 7) // 8) * 8
    cc = jnp.zeros((rows, d), F32).at[:bsz].set(c).at[bsz].set(c_ctx)
    mod = _mod(cc, ada_w[l], ada_b[l][None, :])
    sh1, sc1, g1, sh2, sc2, g2 = (m[:, None, :] for m in jnp.split(mod[:bsz], 6, axis=-1))
    csh1, csc1 = (jnp.broadcast_to(m[None, None, :], (bsz, 1, d)) for m in jnp.split(mod[bsz], 6)[:2])

    w_rw = w_in[l][:, :RWKV_COLS].astype(BF16)
    w_cv = w_in[l][:, RWKV_COLS:].astype(BF16)
    n1g = norm1_g[l][None, :]
    px_rw, px_cv = _in_proj(x, sh1, sc1, n1g, w_rw, w_cv, 512)
    pc_rw, _ = _in_proj(ctx, csh1, csc1, n1g, w_rw, w_cv, 256)

    zw = jnp.zeros((LORA_W, 2 * D_RWKV), F32)
    lora = jnp.concatenate([
        jnp.concatenate([w_lora_up[l, 0], w_lora_up[l, 1], zw], axis=1),
        jnp.concatenate([zw, a_lora_up[l, 0], a_lora_up[l, 1]], axis=1)], axis=0).astype(BF16)
    ridx = jnp.arange(CHUNK)
    tril = jnp.where(ridx[None, :] <= ridx[:, None], 1.0, 0.0).astype(BF16)
    triu = jnp.where(ridx[None, :] >= ridx[:, None], 1.0, 0.0).astype(BF16)
    ones_bd = _block_diag_ones(PAIR, HEAD).astype(BF16)
    consts = (shift_mu[l][None, :], lora, w0[l], a0[l], k_k[l][None, :], k_a[l][None, :],
              r_k[l].reshape(1, D_RWKV), g_lora_up[l].astype(BF16), ones_bd, tril, triu)

    chunks_c = _prep(pc_rw, consts, False, lc)[:5]
    *chunks_x, bonus, gate = _prep(px_rw, consts, True, tt)

    s_zero = jnp.zeros((bsz, 2, N_PAIR, HEAD, PAIR), F32)
    _, _, s_ctx = _scan(*chunks_c, s_zero)
    yf, yb, _ = _scan(*chunks_x, s_ctx)

    rw_t = router_w[l].T
    rwh = rw_t.astype(BF16)
    rwl = (rw_t - rwh.astype(F32)).astype(BF16)
    avg = _block_diag_ones(D_RWKV, HEAD, 1.0 / HEAD).astype(BF16)
    xm, hx, aff_t = _out(yf, yb, bonus, gate, px_cv, x, g1, sh2, sc2, norm2_g[l][None, :], gn_g[l][None, :],
                         gn_b[l][None, :], conv_w[l], w_out[l].astype(BF16), rwh, rwl, avg, 2 * tt)

    r128 = jnp.arange(128)
    tri128 = jnp.where(r128[:, None] <= r128[None, :], 1.0, 0.0).astype(BF16)
    cnt, slot, edges = _topk(aff_t, tri128, cap, sb)

    ye = _moe(edges[:, :, :cap // sb + 1].reshape(-1), hx, slot.reshape(bsz, ne, t // TOKEN_ROW, TOKEN_ROW),
              exp_w_gate[l], exp_w_up[l], exp_w_down[l], cap, win, sb)
    tr = lambda a: jnp.transpose(a, (0, 2, 1))
    first_slot = jnp.concatenate([cnt[:, :, ::COMB_ROWS], jnp.full((bsz, ne, 1), cap, jnp.int32)], axis=-1).reshape(-1)
    return _comb(first_slot, xm, ye, tr(slot), tr(aff_t), g2, final_g[None, :], tt)
```

```python
import functools
import math

import jax
import jax.numpy as jnp
from jax import lax
from jax.experimental import pallas as pl
from jax.experimental.pallas import tpu as pltpu

F32 = jnp.float32
BF16 = jnp.bfloat16
HIGHEST = lax.Precision.HIGHEST

GRID_W = 64
D_RWKV = 512
D_CONV = 512
HEAD = 64
LORA_W = 64
LORA_A = 64
LORA_G = 128
N_EXPERTS = 16
EC_CAPACITY = 2
NORM_EPS = 1e-6
GN_EPS = 64e-5
RWKV_COLS = 3 * D_RWKV + LORA_W + LORA_A + LORA_G
CONV_COLS = 3 * D_CONV

LANES = 128
CHUNK = 64
PAIR = 2 * HEAD
N_PAIR = D_RWKV // PAIR
VMEM_LIMIT = 48 * 1024 * 1024

IN_PROJ_ROWS = 512
PREP_ROWS = 256
OUT_TILE = 512
COMB_TILE = 256
SLOT_BLOCK = 128
GATHER_WIN = 1536


def _cparams(sem):
    return pltpu.CompilerParams(dimension_semantics=sem, vmem_limit_bytes=VMEM_LIMIT)


def _dot(a, b):
    return jnp.dot(a, b, preferred_element_type=F32)


def _dot_nt(a, b):
    return lax.dot_general(a, b, (((1,), (1,)), ((), ())), preferred_element_type=F32)


def _split2(x):
    hi = x.astype(BF16)
    lo = (x - hi.astype(F32)).astype(BF16)
    return hi, lo


def _split3(x):
    hi = x.astype(BF16)
    r = x - hi.astype(F32)
    mid = r.astype(BF16)
    lo = (r - mid.astype(F32)).astype(BF16)
    return hi, mid, lo


def _seg_dot(x, m):
    hi, lo = _split2(x)
    return _dot(hi, m) + _dot(lo, m)


def _mod_kernel(c_ref, w_ref, b_ref, o_ref):
    c = c_ref[...]
    s = c * jax.nn.sigmoid(c)
    o_ref[...] = jnp.dot(s, w_ref[...], precision=HIGHEST, preferred_element_type=F32) + b_ref[...]


def _mod(cc, w, b):
    rows, d = cc.shape
    n = w.shape[1]
    tn = 1024
    return pl.pallas_call(
        _mod_kernel,
        grid=(n // tn,),
        in_specs=[pl.BlockSpec((rows, d), lambda j: (0, 0)),
                  pl.BlockSpec((d, tn), lambda j: (0, j)),
                  pl.BlockSpec((1, tn), lambda j: (0, j))],
        out_specs=pl.BlockSpec((rows, tn), lambda j: (0, j)),
        out_shape=jax.ShapeDtypeStruct((rows, n), F32),
        compiler_params=_cparams(("parallel",)),
        name="mod",
    )(cc, w, b)


def _in_proj_kernel(x_ref, sh_ref, sc_ref, g_ref, wrw_ref, wcv_ref, orw_ref, ocv_ref):
    x = x_ref[0]
    ms = jnp.mean(x * x, axis=-1, keepdims=True)
    h = x * lax.rsqrt(ms + NORM_EPS) * g_ref[...]
    h = (h * (1.0 + sc_ref[0]) + sh_ref[0]).astype(BF16)
    orw_ref[0] = _dot(h, wrw_ref[...])
    ocv_ref[0] = _dot(h, wcv_ref[...])


def _in_proj(x, sh, sc, g, w_rw, w_cv, tm):
    bsz, l, d = x.shape
    return pl.pallas_call(
        _in_proj_kernel,
        grid=(bsz, l // tm),
        in_specs=[pl.BlockSpec((1, tm, d), lambda b, i: (b, i, 0)),
                  pl.BlockSpec((1, 1, d), lambda b, i: (b, 0, 0)),
                  pl.BlockSpec((1, 1, d), lambda b, i: (b, 0, 0)),
                  pl.BlockSpec((1, d), lambda b, i: (0, 0)),
                  pl.BlockSpec((d, RWKV_COLS), lambda b, i: (0, 0)),
                  pl.BlockSpec((d, CONV_COLS), lambda b, i: (0, 0))],
        out_specs=[pl.BlockSpec((1, tm, RWKV_COLS), lambda b, i: (b, i, 0)),
                   pl.BlockSpec((1, tm, CONV_COLS), lambda b, i: (b, i, 0))],
        out_shape=[jax.ShapeDtypeStruct((bsz, l, RWKV_COLS), F32),
                   jax.ShapeDtypeStruct((bsz, l, CONV_COLS), F32)],
        compiler_params=_cparams(("parallel", "parallel")),
        name="in_proj",
    )(x, sh, sc, g, w_rw, w_cv)


def _bf(x):
    return x.astype(BF16)


def _chunk_local(chains):
    c = CHUNK
    n = range(len(chains))
    lane = lax.broadcasted_iota(jnp.int32, (c, PAIR), 1)
    h0 = lane < HEAD
    tcol = lane & (c - 1)
    trow = lax.broadcasted_iota(jnp.int32, (c, PAIR), 0)
    eye = jnp.where(tcol == trow, 1.0, 0.0)
    masks = {rev: ((tcol > trow) if rev else (tcol < trow), (tcol >= trow) if rev else (tcol <= trow))
             for rev in (False, True)}
    zb = jnp.zeros((c, PAIR), BF16)

    def stack(x):
        return jnp.concatenate([jnp.where(h0, x, zb), jnp.where(h0, zb, x)], axis=0)

    gram = [_dot_nt(jnp.concatenate([ch[1], ch[2]], axis=0), jnp.concatenate([stack(ch[3]), stack(ch[4])], axis=0))
            for ch in chains]
    lab = [jnp.where(masks[chains[i][0]][0], gram[i][:c, :PAIR], 0.0) for i in n]
    lak = [jnp.where(masks[chains[i][0]][0], gram[i][:c, PAIR:], 0.0) for i in n]
    mrb = [jnp.where(masks[chains[i][0]][1], gram[i][c:, :PAIR], 0.0) for i in n]
    mrk = [jnp.where(masks[chains[i][0]][1], gram[i][c:, PAIR:], 0.0) for i in n]

    labb = [_bf(x) for x in lab]
    pw = [_dot(labb[i], stack(labb[i])) for i in n]
    tp = [eye + lab[i] for i in n]
    for _ in range(4):
        pwb = [_bf(x) for x in pw]
        both = [_dot(pwb[i], jnp.concatenate([stack(pwb[i]), stack(_bf(tp[i]))], axis=1)) for i in n]
        pw = [x[:, :PAIR] for x in both]
        tp = [tp[i] + both[i][:, PAIR:] for i in n]
    tinv = [tp[i] + _dot(_bf(pw[i]), stack(_bf(tp[i]))) for i in n]

    lmv = [_dot(_bf(jnp.concatenate([lak[i], mrk[i]], axis=0)), stack(chains[i][7])) for i in n]
    x = [_dot(_bf(tinv[i]), jnp.concatenate([stack(chains[i][1]), stack(_bf(lmv[i][:c]))], axis=1))
         for i in n]
    z = [_dot(_bf(mrb[i]), jnp.concatenate([stack(_bf(x[i][:, :PAIR])), stack(_bf(x[i][:, PAIR:]))], axis=1))
         for i in n]
    rbar = [_bf(chains[i][2].astype(F32) + z[i][:, :PAIR]) for i in n]
    y0 = [z[i][:, PAIR:] + lmv[i][c:] for i in n]

    uv = [jnp.concatenate([x[i][:, PAIR:], chains[i][7].astype(F32)], axis=0) for i in n]
    mxf = [_dot(_bf(x[i][:, :PAIR].T), chains[i][5]) for i in n]
    nnf = [_dot(_bf(uv[i].T), jnp.concatenate([chains[i][5], chains[i][6]], axis=0)) for i in n]
    mx = [_bf(jnp.where(h0, m[:c], m[c:])) for m in mxf]
    nn = [jnp.where(h0, m[:c], m[c:]) for m in nnf]
    return list(zip(rbar, y0, mx, nn))


DIR_COLS = 6 * D_RWKV
SCR_COLS = 2 * DIR_COLS + D_RWKV


def _head_sums(x, ones):
    return jnp.concatenate([_seg_dot(x[:, g * PAIR:(g + 1) * PAIR], ones) for g in range(N_PAIR)], axis=1)


def _prep_rows(c, p, shifted, lo, mu_ref, w0_ref, a0_ref, kk_ref, ka_ref, rk_ref, ones_ref, tril_ref, triu_ref,
               wc_ref, bonus_ref, scr_ref):
    rows = slice(c * CHUNK, (c + 1) * CHUNK)
    pm = p + mu_ref[:, 0:3 * D_RWKV] * (shifted - p)
    xr = pm[:, 0:D_RWKV]
    xk = pm[:, D_RWKV:2 * D_RWKV]
    xv = pm[:, 2 * D_RWKV:3 * D_RWKV]

    ones = ones_ref[...]
    kraw = xk * kk_ref[...]
    kk = kraw / jnp.maximum(jnp.sqrt(_head_sums(kraw * kraw, ones)), 1e-12)

    scr_ref[:, 2 * DIR_COLS:] = xv.astype(BF16)
    ksum = None
    for d in range(2):
        z = w0_ref[d:d + 1, :] + lo[:, d * D_RWKV:(d + 1) * D_RWKV]
        ld = -math.exp(-0.5) * jax.nn.sigmoid(z)
        ag = jax.nn.sigmoid(a0_ref[d:d + 1, :] + lo[:, (2 + d) * D_RWKV:(3 + d) * D_RWKV])
        kd = xk * (1.0 + (ag - 1.0) * ka_ref[...])
        bb = kk * ag
        ksum = kd if ksum is None else ksum + kd
        tri = tril_ref[...] if d == 0 else triu_ref[...]
        h3 = _split3(ld)
        cs = _dot(tri, h3[0]) + _dot(tri, h3[1]) + _dot(tri, h3[2])
        e_in = jnp.exp(cs)
        e_ex = jnp.exp(cs - ld)
        e_neg = jnp.exp(-cs)
        bt = bb * e_neg
        kt = kd * e_neg
        edge = CHUNK - 1 if d == 0 else 0
        wrow = e_in[edge:edge + 1, :]
        wc_ref[0, c, :, d * D_RWKV:(d + 1) * D_RWKV] = wrow
        base = d * DIR_COLS
        for j, val in enumerate((-kk * e_ex, xr * e_in, bt, kt, bt * wrow, kt * wrow)):
            scr_ref[:, base + j * D_RWKV:base + (j + 1) * D_RWKV] = val.astype(BF16)

    bonus_ref[0, rows, :] = _head_sums(xr * ksum * rk_ref[...], ones) * xv


PREP_GROUP = 2


def _prep_chunks(cs, scr_refs, rbar_ref, y0_ref, mx_ref, nn_ref):
    chains = []
    for scr_ref in scr_refs:
        for d in range(2):
            for pp in range(N_PAIR):
                cols = [d * DIR_COLS + j * D_RWKV + pp * PAIR for j in range(6)] + [2 * DIR_COLS + pp * PAIR]
                chains.append((d == 1,) + tuple(scr_ref[:, k:k + PAIR] for k in cols))
    per_chunk = 2 * N_PAIR
    for idx, (rbar, y0, mx, nn) in enumerate(_chunk_local(chains)):
        c = cs[idx // per_chunk]
        rows = slice(c * CHUNK, (c + 1) * CHUNK)
        col = (idx % per_chunk // N_PAIR) * D_RWKV + (idx % N_PAIR) * PAIR
        rbar_ref[0, rows, col:col + PAIR] = rbar
        y0_ref[0, rows, col:col + PAIR] = y0
        mx_ref[0, c, :, col:col + PAIR] = mx
        nn_ref[0, c, :, col:col + PAIR] = nn


N_PREP_CONSTS = 11


def _prep_kernel(latent, tt, p_ref, *rest):
    if latent:
        prev_ref, next_ref = rest[:2]
        rest = rest[2:]
    (mu_ref, lw_ref, w0_ref, a0_ref, kk_ref, ka_ref, rk_ref, gup_ref, ones_ref, tril_ref,
     triu_ref) = rest[:N_PREP_CONSTS]
    rbar_ref, y0_ref, mx_ref, nn_ref, wc_ref, bonus_ref, g_ref = rest[N_PREP_CONSTS:N_PREP_CONSTS + 7]
    *scr, lo_ref, ext_ref = rest[N_PREP_CONSTS + 7:]
    if latent:
        i = pl.program_id(1)
        n = pl.num_programs(1)
        ext_ref[0:GRID_W] = jnp.where(i > 0, prev_ref[0], 0.0)
        ext_ref[GRID_W:GRID_W + tt] = p_ref[0]
        ext_ref[GRID_W + tt:] = jnp.where(i < n - 1, next_ref[0], 0.0)

        def shifted(r0, nrows, cols):
            shape = (nrows, cols.stop - cols.start)
            trow = lax.broadcasted_iota(jnp.int32, shape, 0) & (GRID_W - 1)
            q = lax.broadcasted_iota(jnp.int32, shape, 1) & 3
            left = jnp.where(trow == 0, 0.0, ext_ref[GRID_W - 1 + r0:GRID_W - 1 + r0 + nrows, cols])
            right = jnp.where(trow == GRID_W - 1, 0.0, ext_ref[GRID_W + 1 + r0:GRID_W + 1 + r0 + nrows, cols])
            up = ext_ref[r0:r0 + nrows, cols]
            down = ext_ref[2 * GRID_W + r0:2 * GRID_W + r0 + nrows, cols]
            return jnp.where(q == 0, left, jnp.where(q == 1, right, jnp.where(q == 2, up, down)))
    else:
        ext_ref[0:8] = jnp.zeros((8, RWKV_COLS), F32)
        ext_ref[8:8 + tt] = p_ref[0]
        ext_ref[8 + tt:] = jnp.zeros((8, RWKV_COLS), F32)

        def shifted(r0, nrows, cols):
            q = lax.broadcasted_iota(jnp.int32, (nrows, cols.stop - cols.start), 1) & 1
            return jnp.where(q == 0, ext_ref[7 + r0:7 + r0 + nrows, cols], ext_ref[9 + r0:9 + r0 + nrows, cols])

    tail = slice(3 * D_RWKV, RWKV_COLS)
    p_t = p_ref[0, :, tail]
    pm_t = p_t + mu_ref[:, tail] * (shifted(0, tt, tail) - p_t)
    xwa = pm_t[:, 0:LORA_W + LORA_A]
    lane = lax.broadcasted_iota(jnp.int32, xwa.shape, 1)
    lin = jnp.where(lane < LORA_W, jnp.tanh(xwa), xwa).astype(BF16)
    lo_ref[...] = _dot(lin, lw_ref[...])
    g_ref[0] = _dot(jax.nn.sigmoid(pm_t[:, LORA_W + LORA_A:]).astype(BF16), gup_ref[...])

    head = slice(0, 3 * D_RWKV)

    def rows_part(c):
        rows = slice(c * CHUNK, (c + 1) * CHUNK)
        _prep_rows(c, p_ref[0, rows, head], shifted(c * CHUNK, CHUNK, head), lo_ref[rows, :], mu_ref, w0_ref,
                   a0_ref, kk_ref, ka_ref, rk_ref, ones_ref, tril_ref, triu_ref, wc_ref, bonus_ref,
                   scr[c % (2 * PREP_GROUP)])

    groups = [list(range(g, g + PREP_GROUP)) for g in range(0, tt // CHUNK, PREP_GROUP)]
    for c in groups[0]:
        rows_part(c)
    for k, cs in enumerate(groups):
        if k + 1 < len(groups):
            for c in groups[k + 1]:
                rows_part(c)
        _prep_chunks(cs, [scr[c % (2 * PREP_GROUP)] for c in cs], rbar_ref, y0_ref, mx_ref, nn_ref)


def _prep(p_rw, consts, latent, tt):
    bsz, l, _ = p_rw.shape
    nc = l // CHUNK
    cpt = tt // CHUNK
    assert len(consts) == N_PREP_CONSTS
    const_specs = [pl.BlockSpec(c.shape, lambda b, i, nd=c.ndim: (0,) * nd) for c in consts]
    kern = functools.partial(_prep_kernel, latent, tt)
    if latent:
        nblk = l // GRID_W
        in_specs = [pl.BlockSpec((1, tt, RWKV_COLS), lambda b, i: (b, i, 0)),
                    pl.BlockSpec((1, GRID_W, RWKV_COLS),
                                 lambda b, i: (b, jnp.maximum(i * (tt // GRID_W) - 1, 0), 0)),
                    pl.BlockSpec((1, GRID_W, RWKV_COLS),
                                 lambda b, i: (b, jnp.minimum((i + 1) * (tt // GRID_W), nblk - 1), 0))]
        args = (p_rw, p_rw, p_rw)
        ext_rows = tt + 2 * GRID_W
    else:
        assert tt == l
        in_specs = [pl.BlockSpec((1, tt, RWKV_COLS), lambda b, i: (b, i, 0))]
        args = (p_rw,)
        ext_rows = tt + 16
    row = lambda b, i: (b, i, 0)
    chunk = lambda b, i: (b, i, 0, 0)
    w2 = 2 * D_RWKV
    return pl.pallas_call(
        kern,
        grid=(bsz, l // tt),
        in_specs=in_specs + const_specs,
        out_specs=[pl.BlockSpec((1, tt, w2), row), pl.BlockSpec((1, tt, w2), row),
                   pl.BlockSpec((1, cpt, CHUNK, w2), chunk), pl.BlockSpec((1, cpt, CHUNK, w2), chunk),
                   pl.BlockSpec((1, cpt, 1, w2), chunk),
                   pl.BlockSpec((1, tt, D_RWKV), row), pl.BlockSpec((1, tt, D_RWKV), row)],
        out_shape=[jax.ShapeDtypeStruct((bsz, l, w2), BF16), jax.ShapeDtypeStruct((bsz, l, w2), F32),
                   jax.ShapeDtypeStruct((bsz, nc, CHUNK, w2), BF16), jax.ShapeDtypeStruct((bsz, nc, CHUNK, w2), F32),
                   jax.ShapeDtypeStruct((bsz, nc, 1, w2), F32),
                   jax.ShapeDtypeStruct((bsz, l, D_RWKV), F32), jax.ShapeDtypeStruct((bsz, l, D_RWKV), F32)],
        scratch_shapes=[pltpu.VMEM((CHUNK, SCR_COLS), BF16)] * (2 * PREP_GROUP)
        + [pltpu.VMEM((tt, 4 * D_RWKV), F32), pltpu.VMEM((ext_rows, RWKV_COLS), F32)],
        compiler_params=_cparams(("parallel", "parallel")),
        name="prep_latent" if latent else "prep_ctx",
    )(*args, *consts)


SCAN_CHUNKS = 4
SCAN_BATCH = 2


def _scan_kernel(rbf_ref, rbb_ref, y0f_ref, y0b_ref, mxf_ref, mxb_ref, nnf_ref, nnb_ref, wcf_ref, wcb_ref,
                 s0_ref, yf_ref, yb_ref, s_ref):
    i = pl.program_id(1)

    @pl.when(i == 0)
    def _():
        s_ref[...] = s0_ref[...]

    c = CHUNK
    h0 = lax.broadcasted_iota(jnp.int32, (c, PAIR), 1) < HEAD
    zb = jnp.zeros((c, PAIR), BF16)

    def stack(x):
        return jnp.concatenate([jnp.where(h0, x, zb), jnp.where(h0, zb, x)], axis=0)

    dirs = ((rbf_ref, y0f_ref, mxf_ref, nnf_ref, wcf_ref, yf_ref), (rbb_ref, y0b_ref, mxb_ref, nnb_ref, wcb_ref, yb_ref))
    idx = [(n, d, p) for n in range(SCAN_BATCH) for d in range(2) for p in range(N_PAIR)]
    lanes = [slice(p * PAIR, (p + 1) * PAIR) for _, _, p in idx]
    s = [s_ref[n, d, p] for n, d, p in idx]
    for q in range(SCAN_CHUNKS):
        cq = (q, SCAN_CHUNKS - 1 - q)
        rows = [slice(cq[d] * c, (cq[d] + 1) * c) for _, d, _ in idx]
        sb = [_bf(x) for x in s]
        y = [_dot_nt(dirs[d][0][n, rows[k], lanes[k]], stack(sb[k])) for k, (n, d, _) in enumerate(idx)]
        sm = [_dot(sb[k], stack(dirs[d][2][n, cq[d], :, lanes[k]])) for k, (n, d, _) in enumerate(idx)]
        s_next = []
        for k, (n, d, p) in enumerate(idx):
            dirs[d][5][n, rows[k], lanes[k]] = y[k] + dirs[d][1][n, rows[k], lanes[k]]
            s_next.append(s[k] * dirs[d][4][n, cq[d], :, lanes[k]] + sm[k] + dirs[d][3][n, cq[d], :, lanes[k]])
        s = s_next
    for k, (n, d, p) in enumerate(idx):
        s_ref[n, d, p] = s[k]


def _scan(rbar, y0, mx, nn, wc, s0):
    bsz, l, _ = rbar.shape
    sub, nb = SCAN_CHUNKS, SCAN_BATCH
    nc = l // (CHUNK * sub)
    row_f = pl.BlockSpec((nb, sub * CHUNK, D_RWKV), lambda b, i: (b, i, 0))
    row_b = pl.BlockSpec((nb, sub * CHUNK, D_RWKV), lambda b, i: (b, nc - 1 - i, 1))
    chk_f = pl.BlockSpec((nb, sub, CHUNK, D_RWKV), lambda b, i: (b, i, 0, 0))
    chk_b = pl.BlockSpec((nb, sub, CHUNK, D_RWKV), lambda b, i: (b, nc - 1 - i, 0, 1))
    wc_f = pl.BlockSpec((nb, sub, 1, D_RWKV), lambda b, i: (b, i, 0, 0))
    wc_b = pl.BlockSpec((nb, sub, 1, D_RWKV), lambda b, i: (b, nc - 1 - i, 0, 1))
    st_spec = pl.BlockSpec((nb, 2, N_PAIR, HEAD, PAIR), lambda b, i: (b, 0, 0, 0, 0))
    return pl.pallas_call(
        _scan_kernel,
        grid=(bsz // nb, nc),
        in_specs=[row_f, row_b, row_f, row_b, chk_f, chk_b, chk_f, chk_b, wc_f, wc_b, st_spec],
        out_specs=[row_f, pl.BlockSpec((nb, sub * CHUNK, D_RWKV), lambda b, i: (b, nc - 1 - i, 0)), st_spec],
        out_shape=[jax.ShapeDtypeStruct((bsz, l, D_RWKV), F32),
                   jax.ShapeDtypeStruct((bsz, l, D_RWKV), F32),
                   jax.ShapeDtypeStruct((bsz, 2, N_PAIR, HEAD, PAIR), F32)],
        compiler_params=_cparams(("parallel", "arbitrary")),
        name="scan",
    )(rbar, rbar, y0, y0, mx, mx, nn, nn, wc, wc, s0)


OUT_ROWS = 128


def _out_kernel(tt, yf_ref, yb_ref, bonus_ref, g_ref, cv_ref, cvp_ref, cvn_ref, x_ref, g1_ref, sh2_ref,
                sc2_ref, n2g_ref, gng_ref, gnb_ref, convw_ref, wout_ref, rwh_ref, rwl_ref, avg_ref,
                xm_ref, hx_ref, aff_ref):
    i = pl.program_id(1)
    n = pl.num_programs(1)

    cv = cv_ref[0]
    b_gate = cv[:, 0:D_CONV]
    cu = cv[:, D_CONV:2 * D_CONV] * cv[:, 2 * D_CONV:]
    cvp = cvp_ref[0]
    cvn = cvn_ref[0]
    cu_prev = jnp.where(i > 0, cvp[7:8, D_CONV:2 * D_CONV] * cvp[7:8, 2 * D_CONV:], 0.0)
    cu_next = jnp.where(i < n - 1, cvn[0:1, D_CONV:2 * D_CONV] * cvn[0:1, 2 * D_CONV:], 0.0)
    ridx = lax.broadcasted_iota(jnp.int32, cu.shape, 0)
    cu_m1 = jnp.where(ridx == 0, cu_prev, pltpu.roll(cu, 1, 0))
    cu_p1 = jnp.where(ridx == tt - 1, cu_next, pltpu.roll(cu, tt - 1, 0))
    conv = convw_ref[0:1, :] * cu_m1 + convw_ref[1:2, :] * cu + convw_ref[2:3, :] * cu_p1
    bx = (b_gate * conv).astype(BF16)

    parts = [slice(k * OUT_ROWS, (k + 1) * OUT_ROWS) for k in range(tt // OUT_ROWS)]
    avg = avg_ref[...]
    rwh = rwh_ref[...]
    rwl = rwl_ref[...]
    y = [yf_ref[0, r, :] + yb_ref[0, r, :] for r in parts]
    mu = [_head_sums(v, avg) for v in y]
    dlt = [a - b for a, b in zip(y, mu)]
    var = [_head_sums(v * v, avg) for v in dlt]
    yn = [a * lax.rsqrt(b + GN_EPS) * gng_ref[...] + gnb_ref[...] for a, b in zip(dlt, var)]
    ax = [((a + bonus_ref[0, r, :]) * g_ref[0, r, :]).astype(BF16) for a, r in zip(yn, parts)]
    mix = [_dot(a, wout_ref[0:D_RWKV, :]) + _dot(bx[r], wout_ref[D_RWKV:, :]) for a, r in zip(ax, parts)]
    xm = [x_ref[0, r, :] + g1_ref[0] * a for a, r in zip(mix, parts)]
    ms = [jnp.mean(v * v, axis=-1, keepdims=True) for v in xm]
    hx = [a * lax.rsqrt(b + NORM_EPS) * n2g_ref[...] for a, b in zip(xm, ms)]
    hx = [_split2(v * (1.0 + sc2_ref[0]) + sh2_ref[0]) for v in hx]
    logits = [_dot_nt(rwh, hi) + _dot_nt(rwh, lo) + _dot_nt(rwl, hi) for hi, lo in hx]
    for k, r in enumerate(parts):
        xm_ref[0, r, :] = xm[k]
        hx_ref[0, r, :] = hx[k][0]
        m = jnp.max(logits[k], axis=0, keepdims=True)
        ex = jnp.exp(logits[k] - m)
        aff_ref[0, :, r] = ex / jnp.sum(ex, axis=0, keepdims=True)


def _out(yf, yb, bonus, g, p_cv, x, g1, sh2, sc2, n2g, gng, gnb, convw, wout, rwh, rwl, avg, tt):
    bsz, t, d = x.shape
    ne = rwh.shape[0]
    nb8 = t // 8
    row = lambda b, i: (b, i, 0)
    per_b = lambda b, i: (b, 0, 0)
    const2 = lambda b, i: (0, 0)
    return pl.pallas_call(
        functools.partial(_out_kernel, tt),
        grid=(bsz, t // tt),
        in_specs=[pl.BlockSpec((1, tt, D_RWKV), row), pl.BlockSpec((1, tt, D_RWKV), row),
                  pl.BlockSpec((1, tt, D_RWKV), row), pl.BlockSpec((1, tt, D_RWKV), row),
                  pl.BlockSpec((1, tt, CONV_COLS), row),
                  pl.BlockSpec((1, 8, CONV_COLS), lambda b, i: (b, jnp.maximum(i * (tt // 8) - 1, 0), 0)),
                  pl.BlockSpec((1, 8, CONV_COLS), lambda b, i: (b, jnp.minimum((i + 1) * (tt // 8), nb8 - 1), 0)),
                  pl.BlockSpec((1, tt, d), row),
                  pl.BlockSpec((1, 1, d), per_b), pl.BlockSpec((1, 1, d), per_b), pl.BlockSpec((1, 1, d), per_b),
                  pl.BlockSpec((1, d), const2), pl.BlockSpec((1, D_RWKV), const2), pl.BlockSpec((1, D_RWKV), const2),
                  pl.BlockSpec((3, D_CONV), const2), pl.BlockSpec((D_RWKV + D_CONV, d), const2),
                  pl.BlockSpec((ne, d), const2), pl.BlockSpec((ne, d), const2),
                  pl.BlockSpec((PAIR, PAIR), const2)],
        out_specs=[pl.BlockSpec((1, tt, d), row), pl.BlockSpec((1, tt, d), row),
                   pl.BlockSpec((1, ne, tt), lambda b, i: (b, 0, i))],
        out_shape=[jax.ShapeDtypeStruct((bsz, t, d), F32), jax.ShapeDtypeStruct((bsz, t, d), BF16),
                   jax.ShapeDtypeStruct((bsz, ne, t), F32)],
        compiler_params=_cparams(("parallel", "parallel")),
        name="out",
    )(yf, yb, bonus, g, p_cv, p_cv, p_cv, x, g1, sh2, sc2, n2g, gng, gnb, convw, wout, rwh, rwl, avg)


def _prefix_blocks(mask_fn, t, tri, emit):
    carry = None
    for j in range(t // LANES):
        m = mask_fn(j)
        inc = _dot(m.astype(BF16), tri)
        carry = jnp.zeros_like(inc[:, 0:1]) if carry is None else carry
        emit(j, m, inc - m + carry)
        carry = carry + inc[:, LANES - 1:LANES]


def _topk_kernel(cap, sb, aff_ref, tri_ref, cnt_ref, slot_ref, edge_ref):
    t = aff_ref.shape[2]
    aff = aff_ref[0]

    def body(k, bits):
        cand = bits | jnp.left_shift(jnp.int32(1), 30 - k)
        cnt = jnp.sum(jnp.where(aff >= pltpu.bitcast(cand, F32), 1, 0), axis=-1, keepdims=True)
        return jnp.where(cnt >= cap, cand, bits)

    bits = lax.fori_loop(0, 31, body, jnp.zeros((aff.shape[0], 1), jnp.int32))
    thr = pltpu.bitcast(bits, F32)
    above = pltpu.bitcast(bits + 1, F32)
    n_gt = jnp.sum(jnp.where(aff >= above, 1, 0), axis=-1, keepdims=True)
    need = (cap - n_gt).astype(F32)
    tri = tri_ref[...]

    def blk(j):
        return aff[:, j * LANES:(j + 1) * LANES]

    def emit_sel(j, eq, before):
        take = (blk(j) >= above) | ((eq > 0.5) & (before < need))
        slot_ref[0, :, j * LANES:(j + 1) * LANES] = jnp.where(take, 1, 0)

    _prefix_blocks(lambda j: jnp.where((blk(j) >= thr) & (blk(j) < above), 1.0, 0.0), t, tri, emit_sel)

    def emit_slot(j, m, before):
        count = before.astype(jnp.int32)
        cnt_ref[0, :, j * LANES:(j + 1) * LANES] = count
        slot_ref[0, :, j * LANES:(j + 1) * LANES] = jnp.where(m > 0.5, count, -1)

    _prefix_blocks(lambda j: slot_ref[0, :, j * LANES:(j + 1) * LANES].astype(F32), t, tri, emit_slot)

    cnt = cnt_ref[0]
    lane = lax.broadcasted_iota(jnp.int32, (cnt.shape[0], LANES), 1)
    edges = jnp.zeros((cnt.shape[0], LANES), jnp.int32)
    for s in range(1, cap // sb + 1):
        below = jnp.sum(jnp.where(cnt < s * sb, 1, 0), axis=-1, keepdims=True)
        edges = jnp.where(lane == s, below, edges)
    edge_ref[0] = edges


def _topk(aff_t, tri, cap, sb):
    bsz, ne, t = aff_t.shape
    spec = pl.BlockSpec((1, ne, t), lambda b: (b, 0, 0))
    return pl.pallas_call(
        functools.partial(_topk_kernel, cap, sb),
        grid=(bsz,),
        in_specs=[spec, pl.BlockSpec((LANES, LANES), lambda b: (0, 0))],
        out_specs=[spec, spec, pl.BlockSpec((1, ne, LANES), lambda b: (b, 0, 0))],
        out_shape=[jax.ShapeDtypeStruct((bsz, ne, t), jnp.int32)] * 2 + [jax.ShapeDtypeStruct((bsz, ne, LANES), jnp.int32)],
        compiler_params=_cparams(("parallel",)),
        name="topk",
    )(aff_t, tri)


TOKEN_ROW = LANES


def _moe_kernel(win, sb, edge_ref, hx_ref, slot_ref, wg_ref, wu_ref, wd_ref, ye_ref, xs_ref):
    b = pl.program_id(0)
    e = pl.program_id(1)
    ne = pl.num_programs(1)
    cap = xs_ref.shape[0]
    nblk = cap // sb
    nrow = hx_ref.shape[1] // TOKEN_ROW
    wrows = win // TOKEN_ROW
    base = (b * ne + e) * (nblk + 1)
    for s in range(nblk):
        blk = slice(s * sb, (s + 1) * sb)
        target = lax.broadcasted_iota(jnp.int32, (sb, TOKEN_ROW), 0) + s * sb
        r0 = jnp.minimum(edge_ref[base + s] // TOKEN_ROW, nrow - wrows)
        rows = slot_ref[0, 0, pl.ds(r0, wrows), :]
        onehot = jnp.concatenate([jnp.where(rows[k:k + 1, :] == target, 1.0, 0.0).astype(BF16)
                                  for k in range(wrows)], axis=1)
        t0 = pl.multiple_of(r0 * TOKEN_ROW, TOKEN_ROW)
        xs_ref[blk, :] = _dot(onehot, hx_ref[0, pl.ds(t0, win), :])

        def extra_row(r, carry, blk=blk, target=target):
            hit = jnp.where(slot_ref[0, 0, pl.ds(r, 1), :] == target, 1.0, 0.0).astype(BF16)
            tr = pl.multiple_of(r * TOKEN_ROW, TOKEN_ROW)
            xs_ref[blk, :] += _dot(hit, hx_ref[0, pl.ds(tr, TOKEN_ROW), :])
            return carry

        r_end = (edge_ref[base + s + 1] + TOKEN_ROW - 1) // TOKEN_ROW
        lax.fori_loop(r0 + wrows, r_end, extra_row, 0)

    xs = xs_ref[...].astype(BF16)
    h1 = _dot(xs, wg_ref[0].astype(BF16))
    h2 = _dot(xs, wu_ref[0].astype(BF16))
    hid = (h1 * jax.nn.sigmoid(h1) * h2).astype(BF16)
    ye_ref[0, 0] = _dot(hid, wd_ref[0].astype(BF16)).astype(BF16)


def _moe(edges, hx, slot4, wg, wu, wd, cap, win, sb):
    bsz, t, d = hx.shape
    ne, _, f = wg.shape
    nrow = t // TOKEN_ROW
    grid_spec = pltpu.PrefetchScalarGridSpec(
        num_scalar_prefetch=1,
        grid=(bsz, ne),
        in_specs=[pl.BlockSpec((1, t, d), lambda b, e, s: (b, 0, 0), pipeline_mode=pl.Buffered(1)),
                  pl.BlockSpec((1, 1, nrow, TOKEN_ROW), lambda b, e, s: (b, e, 0, 0)),
                  pl.BlockSpec((1, d, f), lambda b, e, s: (e, 0, 0)),
                  pl.BlockSpec((1, d, f), lambda b, e, s: (e, 0, 0)),
                  pl.BlockSpec((1, f, d), lambda b, e, s: (e, 0, 0))],
        out_specs=pl.BlockSpec((1, 1, cap, d), lambda b, e, s: (b, e, 0, 0)),
        scratch_shapes=[pltpu.VMEM((cap, d), F32)],
    )
    return pl.pallas_call(
        functools.partial(_moe_kernel, win, sb),
        grid_spec=grid_spec,
        out_shape=jax.ShapeDtypeStruct((bsz, ne, cap, d), BF16),
        compiler_params=_cparams(("parallel", "arbitrary")),
        name="moe",
    )(edges, hx, slot4, wg, wu, wd)


COMB_ROWS = 128
COMB_WIN = 64
SLOT_ALIGN = 16


def _comb_rows(win, groups, xm_ref, ye_ref, slotc, affc, g2_ref, fg_ref, o_ref):
    ne, cap = ye_ref.shape[1], ye_ref.shape[2]
    lane_slot = lax.broadcasted_iota(jnp.int32, (COMB_ROWS, win), 1)
    lhs, rhs = [], []
    for h, first in groups:
        rows = slice(h * COMB_ROWS, (h + 1) * COMB_ROWS)
        hi, lo, ywin = [], [], []
        for e in range(ne):
            start = pl.multiple_of(jnp.minimum(first[e] & ~(SLOT_ALIGN - 1), cap - win), SLOT_ALIGN)
            hit = slotc[rows, e:e + 1] - start == lane_slot
            val = affc[rows, e:e + 1]
            v_hi = val.astype(BF16).astype(F32)
            hi.append(jnp.where(hit, v_hi, 0.0).astype(BF16))
            lo.append(jnp.where(hit, val - v_hi, 0.0).astype(BF16))
            ywin.append(ye_ref[0, e, pl.ds(start, win), :])
        lhs.append(jnp.concatenate([jnp.concatenate(hi, axis=1), jnp.concatenate(lo, axis=1)], axis=0))
        rhs.append(jnp.concatenate(ywin, axis=0))
    both = [_dot(a, y) for a, y in zip(lhs, rhs)]
    for (h, _), bt in zip(groups, both):
        rows = slice(h * COMB_ROWS, (h + 1) * COMB_ROWS)
        x = xm_ref[0, rows, :] + g2_ref[0] * (bt[:COMB_ROWS] + bt[COMB_ROWS:])
        ms = jnp.mean(x * x, axis=-1, keepdims=True)
        o_ref[0, rows, :] = x * lax.rsqrt(ms + NORM_EPS) * fg_ref[...]


def _comb_kernel(tk, win, full, tsp_ref, xm_ref, ye_ref, slotc_ref, affc_ref, g2_ref, fg_ref, o_ref):
    b = pl.program_id(0)
    j = pl.program_id(1)
    nh = tk // COMB_ROWS
    ntile = pl.num_programs(1) * nh + 1
    ne, cap = ye_ref.shape[1], ye_ref.shape[2]
    slotc = slotc_ref[0]
    affc = affc_ref[0]
    refs = (xm_ref, ye_ref, slotc, affc, g2_ref, fg_ref, o_ref)
    groups = [(h, [tsp_ref[(b * ne + e) * ntile + j * nh + h] for e in range(ne)]) for h in range(nh)]
    _comb_rows(win, groups, *refs)
    if full != win:
        for h, first in groups:
            over = None
            for e in range(ne):
                start = jnp.minimum(first[e] & ~(SLOT_ALIGN - 1), cap - win)
                miss = tsp_ref[(b * ne + e) * ntile + j * nh + h + 1] > start + win
                over = miss if over is None else over | miss

            @pl.when(over)
            def _(h=h, first=first):
                _comb_rows(full, [(h, first)], *refs)


def _comb(tsp, xm, ye, slotc, affc, g2, fg, tk):
    bsz, t, d = xm.shape
    ne, cap = ye.shape[1], ye.shape[2]
    full = min(2 * COMB_ROWS, cap)
    assert full == cap or full >= COMB_ROWS + SLOT_ALIGN
    win = min(COMB_WIN, full)
    grid_spec = pltpu.PrefetchScalarGridSpec(
        num_scalar_prefetch=1,
        grid=(bsz, t // tk),
        in_specs=[pl.BlockSpec((1, tk, d), lambda b, j, s: (b, j, 0)),
                  pl.BlockSpec((1, ne, cap, d), lambda b, j, s: (b, 0, 0, 0)),
                  pl.BlockSpec((1, tk, ne), lambda b, j, s: (b, j, 0)),
                  pl.BlockSpec((1, tk, ne), lambda b, j, s: (b, j, 0)),
                  pl.BlockSpec((1, 1, d), lambda b, j, s: (b, 0, 0)),
                  pl.BlockSpec((1, d), lambda b, j, s: (0, 0))],
        out_specs=pl.BlockSpec((1, tk, d), lambda b, j, s: (b, j, 0)),
    )
    return pl.pallas_call(
        functools.partial(_comb_kernel, tk, win, full),
        grid_spec=grid_spec,
        out_shape=jax.ShapeDtypeStruct((bsz, t, d), F32),
        compiler_params=_cparams(("parallel", "arbitrary")),
        name="comb",
    )(tsp, xm, ye, slotc, affc, g2, fg)


def _block_diag_ones(n, blk, value=1.0):
    r = jnp.arange(n)
    return jnp.where((r[:, None] // blk) == (r[None, :] // blk), value, 0.0)


def kernel(x, c, ctx, c_ctx, ada_w, ada_b, norm1_g, norm2_g, w_in, shift_mu, w0, w_lora_up, a0, a_lora_up, k_k, k_a,
           r_k, g_lora_up, gn_g, gn_b, conv_w, w_out, router_w, exp_w_gate, exp_w_up, exp_w_down, final_g):
    bsz, t, d = x.shape
    lc = ctx.shape[1]
    ne = router_w.shape[-1]
    cap = EC_CAPACITY * t // ne
    sb = min(SLOT_BLOCK, cap)
    win = min(GATHER_WIN, t)
    l = 0

    rows = ((bsz + 1 + 7) // 8) * 8
    cc = jnp.zeros((rows, d), F32).at[:bsz].set(c).at[bsz].set(c_ctx)
    mod = _mod(cc, ada_w[l], ada_b[l][None, :])
    sh1, sc1, g1, sh2, sc2, g2 = (m[:, None, :] for m in jnp.split(mod[:bsz], 6, axis=-1))
    csh1, csc1 = (jnp.broadcast_to(m[None, None, :], (bsz, 1, d)) for m in jnp.split(mod[bsz], 6)[:2])

    w_rw = w_in[l][:, :RWKV_COLS].astype(BF16)
    w_cv = w_in[l][:, RWKV_COLS:].astype(BF16)
    n1g = norm1_g[l][None, :]
    px_rw, px_cv = _in_proj(x, sh1, sc1, n1g, w_rw, w_cv, IN_PROJ_ROWS)
    pc_rw, _ = _in_proj(ctx, csh1, csc1, n1g, w_rw, w_cv, min(IN_PROJ_ROWS, lc))

    zw = jnp.zeros((LORA_W, 2 * D_RWKV), F32)
    lora = jnp.concatenate([
        jnp.concatenate([w_lora_up[l, 0], w_lora_up[l, 1], zw], axis=1),
        jnp.concatenate([zw, a_lora_up[l, 0], a_lora_up[l, 1]], axis=1)], axis=0).astype(BF16)
    ridx = jnp.arange(CHUNK)
    tril = jnp.where(ridx[None, :] <= ridx[:, None], 1.0, 0.0).astype(BF16)
    triu = jnp.where(ridx[None, :] >= ridx[:, None], 1.0, 0.0).astype(BF16)
    ones_bd = _block_diag_ones(PAIR, HEAD).astype(BF16)
    consts = (shift_mu[l][None, :], lora, w0[l], a0[l], k_k[l][None, :], k_a[l][None, :],
              r_k[l].reshape(1, D_RWKV), g_lora_up[l].astype(BF16), ones_bd, tril, triu)

    chunks_c = _prep(pc_rw, consts, False, lc)[:5]
    *chunks_x, bonus, gate = _prep(px_rw, consts, True, PREP_ROWS)

    s_zero = jnp.zeros((bsz, 2, N_PAIR, HEAD, PAIR), F32)
    _, _, s_ctx = _scan(*chunks_c, s_zero)
    yf, yb, _ = _scan(*chunks_x, s_ctx)

    rw_t = router_w[l].T
    rwh = rw_t.astype(BF16)
    rwl = (rw_t - rwh.astype(F32)).astype(BF16)
    avg = _block_diag_ones(PAIR, HEAD, 1.0 / HEAD).astype(BF16)
    xm, hx, aff_t = _out(yf, yb, bonus, gate, px_cv, x, g1, sh2, sc2, norm2_g[l][None, :], gn_g[l][None, :],
                         gn_b[l][None, :], conv_w[l], w_out[l].astype(BF16), rwh, rwl, avg, OUT_TILE)

    lane_idx = jnp.arange(TOKEN_ROW)
    tri_lanes = jnp.where(lane_idx[:, None] <= lane_idx[None, :], 1.0, 0.0).astype(BF16)
    cnt, slot, edges = _topk(aff_t, tri_lanes, cap, sb)

    ye = _moe(edges[:, :, :cap // sb + 1].reshape(-1), hx, slot.reshape(bsz, ne, t // TOKEN_ROW, TOKEN_ROW),
              exp_w_gate[l], exp_w_up[l], exp_w_down[l], cap, win, sb)
    tr = lambda a: jnp.transpose(a, (0, 2, 1))
    first_slot = jnp.concatenate([cnt[:, :, ::COMB_ROWS], jnp.full((bsz, ne, 1), cap, jnp.int32)], axis=-1).reshape(-1)
    return _comb(first_slot, xm, ye, tr(slot), tr(aff_t), g2, final_g[None, :], COMB_TILE)
```

```python
import functools
import math

import jax
import jax.numpy as jnp
from jax import lax
from jax.experimental import pallas as pl
from jax.experimental.pallas import tpu as pltpu

F32 = jnp.float32
BF16 = jnp.bfloat16
HIGHEST = lax.Precision.HIGHEST

GRID_W = 64
D_RWKV = 512
D_CONV = 512
HEAD = 64
LORA_W = 64
LORA_A = 64
LORA_G = 128
N_EXPERTS = 16
EC_CAPACITY = 2
NORM_EPS = 1e-6
GN_EPS = 64e-5
RWKV_COLS = 3 * D_RWKV + LORA_W + LORA_A + LORA_G
CONV_COLS = 3 * D_CONV

LANES = 128
CHUNK = 64
PAIR = 2 * HEAD
N_PAIR = D_RWKV // PAIR
VMEM_LIMIT = 48 * 1024 * 1024

IN_PROJ_ROWS = 512
PREP_ROWS = 256
OUT_TILE = 512
COMB_TILE = 256
SLOT_BLOCK = 128
GATHER_WIN = 1536


def _cparams(sem):
    return pltpu.CompilerParams(dimension_semantics=sem, vmem_limit_bytes=VMEM_LIMIT)


def _dot(a, b):
    return jnp.dot(a, b, preferred_element_type=F32)


def _dot_nt(a, b):
    return lax.dot_general(a, b, (((1,), (1,)), ((), ())), preferred_element_type=F32)


def _split2(x):
    hi = x.astype(BF16)
    lo = (x - hi.astype(F32)).astype(BF16)
    return hi, lo


def _split3(x):
    hi = x.astype(BF16)
    r = x - hi.astype(F32)
    mid = r.astype(BF16)
    lo = (r - mid.astype(F32)).astype(BF16)
    return hi, mid, lo


def _seg_dot(x, m):
    hi, lo = _split2(x)
    return _dot(hi, m) + _dot(lo, m)


def _mod_kernel(c_ref, w_ref, b_ref, o_ref):
    c = c_ref[...]
    s = c * jax.nn.sigmoid(c)
    o_ref[...] = jnp.dot(s, w_ref[...], precision=HIGHEST, preferred_element_type=F32) + b_ref[...]


def _mod(cc, w, b):
    rows, d = cc.shape
    n = w.shape[1]
    tn = 1024
    return pl.pallas_call(
        _mod_kernel,
        grid=(n // tn,),
        in_specs=[pl.BlockSpec((rows, d), lambda j: (0, 0)),
                  pl.BlockSpec((d, tn), lambda j: (0, j)),
                  pl.BlockSpec((1, tn), lambda j: (0, j))],
        out_specs=pl.BlockSpec((rows, tn), lambda j: (0, j)),
        out_shape=jax.ShapeDtypeStruct((rows, n), F32),
        compiler_params=_cparams(("parallel",)),
        name="mod",
    )(cc, w, b)


def _in_proj_kernel(x_ref, sh_ref, sc_ref, g_ref, wrw_ref, wcv_ref, orw_ref, ocv_ref):
    x = x_ref[0]
    ms = jnp.mean(x * x, axis=-1, keepdims=True)
    h = x * lax.rsqrt(ms + NORM_EPS) * g_ref[...]
    h = (h * (1.0 + sc_ref[0]) + sh_ref[0]).astype(BF16)
    orw_ref[0] = _dot(h, wrw_ref[...])
    ocv_ref[0] = _dot(h, wcv_ref[...])


def _in_proj(x, sh, sc, g, w_rw, w_cv, tm):
    bsz, l, d = x.shape
    return pl.pallas_call(
        _in_proj_kernel,
        grid=(bsz, l // tm),
        in_specs=[pl.BlockSpec((1, tm, d), lambda b, i: (b, i, 0)),
                  pl.BlockSpec((1, 1, d), lambda b, i: (b, 0, 0)),
                  pl.BlockSpec((1, 1, d), lambda b, i: (b, 0, 0)),
                  pl.BlockSpec((1, d), lambda b, i: (0, 0)),
                  pl.BlockSpec((d, RWKV_COLS), lambda b, i: (0, 0)),
                  pl.BlockSpec((d, CONV_COLS), lambda b, i: (0, 0))],
        out_specs=[pl.BlockSpec((1, tm, RWKV_COLS), lambda b, i: (b, i, 0)),
                   pl.BlockSpec((1, tm, CONV_COLS), lambda b, i: (b, i, 0))],
        out_shape=[jax.ShapeDtypeStruct((bsz, l, RWKV_COLS), F32),
                   jax.ShapeDtypeStruct((bsz, l, CONV_COLS), F32)],
        compiler_params=_cparams(("parallel", "parallel")),
        name="in_proj",
    )(x, sh, sc, g, w_rw, w_cv)


def _bf(x):
    return x.astype(BF16)


def _chunk_local(chains):
    c = CHUNK
    n = range(len(chains))
    lane = lax.broadcasted_iota(jnp.int32, (c, PAIR), 1)
    h0 = lane < HEAD
    tcol = lane & (c - 1)
    trow = lax.broadcasted_iota(jnp.int32, (c, PAIR), 0)
    eye = jnp.where(tcol == trow, 1.0, 0.0)
    masks = {rev: ((tcol > trow) if rev else (tcol < trow), (tcol >= trow) if rev else (tcol <= trow))
             for rev in (False, True)}
    zb = jnp.zeros((c, PAIR), BF16)

    def stack(x):
        return jnp.concatenate([jnp.where(h0, x, zb), jnp.where(h0, zb, x)], axis=0)

    gram = [_dot_nt(jnp.concatenate([ch[1], ch[2]], axis=0), jnp.concatenate([stack(ch[3]), stack(ch[4])], axis=0))
            for ch in chains]
    lab = [jnp.where(masks[chains[i][0]][0], gram[i][:c, :PAIR], 0.0) for i in n]
    lak = [jnp.where(masks[chains[i][0]][0], gram[i][:c, PAIR:], 0.0) for i in n]
    mrb = [jnp.where(masks[chains[i][0]][1], gram[i][c:, :PAIR], 0.0) for i in n]
    mrk = [jnp.where(masks[chains[i][0]][1], gram[i][c:, PAIR:], 0.0) for i in n]

    labb = [_bf(x) for x in lab]
    pw = [_dot(labb[i], stack(labb[i])) for i in n]
    tp = [eye + lab[i] for i in n]
    for _ in range(4):
        pwb = [_bf(x) for x in pw]
        both = [_dot(pwb[i], jnp.concatenate([stack(pwb[i]), stack(_bf(tp[i]))], axis=1)) for i in n]
        pw = [x[:, :PAIR] for x in both]
        tp = [tp[i] + both[i][:, PAIR:] for i in n]
    tinv = [tp[i] + _dot(_bf(pw[i]), stack(_bf(tp[i]))) for i in n]

    lmv = [_dot(_bf(jnp.concatenate([lak[i], mrk[i]], axis=0)), stack(chains[i][7])) for i in n]
    x = [_dot(_bf(tinv[i]), jnp.concatenate([stack(chains[i][1]), stack(_bf(lmv[i][:c]))], axis=1))
         for i in n]
    z = [_dot(_bf(mrb[i]), jnp.concatenate([stack(_bf(x[i][:, :PAIR])), stack(_bf(x[i][:, PAIR:]))], axis=1))
         for i in n]
    rbar = [_bf(chains[i][2].astype(F32) + z[i][:, :PAIR]) for i in n]
    y0 = [z[i][:, PAIR:] + lmv[i][c:] for i in n]

    uv = [jnp.concatenate([x[i][:, PAIR:], chains[i][7].astype(F32)], axis=0) for i in n]
    mxf = [_dot(_bf(x[i][:, :PAIR].T), chains[i][5]) for i in n]
    nnf = [_dot(_bf(uv[i].T), jnp.concatenate([chains[i][5], chains[i][6]], axis=0)) for i in n]
    mx = [_bf(jnp.where(h0, m[:c], m[c:])) for m in mxf]
    nn = [jnp.where(h0, m[:c], m[c:]) for m in nnf]
    return list(zip(rbar, y0, mx, nn))


DIR_COLS = 6 * D_RWKV
SCR_COLS = 2 * DIR_COLS + D_RWKV


def _head_sums(x, ones):
    return jnp.concatenate([_seg_dot(x[:, g * PAIR:(g + 1) * PAIR], ones) for g in range(N_PAIR)], axis=1)


def _prep_rows(c, p, shifted, lo, mu_ref, w0_ref, a0_ref, kk_ref, ka_ref, rk_ref, ones_ref, tril_ref, triu_ref,
               wc_ref, bonus_ref, scr_ref):
    rows = slice(c * CHUNK, (c + 1) * CHUNK)
    pm = p + mu_ref[:, 0:3 * D_RWKV] * (shifted - p)
    xr = pm[:, 0:D_RWKV]
    xk = pm[:, D_RWKV:2 * D_RWKV]
    xv = pm[:, 2 * D_RWKV:3 * D_RWKV]

    ones = ones_ref[...]
    kraw = xk * kk_ref[...]
    kk = kraw / jnp.maximum(jnp.sqrt(_head_sums(kraw * kraw, ones)), 1e-12)

    scr_ref[:, 2 * DIR_COLS:] = xv.astype(BF16)
    ksum = None
    for d in range(2):
        z = w0_ref[d:d + 1, :] + lo[:, d * D_RWKV:(d + 1) * D_RWKV]
        ld = -math.exp(-0.5) * jax.nn.sigmoid(z)
        ag = jax.nn.sigmoid(a0_ref[d:d + 1, :] + lo[:, (2 + d) * D_RWKV:(3 + d) * D_RWKV])
        kd = xk * (1.0 + (ag - 1.0) * ka_ref[...])
        bb = kk * ag
        ksum = kd if ksum is None else ksum + kd
        tri = tril_ref[...] if d == 0 else triu_ref[...]
        h3 = _split3(ld)
        cs = _dot(tri, h3[0]) + _dot(tri, h3[1]) + _dot(tri, h3[2])
        e_in = jnp.exp(cs)
        e_ex = jnp.exp(cs - ld)
        e_neg = jnp.exp(-cs)
        bt = bb * e_neg
        kt = kd * e_neg
        edge = CHUNK - 1 if d == 0 else 0
        wrow = e_in[edge:edge + 1, :]
        wc_ref[0, c, :, d * D_RWKV:(d + 1) * D_RWKV] = wrow
        base = d * DIR_COLS
        for j, val in enumerate((-kk * e_ex, xr * e_in, bt, kt, bt * wrow, kt * wrow)):
            scr_ref[:, base + j * D_RWKV:base + (j + 1) * D_RWKV] = val.astype(BF16)

    bonus_ref[0, rows, :] = _head_sums(xr * ksum * rk_ref[...], ones) * xv


PREP_GROUP = 2


def _prep_chunks(cs, scr_refs, rbar_ref, y0_ref, mx_ref, nn_ref):
    chains = []
    for scr_ref in scr_refs:
        for d in range(2):
            for pp in range(N_PAIR):
                cols = [d * DIR_COLS + j * D_RWKV + pp * PAIR for j in range(6)] + [2 * DIR_COLS + pp * PAIR]
                chains.append((d == 1,) + tuple(scr_ref[:, k:k + PAIR] for k in cols))
    per_chunk = 2 * N_PAIR
    for idx, (rbar, y0, mx, nn) in enumerate(_chunk_local(chains)):
        c = cs[idx // per_chunk]
        rows = slice(c * CHUNK, (c + 1) * CHUNK)
        col = (idx % per_chunk // N_PAIR) * D_RWKV + (idx % N_PAIR) * PAIR
        rbar_ref[0, rows, col:col + PAIR] = rbar
        y0_ref[0, rows, col:col + PAIR] = y0
        mx_ref[0, c, :, col:col + PAIR] = mx
        nn_ref[0, c, :, col:col + PAIR] = nn


N_PREP_CONSTS = 11


def _prep_kernel(latent, tt, p_ref, *rest):
    if latent:
        prev_ref, next_ref = rest[:2]
        rest = rest[2:]
    (mu_ref, lw_ref, w0_ref, a0_ref, kk_ref, ka_ref, rk_ref, gup_ref, ones_ref, tril_ref,
     triu_ref) = rest[:N_PREP_CONSTS]
    rbar_ref, y0_ref, mx_ref, nn_ref, wc_ref, bonus_ref, g_ref = rest[N_PREP_CONSTS:N_PREP_CONSTS + 7]
    *scr, lo_ref, ext_ref = rest[N_PREP_CONSTS + 7:]
    if latent:
        i = pl.program_id(1)
        n = pl.num_programs(1)
        ext_ref[0:GRID_W] = jnp.where(i > 0, prev_ref[0], 0.0)
        ext_ref[GRID_W:GRID_W + tt] = p_ref[0]
        ext_ref[GRID_W + tt:] = jnp.where(i < n - 1, next_ref[0], 0.0)

        def shifted(r0, nrows, cols):
            shape = (nrows, cols.stop - cols.start)
            trow = lax.broadcasted_iota(jnp.int32, shape, 0) & (GRID_W - 1)
            q = lax.broadcasted_iota(jnp.int32, shape, 1) & 3
            left = jnp.where(trow == 0, 0.0, ext_ref[GRID_W - 1 + r0:GRID_W - 1 + r0 + nrows, cols])
            right = jnp.where(trow == GRID_W - 1, 0.0, ext_ref[GRID_W + 1 + r0:GRID_W + 1 + r0 + nrows, cols])
            up = ext_ref[r0:r0 + nrows, cols]
            down = ext_ref[2 * GRID_W + r0:2 * GRID_W + r0 + nrows, cols]
            return jnp.where(q == 0, left, jnp.where(q == 1, right, jnp.where(q == 2, up, down)))
    else:
        ext_ref[0:8] = jnp.zeros((8, RWKV_COLS), F32)
        ext_ref[8:8 + tt] = p_ref[0]
        ext_ref[8 + tt:] = jnp.zeros((8, RWKV_COLS), F32)

        def shifted(r0, nrows, cols):
            q = lax.broadcasted_iota(jnp.int32, (nrows, cols.stop - cols.start), 1) & 1
            return jnp.where(q == 0, ext_ref[7 + r0:7 + r0 + nrows, cols], ext_ref[9 + r0:9 + r0 + nrows, cols])

    tail = slice(3 * D_RWKV, RWKV_COLS)
    p_t = p_ref[0, :, tail]
    pm_t = p_t + mu_ref[:, tail] * (shifted(0, tt, tail) - p_t)
    xwa = pm_t[:, 0:LORA_W + LORA_A]
    lane = lax.broadcasted_iota(jnp.int32, xwa.shape, 1)
    lin = jnp.where(lane < LORA_W, jnp.tanh(xwa), xwa).astype(BF16)
    lo_ref[...] = _dot(lin, lw_ref[...])
    g_ref[0] = _dot(jax.nn.sigmoid(pm_t[:, LORA_W + LORA_A:]).astype(BF16), gup_ref[...])

    head = slice(0, 3 * D_RWKV)

    def rows_part(c):
        rows = slice(c * CHUNK, (c + 1) * CHUNK)
        _prep_rows(c, p_ref[0, rows, head], shifted(c * CHUNK, CHUNK, head), lo_ref[rows, :], mu_ref, w0_ref,
                   a0_ref, kk_ref, ka_ref, rk_ref, ones_ref, tril_ref, triu_ref, wc_ref, bonus_ref,
                   scr[c % (2 * PREP_GROUP)])

    groups = [list(range(g, g + PREP_GROUP)) for g in range(0, tt // CHUNK, PREP_GROUP)]
    for c in groups[0]:
        rows_part(c)
    for k, cs in enumerate(groups):
        if k + 1 < len(groups):
            for c in groups[k + 1]:
                rows_part(c)
        _prep_chunks(cs, [scr[c % (2 * PREP_GROUP)] for c in cs], rbar_ref, y0_ref, mx_ref, nn_ref)


def _prep(p_rw, consts, latent, tt):
    bsz, l, _ = p_rw.shape
    nc = l // CHUNK
    cpt = tt // CHUNK
    assert len(consts) == N_PREP_CONSTS
    const_specs = [pl.BlockSpec(c.shape, lambda b, i, nd=c.ndim: (0,) * nd) for c in consts]
    kern = functools.partial(_prep_kernel, latent, tt)
    if latent:
        nblk = l // GRID_W
        in_specs = [pl.BlockSpec((1, tt, RWKV_COLS), lambda b, i: (b, i, 0)),
                    pl.BlockSpec((1, GRID_W, RWKV_COLS),
                                 lambda b, i: (b, jnp.maximum(i * (tt // GRID_W) - 1, 0), 0)),
                    pl.BlockSpec((1, GRID_W, RWKV_COLS),
                                 lambda b, i: (b, jnp.minimum((i + 1) * (tt // GRID_W), nblk - 1), 0))]
        args = (p_rw, p_rw, p_rw)
        ext_rows = tt + 2 * GRID_W
    else:
        assert tt == l
        in_specs = [pl.BlockSpec((1, tt, RWKV_COLS), lambda b, i: (b, i, 0))]
        args = (p_rw,)
        ext_rows = tt + 16
    row = lambda b, i: (b, i, 0)
    chunk = lambda b, i: (b, i, 0, 0)
    w2 = 2 * D_RWKV
    return pl.pallas_call(
        kern,
        grid=(bsz, l // tt),
        in_specs=in_specs + const_specs,
        out_specs=[pl.BlockSpec((1, tt, w2), row), pl.BlockSpec((1, tt, w2), row),
                   pl.BlockSpec((1, cpt, CHUNK, w2), chunk), pl.BlockSpec((1, cpt, CHUNK, w2), chunk),
                   pl.BlockSpec((1, cpt, 1, w2), chunk),
                   pl.BlockSpec((1, tt, D_RWKV), row), pl.BlockSpec((1, tt, D_RWKV), row)],
        out_shape=[jax.ShapeDtypeStruct((bsz, l, w2), BF16), jax.ShapeDtypeStruct((bsz, l, w2), F32),
                   jax.ShapeDtypeStruct((bsz, nc, CHUNK, w2), BF16), jax.ShapeDtypeStruct((bsz, nc, CHUNK, w2), F32),
                   jax.ShapeDtypeStruct((bsz, nc, 1, w2), F32),
                   jax.ShapeDtypeStruct((bsz, l, D_RWKV), F32), jax.ShapeDtypeStruct((bsz, l, D_RWKV), F32)],
        scratch_shapes=[pltpu.VMEM((CHUNK, SCR_COLS), BF16)] * (2 * PREP_GROUP)
        + [pltpu.VMEM((tt, 4 * D_RWKV), F32), pltpu.VMEM((ext_rows, RWKV_COLS), F32)],
        compiler_params=_cparams(("parallel", "parallel")),
        name="prep_latent" if latent else "prep_ctx",
    )(*args, *consts)


SCAN_CHUNKS = 4
SCAN_BATCH = 4


def _scan_kernel(rbf_ref, rbb_ref, y0f_ref, y0b_ref, mxf_ref, mxb_ref, nnf_ref, nnb_ref, wcf_ref, wcb_ref,
                 s0_ref, yf_ref, yb_ref, s_ref):
    i = pl.program_id(1)

    @pl.when(i == 0)
    def _():
        s_ref[...] = s0_ref[...]

    c = CHUNK
    h0 = lax.broadcasted_iota(jnp.int32, (c, PAIR), 1) < HEAD
    zb = jnp.zeros((c, PAIR), BF16)

    def stack(x):
        return jnp.concatenate([jnp.where(h0, x, zb), jnp.where(h0, zb, x)], axis=0)

    dirs = ((rbf_ref, y0f_ref, mxf_ref, nnf_ref, wcf_ref, yf_ref), (rbb_ref, y0b_ref, mxb_ref, nnb_ref, wcb_ref, yb_ref))
    idx = [(n, d, p) for n in range(SCAN_BATCH) for d in range(2) for p in range(N_PAIR)]
    lanes = [slice(p * PAIR, (p + 1) * PAIR) for _, _, p in idx]
    s = [s_ref[n, d, p] for n, d, p in idx]
    for q in range(SCAN_CHUNKS):
        cq = (q, SCAN_CHUNKS - 1 - q)
        rows = [slice(cq[d] * c, (cq[d] + 1) * c) for _, d, _ in idx]
        sb = [_bf(x) for x in s]
        y = [_dot_nt(dirs[d][0][n, rows[k], lanes[k]], stack(sb[k])) for k, (n, d, _) in enumerate(idx)]
        sm = [_dot(sb[k], stack(dirs[d][2][n, cq[d], :, lanes[k]])) for k, (n, d, _) in enumerate(idx)]
        s_next = []
        for k, (n, d, p) in enumerate(idx):
            dirs[d][5][n, rows[k], lanes[k]] = y[k] + dirs[d][1][n, rows[k], lanes[k]]
            s_next.append(s[k] * dirs[d][4][n, cq[d], :, lanes[k]] + sm[k] + dirs[d][3][n, cq[d], :, lanes[k]])
        s = s_next
    for k, (n, d, p) in enumerate(idx):
        s_ref[n, d, p] = s[k]


def _scan(rbar, y0, mx, nn, wc, s0):
    bsz, l, _ = rbar.shape
    sub, nb = SCAN_CHUNKS, SCAN_BATCH
    assert bsz % nb == 0 and l % (CHUNK * sub) == 0
    nc = l // (CHUNK * sub)
    row_f = pl.BlockSpec((nb, sub * CHUNK, D_RWKV), lambda b, i: (b, i, 0))
    row_b = pl.BlockSpec((nb, sub * CHUNK, D_RWKV), lambda b, i: (b, nc - 1 - i, 1))
    chk_f = pl.BlockSpec((nb, sub, CHUNK, D_RWKV), lambda b, i: (b, i, 0, 0))
    chk_b = pl.BlockSpec((nb, sub, CHUNK, D_RWKV), lambda b, i: (b, nc - 1 - i, 0, 1))
    wc_f = pl.BlockSpec((nb, sub, 1, D_RWKV), lambda b, i: (b, i, 0, 0))
    wc_b = pl.BlockSpec((nb, sub, 1, D_RWKV), lambda b, i: (b, nc - 1 - i, 0, 1))
    st_spec = pl.BlockSpec((nb, 2, N_PAIR, HEAD, PAIR), lambda b, i: (b, 0, 0, 0, 0))
    return pl.pallas_call(
        _scan_kernel,
        grid=(bsz // nb, nc),
        in_specs=[row_f, row_b, row_f, row_b, chk_f, chk_b, chk_f, chk_b, wc_f, wc_b, st_spec],
        out_specs=[row_f, pl.BlockSpec((nb, sub * CHUNK, D_RWKV), lambda b, i: (b, nc - 1 - i, 0)), st_spec],
        out_shape=[jax.ShapeDtypeStruct((bsz, l, D_RWKV), F32),
                   jax.ShapeDtypeStruct((bsz, l, D_RWKV), F32),
                   jax.ShapeDtypeStruct((bsz, 2, N_PAIR, HEAD, PAIR), F32)],
        compiler_params=_cparams(("parallel", "arbitrary")),
        name="scan",
    )(rbar, rbar, y0, y0, mx, mx, nn, nn, wc, wc, s0)


OUT_ROWS = 128


def _out_kernel(tt, yf_ref, yb_ref, bonus_ref, g_ref, cv_ref, cvp_ref, cvn_ref, x_ref, g1_ref, sh2_ref,
                sc2_ref, n2g_ref, gng_ref, gnb_ref, convw_ref, wout_ref, rwh_ref, rwl_ref, avg_ref,
                xm_ref, hx_ref, aff_ref):
    i = pl.program_id(1)
    n = pl.num_programs(1)

    cv = cv_ref[0]
    b_gate = cv[:, 0:D_CONV]
    cu = cv[:, D_CONV:2 * D_CONV] * cv[:, 2 * D_CONV:]
    cvp = cvp_ref[0]
    cvn = cvn_ref[0]
    cu_prev = jnp.where(i > 0, cvp[7:8, D_CONV:2 * D_CONV] * cvp[7:8, 2 * D_CONV:], 0.0)
    cu_next = jnp.where(i < n - 1, cvn[0:1, D_CONV:2 * D_CONV] * cvn[0:1, 2 * D_CONV:], 0.0)
    ridx = lax.broadcasted_iota(jnp.int32, cu.shape, 0)
    cu_m1 = jnp.where(ridx == 0, cu_prev, pltpu.roll(cu, 1, 0))
    cu_p1 = jnp.where(ridx == tt - 1, cu_next, pltpu.roll(cu, tt - 1, 0))
    conv = convw_ref[0:1, :] * cu_m1 + convw_ref[1:2, :] * cu + convw_ref[2:3, :] * cu_p1
    bx = (b_gate * conv).astype(BF16)

    parts = [slice(k * OUT_ROWS, (k + 1) * OUT_ROWS) for k in range(tt // OUT_ROWS)]
    avg = avg_ref[...]
    rwh = rwh_ref[...]
    rwl = rwl_ref[...]
    y = [yf_ref[0, r, :] + yb_ref[0, r, :] for r in parts]
    mu = [_head_sums(v, avg) for v in y]
    dlt = [a - b for a, b in zip(y, mu)]
    var = [_head_sums(v * v, avg) for v in dlt]
    yn = [a * lax.rsqrt(b + GN_EPS) * gng_ref[...] + gnb_ref[...] for a, b in zip(dlt, var)]
    ax = [((a + bonus_ref[0, r, :]) * g_ref[0, r, :]).astype(BF16) for a, r in zip(yn, parts)]
    mix = [_dot(a, wout_ref[0:D_RWKV, :]) + _dot(bx[r], wout_ref[D_RWKV:, :]) for a, r in zip(ax, parts)]
    xm = [x_ref[0, r, :] + g1_ref[0] * a for a, r in zip(mix, parts)]
    ms = [jnp.mean(v * v, axis=-1, keepdims=True) for v in xm]
    hx = [a * lax.rsqrt(b + NORM_EPS) * n2g_ref[...] for a, b in zip(xm, ms)]
    hx = [_split2(v * (1.0 + sc2_ref[0]) + sh2_ref[0]) for v in hx]
    logits = [_dot_nt(rwh, hi) + _dot_nt(rwh, lo) + _dot_nt(rwl, hi) for hi, lo in hx]
    for k, r in enumerate(parts):
        xm_ref[0, r, :] = xm[k]
        hx_ref[0, r, :] = hx[k][0]
        m = jnp.max(logits[k], axis=0, keepdims=True)
        ex = jnp.exp(logits[k] - m)
        aff_ref[0, :, r] = ex / jnp.sum(ex, axis=0, keepdims=True)


def _out(yf, yb, bonus, g, p_cv, x, g1, sh2, sc2, n2g, gng, gnb, convw, wout, rwh, rwl, avg, tt):
    bsz, t, d = x.shape
    ne = rwh.shape[0]
    nb8 = t // 8
    row = lambda b, i: (b, i, 0)
    per_b = lambda b, i: (b, 0, 0)
    const2 = lambda b, i: (0, 0)
    return pl.pallas_call(
        functools.partial(_out_kernel, tt),
        grid=(bsz, t // tt),
        in_specs=[pl.BlockSpec((1, tt, D_RWKV), row), pl.BlockSpec((1, tt, D_RWKV), row),
                  pl.BlockSpec((1, tt, D_RWKV), row), pl.BlockSpec((1, tt, D_RWKV), row),
                  pl.BlockSpec((1, tt, CONV_COLS), row),
                  pl.BlockSpec((1, 8, CONV_COLS), lambda b, i: (b, jnp.maximum(i * (tt // 8) - 1, 0), 0)),
                  pl.BlockSpec((1, 8, CONV_COLS), lambda b, i: (b, jnp.minimum((i + 1) * (tt // 8), nb8 - 1), 0)),
                  pl.BlockSpec((1, tt, d), row),
                  pl.BlockSpec((1, 1, d), per_b), pl.BlockSpec((1, 1, d), per_b), pl.BlockSpec((1, 1, d), per_b),
                  pl.BlockSpec((1, d), const2), pl.BlockSpec((1, D_RWKV), const2), pl.BlockSpec((1, D_RWKV), const2),
                  pl.BlockSpec((3, D_CONV), const2), pl.BlockSpec((D_RWKV + D_CONV, d), const2),
                  pl.BlockSpec((ne, d), const2), pl.BlockSpec((ne, d), const2),
                  pl.BlockSpec((PAIR, PAIR), const2)],
        out_specs=[pl.BlockSpec((1, tt, d), row), pl.BlockSpec((1, tt, d), row),
                   pl.BlockSpec((1, ne, tt), lambda b, i: (b, 0, i))],
        out_shape=[jax.ShapeDtypeStruct((bsz, t, d), F32), jax.ShapeDtypeStruct((bsz, t, d), BF16),
                   jax.ShapeDtypeStruct((bsz, ne, t), F32)],
        compiler_params=_cparams(("parallel", "parallel")),
        name="out",
    )(yf, yb, bonus, g, p_cv, p_cv, p_cv, x, g1, sh2, sc2, n2g, gng, gnb, convw, wout, rwh, rwl, avg)


def _prefix_blocks(mask_fn, t, tri, emit):
    carry = None
    for j in range(t // LANES):
        m = mask_fn(j)
        inc = _dot(m.astype(BF16), tri)
        carry = jnp.zeros_like(inc[:, 0:1]) if carry is None else carry
        emit(j, m, inc - m + carry)
        carry = carry + inc[:, LANES - 1:LANES]


def _topk_kernel(cap, sb, aff_ref, tri_ref, cnt_ref, slot_ref, edge_ref):
    t = aff_ref.shape[2]
    aff = aff_ref[0]

    def body(k, bits):
        cand = bits | jnp.left_shift(jnp.int32(1), 30 - k)
        cnt = jnp.sum(jnp.where(aff >= pltpu.bitcast(cand, F32), 1, 0), axis=-1, keepdims=True)
        return jnp.where(cnt >= cap, cand, bits)

    bits = lax.fori_loop(0, 31, body, jnp.zeros((aff.shape[0], 1), jnp.int32))
    thr = pltpu.bitcast(bits, F32)
    above = pltpu.bitcast(bits + 1, F32)
    n_gt = jnp.sum(jnp.where(aff >= above, 1, 0), axis=-1, keepdims=True)
    need = (cap - n_gt).astype(F32)
    tri = tri_ref[...]

    def blk(j):
        return aff[:, j * LANES:(j + 1) * LANES]

    def emit_sel(j, eq, before):
        take = (blk(j) >= above) | ((eq > 0.5) & (before < need))
        slot_ref[0, :, j * LANES:(j + 1) * LANES] = jnp.where(take, 1, 0)

    _prefix_blocks(lambda j: jnp.where((blk(j) >= thr) & (blk(j) < above), 1.0, 0.0), t, tri, emit_sel)

    def emit_slot(j, m, before):
        count = before.astype(jnp.int32)
        cnt_ref[0, :, j * LANES:(j + 1) * LANES] = count
        slot_ref[0, :, j * LANES:(j + 1) * LANES] = jnp.where(m > 0.5, count, -1)

    _prefix_blocks(lambda j: slot_ref[0, :, j * LANES:(j + 1) * LANES].astype(F32), t, tri, emit_slot)

    cnt = cnt_ref[0]
    lane = lax.broadcasted_iota(jnp.int32, (cnt.shape[0], LANES), 1)
    edges = jnp.zeros((cnt.shape[0], LANES), jnp.int32)
    for s in range(1, cap // sb + 1):
        below = jnp.sum(jnp.where(cnt < s * sb, 1, 0), axis=-1, keepdims=True)
        edges = jnp.where(lane == s, below, edges)
    edge_ref[0] = edges


def _topk(aff_t, tri, cap, sb):
    bsz, ne, t = aff_t.shape
    spec = pl.BlockSpec((1, ne, t), lambda b: (b, 0, 0))
    return pl.pallas_call(
        functools.partial(_topk_kernel, cap, sb),
        grid=(bsz,),
        in_specs=[spec, pl.BlockSpec((LANES, LANES), lambda b: (0, 0))],
        out_specs=[spec, spec, pl.BlockSpec((1, ne, LANES), lambda b: (b, 0, 0))],
        out_shape=[jax.ShapeDtypeStruct((bsz, ne, t), jnp.int32)] * 2 + [jax.ShapeDtypeStruct((bsz, ne, LANES), jnp.int32)],
        compiler_params=_cparams(("parallel",)),
        name="topk",
    )(aff_t, tri)


TOKEN_ROW = LANES


def _moe_kernel(win, sb, edge_ref, hx_ref, slot_ref, wg_ref, wu_ref, wd_ref, ye_ref, xs_ref):
    b = pl.program_id(0)
    e = pl.program_id(1)
    ne = pl.num_programs(1)
    cap = xs_ref.shape[0]
    nblk = cap // sb
    nrow = hx_ref.shape[1] // TOKEN_ROW
    wrows = win // TOKEN_ROW
    base = (b * ne + e) * (nblk + 1)
    for s in range(nblk):
        blk = slice(s * sb, (s + 1) * sb)
        target = lax.broadcasted_iota(jnp.int32, (sb, TOKEN_ROW), 0) + s * sb
        r0 = jnp.minimum(edge_ref[base + s] // TOKEN_ROW, nrow - wrows)
        rows = slot_ref[0, 0, pl.ds(r0, wrows), :]
        onehot = jnp.concatenate([jnp.where(rows[k:k + 1, :] == target, 1.0, 0.0).astype(BF16)
                                  for k in range(wrows)], axis=1)
        t0 = pl.multiple_of(r0 * TOKEN_ROW, TOKEN_ROW)
        xs_ref[blk, :] = _dot(onehot, hx_ref[0, pl.ds(t0, win), :])

        def extra_row(r, carry, blk=blk, target=target):
            hit = jnp.where(slot_ref[0, 0, pl.ds(r, 1), :] == target, 1.0, 0.0).astype(BF16)
            tr = pl.multiple_of(r * TOKEN_ROW, TOKEN_ROW)
            xs_ref[blk, :] += _dot(hit, hx_ref[0, pl.ds(tr, TOKEN_ROW), :])
            return carry

        r_end = (edge_ref[base + s + 1] + TOKEN_ROW - 1) // TOKEN_ROW
        lax.fori_loop(r0 + wrows, r_end, extra_row, 0)

    xs = xs_ref[...].astype(BF16)
    h1 = _dot(xs, wg_ref[0].astype(BF16))
    h2 = _dot(xs, wu_ref[0].astype(BF16))
    hid = (h1 * jax.nn.sigmoid(h1) * h2).astype(BF16)
    ye_ref[0, 0] = _dot(hid, wd_ref[0].astype(BF16)).astype(BF16)


def _moe(edges, hx, slot4, wg, wu, wd, cap, win, sb):
    bsz, t, d = hx.shape
    ne, _, f = wg.shape
    nrow = t // TOKEN_ROW
    grid_spec = pltpu.PrefetchScalarGridSpec(
        num_scalar_prefetch=1,
        grid=(bsz, ne),
        in_specs=[pl.BlockSpec((1, t, d), lambda b, e, s: (b, 0, 0), pipeline_mode=pl.Buffered(1)),
                  pl.BlockSpec((1, 1, nrow, TOKEN_ROW), lambda b, e, s: (b, e, 0, 0)),
                  pl.BlockSpec((1, d, f), lambda b, e, s: (e, 0, 0)),
                  pl.BlockSpec((1, d, f), lambda b, e, s: (e, 0, 0)),
                  pl.BlockSpec((1, f, d), lambda b, e, s: (e, 0, 0))],
        out_specs=pl.BlockSpec((1, 1, cap, d), lambda b, e, s: (b, e, 0, 0)),
        scratch_shapes=[pltpu.VMEM((cap, d), F32)],
    )
    return pl.pallas_call(
        functools.partial(_moe_kernel, win, sb),
        grid_spec=grid_spec,
        out_shape=jax.ShapeDtypeStruct((bsz, ne, cap, d), BF16),
        compiler_params=_cparams(("parallel", "arbitrary")),
        name="moe",
    )(edges, hx, slot4, wg, wu, wd)


COMB_ROWS = 128
COMB_WIN = 64
SLOT_ALIGN = 16


def _comb_rows(win, groups, xm_ref, ye_ref, slotc, affc, g2_ref, fg_ref, o_ref):
    ne, cap = ye_ref.shape[1], ye_ref.shape[2]
    lane_slot = lax.broadcasted_iota(jnp.int32, (COMB_ROWS, win), 1)
    lhs, rhs = [], []
    for h, first in groups:
        rows = slice(h * COMB_ROWS, (h + 1) * COMB_ROWS)
        hi, lo, ywin = [], [], []
        for e in range(ne):
            start = pl.multiple_of(jnp.minimum(first[e] & ~(SLOT_ALIGN - 1), cap - win), SLOT_ALIGN)
            hit = slotc[rows, e:e + 1] - start == lane_slot
            val = affc[rows, e:e + 1]
            v_hi = val.astype(BF16).astype(F32)
            hi.append(jnp.where(hit, v_hi, 0.0).astype(BF16))
            lo.append(jnp.where(hit, val - v_hi, 0.0).astype(BF16))
            ywin.append(ye_ref[0, e, pl.ds(start, win), :])
        lhs.append(jnp.concatenate([jnp.concatenate(hi, axis=1), jnp.concatenate(lo, axis=1)], axis=0))
        rhs.append(jnp.concatenate(ywin, axis=0))
    both = [_dot(a, y) for a, y in zip(lhs, rhs)]
    for (h, _), bt in zip(groups, both):
        rows = slice(h * COMB_ROWS, (h + 1) * COMB_ROWS)
        x = xm_ref[0, rows, :] + g2_ref[0] * (bt[:COMB_ROWS] + bt[COMB_ROWS:])
        ms = jnp.mean(x * x, axis=-1, keepdims=True)
        o_ref[0, rows, :] = x * lax.rsqrt(ms + NORM_EPS) * fg_ref[...]


def _comb_kernel(tk, win, full, tsp_ref, xm_ref, ye_ref, slotc_ref, affc_ref, g2_ref, fg_ref, o_ref):
    b = pl.program_id(0)
    j = pl.program_id(1)
    nh = tk // COMB_ROWS
    ntile = pl.num_programs(1) * nh + 1
    ne, cap = ye_ref.shape[1], ye_ref.shape[2]
    slotc = slotc_ref[0]
    affc = affc_ref[0]
    refs = (xm_ref, ye_ref, slotc, affc, g2_ref, fg_ref, o_ref)
    groups = [(h, [tsp_ref[(b * ne + e) * ntile + j * nh + h] for e in range(ne)]) for h in range(nh)]
    _comb_rows(win, groups, *refs)
    if full != win:
        for h, first in groups:
            over = None
            for e in range(ne):
                start = jnp.minimum(first[e] & ~(SLOT_ALIGN - 1), cap - win)
                miss = tsp_ref[(b * ne + e) * ntile + j * nh + h + 1] > start + win
                over = miss if over is None else over | miss

            @pl.when(over)
            def _(h=h, first=first):
                _comb_rows(full, [(h, first)], *refs)


def _comb(tsp, xm, ye, slotc, affc, g2, fg, tk):
    bsz, t, d = xm.shape
    ne, cap = ye.shape[1], ye.shape[2]
    full = min(2 * COMB_ROWS, cap)
    assert full == cap or full >= COMB_ROWS + SLOT_ALIGN
    win = min(COMB_WIN, full)
    grid_spec = pltpu.PrefetchScalarGridSpec(
        num_scalar_prefetch=1,
        grid=(bsz, t // tk),
        in_specs=[pl.BlockSpec((1, tk, d), lambda b, j, s: (b, j, 0)),
                  pl.BlockSpec((1, ne, cap, d), lambda b, j, s: (b, 0, 0, 0)),
                  pl.BlockSpec((1, tk, ne), lambda b, j, s: (b, j, 0)),
                  pl.BlockSpec((1, tk, ne), lambda b, j, s: (b, j, 0)),
                  pl.BlockSpec((1, 1, d), lambda b, j, s: (b, 0, 0)),
                  pl.BlockSpec((1, d), lambda b, j, s: (0, 0))],
        out_specs=pl.BlockSpec((1, tk, d), lambda b, j, s: (b, j, 0)),
    )
    return pl.pallas_call(
        functools.partial(_comb_kernel, tk, win, full),
        grid_spec=grid_spec,
        out_shape=jax.ShapeDtypeStruct((bsz, t, d), F32),
        compiler_params=_cparams(("parallel", "arbitrary")),
        name="comb",
    )(tsp, xm, ye, slotc, affc, g2, fg)


def _block_diag_ones(n, blk, value=1.0):
    r = jnp.arange(n)
    return jnp.where((r[:, None] // blk) == (r[None, :] // blk), value, 0.0)


def kernel(x, c, ctx, c_ctx, ada_w, ada_b, norm1_g, norm2_g, w_in, shift_mu, w0, w_lora_up, a0, a_lora_up, k_k, k_a,
           r_k, g_lora_up, gn_g, gn_b, conv_w, w_out, router_w, exp_w_gate, exp_w_up, exp_w_down, final_g):
    bsz, t, d = x.shape
    lc = ctx.shape[1]
    ne = router_w.shape[-1]
    cap = EC_CAPACITY * t // ne
    sb = min(SLOT_BLOCK, cap)
    win = min(GATHER_WIN, t)
    l = 0

    rows = ((bsz + 1 + 7) // 8) * 8
    cc = jnp.zeros((rows, d), F32).at[:bsz].set(c).at[bsz].set(c_ctx)
    mod = _mod(cc, ada_w[l], ada_b[l][None, :])
    sh1, sc1, g1, sh2, sc2, g2 = (m[:, None, :] for m in jnp.split(mod[:bsz], 6, axis=-1))
    csh1, csc1 = (jnp.broadcast_to(m[None, None, :], (bsz, 1, d)) for m in jnp.split(mod[bsz], 6)[:2])

    w_rw = w_in[l][:, :RWKV_COLS].astype(BF16)
    w_cv = w_in[l][:, RWKV_COLS:].astype(BF16)
    n1g = norm1_g[l][None, :]
    px_rw, px_cv = _in_proj(x, sh1, sc1, n1g, w_rw, w_cv, IN_PROJ_ROWS)
    pc_rw, _ = _in_proj(ctx, csh1, csc1, n1g, w_rw, w_cv, min(IN_PROJ_ROWS, lc))

    zw = jnp.zeros((LORA_W, 2 * D_RWKV), F32)
    lora = jnp.concatenate([
        jnp.concatenate([w_lora_up[l, 0], w_lora_up[l, 1], zw], axis=1),
        jnp.concatenate([zw, a_lora_up[l, 0], a_lora_up[l, 1]], axis=1)], axis=0).astype(BF16)
    ridx = jnp.arange(CHUNK)
    tril = jnp.where(ridx[None, :] <= ridx[:, None], 1.0, 0.0).astype(BF16)
    triu = jnp.where(ridx[None, :] >= ridx[:, None], 1.0, 0.0).astype(BF16)
    ones_bd = _block_diag_ones(PAIR, HEAD).astype(BF16)
    consts = (shift_mu[l][None, :], lora, w0[l], a0[l], k_k[l][None, :], k_a[l][None, :],
              r_k[l].reshape(1, D_RWKV), g_lora_up[l].astype(BF16), ones_bd, tril, triu)

    chunks_c = _prep(pc_rw, consts, False, lc)[:5]
    *chunks_x, bonus, gate = _prep(px_rw, consts, True, PREP_ROWS)

    s_zero = jnp.zeros((bsz, 2, N_PAIR, HEAD, PAIR), F32)
    _, _, s_ctx = _scan(*chunks_c, s_zero)
    yf, yb, _ = _scan(*chunks_x, s_ctx)

    rw_t = router_w[l].T
    rwh = rw_t.astype(BF16)
    rwl = (rw_t - rwh.astype(F32)).astype(BF16)
    avg = _block_diag_ones(PAIR, HEAD, 1.0 / HEAD).astype(BF16)
    xm, hx, aff_t = _out(yf, yb, bonus, gate, px_cv, x, g1, sh2, sc2, norm2_g[l][None, :], gn_g[l][None, :],
                         gn_b[l][None, :], conv_w[l], w_out[l].astype(BF16), rwh, rwl, avg, OUT_TILE)

    lane_idx = jnp.arange(TOKEN_ROW)
    tri_lanes = jnp.where(lane_idx[:, None] <= lane_idx[None, :], 1.0, 0.0).astype(BF16)
    cnt, slot, edges = _topk(aff_t, tri_lanes, cap, sb)

    ye = _moe(edges[:, :, :cap // sb + 1].reshape(-1), hx, slot.reshape(bsz, ne, t // TOKEN_ROW, TOKEN_ROW),
              exp_w_gate[l], exp_w_up[l], exp_w_down[l], cap, win, sb)
    tr = lambda a: jnp.transpose(a, (0, 2, 1))
    first_slot = jnp.concatenate([cnt[:, :, ::COMB_ROWS], jnp.full((bsz, ne, 1), cap, jnp.int32)], axis=-1).reshape(-1)
    return _comb(first_slot, xm, ye, tr(slot), tr(aff_t), g2, final_g[None, :], COMB_TILE)
```

```python
import functools
import math

import jax
import jax.numpy as jnp
from jax import lax
from jax.experimental import pallas as pl
from jax.experimental.pallas import tpu as pltpu

F32 = jnp.float32
BF16 = jnp.bfloat16
HIGHEST = lax.Precision.HIGHEST

GRID_W = 64
D_RWKV = 512
D_CONV = 512
HEAD = 64
LORA_W = 64
LORA_A = 64
LORA_G = 128
N_EXPERTS = 16
EC_CAPACITY = 2
NORM_EPS = 1e-6
GN_EPS = 64e-5
RWKV_COLS = 3 * D_RWKV + LORA_W + LORA_A + LORA_G
CONV_COLS = 3 * D_CONV

LANES = 128
CHUNK = 64
PAIR = 2 * HEAD
N_PAIR = D_RWKV // PAIR
VMEM_LIMIT = 48 * 1024 * 1024

IN_PROJ_ROWS = 512
PREP_ROWS = 256
OUT_TILE = 512
COMB_TILE = 256
SLOT_BLOCK = 128
GATHER_WIN = 1536


def _cparams(sem):
    return pltpu.CompilerParams(dimension_semantics=sem, vmem_limit_bytes=VMEM_LIMIT)


def _dot(a, b):
    return jnp.dot(a, b, preferred_element_type=F32)


def _dot_nt(a, b):
    return lax.dot_general(a, b, (((1,), (1,)), ((), ())), preferred_element_type=F32)


def _split2(x):
    hi = x.astype(BF16)
    lo = (x - hi.astype(F32)).astype(BF16)
    return hi, lo


def _split3(x):
    hi = x.astype(BF16)
    r = x - hi.astype(F32)
    mid = r.astype(BF16)
    lo = (r - mid.astype(F32)).astype(BF16)
    return hi, mid, lo


def _seg_dot(x, m):
    hi, lo = _split2(x)
    return _dot(hi, m) + _dot(lo, m)


def _mod_kernel(c_ref, w_ref, b_ref, o_ref):
    c = c_ref[...]
    s = c * jax.nn.sigmoid(c)
    o_ref[...] = jnp.dot(s, w_ref[...], precision=HIGHEST, preferred_element_type=F32) + b_ref[...]


def _mod(cc, w, b):
    rows, d = cc.shape
    n = w.shape[1]
    tn = 1024
    return pl.pallas_call(
        _mod_kernel,
        grid=(n // tn,),
        in_specs=[pl.BlockSpec((rows, d), lambda j: (0, 0)),
                  pl.BlockSpec((d, tn), lambda j: (0, j)),
                  pl.BlockSpec((1, tn), lambda j: (0, j))],
        out_specs=pl.BlockSpec((rows, tn), lambda j: (0, j)),
        out_shape=jax.ShapeDtypeStruct((rows, n), F32),
        compiler_params=_cparams(("parallel",)),
        name="mod",
    )(cc, w, b)


def _in_proj_kernel(x_ref, sh_ref, sc_ref, g_ref, wrw_ref, wcv_ref, orw_ref, ocv_ref):
    x = x_ref[0]
    ms = jnp.mean(x * x, axis=-1, keepdims=True)
    h = x * lax.rsqrt(ms + NORM_EPS) * g_ref[...]
    h = (h * (1.0 + sc_ref[0]) + sh_ref[0]).astype(BF16)
    orw_ref[0] = _dot(h, wrw_ref[...])
    ocv_ref[0] = _dot(h, wcv_ref[...])


def _in_proj(x, sh, sc, g, w_rw, w_cv, tm):
    bsz, l, d = x.shape
    return pl.pallas_call(
        _in_proj_kernel,
        grid=(bsz, l // tm),
        in_specs=[pl.BlockSpec((1, tm, d), lambda b, i: (b, i, 0)),
                  pl.BlockSpec((1, 1, d), lambda b, i: (b, 0, 0)),
                  pl.BlockSpec((1, 1, d), lambda b, i: (b, 0, 0)),
                  pl.BlockSpec((1, d), lambda b, i: (0, 0)),
                  pl.BlockSpec((d, RWKV_COLS), lambda b, i: (0, 0)),
                  pl.BlockSpec((d, CONV_COLS), lambda b, i: (0, 0))],
        out_specs=[pl.BlockSpec((1, tm, RWKV_COLS), lambda b, i: (b, i, 0)),
                   pl.BlockSpec((1, tm, CONV_COLS), lambda b, i: (b, i, 0))],
        out_shape=[jax.ShapeDtypeStruct((bsz, l, RWKV_COLS), F32),
                   jax.ShapeDtypeStruct((bsz, l, CONV_COLS), F32)],
        compiler_params=_cparams(("parallel", "parallel")),
        name="in_proj",
    )(x, sh, sc, g, w_rw, w_cv)


def _bf(x):
    return x.astype(BF16)


def _chunk_local(chains):
    c = CHUNK
    n = range(len(chains))
    lane = lax.broadcasted_iota(jnp.int32, (c, PAIR), 1)
    h0 = lane < HEAD
    tcol = lane & (c - 1)
    trow = lax.broadcasted_iota(jnp.int32, (c, PAIR), 0)
    eye = jnp.where(tcol == trow, 1.0, 0.0)
    masks = {rev: ((tcol > trow) if rev else (tcol < trow), (tcol >= trow) if rev else (tcol <= trow))
             for rev in (False, True)}
    zb = jnp.zeros((c, PAIR), BF16)

    def stack(x):
        return jnp.concatenate([jnp.where(h0, x, zb), jnp.where(h0, zb, x)], axis=0)

    gram = [_dot_nt(jnp.concatenate([ch[1], ch[2]], axis=0), jnp.concatenate([stack(ch[3]), stack(ch[4])], axis=0))
            for ch in chains]
    lab = [jnp.where(masks[chains[i][0]][0], gram[i][:c, :PAIR], 0.0) for i in n]
    lak = [jnp.where(masks[chains[i][0]][0], gram[i][:c, PAIR:], 0.0) for i in n]
    mrb = [jnp.where(masks[chains[i][0]][1], gram[i][c:, :PAIR], 0.0) for i in n]
    mrk = [jnp.where(masks[chains[i][0]][1], gram[i][c:, PAIR:], 0.0) for i in n]

    labb = [_bf(x) for x in lab]
    pw = [_dot(labb[i], stack(labb[i])) for i in n]
    tp = [eye + lab[i] for i in n]
    for _ in range(4):
        pwb = [_bf(x) for x in pw]
        both = [_dot(pwb[i], jnp.concatenate([stack(pwb[i]), stack(_bf(tp[i]))], axis=1)) for i in n]
        pw = [x[:, :PAIR] for x in both]
        tp = [tp[i] + both[i][:, PAIR:] for i in n]
    tinv = [tp[i] + _dot(_bf(pw[i]), stack(_bf(tp[i]))) for i in n]

    lmv = [_dot(_bf(jnp.concatenate([lak[i], mrk[i]], axis=0)), stack(chains[i][7])) for i in n]
    x = [_dot(_bf(tinv[i]), jnp.concatenate([stack(chains[i][1]), stack(_bf(lmv[i][:c]))], axis=1))
         for i in n]
    z = [_dot(_bf(mrb[i]), jnp.concatenate([stack(_bf(x[i][:, :PAIR])), stack(_bf(x[i][:, PAIR:]))], axis=1))
         for i in n]
    rbar = [_bf(chains[i][2].astype(F32) + z[i][:, :PAIR]) for i in n]
    y0 = [z[i][:, PAIR:] + lmv[i][c:] for i in n]

    uv = [jnp.concatenate([x[i][:, PAIR:], chains[i][7].astype(F32)], axis=0) for i in n]
    mxf = [_dot(_bf(x[i][:, :PAIR].T), chains[i][5]) for i in n]
    nnf = [_dot(_bf(uv[i].T), jnp.concatenate([chains[i][5], chains[i][6]], axis=0)) for i in n]
    mx = [_bf(jnp.where(h0, m[:c], m[c:])) for m in mxf]
    nn = [jnp.where(h0, m[:c], m[c:]) for m in nnf]
    return list(zip(rbar, y0, mx, nn))


DIR_COLS = 6 * D_RWKV
SCR_COLS = 2 * DIR_COLS + D_RWKV


def _head_sums(x, ones):
    return jnp.concatenate([_seg_dot(x[:, g * PAIR:(g + 1) * PAIR], ones) for g in range(N_PAIR)], axis=1)


def _prep_rows(c, p, shifted, lo, mu_ref, w0_ref, a0_ref, kk_ref, ka_ref, rk_ref, ones_ref, tril_ref, triu_ref,
               wc_ref, bonus_ref, scr_ref):
    rows = slice(c * CHUNK, (c + 1) * CHUNK)
    pm = p + mu_ref[:, 0:3 * D_RWKV] * (shifted - p)
    xr = pm[:, 0:D_RWKV]
    xk = pm[:, D_RWKV:2 * D_RWKV]
    xv = pm[:, 2 * D_RWKV:3 * D_RWKV]

    ones = ones_ref[...]
    kraw = xk * kk_ref[...]
    kk = kraw / jnp.maximum(jnp.sqrt(_head_sums(kraw * kraw, ones)), 1e-12)

    scr_ref[:, 2 * DIR_COLS:] = xv.astype(BF16)
    ksum = None
    for d in range(2):
        z = w0_ref[d:d + 1, :] + lo[:, d * D_RWKV:(d + 1) * D_RWKV]
        ld = -math.exp(-0.5) * jax.nn.sigmoid(z)
        ag = jax.nn.sigmoid(a0_ref[d:d + 1, :] + lo[:, (2 + d) * D_RWKV:(3 + d) * D_RWKV])
        kd = xk * (1.0 + (ag - 1.0) * ka_ref[...])
        bb = kk * ag
        ksum = kd if ksum is None else ksum + kd
        tri = tril_ref[...] if d == 0 else triu_ref[...]
        h3 = _split3(ld)
        cs = _dot(tri, h3[0]) + _dot(tri, h3[1]) + _dot(tri, h3[2])
        e_in = jnp.exp(cs)
        e_ex = jnp.exp(cs - ld)
        e_neg = jnp.exp(-cs)
        bt = bb * e_neg
        kt = kd * e_neg
        edge = CHUNK - 1 if d == 0 else 0
        wrow = e_in[edge:edge + 1, :]
        wc_ref[0, c, :, d * D_RWKV:(d + 1) * D_RWKV] = wrow
        base = d * DIR_COLS
        for j, val in enumerate((-kk * e_ex, xr * e_in, bt, kt, bt * wrow, kt * wrow)):
            scr_ref[:, base + j * D_RWKV:base + (j + 1) * D_RWKV] = val.astype(BF16)

    bonus_ref[0, rows, :] = _head_sums(xr * ksum * rk_ref[...], ones) * xv


PREP_GROUP = 2


def _prep_chunks(cs, scr_refs, rbar_ref, y0_ref, mx_ref, nn_ref):
    chains = []
    for scr_ref in scr_refs:
        for d in range(2):
            for pp in range(N_PAIR):
                cols = [d * DIR_COLS + j * D_RWKV + pp * PAIR for j in range(6)] + [2 * DIR_COLS + pp * PAIR]
                chains.append((d == 1,) + tuple(scr_ref[:, k:k + PAIR] for k in cols))
    per_chunk = 2 * N_PAIR
    for idx, (rbar, y0, mx, nn) in enumerate(_chunk_local(chains)):
        c = cs[idx // per_chunk]
        rows = slice(c * CHUNK, (c + 1) * CHUNK)
        col = (idx % per_chunk // N_PAIR) * D_RWKV + (idx % N_PAIR) * PAIR
        rbar_ref[0, rows, col:col + PAIR] = rbar
        y0_ref[0, rows, col:col + PAIR] = y0
        mx_ref[0, c, :, col:col + PAIR] = mx
        nn_ref[0, c, :, col:col + PAIR] = nn


N_PREP_CONSTS = 11


def _prep_kernel(latent, tt, p_ref, *rest):
    if latent:
        prev_ref, next_ref = rest[:2]
        rest = rest[2:]
    (mu_ref, lw_ref, w0_ref, a0_ref, kk_ref, ka_ref, rk_ref, gup_ref, ones_ref, tril_ref,
     triu_ref) = rest[:N_PREP_CONSTS]
    rbar_ref, y0_ref, mx_ref, nn_ref, wc_ref, bonus_ref, g_ref = rest[N_PREP_CONSTS:N_PREP_CONSTS + 7]
    *scr, lo_ref, ext_ref = rest[N_PREP_CONSTS + 7:]
    if latent:
        i = pl.program_id(1)
        n = pl.num_programs(1)
        ext_ref[0:GRID_W] = jnp.where(i > 0, prev_ref[0], 0.0)
        ext_ref[GRID_W:GRID_W + tt] = p_ref[0]
        ext_ref[GRID_W + tt:] = jnp.where(i < n - 1, next_ref[0], 0.0)

        def shifted(r0, nrows, cols):
            shape = (nrows, cols.stop - cols.start)
            trow = lax.broadcasted_iota(jnp.int32, shape, 0) & (GRID_W - 1)
            q = lax.broadcasted_iota(jnp.int32, shape, 1) & 3
            left = jnp.where(trow == 0, 0.0, ext_ref[GRID_W - 1 + r0:GRID_W - 1 + r0 + nrows, cols])
            right = jnp.where(trow == GRID_W - 1, 0.0, ext_ref[GRID_W + 1 + r0:GRID_W + 1 + r0 + nrows, cols])
            up = ext_ref[r0:r0 + nrows, cols]
            down = ext_ref[2 * GRID_W + r0:2 * GRID_W + r0 + nrows, cols]
            return jnp.where(q == 0, left, jnp.where(q == 1, right, jnp.where(q == 2, up, down)))
    else:
        ext_ref[0:8] = jnp.zeros((8, RWKV_COLS), F32)
        ext_ref[8:8 + tt] = p_ref[0]
        ext_ref[8 + tt:] = jnp.zeros((8, RWKV_COLS), F32)

        def shifted(r0, nrows, cols):
            q = lax.broadcasted_iota(jnp.int32, (nrows, cols.stop - cols.start), 1) & 1
            return jnp.where(q == 0, ext_ref[7 + r0:7 + r0 + nrows, cols], ext_ref[9 + r0:9 + r0 + nrows, cols])

    tail = slice(3 * D_RWKV, RWKV_COLS)
    p_t = p_ref[0, :, tail]
    pm_t = p_t + mu_ref[:, tail] * (shifted(0, tt, tail) - p_t)
    xwa = pm_t[:, 0:LORA_W + LORA_A]
    lane = lax.broadcasted_iota(jnp.int32, xwa.shape, 1)
    lin = jnp.where(lane < LORA_W, jnp.tanh(xwa), xwa).astype(BF16)
    lo_ref[...] = _dot(lin, lw_ref[...])
    g_ref[0] = _dot(jax.nn.sigmoid(pm_t[:, LORA_W + LORA_A:]).astype(BF16), gup_ref[...])

    head = slice(0, 3 * D_RWKV)

    def rows_part(c):
        rows = slice(c * CHUNK, (c + 1) * CHUNK)
        _prep_rows(c, p_ref[0, rows, head], shifted(c * CHUNK, CHUNK, head), lo_ref[rows, :], mu_ref, w0_ref,
                   a0_ref, kk_ref, ka_ref, rk_ref, ones_ref, tril_ref, triu_ref, wc_ref, bonus_ref,
                   scr[c % (2 * PREP_GROUP)])

    groups = [list(range(g, g + PREP_GROUP)) for g in range(0, tt // CHUNK, PREP_GROUP)]
    for c in groups[0]:
        rows_part(c)
    for k, cs in enumerate(groups):
        if k + 1 < len(groups):
            for c in groups[k + 1]:
                rows_part(c)
        _prep_chunks(cs, [scr[c % (2 * PREP_GROUP)] for c in cs], rbar_ref, y0_ref, mx_ref, nn_ref)


def _prep(p_rw, consts, latent, tt):
    bsz, l, _ = p_rw.shape
    nc = l // CHUNK
    cpt = tt // CHUNK
    assert len(consts) == N_PREP_CONSTS
    const_specs = [pl.BlockSpec(c.shape, lambda b, i, nd=c.ndim: (0,) * nd) for c in consts]
    kern = functools.partial(_prep_kernel, latent, tt)
    if latent:
        nblk = l // GRID_W
        in_specs = [pl.BlockSpec((1, tt, RWKV_COLS), lambda b, i: (b, i, 0)),
                    pl.BlockSpec((1, GRID_W, RWKV_COLS),
                                 lambda b, i: (b, jnp.maximum(i * (tt // GRID_W) - 1, 0), 0)),
                    pl.BlockSpec((1, GRID_W, RWKV_COLS),
                                 lambda b, i: (b, jnp.minimum((i + 1) * (tt // GRID_W), nblk - 1), 0))]
        args = (p_rw, p_rw, p_rw)
        ext_rows = tt + 2 * GRID_W
    else:
        assert tt == l
        in_specs = [pl.BlockSpec((1, tt, RWKV_COLS), lambda b, i: (b, i, 0))]
        args = (p_rw,)
        ext_rows = tt + 16
    row = lambda b, i: (b, i, 0)
    chunk = lambda b, i: (b, i, 0, 0)
    w2 = 2 * D_RWKV
    return pl.pallas_call(
        kern,
        grid=(bsz, l // tt),
        in_specs=in_specs + const_specs,
        out_specs=[pl.BlockSpec((1, tt, w2), row), pl.BlockSpec((1, tt, w2), row),
                   pl.BlockSpec((1, cpt, CHUNK, w2), chunk), pl.BlockSpec((1, cpt, CHUNK, w2), chunk),
                   pl.BlockSpec((1, cpt, 1, w2), chunk),
                   pl.BlockSpec((1, tt, D_RWKV), row), pl.BlockSpec((1, tt, D_RWKV), row)],
        out_shape=[jax.ShapeDtypeStruct((bsz, l, w2), BF16), jax.ShapeDtypeStruct((bsz, l, w2), F32),
                   jax.ShapeDtypeStruct((bsz, nc, CHUNK, w2), BF16), jax.ShapeDtypeStruct((bsz, nc, CHUNK, w2), F32),
                   jax.ShapeDtypeStruct((bsz, nc, 1, w2), F32),
                   jax.ShapeDtypeStruct((bsz, l, D_RWKV), F32), jax.ShapeDtypeStruct((bsz, l, D_RWKV), F32)],
        scratch_shapes=[pltpu.VMEM((CHUNK, SCR_COLS), BF16)] * (2 * PREP_GROUP)
        + [pltpu.VMEM((tt, 4 * D_RWKV), F32), pltpu.VMEM((ext_rows, RWKV_COLS), F32)],
        compiler_params=_cparams(("parallel", "parallel")),
        name="prep_latent" if latent else "prep_ctx",
    )(*args, *consts)


SCAN_CHUNKS = 4
SCAN_BATCH = 4


def _scan_kernel(rbf_ref, rbb_ref, y0f_ref, y0b_ref, mxf_ref, mxb_ref, nnf_ref, nnb_ref, wcf_ref, wcb_ref,
                 s0_ref, yf_ref, yb_ref, s_ref):
    i = pl.program_id(1)

    @pl.when(i == 0)
    def _():
        s_ref[...] = s0_ref[...]

    c = CHUNK
    h0 = lax.broadcasted_iota(jnp.int32, (c, PAIR), 1) < HEAD
    zb = jnp.zeros((c, PAIR), BF16)

    def stack(x):
        return jnp.concatenate([jnp.where(h0, x, zb), jnp.where(h0, zb, x)], axis=0)

    dirs = ((rbf_ref, y0f_ref, mxf_ref, nnf_ref, wcf_ref, yf_ref), (rbb_ref, y0b_ref, mxb_ref, nnb_ref, wcb_ref, yb_ref))
    idx = [(n, d, p) for n in range(SCAN_BATCH) for d in range(2) for p in range(N_PAIR)]
    lanes = [slice(p * PAIR, (p + 1) * PAIR) for _, _, p in idx]
    s = [s_ref[n, d, p] for n, d, p in idx]
    for q in range(SCAN_CHUNKS):
        cq = (q, SCAN_CHUNKS - 1 - q)
        rows = [slice(cq[d] * c, (cq[d] + 1) * c) for _, d, _ in idx]
        sb = [_bf(x) for x in s]
        y = [_dot_nt(dirs[d][0][n, rows[k], lanes[k]], stack(sb[k])) for k, (n, d, _) in enumerate(idx)]
        sm = [_dot(sb[k], stack(dirs[d][2][n, cq[d], :, lanes[k]])) for k, (n, d, _) in enumerate(idx)]
        s_next = []
        for k, (n, d, p) in enumerate(idx):
            dirs[d][5][n, rows[k], lanes[k]] = y[k] + dirs[d][1][n, rows[k], lanes[k]]
            s_next.append(s[k] * dirs[d][4][n, cq[d], :, lanes[k]] + sm[k] + dirs[d][3][n, cq[d], :, lanes[k]])
        s = s_next
    for k, (n, d, p) in enumerate(idx):
        s_ref[n, d, p] = s[k]


def _scan(rbar, y0, mx, nn, wc, s0):
    bsz, l, _ = rbar.shape
    sub, nb = SCAN_CHUNKS, SCAN_BATCH
    assert bsz % nb == 0 and l % (CHUNK * sub) == 0
    nc = l // (CHUNK * sub)
    row_f = pl.BlockSpec((nb, sub * CHUNK, D_RWKV), lambda b, i: (b, i, 0))
    row_b = pl.BlockSpec((nb, sub * CHUNK, D_RWKV), lambda b, i: (b, nc - 1 - i, 1))
    chk_f = pl.BlockSpec((nb, sub, CHUNK, D_RWKV), lambda b, i: (b, i, 0, 0))
    chk_b = pl.BlockSpec((nb, sub, CHUNK, D_RWKV), lambda b, i: (b, nc - 1 - i, 0, 1))
    wc_f = pl.BlockSpec((nb, sub, 1, D_RWKV), lambda b, i: (b, i, 0, 0))
    wc_b = pl.BlockSpec((nb, sub, 1, D_RWKV), lambda b, i: (b, nc - 1 - i, 0, 1))
    st_spec = pl.BlockSpec((nb, 2, N_PAIR, HEAD, PAIR), lambda b, i: (b, 0, 0, 0, 0))
    return pl.pallas_call(
        _scan_kernel,
        grid=(bsz // nb, nc),
        in_specs=[row_f, row_b, row_f, row_b, chk_f, chk_b, chk_f, chk_b, wc_f, wc_b, st_spec],
        out_specs=[row_f, pl.BlockSpec((nb, sub * CHUNK, D_RWKV), lambda b, i: (b, nc - 1 - i, 0)), st_spec],
        out_shape=[jax.ShapeDtypeStruct((bsz, l, D_RWKV), F32),
                   jax.ShapeDtypeStruct((bsz, l, D_RWKV), F32),
                   jax.ShapeDtypeStruct((bsz, 2, N_PAIR, HEAD, PAIR), F32)],
        compiler_params=_cparams(("parallel", "arbitrary")),
        name="scan",
    )(rbar, rbar, y0, y0, mx, mx, nn, nn, wc, wc, s0)


OUT_ROWS = 128


def _out_kernel(tt, yf_ref, yb_ref, bonus_ref, g_ref, cv_ref, cvp_ref, cvn_ref, x_ref, g1_ref, sh2_ref,
                sc2_ref, n2g_ref, gng_ref, gnb_ref, convw_ref, wout_ref, rwh_ref, rwl_ref, avg_ref,
                xm_ref, hx_ref, aff_ref):
    i = pl.program_id(1)
    n = pl.num_programs(1)

    cv = cv_ref[0]
    b_gate = cv[:, 0:D_CONV]
    cu = cv[:, D_CONV:2 * D_CONV] * cv[:, 2 * D_CONV:]
    cvp = cvp_ref[0]
    cvn = cvn_ref[0]
    cu_prev = jnp.where(i > 0, cvp[7:8, D_CONV:2 * D_CONV] * cvp[7:8, 2 * D_CONV:], 0.0)
    cu_next = jnp.where(i < n - 1, cvn[0:1, D_CONV:2 * D_CONV] * cvn[0:1, 2 * D_CONV:], 0.0)
    ridx = lax.broadcasted_iota(jnp.int32, cu.shape, 0)
    cu_m1 = jnp.where(ridx == 0, cu_prev, pltpu.roll(cu, 1, 0))
    cu_p1 = jnp.where(ridx == tt - 1, cu_next, pltpu.roll(cu, tt - 1, 0))
    conv = convw_ref[0:1, :] * cu_m1 + convw_ref[1:2, :] * cu + convw_ref[2:3, :] * cu_p1
    bx = (b_gate * conv).astype(BF16)

    parts = [slice(k * OUT_ROWS, (k + 1) * OUT_ROWS) for k in range(tt // OUT_ROWS)]
    avg = avg_ref[...]
    rwh = rwh_ref[...]
    rwl = rwl_ref[...]
    y = [yf_ref[0, r, :] + yb_ref[0, r, :] for r in parts]
    mu = [_head_sums(v, avg) for v in y]
    dlt = [a - b for a, b in zip(y, mu)]
    var = [_head_sums(v * v, avg) for v in dlt]
    yn = [a * lax.rsqrt(b + GN_EPS) * gng_ref[...] + gnb_ref[...] for a, b in zip(dlt, var)]
    ax = [((a + bonus_ref[0, r, :]) * g_ref[0, r, :]).astype(BF16) for a, r in zip(yn, parts)]
    mix = [_dot(a, wout_ref[0:D_RWKV, :]) + _dot(bx[r], wout_ref[D_RWKV:, :]) for a, r in zip(ax, parts)]
    xm = [x_ref[0, r, :] + g1_ref[0] * a for a, r in zip(mix, parts)]
    ms = [jnp.mean(v * v, axis=-1, keepdims=True) for v in xm]
    hx = [a * lax.rsqrt(b + NORM_EPS) * n2g_ref[...] for a, b in zip(xm, ms)]
    hx = [_split2(v * (1.0 + sc2_ref[0]) + sh2_ref[0]) for v in hx]
    logits = [_dot_nt(rwh, hi) + _dot_nt(rwh, lo) + _dot_nt(rwl, hi) for hi, lo in hx]
    for k, r in enumerate(parts):
        xm_ref[0, r, :] = xm[k]
        hx_ref[0, r, :] = hx[k][0]
        m = jnp.max(logits[k], axis=0, keepdims=True)
        ex = jnp.exp(logits[k] - m)
        aff_ref[0, :, r] = ex / jnp.sum(ex, axis=0, keepdims=True)


def _out(yf, yb, bonus, g, p_cv, x, g1, sh2, sc2, n2g, gng, gnb, convw, wout, rwh, rwl, avg, tt):
    bsz, t, d = x.shape
    ne = rwh.shape[0]
    nb8 = t // 8
    row = lambda b, i: (b, i, 0)
    per_b = lambda b, i: (b, 0, 0)
    const2 = lambda b, i: (0, 0)
    return pl.pallas_call(
        functools.partial(_out_kernel, tt),
        grid=(bsz, t // tt),
        in_specs=[pl.BlockSpec((1, tt, D_RWKV), row), pl.BlockSpec((1, tt, D_RWKV), row),
                  pl.BlockSpec((1, tt, D_RWKV), row), pl.BlockSpec((1, tt, D_RWKV), row),
                  pl.BlockSpec((1, tt, CONV_COLS), row),
                  pl.BlockSpec((1, 8, CONV_COLS), lambda b, i: (b, jnp.maximum(i * (tt // 8) - 1, 0), 0)),
                  pl.BlockSpec((1, 8, CONV_COLS), lambda b, i: (b, jnp.minimum((i + 1) * (tt // 8), nb8 - 1), 0)),
                  pl.BlockSpec((1, tt, d), row),
                  pl.BlockSpec((1, 1, d), per_b), pl.BlockSpec((1, 1, d), per_b), pl.BlockSpec((1, 1, d), per_b),
                  pl.BlockSpec((1, d), const2), pl.BlockSpec((1, D_RWKV), const2), pl.BlockSpec((1, D_RWKV), const2),
                  pl.BlockSpec((3, D_CONV), const2), pl.BlockSpec((D_RWKV + D_CONV, d), const2),
                  pl.BlockSpec((ne, d), const2), pl.BlockSpec((ne, d), const2),
                  pl.BlockSpec((PAIR, PAIR), const2)],
        out_specs=[pl.BlockSpec((1, tt, d), row), pl.BlockSpec((1, tt, d), row),
                   pl.BlockSpec((1, ne, tt), lambda b, i: (b, 0, i))],
        out_shape=[jax.ShapeDtypeStruct((bsz, t, d), F32), jax.ShapeDtypeStruct((bsz, t, d), BF16),
                   jax.ShapeDtypeStruct((bsz, ne, t), F32)],
        compiler_params=_cparams(("parallel", "parallel")),
        name="out",
    )(yf, yb, bonus, g, p_cv, p_cv, p_cv, x, g1, sh2, sc2, n2g, gng, gnb, convw, wout, rwh, rwl, avg)


def _prefix_blocks(mask_fn, t, tri, emit):
    carry = None
    for j in range(t // LANES):
        m = mask_fn(j)
        inc = _dot(m.astype(BF16), tri)
        carry = jnp.zeros_like(inc[:, 0:1]) if carry is None else carry
        emit(j, m, inc - m + carry)
        carry = carry + inc[:, LANES - 1:LANES]


def _topk_kernel(cap, sb, aff_ref, tri_ref, cnt_ref, slot_ref, edge_ref):
    t = aff_ref.shape[2]
    aff = aff_ref[0]

    def body(k, bits):
        cand = bits | jnp.left_shift(jnp.int32(1), 30 - k)
        cnt = jnp.sum(jnp.where(aff >= pltpu.bitcast(cand, F32), 1, 0), axis=-1, keepdims=True)
        return jnp.where(cnt >= cap, cand, bits)

    bits = lax.fori_loop(0, 31, body, jnp.zeros((aff.shape[0], 1), jnp.int32))
    thr = pltpu.bitcast(bits, F32)
    above = pltpu.bitcast(bits + 1, F32)
    n_gt = jnp.sum(jnp.where(aff >= above, 1, 0), axis=-1, keepdims=True)
    need = (cap - n_gt).astype(F32)
    tri = tri_ref[...]

    def blk(j):
        return aff[:, j * LANES:(j + 1) * LANES]

    def emit_sel(j, eq, before):
        take = (blk(j) >= above) | ((eq > 0.5) & (before < need))
        slot_ref[0, :, j * LANES:(j + 1) * LANES] = jnp.where(take, 1, 0)

    _prefix_blocks(lambda j: jnp.where((blk(j) >= thr) & (blk(j) < above), 1.0, 0.0), t, tri, emit_sel)

    def emit_slot(j, m, before):
        count = before.astype(jnp.int32)
        cnt_ref[0, :, j * LANES:(j + 1) * LANES] = count
        slot_ref[0, :, j * LANES:(j + 1) * LANES] = jnp.where(m > 0.5, count, -1)

    _prefix_blocks(lambda j: slot_ref[0, :, j * LANES:(j + 1) * LANES].astype(F32), t, tri, emit_slot)

    cnt = cnt_ref[0]
    lane = lax.broadcasted_iota(jnp.int32, (cnt.shape[0], LANES), 1)
    edges = jnp.zeros((cnt.shape[0], LANES), jnp.int32)
    for s in range(1, cap // sb + 1):
        below = jnp.sum(jnp.where(cnt < s * sb, 1, 0), axis=-1, keepdims=True)
        edges = jnp.where(lane == s, below, edges)
    edge_ref[0] = edges


def _topk(aff_t, tri, cap, sb):
    bsz, ne, t = aff_t.shape
    spec = pl.BlockSpec((1, ne, t), lambda b: (b, 0, 0))
    return pl.pallas_call(
        functools.partial(_topk_kernel, cap, sb),
        grid=(bsz,),
        in_specs=[spec, pl.BlockSpec((LANES, LANES), lambda b: (0, 0))],
        out_specs=[spec, spec, pl.BlockSpec((1, ne, LANES), lambda b: (b, 0, 0))],
        out_shape=[jax.ShapeDtypeStruct((bsz, ne, t), jnp.int32)] * 2 + [jax.ShapeDtypeStruct((bsz, ne, LANES), jnp.int32)],
        compiler_params=_cparams(("parallel",)),
        name="topk",
    )(aff_t, tri)


TOKEN_ROW = LANES


def _moe_kernel(win, sb, edge_ref, hx_ref, slot_ref, wg_ref, wu_ref, wd_ref, ye_ref, xs_ref):
    b = pl.program_id(0)
    e = pl.program_id(1)
    ne = pl.num_programs(1)
    cap = xs_ref.shape[0]
    nblk = cap // sb
    nrow = hx_ref.shape[1] // TOKEN_ROW
    wrows = win // TOKEN_ROW
    base = (b * ne + e) * (nblk + 1)
    for s in range(nblk):
        blk = slice(s * sb, (s + 1) * sb)
        target = lax.broadcasted_iota(jnp.int32, (sb, TOKEN_ROW), 0) + s * sb
        r0 = jnp.minimum(edge_ref[base + s] // TOKEN_ROW, nrow - wrows)
        rows = slot_ref[0, 0, pl.ds(r0, wrows), :]
        onehot = jnp.concatenate([jnp.where(rows[k:k + 1, :] == target, 1.0, 0.0).astype(BF16)
                                  for k in range(wrows)], axis=1)
        t0 = pl.multiple_of(r0 * TOKEN_ROW, TOKEN_ROW)
        xs_ref[blk, :] = _dot(onehot, hx_ref[0, pl.ds(t0, win), :])

        def extra_row(r, carry, blk=blk, target=target):
            hit = jnp.where(slot_ref[0, 0, pl.ds(r, 1), :] == target, 1.0, 0.0).astype(BF16)
            tr = pl.multiple_of(r * TOKEN_ROW, TOKEN_ROW)
            xs_ref[blk, :] += _dot(hit, hx_ref[0, pl.ds(tr, TOKEN_ROW), :])
            return carry

        r_end = (edge_ref[base + s + 1] + TOKEN_ROW - 1) // TOKEN_ROW
        lax.fori_loop(r0 + wrows, r_end, extra_row, 0)

    xs = xs_ref[...].astype(BF16)
    h1 = _dot(xs, wg_ref[0].astype(BF16))
    h2 = _dot(xs, wu_ref[0].astype(BF16))
    hid = (h1 * jax.nn.sigmoid(h1) * h2).astype(BF16)
    ye_ref[0, 0] = _dot(hid, wd_ref[0].astype(BF16)).astype(BF16)


def _moe(edges, hx, slot4, wg, wu, wd, cap, win, sb):
    bsz, t, d = hx.shape
    ne, _, f = wg.shape
    nrow = t // TOKEN_ROW
    grid_spec = pltpu.PrefetchScalarGridSpec(
        num_scalar_prefetch=1,
        grid=(bsz, ne),
        in_specs=[pl.BlockSpec((1, t, d), lambda b, e, s: (b, 0, 0), pipeline_mode=pl.Buffered(1)),
                  pl.BlockSpec((1, 1, nrow, TOKEN_ROW), lambda b, e, s: (b, e, 0, 0)),
                  pl.BlockSpec((1, d, f), lambda b, e, s: (e, 0, 0)),
                  pl.BlockSpec((1, d, f), lambda b, e, s: (e, 0, 0)),
                  pl.BlockSpec((1, f, d), lambda b, e, s: (e, 0, 0))],
        out_specs=pl.BlockSpec((1, 1, cap, d), lambda b, e, s: (b, e, 0, 0)),
        scratch_shapes=[pltpu.VMEM((cap, d), F32)],
    )
    return pl.pallas_call(
        functools.partial(_moe_kernel, win, sb),
        grid_spec=grid_spec,
        out_shape=jax.ShapeDtypeStruct((bsz, ne, cap, d), BF16),
        compiler_params=_cparams(("parallel", "arbitrary")),
        name="moe",
    )(edges, hx, slot4, wg, wu, wd)


COMB_ROWS = 128
COMB_WIN = 64
SLOT_ALIGN = 16


def _comb_rows(win, groups, xm_ref, ye_ref, slotc, affc, g2_ref, fg_ref, o_ref):
    ne, cap = ye_ref.shape[1], ye_ref.shape[2]
    wide = ne * win
    lane_w = lax.broadcasted_iota(jnp.int32, (ne, wide), 1)
    expand = jnp.where(lane_w // win == lax.broadcasted_iota(jnp.int32, (ne, wide), 0), 1.0, 0.0).astype(BF16)
    slot_in_win = (lax.broadcasted_iota(jnp.int32, (COMB_ROWS, wide), 1) % win).astype(F32)
    lane_e = lax.broadcasted_iota(jnp.int32, (1, ne), 1)
    lhs, rhs = [], []
    for h, first in groups:
        rows = slice(h * COMB_ROWS, (h + 1) * COMB_ROWS)
        starts = [pl.multiple_of(jnp.minimum(first[e] & ~(SLOT_ALIGN - 1), cap - win), SLOT_ALIGN) for e in range(ne)]
        start_row = jnp.zeros((1, ne), jnp.int32)
        for e in range(ne):
            start_row = jnp.where(lane_e == e, starts[e], start_row)
        offset = _dot((slotc[rows, :] - start_row).astype(F32).astype(BF16), expand)
        val_hi, val_lo = _split2(affc[rows, :])
        hit = offset == slot_in_win
        hi = jnp.where(hit, _dot(val_hi, expand), 0.0).astype(BF16)
        lo = jnp.where(hit, _dot(val_lo, expand), 0.0).astype(BF16)
        lhs.append(jnp.concatenate([hi, lo], axis=0))
        rhs.append(jnp.concatenate([ye_ref[0, e, pl.ds(starts[e], win), :] for e in range(ne)], axis=0))
    both = [_dot(a, y) for a, y in zip(lhs, rhs)]
    for (h, _), bt in zip(groups, both):
        rows = slice(h * COMB_ROWS, (h + 1) * COMB_ROWS)
        x = xm_ref[0, rows, :] + g2_ref[0] * (bt[:COMB_ROWS] + bt[COMB_ROWS:])
        ms = jnp.mean(x * x, axis=-1, keepdims=True)
        o_ref[0, rows, :] = x * lax.rsqrt(ms + NORM_EPS) * fg_ref[...]


def _comb_kernel(tk, win, full, tsp_ref, xm_ref, ye_ref, slotc_ref, affc_ref, g2_ref, fg_ref, o_ref):
    b = pl.program_id(0)
    j = pl.program_id(1)
    nh = tk // COMB_ROWS
    ntile = pl.num_programs(1) * nh + 1
    ne, cap = ye_ref.shape[1], ye_ref.shape[2]
    slotc = slotc_ref[0]
    affc = affc_ref[0]
    refs = (xm_ref, ye_ref, slotc, affc, g2_ref, fg_ref, o_ref)
    groups = [(h, [tsp_ref[(b * ne + e) * ntile + j * nh + h] for e in range(ne)]) for h in range(nh)]
    _comb_rows(win, groups, *refs)
    if full != win:
        for h, first in groups:
            over = None
            for e in range(ne):
                start = jnp.minimum(first[e] & ~(SLOT_ALIGN - 1), cap - win)
                miss = tsp_ref[(b * ne + e) * ntile + j * nh + h + 1] > start + win
                over = miss if over is None else over | miss

            @pl.when(over)
            def _(h=h, first=first):
                _comb_rows(full, [(h, first)], *refs)


def _comb(tsp, xm, ye, slotc, affc, g2, fg, tk):
    bsz, t, d = xm.shape
    ne, cap = ye.shape[1], ye.shape[2]
    full = min(2 * COMB_ROWS, cap)
    assert full == cap or full >= COMB_ROWS + SLOT_ALIGN
    assert full <= 256
    win = min(COMB_WIN, full)
    grid_spec = pltpu.PrefetchScalarGridSpec(
        num_scalar_prefetch=1,
        grid=(bsz, t // tk),
        in_specs=[pl.BlockSpec((1, tk, d), lambda b, j, s: (b, j, 0)),
                  pl.BlockSpec((1, ne, cap, d), lambda b, j, s: (b, 0, 0, 0)),
                  pl.BlockSpec((1, tk, ne), lambda b, j, s: (b, j, 0)),
                  pl.BlockSpec((1, tk, ne), lambda b, j, s: (b, j, 0)),
                  pl.BlockSpec((1, 1, d), lambda b, j, s: (b, 0, 0)),
                  pl.BlockSpec((1, d), lambda b, j, s: (0, 0))],
        out_specs=pl.BlockSpec((1, tk, d), lambda b, j, s: (b, j, 0)),
    )
    return pl.pallas_call(
        functools.partial(_comb_kernel, tk, win, full),
        grid_spec=grid_spec,
        out_shape=jax.ShapeDtypeStruct((bsz, t, d), F32),
        compiler_params=_cparams(("parallel", "arbitrary")),
        name="comb",
    )(tsp, xm, ye, slotc, affc, g2, fg)


def _block_diag_ones(n, blk, value=1.0):
    r = jnp.arange(n)
    return jnp.where((r[:, None] // blk) == (r[None, :] // blk), value, 0.0)


def kernel(x, c, ctx, c_ctx, ada_w, ada_b, norm1_g, norm2_g, w_in, shift_mu, w0, w_lora_up, a0, a_lora_up, k_k, k_a,
           r_k, g_lora_up, gn_g, gn_b, conv_w, w_out, router_w, exp_w_gate, exp_w_up, exp_w_down, final_g):
    bsz, t, d = x.shape
    lc = ctx.shape[1]
    ne = router_w.shape[-1]
    cap = EC_CAPACITY * t // ne
    sb = min(SLOT_BLOCK, cap)
    win = min(GATHER_WIN, t)
    l = 0

    rows = ((bsz + 1 + 7) // 8) * 8
    cc = jnp.zeros((rows, d), F32).at[:bsz].set(c).at[bsz].set(c_ctx)
    mod = _mod(cc, ada_w[l], ada_b[l][None, :])
    sh1, sc1, g1, sh2, sc2, g2 = (m[:, None, :] for m in jnp.split(mod[:bsz], 6, axis=-1))
    csh1, csc1 = (jnp.broadcast_to(m[None, None, :], (bsz, 1, d)) for m in jnp.split(mod[bsz], 6)[:2])

    w_rw = w_in[l][:, :RWKV_COLS].astype(BF16)
    w_cv = w_in[l][:, RWKV_COLS:].astype(BF16)
    n1g = norm1_g[l][None, :]
    px_rw, px_cv = _in_proj(x, sh1, sc1, n1g, w_rw, w_cv, IN_PROJ_ROWS)
    pc_rw, _ = _in_proj(ctx, csh1, csc1, n1g, w_rw, w_cv, min(IN_PROJ_ROWS, lc))

    zw = jnp.zeros((LORA_W, 2 * D_RWKV), F32)
    lora = jnp.concatenate([
        jnp.concatenate([w_lora_up[l, 0], w_lora_up[l, 1], zw], axis=1),
        jnp.concatenate([zw, a_lora_up[l, 0], a_lora_up[l, 1]], axis=1)], axis=0).astype(BF16)
    ridx = jnp.arange(CHUNK)
    tril = jnp.where(ridx[None, :] <= ridx[:, None], 1.0, 0.0).astype(BF16)
    triu = jnp.where(ridx[None, :] >= ridx[:, None], 1.0, 0.0).astype(BF16)
    ones_bd = _block_diag_ones(PAIR, HEAD).astype(BF16)
    consts = (shift_mu[l][None, :], lora, w0[l], a0[l], k_k[l][None, :], k_a[l][None, :],
              r_k[l].reshape(1, D_RWKV), g_lora_up[l].astype(BF16), ones_bd, tril, triu)

    chunks_c = _prep(pc_rw, consts, False, lc)[:5]
    *chunks_x, bonus, gate = _prep(px_rw, consts, True, PREP_ROWS)

    s_zero = jnp.zeros((bsz, 2, N_PAIR, HEAD, PAIR), F32)
    _, _, s_ctx = _scan(*chunks_c, s_zero)
    yf, yb, _ = _scan(*chunks_x, s_ctx)

    rw_t = router_w[l].T
    rwh = rw_t.astype(BF16)
    rwl = (rw_t - rwh.astype(F32)).astype(BF16)
    avg = _block_diag_ones(PAIR, HEAD, 1.0 / HEAD).astype(BF16)
    xm, hx, aff_t = _out(yf, yb, bonus, gate, px_cv, x, g1, sh2, sc2, norm2_g[l][None, :], gn_g[l][None, :],
                         gn_b[l][None, :], conv_w[l], w_out[l].astype(BF16), rwh, rwl, avg, OUT_TILE)

    lane_idx = jnp.arange(TOKEN_ROW)
    tri_lanes = jnp.where(lane_idx[:, None] <= lane_idx[None, :], 1.0, 0.0).astype(BF16)
    cnt, slot, edges = _topk(aff_t, tri_lanes, cap, sb)

    ye = _moe(edges[:, :, :cap // sb + 1].reshape(-1), hx, slot.reshape(bsz, ne, t // TOKEN_ROW, TOKEN_ROW),
              exp_w_gate[l], exp_w_up[l], exp_w_down[l], cap, win, sb)
    tr = lambda a: jnp.transpose(a, (0, 2, 1))
    first_slot = jnp.concatenate([cnt[:, :, ::COMB_ROWS], jnp.full((bsz, ne, 1), cap, jnp.int32)], axis=-1).reshape(-1)
    return _comb(first_slot, xm, ye, tr(slot), tr(aff_t), g2, final_g[None, :], COMB_TILE)
```

```python
import functools
import math

import jax
import jax.numpy as jnp
from jax import lax
from jax.experimental import pallas as pl
from jax.experimental.pallas import tpu as pltpu

F32 = jnp.float32
BF16 = jnp.bfloat16
HIGHEST = lax.Precision.HIGHEST

GRID_W = 64
D_RWKV = 512
D_CONV = 512
HEAD = 64
LORA_W = 64
LORA_A = 64
LORA_G = 128
EC_CAPACITY = 2
NORM_EPS = 1e-6
GN_EPS = 64e-5
RWKV_COLS = 3 * D_RWKV + LORA_W + LORA_A + LORA_G
CONV_COLS = 3 * D_CONV

LANES = 128
CHUNK = 64
PAIR = 2 * HEAD
N_PAIR = D_RWKV // PAIR
VMEM_LIMIT = 48 * 1024 * 1024

IN_PROJ_ROWS = 512
PREP_ROWS = 512
OUT_TILE = 512
COMB_TILE = 256
SLOT_BLOCK = 128
GATHER_WIN = 1536


def _cparams(sem):
    return pltpu.CompilerParams(dimension_semantics=sem, vmem_limit_bytes=VMEM_LIMIT)


def _dot(a, b):
    return jnp.dot(a, b, preferred_element_type=F32)


def _dot_nt(a, b):
    return lax.dot_general(a, b, (((1,), (1,)), ((), ())), preferred_element_type=F32)


def _split2(x):
    hi = x.astype(BF16)
    lo = (x - hi.astype(F32)).astype(BF16)
    return hi, lo


def _split3(x):
    hi = x.astype(BF16)
    r = x - hi.astype(F32)
    mid = r.astype(BF16)
    lo = (r - mid.astype(F32)).astype(BF16)
    return hi, mid, lo


def _seg_dot(x, m):
    hi, lo = _split2(x)
    return _dot(hi, m) + _dot(lo, m)


def _mod_kernel(c_ref, w_ref, b_ref, o_ref):
    c = c_ref[...]
    s = c * jax.nn.sigmoid(c)
    o_ref[...] = jnp.dot(s, w_ref[...], precision=HIGHEST, preferred_element_type=F32) + b_ref[...]


def _mod(cc, w, b):
    rows, d = cc.shape
    n = w.shape[1]
    tn = 1024
    return pl.pallas_call(
        _mod_kernel,
        grid=(n // tn,),
        in_specs=[pl.BlockSpec((rows, d), lambda j: (0, 0)),
                  pl.BlockSpec((d, tn), lambda j: (0, j)),
                  pl.BlockSpec((1, tn), lambda j: (0, j))],
        out_specs=pl.BlockSpec((rows, tn), lambda j: (0, j)),
        out_shape=jax.ShapeDtypeStruct((rows, n), F32),
        compiler_params=_cparams(("parallel",)),
        name="mod",
    )(cc, w, b)


def _in_proj_kernel(x_ref, sh_ref, sc_ref, g_ref, wrw_ref, wcv_ref, orw_ref, ocv_ref):
    x = x_ref[0]
    ms = jnp.mean(x * x, axis=-1, keepdims=True)
    h = x * lax.rsqrt(ms + NORM_EPS) * g_ref[...]
    h = (h * (1.0 + sc_ref[0]) + sh_ref[0]).astype(BF16)
    orw_ref[0] = _dot(h, wrw_ref[...])
    ocv_ref[0] = _dot(h, wcv_ref[...])


def _in_proj(x, sh, sc, g, w_rw, w_cv, tm):
    bsz, l, d = x.shape
    return pl.pallas_call(
        _in_proj_kernel,
        grid=(bsz, l // tm),
        in_specs=[pl.BlockSpec((1, tm, d), lambda b, i: (b, i, 0)),
                  pl.BlockSpec((1, 1, d), lambda b, i: (b, 0, 0)),
                  pl.BlockSpec((1, 1, d), lambda b, i: (b, 0, 0)),
                  pl.BlockSpec((1, d), lambda b, i: (0, 0)),
                  pl.BlockSpec((d, RWKV_COLS), lambda b, i: (0, 0)),
                  pl.BlockSpec((d, CONV_COLS), lambda b, i: (0, 0))],
        out_specs=[pl.BlockSpec((1, tm, RWKV_COLS), lambda b, i: (b, i, 0)),
                   pl.BlockSpec((1, tm, CONV_COLS), lambda b, i: (b, i, 0))],
        out_shape=[jax.ShapeDtypeStruct((bsz, l, RWKV_COLS), F32),
                   jax.ShapeDtypeStruct((bsz, l, CONV_COLS), F32)],
        compiler_params=_cparams(("parallel", "parallel")),
        name="in_proj",
    )(x, sh, sc, g, w_rw, w_cv)


def _bf(x):
    return x.astype(BF16)


def _chunk_local(chains):
    c = CHUNK
    n = range(len(chains))
    lane = lax.broadcasted_iota(jnp.int32, (c, PAIR), 1)
    h0 = lane < HEAD
    tcol = lane & (c - 1)
    trow = lax.broadcasted_iota(jnp.int32, (c, PAIR), 0)
    eye = jnp.where(tcol == trow, 1.0, 0.0)
    masks = {rev: ((tcol > trow) if rev else (tcol < trow), (tcol >= trow) if rev else (tcol <= trow))
             for rev in (False, True)}
    zb = jnp.zeros((c, PAIR), BF16)

    def stack(x):
        return jnp.concatenate([jnp.where(h0, x, zb), jnp.where(h0, zb, x)], axis=0)

    gram = [_dot_nt(jnp.concatenate([ch[1], ch[2]], axis=0), jnp.concatenate([stack(ch[3]), stack(ch[4])], axis=0))
            for ch in chains]
    lab = [jnp.where(masks[chains[i][0]][0], gram[i][:c, :PAIR], 0.0) for i in n]
    lak = [jnp.where(masks[chains[i][0]][0], gram[i][:c, PAIR:], 0.0) for i in n]
    mrb = [jnp.where(masks[chains[i][0]][1], gram[i][c:, :PAIR], 0.0) for i in n]
    mrk = [jnp.where(masks[chains[i][0]][1], gram[i][c:, PAIR:], 0.0) for i in n]

    labb = [_bf(x) for x in lab]
    pw = [_dot(labb[i], stack(labb[i])) for i in n]
    tp = [eye + lab[i] for i in n]
    for _ in range(4):
        pwb = [_bf(x) for x in pw]
        both = [_dot(pwb[i], jnp.concatenate([stack(pwb[i]), stack(_bf(tp[i]))], axis=1)) for i in n]
        pw = [x[:, :PAIR] for x in both]
        tp = [tp[i] + both[i][:, PAIR:] for i in n]
    tinv = [tp[i] + _dot(_bf(pw[i]), stack(_bf(tp[i]))) for i in n]

    lmv = [_dot(_bf(jnp.concatenate([lak[i], mrk[i]], axis=0)), stack(chains[i][7])) for i in n]
    x = [_dot(_bf(tinv[i]), jnp.concatenate([stack(chains[i][1]), stack(_bf(lmv[i][:c]))], axis=1))
         for i in n]
    z = [_dot(_bf(mrb[i]), jnp.concatenate([stack(_bf(x[i][:, :PAIR])), stack(_bf(x[i][:, PAIR:]))], axis=1))
         for i in n]
    rbar = [_bf(chains[i][2].astype(F32) + z[i][:, :PAIR]) for i in n]
    y0 = [z[i][:, PAIR:] + lmv[i][c:] for i in n]

    uv = [jnp.concatenate([x[i][:, PAIR:], chains[i][7].astype(F32)], axis=0) for i in n]
    mxf = [_dot(_bf(x[i][:, :PAIR].T), chains[i][5]) for i in n]
    nnf = [_dot(_bf(uv[i].T), jnp.concatenate([chains[i][5], chains[i][6]], axis=0)) for i in n]
    mx = [_bf(jnp.where(h0, m[:c], m[c:])) for m in mxf]
    nn = [jnp.where(h0, m[:c], m[c:]) for m in nnf]
    return list(zip(rbar, y0, mx, nn))


DIR_COLS = 6 * D_RWKV
SCR_COLS = 2 * DIR_COLS + D_RWKV


def _head_sums(x, ones):
    return jnp.concatenate([_seg_dot(x[:, g * PAIR:(g + 1) * PAIR], ones) for g in range(N_PAIR)], axis=1)


def _prep_rows(c, p, shifted, lo, mu_ref, w0_ref, a0_ref, kk_ref, ka_ref, rk_ref, ones_ref, tril_ref, triu_ref,
               wc_ref, bonus_ref, scr_ref):
    rows = slice(c * CHUNK, (c + 1) * CHUNK)
    pm = p + mu_ref[:, 0:3 * D_RWKV] * (shifted - p)
    xr = pm[:, 0:D_RWKV]
    xk = pm[:, D_RWKV:2 * D_RWKV]
    xv = pm[:, 2 * D_RWKV:3 * D_RWKV]

    ones = ones_ref[...]
    kraw = xk * kk_ref[...]
    kk = kraw / jnp.maximum(jnp.sqrt(_head_sums(kraw * kraw, ones)), 1e-12)

    scr_ref[:, 2 * DIR_COLS:] = xv.astype(BF16)
    ksum = None
    for d in range(2):
        z = w0_ref[d:d + 1, :] + lo[:, d * D_RWKV:(d + 1) * D_RWKV]
        ld = -math.exp(-0.5) * jax.nn.sigmoid(z)
        ag = jax.nn.sigmoid(a0_ref[d:d + 1, :] + lo[:, (2 + d) * D_RWKV:(3 + d) * D_RWKV])
        kd = xk * (1.0 + (ag - 1.0) * ka_ref[...])
        bb = kk * ag
        ksum = kd if ksum is None else ksum + kd
        tri = tril_ref[...] if d == 0 else triu_ref[...]
        h3 = _split3(ld)
        cs = _dot(tri, h3[0]) + _dot(tri, h3[1]) + _dot(tri, h3[2])
        e_in = jnp.exp(cs)
        e_ex = jnp.exp(cs - ld)
        e_neg = jnp.exp(-cs)
        bt = bb * e_neg
        kt = kd * e_neg
        edge = CHUNK - 1 if d == 0 else 0
        wrow = e_in[edge:edge + 1, :]
        wc_ref[0, c, :, d * D_RWKV:(d + 1) * D_RWKV] = wrow
        base = d * DIR_COLS
        for j, val in enumerate((-kk * e_ex, xr * e_in, bt, kt, bt * wrow, kt * wrow)):
            scr_ref[:, base + j * D_RWKV:base + (j + 1) * D_RWKV] = val.astype(BF16)

    bonus_ref[0, rows, :] = _head_sums(xr * ksum * rk_ref[...], ones) * xv


PREP_GROUP = 2


def _prep_chunks(cs, scr_refs, rbar_ref, y0_ref, mx_ref, nn_ref):
    chains = []
    for scr_ref in scr_refs:
        for d in range(2):
            for pp in range(N_PAIR):
                cols = [d * DIR_COLS + j * D_RWKV + pp * PAIR for j in range(6)] + [2 * DIR_COLS + pp * PAIR]
                chains.append((d == 1,) + tuple(scr_ref[:, k:k + PAIR] for k in cols))
    per_chunk = 2 * N_PAIR
    for idx, (rbar, y0, mx, nn) in enumerate(_chunk_local(chains)):
        c = cs[idx // per_chunk]
        rows = slice(c * CHUNK, (c + 1) * CHUNK)
        col = (idx % per_chunk // N_PAIR) * D_RWKV + (idx % N_PAIR) * PAIR
        rbar_ref[0, rows, col:col + PAIR] = rbar
        y0_ref[0, rows, col:col + PAIR] = y0
        mx_ref[0, c, :, col:col + PAIR] = mx
        nn_ref[0, c, :, col:col + PAIR] = nn


N_PREP_CONSTS = 11


def _prep_kernel(latent, tt, p_ref, *rest):
    if latent:
        prev_ref, next_ref = rest[:2]
        rest = rest[2:]
    (mu_ref, lw_ref, w0_ref, a0_ref, kk_ref, ka_ref, rk_ref, gup_ref, ones_ref, tril_ref,
     triu_ref) = rest[:N_PREP_CONSTS]
    rbar_ref, y0_ref, mx_ref, nn_ref, wc_ref, bonus_ref, g_ref = rest[N_PREP_CONSTS:N_PREP_CONSTS + 7]
    *scr, lo_ref, ext_ref = rest[N_PREP_CONSTS + 7:]
    if latent:
        i = pl.program_id(1)
        n = pl.num_programs(1)
        ext_ref[0:GRID_W] = jnp.where(i > 0, prev_ref[0], 0.0)
        ext_ref[GRID_W:GRID_W + tt] = p_ref[0]
        ext_ref[GRID_W + tt:] = jnp.where(i < n - 1, next_ref[0], 0.0)

        def shifted(r0, nrows, cols):
            shape = (nrows, cols.stop - cols.start)
            trow = lax.broadcasted_iota(jnp.int32, shape, 0) & (GRID_W - 1)
            q = lax.broadcasted_iota(jnp.int32, shape, 1) & 3
            left = jnp.where(trow == 0, 0.0, ext_ref[GRID_W - 1 + r0:GRID_W - 1 + r0 + nrows, cols])
            right = jnp.where(trow == GRID_W - 1, 0.0, ext_ref[GRID_W + 1 + r0:GRID_W + 1 + r0 + nrows, cols])
            up = ext_ref[r0:r0 + nrows, cols]
            down = ext_ref[2 * GRID_W + r0:2 * GRID_W + r0 + nrows, cols]
            return jnp.where(q == 0, left, jnp.where(q == 1, right, jnp.where(q == 2, up, down)))
    else:
        ext_ref[0:8] = jnp.zeros((8, RWKV_COLS), F32)
        ext_ref[8:8 + tt] = p_ref[0]
        ext_ref[8 + tt:] = jnp.zeros((8, RWKV_COLS), F32)

        def shifted(r0, nrows, cols):
            q = lax.broadcasted_iota(jnp.int32, (nrows, cols.stop - cols.start), 1) & 1
            return jnp.where(q == 0, ext_ref[7 + r0:7 + r0 + nrows, cols], ext_ref[9 + r0:9 + r0 + nrows, cols])

    tail = slice(3 * D_RWKV, RWKV_COLS)
    p_t = p_ref[0, :, tail]
    pm_t = p_t + mu_ref[:, tail] * (shifted(0, tt, tail) - p_t)
    xwa = pm_t[:, 0:LORA_W + LORA_A]
    lane = lax.broadcasted_iota(jnp.int32, xwa.shape, 1)
    lin = jnp.where(lane < LORA_W, jnp.tanh(xwa), xwa).astype(BF16)
    lo_ref[...] = _dot(lin, lw_ref[...])
    g_ref[0] = _dot(jax.nn.sigmoid(pm_t[:, LORA_W + LORA_A:]).astype(BF16), gup_ref[...])

    head = slice(0, 3 * D_RWKV)

    def rows_part(c):
        rows = slice(c * CHUNK, (c + 1) * CHUNK)
        _prep_rows(c, p_ref[0, rows, head], shifted(c * CHUNK, CHUNK, head), lo_ref[rows, :], mu_ref, w0_ref,
                   a0_ref, kk_ref, ka_ref, rk_ref, ones_ref, tril_ref, triu_ref, wc_ref, bonus_ref,
                   scr[c % (2 * PREP_GROUP)])

    groups = [list(range(g, g + PREP_GROUP)) for g in range(0, tt // CHUNK, PREP_GROUP)]
    for c in groups[0]:
        rows_part(c)
    for k, cs in enumerate(groups):
        if k + 1 < len(groups):
            for c in groups[k + 1]:
                rows_part(c)
        _prep_chunks(cs, [scr[c % (2 * PREP_GROUP)] for c in cs], rbar_ref, y0_ref, mx_ref, nn_ref)


def _prep(p_rw, consts, latent, tt):
    bsz, l, _ = p_rw.shape
    nc = l // CHUNK
    cpt = tt // CHUNK
    assert len(consts) == N_PREP_CONSTS
    const_specs = [pl.BlockSpec(c.shape, lambda b, i, nd=c.ndim: (0,) * nd) for c in consts]
    kern = functools.partial(_prep_kernel, latent, tt)
    if latent:
        nblk = l // GRID_W
        in_specs = [pl.BlockSpec((1, tt, RWKV_COLS), lambda b, i: (b, i, 0)),
                    pl.BlockSpec((1, GRID_W, RWKV_COLS),
                                 lambda b, i: (b, jnp.maximum(i * (tt // GRID_W) - 1, 0), 0)),
                    pl.BlockSpec((1, GRID_W, RWKV_COLS),
                                 lambda b, i: (b, jnp.minimum((i + 1) * (tt // GRID_W), nblk - 1), 0))]
        args = (p_rw, p_rw, p_rw)
        ext_rows = tt + 2 * GRID_W
    else:
        assert tt == l
        in_specs = [pl.BlockSpec((1, tt, RWKV_COLS), lambda b, i: (b, i, 0))]
        args = (p_rw,)
        ext_rows = tt + 16
    row = lambda b, i: (b, i, 0)
    chunk = lambda b, i: (b, i, 0, 0)
    w2 = 2 * D_RWKV
    return pl.pallas_call(
        kern,
        grid=(bsz, l // tt),
        in_specs=in_specs + const_specs,
        out_specs=[pl.BlockSpec((1, tt, w2), row), pl.BlockSpec((1, tt, w2), row),
                   pl.BlockSpec((1, cpt, CHUNK, w2), chunk), pl.BlockSpec((1, cpt, CHUNK, w2), chunk),
                   pl.BlockSpec((1, cpt, 1, w2), chunk),
                   pl.BlockSpec((1, tt, D_RWKV), row), pl.BlockSpec((1, tt, D_RWKV), row)],
        out_shape=[jax.ShapeDtypeStruct((bsz, l, w2), BF16), jax.ShapeDtypeStruct((bsz, l, w2), F32),
                   jax.ShapeDtypeStruct((bsz, nc, CHUNK, w2), BF16), jax.ShapeDtypeStruct((bsz, nc, CHUNK, w2), F32),
                   jax.ShapeDtypeStruct((bsz, nc, 1, w2), F32),
                   jax.ShapeDtypeStruct((bsz, l, D_RWKV), F32), jax.ShapeDtypeStruct((bsz, l, D_RWKV), F32)],
        scratch_shapes=[pltpu.VMEM((CHUNK, SCR_COLS), BF16)] * (2 * PREP_GROUP)
        + [pltpu.VMEM((tt, 4 * D_RWKV), F32), pltpu.VMEM((ext_rows, RWKV_COLS), F32)],
        compiler_params=_cparams(("parallel", "parallel")),
        name="prep_latent" if latent else "prep_ctx",
    )(*args, *consts)


SCAN_CHUNKS = 4
SCAN_BATCH = 4


def _scan_kernel(rbf_ref, rbb_ref, y0f_ref, y0b_ref, mxf_ref, mxb_ref, nnf_ref, nnb_ref, wcf_ref, wcb_ref,
                 s0_ref, yf_ref, yb_ref, s_ref):
    i = pl.program_id(1)

    @pl.when(i == 0)
    def _():
        s_ref[...] = s0_ref[...]

    c = CHUNK
    h0 = lax.broadcasted_iota(jnp.int32, (c, PAIR), 1) < HEAD
    zb = jnp.zeros((c, PAIR), BF16)

    def stack(x):
        return jnp.concatenate([jnp.where(h0, x, zb), jnp.where(h0, zb, x)], axis=0)

    dirs = ((rbf_ref, y0f_ref, mxf_ref, nnf_ref, wcf_ref, yf_ref), (rbb_ref, y0b_ref, mxb_ref, nnb_ref, wcb_ref, yb_ref))
    idx = [(n, d, p) for n in range(SCAN_BATCH) for d in range(2) for p in range(N_PAIR)]
    lanes = [slice(p * PAIR, (p + 1) * PAIR) for _, _, p in idx]
    s = [s_ref[n, d, p] for n, d, p in idx]
    for q in range(SCAN_CHUNKS):
        cq = (q, SCAN_CHUNKS - 1 - q)
        rows = [slice(cq[d] * c, (cq[d] + 1) * c) for _, d, _ in idx]
        sb = [_bf(x) for x in s]
        y = [_dot_nt(dirs[d][0][n, rows[k], lanes[k]], stack(sb[k])) for k, (n, d, _) in enumerate(idx)]
        sm = [_dot(sb[k], stack(dirs[d][2][n, cq[d], :, lanes[k]])) for k, (n, d, _) in enumerate(idx)]
        s_next = []
        for k, (n, d, p) in enumerate(idx):
            dirs[d][5][n, rows[k], lanes[k]] = y[k] + dirs[d][1][n, rows[k], lanes[k]]
            s_next.append(s[k] * dirs[d][4][n, cq[d], :, lanes[k]] + sm[k] + dirs[d][3][n, cq[d], :, lanes[k]])
        s = s_next
    for k, (n, d, p) in enumerate(idx):
        s_ref[n, d, p] = s[k]


def _scan(rbar, y0, mx, nn, wc, s0):
    bsz, l, _ = rbar.shape
    sub, nb = SCAN_CHUNKS, SCAN_BATCH
    assert bsz % nb == 0 and l % (CHUNK * sub) == 0
    nc = l // (CHUNK * sub)
    row_f = pl.BlockSpec((nb, sub * CHUNK, D_RWKV), lambda b, i: (b, i, 0))
    row_b = pl.BlockSpec((nb, sub * CHUNK, D_RWKV), lambda b, i: (b, nc - 1 - i, 1))
    chk_f = pl.BlockSpec((nb, sub, CHUNK, D_RWKV), lambda b, i: (b, i, 0, 0))
    chk_b = pl.BlockSpec((nb, sub, CHUNK, D_RWKV), lambda b, i: (b, nc - 1 - i, 0, 1))
    wc_f = pl.BlockSpec((nb, sub, 1, D_RWKV), lambda b, i: (b, i, 0, 0))
    wc_b = pl.BlockSpec((nb, sub, 1, D_RWKV), lambda b, i: (b, nc - 1 - i, 0, 1))
    st_spec = pl.BlockSpec((nb, 2, N_PAIR, HEAD, PAIR), lambda b, i: (b, 0, 0, 0, 0))
    return pl.pallas_call(
        _scan_kernel,
        grid=(bsz // nb, nc),
        in_specs=[row_f, row_b, row_f, row_b, chk_f, chk_b, chk_f, chk_b, wc_f, wc_b, st_spec],
        out_specs=[row_f, pl.BlockSpec((nb, sub * CHUNK, D_RWKV), lambda b, i: (b, nc - 1 - i, 0)), st_spec],
        out_shape=[jax.ShapeDtypeStruct((bsz, l, D_RWKV), F32),
                   jax.ShapeDtypeStruct((bsz, l, D_RWKV), F32),
                   jax.ShapeDtypeStruct((bsz, 2, N_PAIR, HEAD, PAIR), F32)],
        compiler_params=_cparams(("parallel", "arbitrary")),
        name="scan",
    )(rbar, rbar, y0, y0, mx, mx, nn, nn, wc, wc, s0)


OUT_ROWS = 128


def _out_kernel(tt, yf_ref, yb_ref, bonus_ref, g_ref, cv_ref, cvp_ref, cvn_ref, x_ref, g1_ref, sh2_ref,
                sc2_ref, n2g_ref, gng_ref, gnb_ref, convw_ref, wout_ref, rwh_ref, rwl_ref, avg_ref,
                xm_ref, hx_ref, aff_ref):
    i = pl.program_id(1)
    n = pl.num_programs(1)

    cv = cv_ref[0]
    b_gate = cv[:, 0:D_CONV]
    cu = cv[:, D_CONV:2 * D_CONV] * cv[:, 2 * D_CONV:]
    cvp = cvp_ref[0]
    cvn = cvn_ref[0]
    cu_prev = jnp.where(i > 0, cvp[7:8, D_CONV:2 * D_CONV] * cvp[7:8, 2 * D_CONV:], 0.0)
    cu_next = jnp.where(i < n - 1, cvn[0:1, D_CONV:2 * D_CONV] * cvn[0:1, 2 * D_CONV:], 0.0)
    ridx = lax.broadcasted_iota(jnp.int32, cu.shape, 0)
    cu_m1 = jnp.where(ridx == 0, cu_prev, pltpu.roll(cu, 1, 0))
    cu_p1 = jnp.where(ridx == tt - 1, cu_next, pltpu.roll(cu, tt - 1, 0))
    conv = convw_ref[0:1, :] * cu_m1 + convw_ref[1:2, :] * cu + convw_ref[2:3, :] * cu_p1
    bx = (b_gate * conv).astype(BF16)

    parts = [slice(k * OUT_ROWS, (k + 1) * OUT_ROWS) for k in range(tt // OUT_ROWS)]
    avg = avg_ref[...]
    rwh = rwh_ref[...]
    rwl = rwl_ref[...]
    y = [yf_ref[0, r, :] + yb_ref[0, r, :] for r in parts]
    mu = [_head_sums(v, avg) for v in y]
    dlt = [a - b for a, b in zip(y, mu)]
    var = [_head_sums(v * v, avg) for v in dlt]
    yn = [a * lax.rsqrt(b + GN_EPS) * gng_ref[...] + gnb_ref[...] for a, b in zip(dlt, var)]
    ax = [((a + bonus_ref[0, r, :]) * g_ref[0, r, :]).astype(BF16) for a, r in zip(yn, parts)]
    mix = [_dot(a, wout_ref[0:D_RWKV, :]) + _dot(bx[r], wout_ref[D_RWKV:, :]) for a, r in zip(ax, parts)]
    xm = [x_ref[0, r, :] + g1_ref[0] * a for a, r in zip(mix, parts)]
    ms = [jnp.mean(v * v, axis=-1, keepdims=True) for v in xm]
    hx = [a * lax.rsqrt(b + NORM_EPS) * n2g_ref[...] for a, b in zip(xm, ms)]
    hx = [_split2(v * (1.0 + sc2_ref[0]) + sh2_ref[0]) for v in hx]
    logits = [_dot_nt(rwh, hi) + _dot_nt(rwh, lo) + _dot_nt(rwl, hi) for hi, lo in hx]
    for k, r in enumerate(parts):
        xm_ref[0, r, :] = xm[k]
        hx_ref[0, r, :] = hx[k][0]
        m = jnp.max(logits[k], axis=0, keepdims=True)
        ex = jnp.exp(logits[k] - m)
        aff_ref[0, :, r] = ex / jnp.sum(ex, axis=0, keepdims=True)


def _out(yf, yb, bonus, g, p_cv, x, g1, sh2, sc2, n2g, gng, gnb, convw, wout, rwh, rwl, avg, tt):
    bsz, t, d = x.shape
    ne = rwh.shape[0]
    nb8 = t // 8
    row = lambda b, i: (b, i, 0)
    per_b = lambda b, i: (b, 0, 0)
    const2 = lambda b, i: (0, 0)
    return pl.pallas_call(
        functools.partial(_out_kernel, tt),
        grid=(bsz, t // tt),
        in_specs=[pl.BlockSpec((1, tt, D_RWKV), row), pl.BlockSpec((1, tt, D_RWKV), row),
                  pl.BlockSpec((1, tt, D_RWKV), row), pl.BlockSpec((1, tt, D_RWKV), row),
                  pl.BlockSpec((1, tt, CONV_COLS), row),
                  pl.BlockSpec((1, 8, CONV_COLS), lambda b, i: (b, jnp.maximum(i * (tt // 8) - 1, 0), 0)),
                  pl.BlockSpec((1, 8, CONV_COLS), lambda b, i: (b, jnp.minimum((i + 1) * (tt // 8), nb8 - 1), 0)),
                  pl.BlockSpec((1, tt, d), row),
                  pl.BlockSpec((1, 1, d), per_b), pl.BlockSpec((1, 1, d), per_b), pl.BlockSpec((1, 1, d), per_b),
                  pl.BlockSpec((1, d), const2), pl.BlockSpec((1, D_RWKV), const2), pl.BlockSpec((1, D_RWKV), const2),
                  pl.BlockSpec((3, D_CONV), const2), pl.BlockSpec((D_RWKV + D_CONV, d), const2),
                  pl.BlockSpec((ne, d), const2), pl.BlockSpec((ne, d), const2),
                  pl.BlockSpec((PAIR, PAIR), const2)],
        out_specs=[pl.BlockSpec((1, tt, d), row), pl.BlockSpec((1, tt, d), row),
                   pl.BlockSpec((1, ne, tt), lambda b, i: (b, 0, i))],
        out_shape=[jax.ShapeDtypeStruct((bsz, t, d), F32), jax.ShapeDtypeStruct((bsz, t, d), BF16),
                   jax.ShapeDtypeStruct((bsz, ne, t), F32)],
        compiler_params=_cparams(("parallel", "parallel")),
        name="out",
    )(yf, yb, bonus, g, p_cv, p_cv, p_cv, x, g1, sh2, sc2, n2g, gng, gnb, convw, wout, rwh, rwl, avg)


def _prefix_blocks(mask_fn, t, tri, emit):
    carry = None
    for j in range(t // LANES):
        m = mask_fn(j)
        inc = _dot(m.astype(BF16), tri)
        carry = jnp.zeros_like(inc[:, 0:1]) if carry is None else carry
        emit(j, m, inc - m + carry)
        carry = carry + inc[:, LANES - 1:LANES]


def _topk_kernel(cap, sb, aff_ref, tri_ref, cnt_ref, slot_ref, edge_ref):
    t = aff_ref.shape[2]
    aff = aff_ref[0]

    def body(k, bits):
        cand = bits | jnp.left_shift(jnp.int32(1), 30 - k)
        cnt = jnp.sum(jnp.where(aff >= pltpu.bitcast(cand, F32), 1, 0), axis=-1, keepdims=True)
        return jnp.where(cnt >= cap, cand, bits)

    bits = lax.fori_loop(0, 31, body, jnp.zeros((aff.shape[0], 1), jnp.int32))
    thr = pltpu.bitcast(bits, F32)
    above = pltpu.bitcast(bits + 1, F32)
    n_gt = jnp.sum(jnp.where(aff >= above, 1, 0), axis=-1, keepdims=True)
    need = (cap - n_gt).astype(F32)
    tri = tri_ref[...]

    def blk(j):
        return aff[:, j * LANES:(j + 1) * LANES]

    def emit_sel(j, eq, before):
        take = (blk(j) >= above) | ((eq > 0.5) & (before < need))
        slot_ref[0, :, j * LANES:(j + 1) * LANES] = jnp.where(take, 1, 0)

    _prefix_blocks(lambda j: jnp.where((blk(j) >= thr) & (blk(j) < above), 1.0, 0.0), t, tri, emit_sel)

    def emit_slot(j, m, before):
        count = before.astype(jnp.int32)
        cnt_ref[0, :, j * LANES:(j + 1) * LANES] = count
        slot_ref[0, :, j * LANES:(j + 1) * LANES] = jnp.where(m > 0.5, count, -1)

    _prefix_blocks(lambda j: slot_ref[0, :, j * LANES:(j + 1) * LANES].astype(F32), t, tri, emit_slot)

    cnt = cnt_ref[0]
    lane = lax.broadcasted_iota(jnp.int32, (cnt.shape[0], LANES), 1)
    edges = jnp.zeros((cnt.shape[0], LANES), jnp.int32)
    for s in range(1, cap // sb + 1):
        below = jnp.sum(jnp.where(cnt < s * sb, 1, 0), axis=-1, keepdims=True)
        edges = jnp.where(lane == s, below, edges)
    edge_ref[0] = edges


def _topk(aff_t, tri, cap, sb):
    bsz, ne, t = aff_t.shape
    spec = pl.BlockSpec((1, ne, t), lambda b: (b, 0, 0))
    return pl.pallas_call(
        functools.partial(_topk_kernel, cap, sb),
        grid=(bsz,),
        in_specs=[spec, pl.BlockSpec((LANES, LANES), lambda b: (0, 0))],
        out_specs=[spec, spec, pl.BlockSpec((1, ne, LANES), lambda b: (b, 0, 0))],
        out_shape=[jax.ShapeDtypeStruct((bsz, ne, t), jnp.int32)] * 2 + [jax.ShapeDtypeStruct((bsz, ne, LANES), jnp.int32)],
        compiler_params=_cparams(("parallel",)),
        name="topk",
    )(aff_t, tri)


TOKEN_ROW = LANES


def _moe_kernel(win, sb, edge_ref, hx_ref, slot_ref, wg_ref, wu_ref, wd_ref, ye_ref, xs_ref):
    b = pl.program_id(0)
    e = pl.program_id(1)
    ne = pl.num_programs(1)
    cap = xs_ref.shape[0]
    nblk = cap // sb
    nrow = hx_ref.shape[1] // TOKEN_ROW
    wrows = win // TOKEN_ROW
    base = (b * ne + e) * (nblk + 1)
    for s in range(nblk):
        blk = slice(s * sb, (s + 1) * sb)
        target = lax.broadcasted_iota(jnp.int32, (sb, TOKEN_ROW), 0) + s * sb
        r0 = jnp.minimum(edge_ref[base + s] // TOKEN_ROW, nrow - wrows)
        rows = slot_ref[0, 0, pl.ds(r0, wrows), :]
        onehot = jnp.concatenate([jnp.where(rows[k:k + 1, :] == target, 1.0, 0.0).astype(BF16)
                                  for k in range(wrows)], axis=1)
        t0 = pl.multiple_of(r0 * TOKEN_ROW, TOKEN_ROW)
        xs_ref[blk, :] = _dot(onehot, hx_ref[0, pl.ds(t0, win), :])

        def extra_row(r, carry, blk=blk, target=target):
            hit = jnp.where(slot_ref[0, 0, pl.ds(r, 1), :] == target, 1.0, 0.0).astype(BF16)
            tr = pl.multiple_of(r * TOKEN_ROW, TOKEN_ROW)
            xs_ref[blk, :] += _dot(hit, hx_ref[0, pl.ds(tr, TOKEN_ROW), :])
            return carry

        r_end = (edge_ref[base + s + 1] + TOKEN_ROW - 1) // TOKEN_ROW
        lax.fori_loop(r0 + wrows, r_end, extra_row, 0)

    xs = xs_ref[...].astype(BF16)
    h1 = _dot(xs, wg_ref[0].astype(BF16))
    h2 = _dot(xs, wu_ref[0].astype(BF16))
    hid = (h1 * jax.nn.sigmoid(h1) * h2).astype(BF16)
    ye_ref[0, 0] = _dot(hid, wd_ref[0].astype(BF16)).astype(BF16)


def _moe(edges, hx, slot4, wg, wu, wd, cap, win, sb):
    bsz, t, d = hx.shape
    ne, _, f = wg.shape
    nrow = t // TOKEN_ROW
    grid_spec = pltpu.PrefetchScalarGridSpec(
        num_scalar_prefetch=1,
        grid=(bsz, ne),
        in_specs=[pl.BlockSpec((1, t, d), lambda b, e, s: (b, 0, 0), pipeline_mode=pl.Buffered(1)),
                  pl.BlockSpec((1, 1, nrow, TOKEN_ROW), lambda b, e, s: (b, e, 0, 0)),
                  pl.BlockSpec((1, d, f), lambda b, e, s: (e, 0, 0)),
                  pl.BlockSpec((1, d, f), lambda b, e, s: (e, 0, 0)),
                  pl.BlockSpec((1, f, d), lambda b, e, s: (e, 0, 0))],
        out_specs=pl.BlockSpec((1, 1, cap, d), lambda b, e, s: (b, e, 0, 0)),
        scratch_shapes=[pltpu.VMEM((cap, d), F32)],
    )
    return pl.pallas_call(
        functools.partial(_moe_kernel, win, sb),
        grid_spec=grid_spec,
        out_shape=jax.ShapeDtypeStruct((bsz, ne, cap, d), BF16),
        compiler_params=_cparams(("parallel", "arbitrary")),
        name="moe",
    )(edges, hx, slot4, wg, wu, wd)


COMB_ROWS = 128
COMB_WIN = 64
SLOT_ALIGN = 16


def _comb_rows(win, groups, xm_ref, ye_ref, slotc, affc, g2_ref, fg_ref, o_ref):
    ne, cap = ye_ref.shape[1], ye_ref.shape[2]
    wide = ne * win
    lane_w = lax.broadcasted_iota(jnp.int32, (ne, wide), 1)
    expand = jnp.where(lane_w // win == lax.broadcasted_iota(jnp.int32, (ne, wide), 0), 1.0, 0.0).astype(BF16)
    slot_in_win = (lax.broadcasted_iota(jnp.int32, (COMB_ROWS, wide), 1) % win).astype(F32)
    lane_e = lax.broadcasted_iota(jnp.int32, (1, ne), 1)
    lhs, rhs = [], []
    for h, first in groups:
        rows = slice(h * COMB_ROWS, (h + 1) * COMB_ROWS)
        starts = [pl.multiple_of(jnp.minimum(first[e] & ~(SLOT_ALIGN - 1), cap - win), SLOT_ALIGN) for e in range(ne)]
        start_row = jnp.zeros((1, ne), jnp.int32)
        for e in range(ne):
            start_row = jnp.where(lane_e == e, starts[e], start_row)
        offset = _dot((slotc[rows, :] - start_row).astype(F32).astype(BF16), expand)
        val_hi, val_lo = _split2(affc[rows, :])
        hit = offset == slot_in_win
        hi = jnp.where(hit, _dot(val_hi, expand), 0.0).astype(BF16)
        lo = jnp.where(hit, _dot(val_lo, expand), 0.0).astype(BF16)
        lhs.append(jnp.concatenate([hi, lo], axis=0))
        rhs.append(jnp.concatenate([ye_ref[0, e, pl.ds(starts[e], win), :] for e in range(ne)], axis=0))
    both = [_dot(a, y) for a, y in zip(lhs, rhs)]
    for (h, _), bt in zip(groups, both):
        rows = slice(h * COMB_ROWS, (h + 1) * COMB_ROWS)
        x = xm_ref[0, rows, :] + g2_ref[0] * (bt[:COMB_ROWS] + bt[COMB_ROWS:])
        ms = jnp.mean(x * x, axis=-1, keepdims=True)
        o_ref[0, rows, :] = x * lax.rsqrt(ms + NORM_EPS) * fg_ref[...]


def _comb_kernel(tk, win, full, tsp_ref, xm_ref, ye_ref, slotc_ref, affc_ref, g2_ref, fg_ref, o_ref):
    b = pl.program_id(0)
    j = pl.program_id(1)
    nh = tk // COMB_ROWS
    ntile = pl.num_programs(1) * nh + 1
    ne, cap = ye_ref.shape[1], ye_ref.shape[2]
    slotc = slotc_ref[0]
    affc = affc_ref[0]
    refs = (xm_ref, ye_ref, slotc, affc, g2_ref, fg_ref, o_ref)
    groups = [(h, [tsp_ref[(b * ne + e) * ntile + j * nh + h] for e in range(ne)]) for h in range(nh)]
    _comb_rows(win, groups, *refs)
    if full != win:
        for h, first in groups:
            over = None
            for e in range(ne):
                start = jnp.minimum(first[e] & ~(SLOT_ALIGN - 1), cap - win)
                miss = tsp_ref[(b * ne + e) * ntile + j * nh + h + 1] > start + win
                over = miss if over is None else over | miss

            @pl.when(over)
            def _(h=h, first=first):
                _comb_rows(full, [(h, first)], *refs)


def _comb(tsp, xm, ye, slotc, affc, g2, fg, tk):
    bsz, t, d = xm.shape
    ne, cap = ye.shape[1], ye.shape[2]
    full = min(2 * COMB_ROWS, cap)
    assert full == cap or full >= COMB_ROWS + SLOT_ALIGN
    assert full <= 256
    win = min(COMB_WIN, full)
    grid_spec = pltpu.PrefetchScalarGridSpec(
        num_scalar_prefetch=1,
        grid=(bsz, t // tk),
        in_specs=[pl.BlockSpec((1, tk, d), lambda b, j, s: (b, j, 0)),
                  pl.BlockSpec((1, ne, cap, d), lambda b, j, s: (b, 0, 0, 0)),
                  pl.BlockSpec((1, tk, ne), lambda b, j, s: (b, j, 0)),
                  pl.BlockSpec((1, tk, ne), lambda b, j, s: (b, j, 0)),
                  pl.BlockSpec((1, 1, d), lambda b, j, s: (b, 0, 0)),
                  pl.BlockSpec((1, d), lambda b, j, s: (0, 0))],
        out_specs=pl.BlockSpec((1, tk, d), lambda b, j, s: (b, j, 0)),
    )
    return pl.pallas_call(
        functools.partial(_comb_kernel, tk, win, full),
        grid_spec=grid_spec,
        out_shape=jax.ShapeDtypeStruct((bsz, t, d), F32),
        compiler_params=_cparams(("parallel", "arbitrary")),
        name="comb",
    )(tsp, xm, ye, slotc, affc, g2, fg)


def _block_diag_ones(n, blk, value=1.0):
    r = jnp.arange(n)
    return jnp.where((r[:, None] // blk) == (r[None, :] // blk), value, 0.0)


def kernel(x, c, ctx, c_ctx, ada_w, ada_b, norm1_g, norm2_g, w_in, shift_mu, w0, w_lora_up, a0, a_lora_up, k_k, k_a,
           r_k, g_lora_up, gn_g, gn_b, conv_w, w_out, router_w, exp_w_gate, exp_w_up, exp_w_down, final_g):
    bsz, t, d = x.shape
    lc = ctx.shape[1]
    ne = router_w.shape[-1]
    cap = EC_CAPACITY * t // ne
    sb = min(SLOT_BLOCK, cap)
    win = min(GATHER_WIN, t)
    l = 0

    rows = ((bsz + 1 + 7) // 8) * 8
    cc = jnp.zeros((rows, d), F32).at[:bsz].set(c).at[bsz].set(c_ctx)
    mod = _mod(cc, ada_w[l], ada_b[l][None, :])
    sh1, sc1, g1, sh2, sc2, g2 = (m[:, None, :] for m in jnp.split(mod[:bsz], 6, axis=-1))
    csh1, csc1 = (jnp.broadcast_to(m[None, None, :], (bsz, 1, d)) for m in jnp.split(mod[bsz], 6)[:2])

    w_rw = w_in[l][:, :RWKV_COLS].astype(BF16)
    w_cv = w_in[l][:, RWKV_COLS:].astype(BF16)
    n1g = norm1_g[l][None, :]
    px_rw, px_cv = _in_proj(x, sh1, sc1, n1g, w_rw, w_cv, IN_PROJ_ROWS)
    pc_rw, _ = _in_proj(ctx, csh1, csc1, n1g, w_rw, w_cv, min(IN_PROJ_ROWS, lc))

    zw = jnp.zeros((LORA_W, 2 * D_RWKV), F32)
    lora = jnp.concatenate([
        jnp.concatenate([w_lora_up[l, 0], w_lora_up[l, 1], zw], axis=1),
        jnp.concatenate([zw, a_lora_up[l, 0], a_lora_up[l, 1]], axis=1)], axis=0).astype(BF16)
    ridx = jnp.arange(CHUNK)
    tril = jnp.where(ridx[None, :] <= ridx[:, None], 1.0, 0.0).astype(BF16)
    triu = jnp.where(ridx[None, :] >= ridx[:, None], 1.0, 0.0).astype(BF16)
    ones_bd = _block_diag_ones(PAIR, HEAD).astype(BF16)
    consts = (shift_mu[l][None, :], lora, w0[l], a0[l], k_k[l][None, :], k_a[l][None, :],
              r_k[l].reshape(1, D_RWKV), g_lora_up[l].astype(BF16), ones_bd, tril, triu)

    chunks_c = _prep(pc_rw, consts, False, lc)[:5]
    *chunks_x, bonus, gate = _prep(px_rw, consts, True, PREP_ROWS)

    s_zero = jnp.zeros((bsz, 2, N_PAIR, HEAD, PAIR), F32)
    _, _, s_ctx = _scan(*chunks_c, s_zero)
    yf, yb, _ = _scan(*chunks_x, s_ctx)

    rw_t = router_w[l].T
    rwh = rw_t.astype(BF16)
    rwl = (rw_t - rwh.astype(F32)).astype(BF16)
    avg = _block_diag_ones(PAIR, HEAD, 1.0 / HEAD).astype(BF16)
    xm, hx, aff_t = _out(yf, yb, bonus, gate, px_cv, x, g1, sh2, sc2, norm2_g[l][None, :], gn_g[l][None, :],
                         gn_b[l][None, :], conv_w[l], w_out[l].astype(BF16), rwh, rwl, avg, OUT_TILE)

    lane_idx = jnp.arange(TOKEN_ROW)
    tri_lanes = jnp.where(lane_idx[:, None] <= lane_idx[None, :], 1.0, 0.0).astype(BF16)
    cnt, slot, edges = _topk(aff_t, tri_lanes, cap, sb)

    ye = _moe(edges[:, :, :cap // sb + 1].reshape(-1), hx, slot.reshape(bsz, ne, t // TOKEN_ROW, TOKEN_ROW),
              exp_w_gate[l], exp_w_up[l], exp_w_down[l], cap, win, sb)
    tr = lambda a: jnp.transpose(a, (0, 2, 1))
    first_slot = jnp.concatenate([cnt[:, :, ::COMB_ROWS], jnp.full((bsz, ne, 1), cap, jnp.int32)], axis=-1).reshape(-1)
    return _comb(first_slot, xm, ye, tr(slot), tr(aff_t), g2, final_g[None, :], COMB_TILE)
```

```python
import functools
import math

import jax
import jax.numpy as jnp
from jax import lax
from jax.experimental import pallas as pl
from jax.experimental.pallas import tpu as pltpu

F32 = jnp.float32
BF16 = jnp.bfloat16
HIGHEST = lax.Precision.HIGHEST

GRID_W = 64
D_RWKV = 512
D_CONV = 512
HEAD = 64
LORA_W = 64
LORA_A = 64
LORA_G = 128
EC_CAPACITY = 2
NORM_EPS = 1e-6
GN_EPS = 64e-5
RWKV_COLS = 3 * D_RWKV + LORA_W + LORA_A + LORA_G
CONV_COLS = 3 * D_CONV

LANES = 128
CHUNK = 64
PAIR = 2 * HEAD
N_PAIR = D_RWKV // PAIR
VMEM_LIMIT = 48 * 1024 * 1024

IN_PROJ_ROWS = 512
PREP_ROWS = 512
OUT_TILE = 512
COMB_TILE = 512
SLOT_BLOCK = 128
GATHER_WIN = 1536


def _cparams(sem):
    return pltpu.CompilerParams(dimension_semantics=sem, vmem_limit_bytes=VMEM_LIMIT)


def _dot(a, b):
    return jnp.dot(a, b, preferred_element_type=F32)


def _dot_nt(a, b):
    return lax.dot_general(a, b, (((1,), (1,)), ((), ())), preferred_element_type=F32)


def _split2(x):
    hi = x.astype(BF16)
    lo = (x - hi.astype(F32)).astype(BF16)
    return hi, lo


def _split3(x):
    hi = x.astype(BF16)
    r = x - hi.astype(F32)
    mid = r.astype(BF16)
    lo = (r - mid.astype(F32)).astype(BF16)
    return hi, mid, lo


def _seg_dot(x, m):
    hi, lo = _split2(x)
    return _dot(hi, m) + _dot(lo, m)


def _mod_kernel(c_ref, w_ref, b_ref, o_ref):
    c = c_ref[...]
    s = c * jax.nn.sigmoid(c)
    o_ref[...] = jnp.dot(s, w_ref[...], precision=HIGHEST, preferred_element_type=F32) + b_ref[...]


def _mod(cc, w, b):
    rows, d = cc.shape
    n = w.shape[1]
    tn = 1024
    return pl.pallas_call(
        _mod_kernel,
        grid=(n // tn,),
        in_specs=[pl.BlockSpec((rows, d), lambda j: (0, 0)),
                  pl.BlockSpec((d, tn), lambda j: (0, j)),
                  pl.BlockSpec((1, tn), lambda j: (0, j))],
        out_specs=pl.BlockSpec((rows, tn), lambda j: (0, j)),
        out_shape=jax.ShapeDtypeStruct((rows, n), F32),
        compiler_params=_cparams(("parallel",)),
        name="mod",
    )(cc, w, b)


def _in_proj_kernel(x_ref, sh_ref, sc_ref, g_ref, wrw_ref, wcv_ref, orw_ref, ocv_ref):
    x = x_ref[0]
    ms = jnp.mean(x * x, axis=-1, keepdims=True)
    h = x * lax.rsqrt(ms + NORM_EPS) * g_ref[...]
    h = (h * (1.0 + sc_ref[0]) + sh_ref[0]).astype(BF16)
    orw_ref[0] = _dot(h, wrw_ref[...])
    ocv_ref[0] = _dot(h, wcv_ref[...])


def _in_proj(x, sh, sc, g, w_rw, w_cv, tm):
    bsz, l, d = x.shape
    return pl.pallas_call(
        _in_proj_kernel,
        grid=(bsz, l // tm),
        in_specs=[pl.BlockSpec((1, tm, d), lambda b, i: (b, i, 0)),
                  pl.BlockSpec((1, 1, d), lambda b, i: (b, 0, 0)),
                  pl.BlockSpec((1, 1, d), lambda b, i: (b, 0, 0)),
                  pl.BlockSpec((1, d), lambda b, i: (0, 0)),
                  pl.BlockSpec((d, RWKV_COLS), lambda b, i: (0, 0)),
                  pl.BlockSpec((d, CONV_COLS), lambda b, i: (0, 0))],
        out_specs=[pl.BlockSpec((1, tm, RWKV_COLS), lambda b, i: (b, i, 0)),
                   pl.BlockSpec((1, tm, CONV_COLS), lambda b, i: (b, i, 0))],
        out_shape=[jax.ShapeDtypeStruct((bsz, l, RWKV_COLS), F32),
                   jax.ShapeDtypeStruct((bsz, l, CONV_COLS), F32)],
        compiler_params=_cparams(("parallel", "parallel")),
        name="in_proj",
    )(x, sh, sc, g, w_rw, w_cv)


def _bf(x):
    return x.astype(BF16)


def _chunk_local(chains):
    c = CHUNK
    n = range(len(chains))
    lane = lax.broadcasted_iota(jnp.int32, (c, PAIR), 1)
    h0 = lane < HEAD
    tcol = lane & (c - 1)
    trow = lax.broadcasted_iota(jnp.int32, (c, PAIR), 0)
    eye = jnp.where(tcol == trow, 1.0, 0.0)
    masks = {rev: ((tcol > trow) if rev else (tcol < trow), (tcol >= trow) if rev else (tcol <= trow))
             for rev in (False, True)}
    zb = jnp.zeros((c, PAIR), BF16)

    def stack(x):
        return jnp.concatenate([jnp.where(h0, x, zb), jnp.where(h0, zb, x)], axis=0)

    gram = [_dot_nt(jnp.concatenate([ch[1], ch[2]], axis=0), jnp.concatenate([stack(ch[3]), stack(ch[4])], axis=0))
            for ch in chains]
    lab = [jnp.where(masks[chains[i][0]][0], gram[i][:c, :PAIR], 0.0) for i in n]
    lak = [jnp.where(masks[chains[i][0]][0], gram[i][:c, PAIR:], 0.0) for i in n]
    mrb = [jnp.where(masks[chains[i][0]][1], gram[i][c:, :PAIR], 0.0) for i in n]
    mrk = [jnp.where(masks[chains[i][0]][1], gram[i][c:, PAIR:], 0.0) for i in n]

    labb = [_bf(x) for x in lab]
    pw = [_dot(labb[i], stack(labb[i])) for i in n]
    tp = [eye + lab[i] for i in n]
    for _ in range(4):
        pwb = [_bf(x) for x in pw]
        both = [_dot(pwb[i], jnp.concatenate([stack(pwb[i]), stack(_bf(tp[i]))], axis=1)) for i in n]
        pw = [x[:, :PAIR] for x in both]
        tp = [tp[i] + both[i][:, PAIR:] for i in n]
    tinv = [tp[i] + _dot(_bf(pw[i]), stack(_bf(tp[i]))) for i in n]

    lmv = [_dot(_bf(jnp.concatenate([lak[i], mrk[i]], axis=0)), stack(chains[i][7])) for i in n]
    x = [_dot(_bf(tinv[i]), jnp.concatenate([stack(chains[i][1]), stack(_bf(lmv[i][:c]))], axis=1))
         for i in n]
    z = [_dot(_bf(mrb[i]), jnp.concatenate([stack(_bf(x[i][:, :PAIR])), stack(_bf(x[i][:, PAIR:]))], axis=1))
         for i in n]
    rbar = [_bf(chains[i][2].astype(F32) + z[i][:, :PAIR]) for i in n]
    y0 = [z[i][:, PAIR:] + lmv[i][c:] for i in n]

    uv = [jnp.concatenate([x[i][:, PAIR:], chains[i][7].astype(F32)], axis=0) for i in n]
    mxf = [_dot(_bf(x[i][:, :PAIR].T), chains[i][5]) for i in n]
    nnf = [_dot(_bf(uv[i].T), jnp.concatenate([chains[i][5], chains[i][6]], axis=0)) for i in n]
    mx = [_bf(jnp.where(h0, m[:c], m[c:])) for m in mxf]
    nn = [jnp.where(h0, m[:c], m[c:]) for m in nnf]
    return list(zip(rbar, y0, mx, nn))


DIR_COLS = 6 * D_RWKV
SCR_COLS = 2 * DIR_COLS + D_RWKV


def _head_sums(x, ones):
    return jnp.concatenate([_seg_dot(x[:, g * PAIR:(g + 1) * PAIR], ones) for g in range(N_PAIR)], axis=1)


def _prep_rows(c, p, shifted, lo, mu_ref, w0_ref, a0_ref, kk_ref, ka_ref, rk_ref, ones_ref, tril_ref, triu_ref,
               wc_ref, bonus_ref, scr_ref):
    rows = slice(c * CHUNK, (c + 1) * CHUNK)
    pm = p + mu_ref[:, 0:3 * D_RWKV] * (shifted - p)
    xr = pm[:, 0:D_RWKV]
    xk = pm[:, D_RWKV:2 * D_RWKV]
    xv = pm[:, 2 * D_RWKV:3 * D_RWKV]

    ones = ones_ref[...]
    kraw = xk * kk_ref[...]
    kk = kraw / jnp.maximum(jnp.sqrt(_head_sums(kraw * kraw, ones)), 1e-12)

    scr_ref[:, 2 * DIR_COLS:] = xv.astype(BF16)
    ksum = None
    for d in range(2):
        z = w0_ref[d:d + 1, :] + lo[:, d * D_RWKV:(d + 1) * D_RWKV]
        ld = -math.exp(-0.5) * jax.nn.sigmoid(z)
        ag = jax.nn.sigmoid(a0_ref[d:d + 1, :] + lo[:, (2 + d) * D_RWKV:(3 + d) * D_RWKV])
        kd = xk * (1.0 + (ag - 1.0) * ka_ref[...])
        bb = kk * ag
        ksum = kd if ksum is None else ksum + kd
        tri = tril_ref[...] if d == 0 else triu_ref[...]
        h3 = _split3(ld)
        cs = _dot(tri, h3[0]) + _dot(tri, h3[1]) + _dot(tri, h3[2])
        e_in = jnp.exp(cs)
        e_ex = jnp.exp(cs - ld)
        e_neg = jnp.exp(-cs)
        bt = bb * e_neg
        kt = kd * e_neg
        edge = CHUNK - 1 if d == 0 else 0
        wrow = e_in[edge:edge + 1, :]
        wc_ref[0, c, :, d * D_RWKV:(d + 1) * D_RWKV] = wrow
        base = d * DIR_COLS
        for j, val in enumerate((-kk * e_ex, xr * e_in, bt, kt, bt * wrow, kt * wrow)):
            scr_ref[:, base + j * D_RWKV:base + (j + 1) * D_RWKV] = val.astype(BF16)

    bonus_ref[0, rows, :] = (_head_sums(xr * ksum * rk_ref[...], ones) * xv).astype(BF16)


PREP_GROUP = 2


def _prep_chunks(cs, scr_refs, rbar_ref, y0_ref, mx_ref, nn_ref):
    chains = []
    for scr_ref in scr_refs:
        for d in range(2):
            for pp in range(N_PAIR):
                cols = [d * DIR_COLS + j * D_RWKV + pp * PAIR for j in range(6)] + [2 * DIR_COLS + pp * PAIR]
                chains.append((d == 1,) + tuple(scr_ref[:, k:k + PAIR] for k in cols))
    per_chunk = 2 * N_PAIR
    for idx, (rbar, y0, mx, nn) in enumerate(_chunk_local(chains)):
        c = cs[idx // per_chunk]
        rows = slice(c * CHUNK, (c + 1) * CHUNK)
        col = (idx % per_chunk // N_PAIR) * D_RWKV + (idx % N_PAIR) * PAIR
        rbar_ref[0, rows, col:col + PAIR] = rbar
        y0_ref[0, rows, col:col + PAIR] = y0
        mx_ref[0, c, :, col:col + PAIR] = mx
        nn_ref[0, c, :, col:col + PAIR] = nn


N_PREP_CONSTS = 11


def _prep_kernel(latent, tt, p_ref, *rest):
    if latent:
        prev_ref, next_ref = rest[:2]
        rest = rest[2:]
    (mu_ref, lw_ref, w0_ref, a0_ref, kk_ref, ka_ref, rk_ref, gup_ref, ones_ref, tril_ref,
     triu_ref) = rest[:N_PREP_CONSTS]
    rbar_ref, y0_ref, mx_ref, nn_ref, wc_ref, bonus_ref, g_ref = rest[N_PREP_CONSTS:N_PREP_CONSTS + 7]
    *scr, lo_ref, ext_ref = rest[N_PREP_CONSTS + 7:]
    if latent:
        i = pl.program_id(1)
        n = pl.num_programs(1)
        ext_ref[0:GRID_W] = jnp.where(i > 0, prev_ref[0], 0.0)
        ext_ref[GRID_W:GRID_W + tt] = p_ref[0]
        ext_ref[GRID_W + tt:] = jnp.where(i < n - 1, next_ref[0], 0.0)

        def shifted(r0, nrows, cols):
            shape = (nrows, cols.stop - cols.start)
            trow = lax.broadcasted_iota(jnp.int32, shape, 0) & (GRID_W - 1)
            q = lax.broadcasted_iota(jnp.int32, shape, 1) & 3
            left = jnp.where(trow == 0, 0.0, ext_ref[GRID_W - 1 + r0:GRID_W - 1 + r0 + nrows, cols])
            right = jnp.where(trow == GRID_W - 1, 0.0, ext_ref[GRID_W + 1 + r0:GRID_W + 1 + r0 + nrows, cols])
            up = ext_ref[r0:r0 + nrows, cols]
            down = ext_ref[2 * GRID_W + r0:2 * GRID_W + r0 + nrows, cols]
            return jnp.where(q == 0, left, jnp.where(q == 1, right, jnp.where(q == 2, up, down)))
    else:
        ext_ref[0:8] = jnp.zeros((8, RWKV_COLS), F32)
        ext_ref[8:8 + tt] = p_ref[0]
        ext_ref[8 + tt:] = jnp.zeros((8, RWKV_COLS), F32)

        def shifted(r0, nrows, cols):
            q = lax.broadcasted_iota(jnp.int32, (nrows, cols.stop - cols.start), 1) & 1
            return jnp.where(q == 0, ext_ref[7 + r0:7 + r0 + nrows, cols], ext_ref[9 + r0:9 + r0 + nrows, cols])

    tail = slice(3 * D_RWKV, RWKV_COLS)
    p_t = p_ref[0, :, tail]
    pm_t = p_t + mu_ref[:, tail] * (shifted(0, tt, tail) - p_t)
    xwa = pm_t[:, 0:LORA_W + LORA_A]
    lane = lax.broadcasted_iota(jnp.int32, xwa.shape, 1)
    lin = jnp.where(lane < LORA_W, jnp.tanh(xwa), xwa).astype(BF16)
    lo_ref[...] = _dot(lin, lw_ref[...])
    g_ref[0] = _dot(jax.nn.sigmoid(pm_t[:, LORA_W + LORA_A:]).astype(BF16), gup_ref[...]).astype(BF16)

    head = slice(0, 3 * D_RWKV)

    def rows_part(c):
        rows = slice(c * CHUNK, (c + 1) * CHUNK)
        _prep_rows(c, p_ref[0, rows, head], shifted(c * CHUNK, CHUNK, head), lo_ref[rows, :], mu_ref, w0_ref,
                   a0_ref, kk_ref, ka_ref, rk_ref, ones_ref, tril_ref, triu_ref, wc_ref, bonus_ref,
                   scr[c % (2 * PREP_GROUP)])

    groups = [list(range(g, g + PREP_GROUP)) for g in range(0, tt // CHUNK, PREP_GROUP)]
    for c in groups[0]:
        rows_part(c)
    for k, cs in enumerate(groups):
        if k + 1 < len(groups):
            for c in groups[k + 1]:
                rows_part(c)
        _prep_chunks(cs, [scr[c % (2 * PREP_GROUP)] for c in cs], rbar_ref, y0_ref, mx_ref, nn_ref)


def _prep(p_rw, consts, latent, tt):
    bsz, l, _ = p_rw.shape
    nc = l // CHUNK
    cpt = tt // CHUNK
    assert len(consts) == N_PREP_CONSTS
    const_specs = [pl.BlockSpec(c.shape, lambda b, i, nd=c.ndim: (0,) * nd) for c in consts]
    kern = functools.partial(_prep_kernel, latent, tt)
    if latent:
        nblk = l // GRID_W
        in_specs = [pl.BlockSpec((1, tt, RWKV_COLS), lambda b, i: (b, i, 0)),
                    pl.BlockSpec((1, GRID_W, RWKV_COLS),
                                 lambda b, i: (b, jnp.maximum(i * (tt // GRID_W) - 1, 0), 0)),
                    pl.BlockSpec((1, GRID_W, RWKV_COLS),
                                 lambda b, i: (b, jnp.minimum((i + 1) * (tt // GRID_W), nblk - 1), 0))]
        args = (p_rw, p_rw, p_rw)
        ext_rows = tt + 2 * GRID_W
    else:
        assert tt == l
        in_specs = [pl.BlockSpec((1, tt, RWKV_COLS), lambda b, i: (b, i, 0))]
        args = (p_rw,)
        ext_rows = tt + 16
    row = lambda b, i: (b, i, 0)
    chunk = lambda b, i: (b, i, 0, 0)
    w2 = 2 * D_RWKV
    return pl.pallas_call(
        kern,
        grid=(bsz, l // tt),
        in_specs=in_specs + const_specs,
        out_specs=[pl.BlockSpec((1, tt, w2), row), pl.BlockSpec((1, tt, w2), row),
                   pl.BlockSpec((1, cpt, CHUNK, w2), chunk), pl.BlockSpec((1, cpt, CHUNK, w2), chunk),
                   pl.BlockSpec((1, cpt, 1, w2), chunk),
                   pl.BlockSpec((1, tt, D_RWKV), row), pl.BlockSpec((1, tt, D_RWKV), row)],
        out_shape=[jax.ShapeDtypeStruct((bsz, l, w2), BF16), jax.ShapeDtypeStruct((bsz, l, w2), F32),
                   jax.ShapeDtypeStruct((bsz, nc, CHUNK, w2), BF16), jax.ShapeDtypeStruct((bsz, nc, CHUNK, w2), F32),
                   jax.ShapeDtypeStruct((bsz, nc, 1, w2), F32),
                   jax.ShapeDtypeStruct((bsz, l, D_RWKV), BF16), jax.ShapeDtypeStruct((bsz, l, D_RWKV), BF16)],
        scratch_shapes=[pltpu.VMEM((CHUNK, SCR_COLS), BF16)] * (2 * PREP_GROUP)
        + [pltpu.VMEM((tt, 4 * D_RWKV), F32), pltpu.VMEM((ext_rows, RWKV_COLS), F32)],
        compiler_params=_cparams(("parallel", "parallel")),
        name="prep_latent" if latent else "prep_ctx",
    )(*args, *consts)


SCAN_CHUNKS = 4
SCAN_BATCH = 4


def _scan_kernel(rbf_ref, rbb_ref, y0f_ref, y0b_ref, mxf_ref, mxb_ref, nnf_ref, nnb_ref, wcf_ref, wcb_ref,
                 s0_ref, yf_ref, yb_ref, s_ref):
    i = pl.program_id(1)

    @pl.when(i == 0)
    def _():
        s_ref[...] = s0_ref[...]

    c = CHUNK
    h0 = lax.broadcasted_iota(jnp.int32, (c, PAIR), 1) < HEAD
    zb = jnp.zeros((c, PAIR), BF16)

    def stack(x):
        return jnp.concatenate([jnp.where(h0, x, zb), jnp.where(h0, zb, x)], axis=0)

    dirs = ((rbf_ref, y0f_ref, mxf_ref, nnf_ref, wcf_ref, yf_ref), (rbb_ref, y0b_ref, mxb_ref, nnb_ref, wcb_ref, yb_ref))
    idx = [(n, d, p) for n in range(SCAN_BATCH) for d in range(2) for p in range(N_PAIR)]
    lanes = [slice(p * PAIR, (p + 1) * PAIR) for _, _, p in idx]
    s = [s_ref[n, d, p] for n, d, p in idx]
    for q in range(SCAN_CHUNKS):
        cq = (q, SCAN_CHUNKS - 1 - q)
        rows = [slice(cq[d] * c, (cq[d] + 1) * c) for _, d, _ in idx]
        sb = [_bf(x) for x in s]
        y = [_dot_nt(dirs[d][0][n, rows[k], lanes[k]], stack(sb[k])) for k, (n, d, _) in enumerate(idx)]
        sm = [_dot(sb[k], stack(dirs[d][2][n, cq[d], :, lanes[k]])) for k, (n, d, _) in enumerate(idx)]
        s_next = []
        for k, (n, d, p) in enumerate(idx):
            dirs[d][5][n, rows[k], lanes[k]] = y[k] + dirs[d][1][n, rows[k], lanes[k]]
            s_next.append(s[k] * dirs[d][4][n, cq[d], :, lanes[k]] + sm[k] + dirs[d][3][n, cq[d], :, lanes[k]])
        s = s_next
    for k, (n, d, p) in enumerate(idx):
        s_ref[n, d, p] = s[k]


def _scan(rbar, y0, mx, nn, wc, s0):
    bsz, l, _ = rbar.shape
    sub, nb = SCAN_CHUNKS, SCAN_BATCH
    assert bsz % nb == 0 and l % (CHUNK * sub) == 0
    nc = l // (CHUNK * sub)
    row_f = pl.BlockSpec((nb, sub * CHUNK, D_RWKV), lambda b, i: (b, i, 0))
    row_b = pl.BlockSpec((nb, sub * CHUNK, D_RWKV), lambda b, i: (b, nc - 1 - i, 1))
    chk_f = pl.BlockSpec((nb, sub, CHUNK, D_RWKV), lambda b, i: (b, i, 0, 0))
    chk_b = pl.BlockSpec((nb, sub, CHUNK, D_RWKV), lambda b, i: (b, nc - 1 - i, 0, 1))
    wc_f = pl.BlockSpec((nb, sub, 1, D_RWKV), lambda b, i: (b, i, 0, 0))
    wc_b = pl.BlockSpec((nb, sub, 1, D_RWKV), lambda b, i: (b, nc - 1 - i, 0, 1))
    st_spec = pl.BlockSpec((nb, 2, N_PAIR, HEAD, PAIR), lambda b, i: (b, 0, 0, 0, 0))
    return pl.pallas_call(
        _scan_kernel,
        grid=(bsz // nb, nc),
        in_specs=[row_f, row_b, row_f, row_b, chk_f, chk_b, chk_f, chk_b, wc_f, wc_b, st_spec],
        out_specs=[row_f, pl.BlockSpec((nb, sub * CHUNK, D_RWKV), lambda b, i: (b, nc - 1 - i, 0)), st_spec],
        out_shape=[jax.ShapeDtypeStruct((bsz, l, D_RWKV), F32),
                   jax.ShapeDtypeStruct((bsz, l, D_RWKV), F32),
                   jax.ShapeDtypeStruct((bsz, 2, N_PAIR, HEAD, PAIR), F32)],
        compiler_params=_cparams(("parallel", "arbitrary")),
        name="scan",
    )(rbar, rbar, y0, y0, mx, mx, nn, nn, wc, wc, s0)


OUT_ROWS = 128


def _out_kernel(tt, yf_ref, yb_ref, bonus_ref, g_ref, cv_ref, cvp_ref, cvn_ref, x_ref, g1_ref, sh2_ref,
                sc2_ref, n2g_ref, gng_ref, gnb_ref, convw_ref, wout_ref, rwh_ref, rwl_ref, avg_ref,
                xm_ref, hx_ref, aff_ref):
    i = pl.program_id(1)
    n = pl.num_programs(1)

    cv = cv_ref[0]
    b_gate = cv[:, 0:D_CONV]
    cu = cv[:, D_CONV:2 * D_CONV] * cv[:, 2 * D_CONV:]
    cvp = cvp_ref[0]
    cvn = cvn_ref[0]
    cu_prev = jnp.where(i > 0, cvp[7:8, D_CONV:2 * D_CONV] * cvp[7:8, 2 * D_CONV:], 0.0)
    cu_next = jnp.where(i < n - 1, cvn[0:1, D_CONV:2 * D_CONV] * cvn[0:1, 2 * D_CONV:], 0.0)
    ridx = lax.broadcasted_iota(jnp.int32, cu.shape, 0)
    cu_m1 = jnp.where(ridx == 0, cu_prev, pltpu.roll(cu, 1, 0))
    cu_p1 = jnp.where(ridx == tt - 1, cu_next, pltpu.roll(cu, tt - 1, 0))
    conv = convw_ref[0:1, :] * cu_m1 + convw_ref[1:2, :] * cu + convw_ref[2:3, :] * cu_p1
    bx = (b_gate * conv).astype(BF16)

    parts = [slice(k * OUT_ROWS, (k + 1) * OUT_ROWS) for k in range(tt // OUT_ROWS)]
    avg = avg_ref[...]
    rwh = rwh_ref[...]
    rwl = rwl_ref[...]
    y = [yf_ref[0, r, :] + yb_ref[0, r, :] for r in parts]
    mu = [_head_sums(v, avg) for v in y]
    dlt = [a - b for a, b in zip(y, mu)]
    var = [_head_sums(v * v, avg) for v in dlt]
    yn = [a * lax.rsqrt(b + GN_EPS) * gng_ref[...] + gnb_ref[...] for a, b in zip(dlt, var)]
    ax = [((a + bonus_ref[0, r, :].astype(F32)) * g_ref[0, r, :].astype(F32)).astype(BF16) for a, r in zip(yn, parts)]
    mix = [_dot(a, wout_ref[0:D_RWKV, :]) + _dot(bx[r], wout_ref[D_RWKV:, :]) for a, r in zip(ax, parts)]
    xm = [x_ref[0, r, :] + g1_ref[0] * a for a, r in zip(mix, parts)]
    ms = [jnp.mean(v * v, axis=-1, keepdims=True) for v in xm]
    hx = [a * lax.rsqrt(b + NORM_EPS) * n2g_ref[...] for a, b in zip(xm, ms)]
    hx = [_split2(v * (1.0 + sc2_ref[0]) + sh2_ref[0]) for v in hx]
    logits = [_dot_nt(rwh, hi) + _dot_nt(rwh, lo) + _dot_nt(rwl, hi) for hi, lo in hx]
    for k, r in enumerate(parts):
        xm_ref[0, r, :] = xm[k]
        hx_ref[0, r, :] = hx[k][0]
        m = jnp.max(logits[k], axis=0, keepdims=True)
        ex = jnp.exp(logits[k] - m)
        aff_ref[0, :, r] = ex / jnp.sum(ex, axis=0, keepdims=True)


def _out(yf, yb, bonus, g, p_cv, x, g1, sh2, sc2, n2g, gng, gnb, convw, wout, rwh, rwl, avg, tt):
    bsz, t, d = x.shape
    ne = rwh.shape[0]
    nb8 = t // 8
    row = lambda b, i: (b, i, 0)
    per_b = lambda b, i: (b, 0, 0)
    const2 = lambda b, i: (0, 0)
    return pl.pallas_call(
        functools.partial(_out_kernel, tt),
        grid=(bsz, t // tt),
        in_specs=[pl.BlockSpec((1, tt, D_RWKV), row), pl.BlockSpec((1, tt, D_RWKV), row),
                  pl.BlockSpec((1, tt, D_RWKV), row), pl.BlockSpec((1, tt, D_RWKV), row),
                  pl.BlockSpec((1, tt, CONV_COLS), row),
                  pl.BlockSpec((1, 8, CONV_COLS), lambda b, i: (b, jnp.maximum(i * (tt // 8) - 1, 0), 0)),
                  pl.BlockSpec((1, 8, CONV_COLS), lambda b, i: (b, jnp.minimum((i + 1) * (tt // 8), nb8 - 1), 0)),
                  pl.BlockSpec((1, tt, d), row),
                  pl.BlockSpec((1, 1, d), per_b), pl.BlockSpec((1, 1, d), per_b), pl.BlockSpec((1, 1, d), per_b),
                  pl.BlockSpec((1, d), const2), pl.BlockSpec((1, D_RWKV), const2), pl.BlockSpec((1, D_RWKV), const2),
                  pl.BlockSpec((3, D_CONV), const2), pl.BlockSpec((D_RWKV + D_CONV, d), const2),
                  pl.BlockSpec((ne, d), const2), pl.BlockSpec((ne, d), const2),
                  pl.BlockSpec((PAIR, PAIR), const2)],
        out_specs=[pl.BlockSpec((1, tt, d), row), pl.BlockSpec((1, tt, d), row),
                   pl.BlockSpec((1, ne, tt), lambda b, i: (b, 0, i))],
        out_shape=[jax.ShapeDtypeStruct((bsz, t, d), F32), jax.ShapeDtypeStruct((bsz, t, d), BF16),
                   jax.ShapeDtypeStruct((bsz, ne, t), F32)],
        compiler_params=_cparams(("parallel", "parallel")),
        name="out",
    )(yf, yb, bonus, g, p_cv, p_cv, p_cv, x, g1, sh2, sc2, n2g, gng, gnb, convw, wout, rwh, rwl, avg)


def _prefix_blocks(mask_fn, t, tri, emit):
    carry = None
    for j in range(t // LANES):
        m = mask_fn(j)
        inc = _dot(m.astype(BF16), tri)
        carry = jnp.zeros_like(inc[:, 0:1]) if carry is None else carry
        emit(j, m, inc - m + carry)
        carry = carry + inc[:, LANES - 1:LANES]


def _topk_kernel(cap, sb, aff_ref, tri_ref, cnt_ref, slot_ref, edge_ref):
    t = aff_ref.shape[2]
    aff = aff_ref[0]

    def body(k, bits):
        cand = bits | jnp.left_shift(jnp.int32(1), 30 - k)
        cnt = jnp.sum(jnp.where(aff >= pltpu.bitcast(cand, F32), 1, 0), axis=-1, keepdims=True)
        return jnp.where(cnt >= cap, cand, bits)

    bits = lax.fori_loop(0, 31, body, jnp.zeros((aff.shape[0], 1), jnp.int32))
    thr = pltpu.bitcast(bits, F32)
    above = pltpu.bitcast(bits + 1, F32)
    n_gt = jnp.sum(jnp.where(aff >= above, 1, 0), axis=-1, keepdims=True)
    need = (cap - n_gt).astype(F32)
    tri = tri_ref[...]

    def blk(j):
        return aff[:, j * LANES:(j + 1) * LANES]

    def emit_sel(j, eq, before):
        take = (blk(j) >= above) | ((eq > 0.5) & (before < need))
        slot_ref[0, :, j * LANES:(j + 1) * LANES] = jnp.where(take, 1, 0)

    _prefix_blocks(lambda j: jnp.where((blk(j) >= thr) & (blk(j) < above), 1.0, 0.0), t, tri, emit_sel)

    def emit_slot(j, m, before):
        count = before.astype(jnp.int32)
        cnt_ref[0, :, j * LANES:(j + 1) * LANES] = count
        slot_ref[0, :, j * LANES:(j + 1) * LANES] = jnp.where(m > 0.5, count, -1)

    _prefix_blocks(lambda j: slot_ref[0, :, j * LANES:(j + 1) * LANES].astype(F32), t, tri, emit_slot)

    cnt = cnt_ref[0]
    lane = lax.broadcasted_iota(jnp.int32, (cnt.shape[0], LANES), 1)
    edges = jnp.zeros((cnt.shape[0], LANES), jnp.int32)
    for s in range(1, cap // sb + 1):
        below = jnp.sum(jnp.where(cnt < s * sb, 1, 0), axis=-1, keepdims=True)
        edges = jnp.where(lane == s, below, edges)
    edge_ref[0] = edges


def _topk(aff_t, tri, cap, sb):
    bsz, ne, t = aff_t.shape
    spec = pl.BlockSpec((1, ne, t), lambda b: (b, 0, 0))
    return pl.pallas_call(
        functools.partial(_topk_kernel, cap, sb),
        grid=(bsz,),
        in_specs=[spec, pl.BlockSpec((LANES, LANES), lambda b: (0, 0))],
        out_specs=[spec, spec, pl.BlockSpec((1, ne, LANES), lambda b: (b, 0, 0))],
        out_shape=[jax.ShapeDtypeStruct((bsz, ne, t), jnp.int32)] * 2 + [jax.ShapeDtypeStruct((bsz, ne, LANES), jnp.int32)],
        compiler_params=_cparams(("parallel",)),
        name="topk",
    )(aff_t, tri)


TOKEN_ROW = LANES


def _moe_kernel(win, sb, edge_ref, hx_ref, slot_ref, wg_ref, wu_ref, wd_ref, ye_ref, xs_ref):
    b = pl.program_id(0)
    e = pl.program_id(1)
    ne = pl.num_programs(1)
    cap = xs_ref.shape[0]
    nblk = cap // sb
    nrow = hx_ref.shape[1] // TOKEN_ROW
    wrows = win // TOKEN_ROW
    base = (b * ne + e) * (nblk + 1)
    for s in range(nblk):
        blk = slice(s * sb, (s + 1) * sb)
        target = lax.broadcasted_iota(jnp.int32, (sb, TOKEN_ROW), 0) + s * sb
        r0 = jnp.minimum(edge_ref[base + s] // TOKEN_ROW, nrow - wrows)
        rows = slot_ref[0, 0, pl.ds(r0, wrows), :]
        onehot = jnp.concatenate([jnp.where(rows[k:k + 1, :] == target, 1.0, 0.0).astype(BF16)
                                  for k in range(wrows)], axis=1)
        t0 = pl.multiple_of(r0 * TOKEN_ROW, TOKEN_ROW)
        xs_ref[blk, :] = _dot(onehot, hx_ref[0, pl.ds(t0, win), :])

        def extra_row(r, carry, blk=blk, target=target):
            hit = jnp.where(slot_ref[0, 0, pl.ds(r, 1), :] == target, 1.0, 0.0).astype(BF16)
            tr = pl.multiple_of(r * TOKEN_ROW, TOKEN_ROW)
            xs_ref[blk, :] += _dot(hit, hx_ref[0, pl.ds(tr, TOKEN_ROW), :])
            return carry

        r_end = (edge_ref[base + s + 1] + TOKEN_ROW - 1) // TOKEN_ROW
        lax.fori_loop(r0 + wrows, r_end, extra_row, 0)

    xs = xs_ref[...].astype(BF16)
    h1 = _dot(xs, wg_ref[0].astype(BF16))
    h2 = _dot(xs, wu_ref[0].astype(BF16))
    hid = (h1 * jax.nn.sigmoid(h1) * h2).astype(BF16)
    ye_ref[0, 0] = _dot(hid, wd_ref[0].astype(BF16)).astype(BF16)


def _moe(edges, hx, slot4, wg, wu, wd, cap, win, sb):
    bsz, t, d = hx.shape
    ne, _, f = wg.shape
    nrow = t // TOKEN_ROW
    grid_spec = pltpu.PrefetchScalarGridSpec(
        num_scalar_prefetch=1,
        grid=(bsz, ne),
        in_specs=[pl.BlockSpec((1, t, d), lambda b, e, s: (b, 0, 0), pipeline_mode=pl.Buffered(1)),
                  pl.BlockSpec((1, 1, nrow, TOKEN_ROW), lambda b, e, s: (b, e, 0, 0)),
                  pl.BlockSpec((1, d, f), lambda b, e, s: (e, 0, 0)),
                  pl.BlockSpec((1, d, f), lambda b, e, s: (e, 0, 0)),
                  pl.BlockSpec((1, f, d), lambda b, e, s: (e, 0, 0))],
        out_specs=pl.BlockSpec((1, 1, cap, d), lambda b, e, s: (b, e, 0, 0)),
        scratch_shapes=[pltpu.VMEM((cap, d), F32)],
    )
    return pl.pallas_call(
        functools.partial(_moe_kernel, win, sb),
        grid_spec=grid_spec,
        out_shape=jax.ShapeDtypeStruct((bsz, ne, cap, d), BF16),
        compiler_params=_cparams(("parallel", "arbitrary")),
        name="moe",
    )(edges, hx, slot4, wg, wu, wd)


COMB_ROWS = 128
COMB_WIN = 64
SLOT_ALIGN = 16


def _comb_rows(win, groups, xm_ref, ye_ref, slotc, affc, g2_ref, fg_ref, o_ref):
    ne, cap = ye_ref.shape[1], ye_ref.shape[2]
    wide = ne * win
    lane_w = lax.broadcasted_iota(jnp.int32, (ne, wide), 1)
    expand = jnp.where(lane_w // win == lax.broadcasted_iota(jnp.int32, (ne, wide), 0), 1.0, 0.0).astype(BF16)
    slot_in_win = (lax.broadcasted_iota(jnp.int32, (COMB_ROWS, wide), 1) % win).astype(F32)
    lane_e = lax.broadcasted_iota(jnp.int32, (1, ne), 1)
    lhs, rhs = [], []
    for h, first in groups:
        rows = slice(h * COMB_ROWS, (h + 1) * COMB_ROWS)
        starts = [pl.multiple_of(jnp.minimum(first[e] & ~(SLOT_ALIGN - 1), cap - win), SLOT_ALIGN) for e in range(ne)]
        start_row = jnp.zeros((1, ne), jnp.int32)
        for e in range(ne):
            start_row = jnp.where(lane_e == e, starts[e], start_row)
        offset = _dot((slotc[rows, :] - start_row).astype(F32).astype(BF16), expand)
        val_hi, val_lo = _split2(affc[rows, :])
        hit = offset == slot_in_win
        hi = jnp.where(hit, _dot(val_hi, expand), 0.0).astype(BF16)
        lo = jnp.where(hit, _dot(val_lo, expand), 0.0).astype(BF16)
        lhs.append(jnp.concatenate([hi, lo], axis=0))
        rhs.append(jnp.concatenate([ye_ref[0, e, pl.ds(starts[e], win), :] for e in range(ne)], axis=0))
    both = [_dot(a, y) for a, y in zip(lhs, rhs)]
    for (h, _), bt in zip(groups, both):
        rows = slice(h * COMB_ROWS, (h + 1) * COMB_ROWS)
        x = xm_ref[0, rows, :] + g2_ref[0] * (bt[:COMB_ROWS] + bt[COMB_ROWS:])
        ms = jnp.mean(x * x, axis=-1, keepdims=True)
        o_ref[0, rows, :] = x * lax.rsqrt(ms + NORM_EPS) * fg_ref[...]


def _comb_kernel(tk, win, full, tsp_ref, xm_ref, ye_ref, slotc_ref, affc_ref, g2_ref, fg_ref, o_ref):
    b = pl.program_id(0)
    j = pl.program_id(1)
    nh = tk // COMB_ROWS
    ntile = pl.num_programs(1) * nh + 1
    ne, cap = ye_ref.shape[1], ye_ref.shape[2]
    slotc = slotc_ref[0]
    affc = affc_ref[0]
    refs = (xm_ref, ye_ref, slotc, affc, g2_ref, fg_ref, o_ref)
    groups = [(h, [tsp_ref[(b * ne + e) * ntile + j * nh + h] for e in range(ne)]) for h in range(nh)]
    _comb_rows(win, groups, *refs)
    if full != win:
        for h, first in groups:
            over = None
            for e in range(ne):
                start = jnp.minimum(first[e] & ~(SLOT_ALIGN - 1), cap - win)
                miss = tsp_ref[(b * ne + e) * ntile + j * nh + h + 1] > start + win
                over = miss if over is None else over | miss

            @pl.when(over)
            def _(h=h, first=first):
                _comb_rows(full, [(h, first)], *refs)


def _comb(tsp, xm, ye, slotc, affc, g2, fg, tk):
    bsz, t, d = xm.shape
    ne, cap = ye.shape[1], ye.shape[2]
    full = min(2 * COMB_ROWS, cap)
    assert full == cap or full >= COMB_ROWS + SLOT_ALIGN
    assert full <= 256
    win = min(COMB_WIN, full)
    grid_spec = pltpu.PrefetchScalarGridSpec(
        num_scalar_prefetch=1,
        grid=(bsz, t // tk),
        in_specs=[pl.BlockSpec((1, tk, d), lambda b, j, s: (b, j, 0)),
                  pl.BlockSpec((1, ne, cap, d), lambda b, j, s: (b, 0, 0, 0)),
                  pl.BlockSpec((1, tk, ne), lambda b, j, s: (b, j, 0)),
                  pl.BlockSpec((1, tk, ne), lambda b, j, s: (b, j, 0)),
                  pl.BlockSpec((1, 1, d), lambda b, j, s: (b, 0, 0)),
                  pl.BlockSpec((1, d), lambda b, j, s: (0, 0))],
        out_specs=pl.BlockSpec((1, tk, d), lambda b, j, s: (b, j, 0)),
    )
    return pl.pallas_call(
        functools.partial(_comb_kernel, tk, win, full),
        grid_spec=grid_spec,
        out_shape=jax.ShapeDtypeStruct((bsz, t, d), F32),
        compiler_params=_cparams(("parallel", "arbitrary")),
        name="comb",
    )(tsp, xm, ye, slotc, affc, g2, fg)


def _block_diag_ones(n, blk, value=1.0):
    r = jnp.arange(n)
    return jnp.where((r[:, None] // blk) == (r[None, :] // blk), value, 0.0)


def kernel(x, c, ctx, c_ctx, ada_w, ada_b, norm1_g, norm2_g, w_in, shift_mu, w0, w_lora_up, a0, a_lora_up, k_k, k_a,
           r_k, g_lora_up, gn_g, gn_b, conv_w, w_out, router_w, exp_w_gate, exp_w_up, exp_w_down, final_g):
    bsz, t, d = x.shape
    lc = ctx.shape[1]
    ne = router_w.shape[-1]
    cap = EC_CAPACITY * t // ne
    sb = min(SLOT_BLOCK, cap)
    win = min(GATHER_WIN, t)
    l = 0

    rows = ((bsz + 1 + 7) // 8) * 8
    cc = jnp.zeros((rows, d), F32).at[:bsz].set(c).at[bsz].set(c_ctx)
    mod = _mod(cc, ada_w[l], ada_b[l][None, :])
    sh1, sc1, g1, sh2, sc2, g2 = (m[:, None, :] for m in jnp.split(mod[:bsz], 6, axis=-1))
    csh1, csc1 = (jnp.broadcast_to(m[None, None, :], (bsz, 1, d)) for m in jnp.split(mod[bsz], 6)[:2])

    w_rw = w_in[l][:, :RWKV_COLS].astype(BF16)
    w_cv = w_in[l][:, RWKV_COLS:].astype(BF16)
    n1g = norm1_g[l][None, :]
    px_rw, px_cv = _in_proj(x, sh1, sc1, n1g, w_rw, w_cv, IN_PROJ_ROWS)
    pc_rw, _ = _in_proj(ctx, csh1, csc1, n1g, w_rw, w_cv, min(IN_PROJ_ROWS, lc))

    zw = jnp.zeros((LORA_W, 2 * D_RWKV), F32)
    lora = jnp.concatenate([
        jnp.concatenate([w_lora_up[l, 0], w_lora_up[l, 1], zw], axis=1),
        jnp.concatenate([zw, a_lora_up[l, 0], a_lora_up[l, 1]], axis=1)], axis=0).astype(BF16)
    ridx = jnp.arange(CHUNK)
    tril = jnp.where(ridx[None, :] <= ridx[:, None], 1.0, 0.0).astype(BF16)
    triu = jnp.where(ridx[None, :] >= ridx[:, None], 1.0, 0.0).astype(BF16)
    ones_bd = _block_diag_ones(PAIR, HEAD).astype(BF16)
    consts = (shift_mu[l][None, :], lora, w0[l], a0[l], k_k[l][None, :], k_a[l][None, :],
              r_k[l].reshape(1, D_RWKV), g_lora_up[l].astype(BF16), ones_bd, tril, triu)

    chunks_c = _prep(pc_rw, consts, False, lc)[:5]
    *chunks_x, bonus, gate = _prep(px_rw, consts, True, PREP_ROWS)

    s_zero = jnp.zeros((bsz, 2, N_PAIR, HEAD, PAIR), F32)
    _, _, s_ctx = _scan(*chunks_c, s_zero)
    yf, yb, _ = _scan(*chunks_x, s_ctx)

    rw_t = router_w[l].T
    rwh = rw_t.astype(BF16)
    rwl = (rw_t - rwh.astype(F32)).astype(BF16)
    avg = _block_diag_ones(PAIR, HEAD, 1.0 / HEAD).astype(BF16)
    xm, hx, aff_t = _out(yf, yb, bonus, gate, px_cv, x, g1, sh2, sc2, norm2_g[l][None, :], gn_g[l][None, :],
                         gn_b[l][None, :], conv_w[l], w_out[l].astype(BF16), rwh, rwl, avg, OUT_TILE)

    lane_idx = jnp.arange(TOKEN_ROW)
    tri_lanes = jnp.where(lane_idx[:, None] <= lane_idx[None, :], 1.0, 0.0).astype(BF16)
    cnt, slot, edges = _topk(aff_t, tri_lanes, cap, sb)

    ye = _moe(edges[:, :, :cap // sb + 1].reshape(-1), hx, slot.reshape(bsz, ne, t // TOKEN_ROW, TOKEN_ROW),
              exp_w_gate[l], exp_w_up[l], exp_w_down[l], cap, win, sb)
    tr = lambda a: jnp.transpose(a, (0, 2, 1))
    first_slot = jnp.concatenate([cnt[:, :, ::COMB_ROWS], jnp.full((bsz, ne, 1), cap, jnp.int32)], axis=-1).reshape(-1)
    return _comb(first_slot, xm, ye, tr(slot), tr(aff_t), g2, final_g[None, :], COMB_TILE)
```

```python
import functools
import math

import jax
import jax.numpy as jnp
from jax import lax
from jax.experimental import pallas as pl
from jax.experimental.pallas import tpu as pltpu

F32 = jnp.float32
BF16 = jnp.bfloat16
HIGHEST = lax.Precision.HIGHEST

GRID_W = 64
D_RWKV = 512
D_CONV = 512
HEAD = 64
LORA_W = 64
LORA_A = 64
LORA_G = 128
EC_CAPACITY = 2
NORM_EPS = 1e-6
GN_EPS = 64e-5
RWKV_COLS = 3 * D_RWKV + LORA_W + LORA_A + LORA_G
CONV_COLS = 3 * D_CONV

LANES = 128
CHUNK = 64
PAIR = 2 * HEAD
N_PAIR = D_RWKV // PAIR
VMEM_LIMIT = 48 * 1024 * 1024

IN_PROJ_ROWS = 512
PREP_ROWS = 512
OUT_TILE = 512
COMB_TILE = 512
SLOT_BLOCK = 128
GATHER_WIN = 1536


def _cparams(sem):
    return pltpu.CompilerParams(dimension_semantics=sem, vmem_limit_bytes=VMEM_LIMIT)


def _dot(a, b):
    return jnp.dot(a, b, preferred_element_type=F32)


def _dot_nt(a, b):
    return lax.dot_general(a, b, (((1,), (1,)), ((), ())), preferred_element_type=F32)


def _split2(x):
    hi = x.astype(BF16)
    lo = (x - hi.astype(F32)).astype(BF16)
    return hi, lo


def _split3(x):
    hi = x.astype(BF16)
    r = x - hi.astype(F32)
    mid = r.astype(BF16)
    lo = (r - mid.astype(F32)).astype(BF16)
    return hi, mid, lo


def _seg_dot(x, m):
    hi, lo = _split2(x)
    return _dot(hi, m) + _dot(lo, m)


def _mod_kernel(c_ref, w_ref, b_ref, o_ref):
    c = c_ref[...]
    s = c * jax.nn.sigmoid(c)
    o_ref[...] = jnp.dot(s, w_ref[...], precision=HIGHEST, preferred_element_type=F32) + b_ref[...]


def _mod(cc, w, b):
    rows, d = cc.shape
    n = w.shape[1]
    tn = 1024
    return pl.pallas_call(
        _mod_kernel,
        grid=(n // tn,),
        in_specs=[pl.BlockSpec((rows, d), lambda j: (0, 0)),
                  pl.BlockSpec((d, tn), lambda j: (0, j)),
                  pl.BlockSpec((1, tn), lambda j: (0, j))],
        out_specs=pl.BlockSpec((rows, tn), lambda j: (0, j)),
        out_shape=jax.ShapeDtypeStruct((rows, n), F32),
        compiler_params=_cparams(("parallel",)),
        name="mod",
    )(cc, w, b)


def _in_proj_kernel(x_ref, sh_ref, sc_ref, g_ref, wrw_ref, wcv_ref, orw_ref, ocv_ref):
    x = x_ref[0]
    ms = jnp.mean(x * x, axis=-1, keepdims=True)
    h = x * lax.rsqrt(ms + NORM_EPS) * g_ref[...]
    h = (h * (1.0 + sc_ref[0]) + sh_ref[0]).astype(BF16)
    orw_ref[0] = _dot(h, wrw_ref[...])
    ocv_ref[0] = _dot(h, wcv_ref[...])


def _in_proj(x, sh, sc, g, w_rw, w_cv, tm):
    bsz, l, d = x.shape
    return pl.pallas_call(
        _in_proj_kernel,
        grid=(bsz, l // tm),
        in_specs=[pl.BlockSpec((1, tm, d), lambda b, i: (b, i, 0)),
                  pl.BlockSpec((1, 1, d), lambda b, i: (b, 0, 0)),
                  pl.BlockSpec((1, 1, d), lambda b, i: (b, 0, 0)),
                  pl.BlockSpec((1, d), lambda b, i: (0, 0)),
                  pl.BlockSpec((d, RWKV_COLS), lambda b, i: (0, 0)),
                  pl.BlockSpec((d, CONV_COLS), lambda b, i: (0, 0))],
        out_specs=[pl.BlockSpec((1, tm, RWKV_COLS), lambda b, i: (b, i, 0)),
                   pl.BlockSpec((1, tm, CONV_COLS), lambda b, i: (b, i, 0))],
        out_shape=[jax.ShapeDtypeStruct((bsz, l, RWKV_COLS), F32),
                   jax.ShapeDtypeStruct((bsz, l, CONV_COLS), F32)],
        compiler_params=_cparams(("parallel", "parallel")),
        name="in_proj",
    )(x, sh, sc, g, w_rw, w_cv)


def _bf(x):
    return x.astype(BF16)


def _chunk_local(chains):
    c = CHUNK
    n = range(len(chains))
    lane = lax.broadcasted_iota(jnp.int32, (c, PAIR), 1)
    h0 = lane < HEAD
    tcol = lane & (c - 1)
    trow = lax.broadcasted_iota(jnp.int32, (c, PAIR), 0)
    eye = jnp.where(tcol == trow, 1.0, 0.0)
    masks = {rev: ((tcol > trow) if rev else (tcol < trow), (tcol >= trow) if rev else (tcol <= trow))
             for rev in (False, True)}
    zb = jnp.zeros((c, PAIR), BF16)

    def stack(x):
        return jnp.concatenate([jnp.where(h0, x, zb), jnp.where(h0, zb, x)], axis=0)

    gram = [_dot_nt(jnp.concatenate([ch[1], ch[2]], axis=0), jnp.concatenate([stack(ch[3]), stack(ch[4])], axis=0))
            for ch in chains]
    lab = [jnp.where(masks[chains[i][0]][0], gram[i][:c, :PAIR], 0.0) for i in n]
    lak = [jnp.where(masks[chains[i][0]][0], gram[i][:c, PAIR:], 0.0) for i in n]
    mrb = [jnp.where(masks[chains[i][0]][1], gram[i][c:, :PAIR], 0.0) for i in n]
    mrk = [jnp.where(masks[chains[i][0]][1], gram[i][c:, PAIR:], 0.0) for i in n]

    labb = [_bf(x) for x in lab]
    pw = [_dot(labb[i], stack(labb[i])) for i in n]
    tp = [eye + lab[i] for i in n]
    for _ in range(4):
        pwb = [_bf(x) for x in pw]
        both = [_dot(pwb[i], jnp.concatenate([stack(pwb[i]), stack(_bf(tp[i]))], axis=1)) for i in n]
        pw = [x[:, :PAIR] for x in both]
        tp = [tp[i] + both[i][:, PAIR:] for i in n]
    tinv = [tp[i] + _dot(_bf(pw[i]), stack(_bf(tp[i]))) for i in n]

    lmv = [_dot(_bf(jnp.concatenate([lak[i], mrk[i]], axis=0)), stack(chains[i][7])) for i in n]
    x = [_dot(_bf(tinv[i]), jnp.concatenate([stack(chains[i][1]), stack(_bf(lmv[i][:c]))], axis=1))
         for i in n]
    z = [_dot(_bf(mrb[i]), jnp.concatenate([stack(_bf(x[i][:, :PAIR])), stack(_bf(x[i][:, PAIR:]))], axis=1))
         for i in n]
    rbar = [_bf(chains[i][2].astype(F32) + z[i][:, :PAIR]) for i in n]
    y0 = [z[i][:, PAIR:] + lmv[i][c:] for i in n]

    uv = [jnp.concatenate([x[i][:, PAIR:], chains[i][7].astype(F32)], axis=0) for i in n]
    mxf = [_dot(_bf(x[i][:, :PAIR].T), chains[i][5]) for i in n]
    nnf = [_dot(_bf(uv[i].T), jnp.concatenate([chains[i][5], chains[i][6]], axis=0)) for i in n]
    mx = [_bf(jnp.where(h0, m[:c], m[c:])) for m in mxf]
    nn = [jnp.where(h0, m[:c], m[c:]) for m in nnf]
    return list(zip(rbar, y0, mx, nn))


DIR_COLS = 6 * D_RWKV
SCR_COLS = 2 * DIR_COLS + D_RWKV


def _head_sums(x, ones):
    return jnp.concatenate([_seg_dot(x[:, g * PAIR:(g + 1) * PAIR], ones) for g in range(N_PAIR)], axis=1)


def _prep_rows(c, p, shifted, lo, mu_ref, w0_ref, a0_ref, kk_ref, ka_ref, rk_ref, ones_ref, tril_ref, triu_ref,
               wc_ref, bonus_ref, scr_ref):
    rows = slice(c * CHUNK, (c + 1) * CHUNK)
    pm = p + mu_ref[:, 0:3 * D_RWKV] * (shifted - p)
    xr = pm[:, 0:D_RWKV]
    xk = pm[:, D_RWKV:2 * D_RWKV]
    xv = pm[:, 2 * D_RWKV:3 * D_RWKV]

    ones = ones_ref[...]
    kraw = xk * kk_ref[...]
    kk = kraw / jnp.maximum(jnp.sqrt(_head_sums(kraw * kraw, ones)), 1e-12)

    scr_ref[:, 2 * DIR_COLS:] = xv.astype(BF16)
    ksum = None
    for d in range(2):
        z = w0_ref[d:d + 1, :] + lo[:, d * D_RWKV:(d + 1) * D_RWKV]
        ld = -math.exp(-0.5) * jax.nn.sigmoid(z)
        ag = jax.nn.sigmoid(a0_ref[d:d + 1, :] + lo[:, (2 + d) * D_RWKV:(3 + d) * D_RWKV])
        kd = xk * (1.0 + (ag - 1.0) * ka_ref[...])
        bb = kk * ag
        ksum = kd if ksum is None else ksum + kd
        tri = tril_ref[...] if d == 0 else triu_ref[...]
        h3 = _split3(ld)
        cs = _dot(tri, h3[0]) + _dot(tri, h3[1]) + _dot(tri, h3[2])
        e_in = jnp.exp(cs)
        e_ex = jnp.exp(cs - ld)
        e_neg = jnp.exp(-cs)
        bt = bb * e_neg
        kt = kd * e_neg
        edge = CHUNK - 1 if d == 0 else 0
        wrow = e_in[edge:edge + 1, :]
        wc_ref[0, c, :, d * D_RWKV:(d + 1) * D_RWKV] = wrow
        base = d * DIR_COLS
        for j, val in enumerate((-kk * e_ex, xr * e_in, bt, kt, bt * wrow, kt * wrow)):
            scr_ref[:, base + j * D_RWKV:base + (j + 1) * D_RWKV] = val.astype(BF16)

    bonus_ref[0, rows, :] = (_head_sums(xr * ksum * rk_ref[...], ones) * xv).astype(BF16)


PREP_GROUP = 2


def _prep_chunks(cs, scr_refs, rbar_ref, y0_ref, mx_ref, nn_ref):
    chains = []
    for scr_ref in scr_refs:
        for d in range(2):
            for pp in range(N_PAIR):
                cols = [d * DIR_COLS + j * D_RWKV + pp * PAIR for j in range(6)] + [2 * DIR_COLS + pp * PAIR]
                chains.append((d == 1,) + tuple(scr_ref[:, k:k + PAIR] for k in cols))
    per_chunk = 2 * N_PAIR
    for idx, (rbar, y0, mx, nn) in enumerate(_chunk_local(chains)):
        c = cs[idx // per_chunk]
        rows = slice(c * CHUNK, (c + 1) * CHUNK)
        col = (idx % per_chunk // N_PAIR) * D_RWKV + (idx % N_PAIR) * PAIR
        rbar_ref[0, rows, col:col + PAIR] = rbar
        y0_ref[0, rows, col:col + PAIR] = y0.astype(BF16)
        mx_ref[0, c, :, col:col + PAIR] = mx
        nn_ref[0, c, :, col:col + PAIR] = nn


N_PREP_CONSTS = 11


def _prep_kernel(latent, tt, p_ref, *rest):
    if latent:
        prev_ref, next_ref = rest[:2]
        rest = rest[2:]
    (mu_ref, lw_ref, w0_ref, a0_ref, kk_ref, ka_ref, rk_ref, gup_ref, ones_ref, tril_ref,
     triu_ref) = rest[:N_PREP_CONSTS]
    rbar_ref, y0_ref, mx_ref, nn_ref, wc_ref, bonus_ref, g_ref = rest[N_PREP_CONSTS:N_PREP_CONSTS + 7]
    *scr, lo_ref, ext_ref = rest[N_PREP_CONSTS + 7:]
    if latent:
        i = pl.program_id(1)
        n = pl.num_programs(1)
        ext_ref[0:GRID_W] = jnp.where(i > 0, prev_ref[0], 0.0)
        ext_ref[GRID_W:GRID_W + tt] = p_ref[0]
        ext_ref[GRID_W + tt:] = jnp.where(i < n - 1, next_ref[0], 0.0)

        def shifted(r0, nrows, cols):
            shape = (nrows, cols.stop - cols.start)
            trow = lax.broadcasted_iota(jnp.int32, shape, 0) & (GRID_W - 1)
            q = lax.broadcasted_iota(jnp.int32, shape, 1) & 3
            left = jnp.where(trow == 0, 0.0, ext_ref[GRID_W - 1 + r0:GRID_W - 1 + r0 + nrows, cols])
            right = jnp.where(trow == GRID_W - 1, 0.0, ext_ref[GRID_W + 1 + r0:GRID_W + 1 + r0 + nrows, cols])
            up = ext_ref[r0:r0 + nrows, cols]
            down = ext_ref[2 * GRID_W + r0:2 * GRID_W + r0 + nrows, cols]
            return jnp.where(q == 0, left, jnp.where(q == 1, right, jnp.where(q == 2, up, down)))
    else:
        ext_ref[0:8] = jnp.zeros((8, RWKV_COLS), F32)
        ext_ref[8:8 + tt] = p_ref[0]
        ext_ref[8 + tt:] = jnp.zeros((8, RWKV_COLS), F32)

        def shifted(r0, nrows, cols):
            q = lax.broadcasted_iota(jnp.int32, (nrows, cols.stop - cols.start), 1) & 1
            return jnp.where(q == 0, ext_ref[7 + r0:7 + r0 + nrows, cols], ext_ref[9 + r0:9 + r0 + nrows, cols])

    tail = slice(3 * D_RWKV, RWKV_COLS)
    p_t = p_ref[0, :, tail]
    pm_t = p_t + mu_ref[:, tail] * (shifted(0, tt, tail) - p_t)
    xwa = pm_t[:, 0:LORA_W + LORA_A]
    lane = lax.broadcasted_iota(jnp.int32, xwa.shape, 1)
    lin = jnp.where(lane < LORA_W, jnp.tanh(xwa), xwa).astype(BF16)
    lo_ref[...] = _dot(lin, lw_ref[...])
    g_ref[0] = _dot(jax.nn.sigmoid(pm_t[:, LORA_W + LORA_A:]).astype(BF16), gup_ref[...]).astype(BF16)

    head = slice(0, 3 * D_RWKV)

    def rows_part(c):
        rows = slice(c * CHUNK, (c + 1) * CHUNK)
        _prep_rows(c, p_ref[0, rows, head], shifted(c * CHUNK, CHUNK, head), lo_ref[rows, :], mu_ref, w0_ref,
                   a0_ref, kk_ref, ka_ref, rk_ref, ones_ref, tril_ref, triu_ref, wc_ref, bonus_ref,
                   scr[c % (2 * PREP_GROUP)])

    groups = [list(range(g, g + PREP_GROUP)) for g in range(0, tt // CHUNK, PREP_GROUP)]
    for c in groups[0]:
        rows_part(c)
    for k, cs in enumerate(groups):
        if k + 1 < len(groups):
            for c in groups[k + 1]:
                rows_part(c)
        _prep_chunks(cs, [scr[c % (2 * PREP_GROUP)] for c in cs], rbar_ref, y0_ref, mx_ref, nn_ref)


def _prep(p_rw, consts, latent, tt):
    bsz, l, _ = p_rw.shape
    nc = l // CHUNK
    cpt = tt // CHUNK
    assert len(consts) == N_PREP_CONSTS
    const_specs = [pl.BlockSpec(c.shape, lambda b, i, nd=c.ndim: (0,) * nd) for c in consts]
    kern = functools.partial(_prep_kernel, latent, tt)
    if latent:
        nblk = l // GRID_W
        in_specs = [pl.BlockSpec((1, tt, RWKV_COLS), lambda b, i: (b, i, 0)),
                    pl.BlockSpec((1, GRID_W, RWKV_COLS),
                                 lambda b, i: (b, jnp.maximum(i * (tt // GRID_W) - 1, 0), 0)),
                    pl.BlockSpec((1, GRID_W, RWKV_COLS),
                                 lambda b, i: (b, jnp.minimum((i + 1) * (tt // GRID_W), nblk - 1), 0))]
        args = (p_rw, p_rw, p_rw)
        ext_rows = tt + 2 * GRID_W
    else:
        assert tt == l
        in_specs = [pl.BlockSpec((1, tt, RWKV_COLS), lambda b, i: (b, i, 0))]
        args = (p_rw,)
        ext_rows = tt + 16
    row = lambda b, i: (b, i, 0)
    chunk = lambda b, i: (b, i, 0, 0)
    w2 = 2 * D_RWKV
    return pl.pallas_call(
        kern,
        grid=(bsz, l // tt),
        in_specs=in_specs + const_specs,
        out_specs=[pl.BlockSpec((1, tt, w2), row), pl.BlockSpec((1, tt, w2), row),
                   pl.BlockSpec((1, cpt, CHUNK, w2), chunk), pl.BlockSpec((1, cpt, CHUNK, w2), chunk),
                   pl.BlockSpec((1, cpt, 1, w2), chunk),
                   pl.BlockSpec((1, tt, D_RWKV), row), pl.BlockSpec((1, tt, D_RWKV), row)],
        out_shape=[jax.ShapeDtypeStruct((bsz, l, w2), BF16), jax.ShapeDtypeStruct((bsz, l, w2), BF16),
                   jax.ShapeDtypeStruct((bsz, nc, CHUNK, w2), BF16), jax.ShapeDtypeStruct((bsz, nc, CHUNK, w2), F32),
                   jax.ShapeDtypeStruct((bsz, nc, 1, w2), F32),
                   jax.ShapeDtypeStruct((bsz, l, D_RWKV), BF16), jax.ShapeDtypeStruct((bsz, l, D_RWKV), BF16)],
        scratch_shapes=[pltpu.VMEM((CHUNK, SCR_COLS), BF16)] * (2 * PREP_GROUP)
        + [pltpu.VMEM((tt, 4 * D_RWKV), F32), pltpu.VMEM((ext_rows, RWKV_COLS), F32)],
        compiler_params=_cparams(("parallel", "parallel")),
        name="prep_latent" if latent else "prep_ctx",
    )(*args, *consts)


SCAN_CHUNKS = 4
SCAN_BATCH = 4


def _scan_kernel(rbf_ref, rbb_ref, y0f_ref, y0b_ref, mxf_ref, mxb_ref, nnf_ref, nnb_ref, wcf_ref, wcb_ref,
                 s0_ref, yf_ref, yb_ref, s_ref):
    i = pl.program_id(1)

    @pl.when(i == 0)
    def _():
        s_ref[...] = s0_ref[...]

    c = CHUNK
    h0 = lax.broadcasted_iota(jnp.int32, (c, PAIR), 1) < HEAD
    zb = jnp.zeros((c, PAIR), BF16)

    def stack(x):
        return jnp.concatenate([jnp.where(h0, x, zb), jnp.where(h0, zb, x)], axis=0)

    dirs = ((rbf_ref, y0f_ref, mxf_ref, nnf_ref, wcf_ref, yf_ref), (rbb_ref, y0b_ref, mxb_ref, nnb_ref, wcb_ref, yb_ref))
    idx = [(n, d, p) for n in range(SCAN_BATCH) for d in range(2) for p in range(N_PAIR)]
    lanes = [slice(p * PAIR, (p + 1) * PAIR) for _, _, p in idx]
    s = [s_ref[n, d, p] for n, d, p in idx]
    for q in range(SCAN_CHUNKS):
        cq = (q, SCAN_CHUNKS - 1 - q)
        rows = [slice(cq[d] * c, (cq[d] + 1) * c) for _, d, _ in idx]
        sb = [_bf(x) for x in s]
        y = [_dot_nt(dirs[d][0][n, rows[k], lanes[k]], stack(sb[k])) for k, (n, d, _) in enumerate(idx)]
        sm = [_dot(sb[k], stack(dirs[d][2][n, cq[d], :, lanes[k]])) for k, (n, d, _) in enumerate(idx)]
        s_next = []
        for k, (n, d, p) in enumerate(idx):
            dirs[d][5][n, rows[k], lanes[k]] = (y[k] + dirs[d][1][n, rows[k], lanes[k]].astype(F32)).astype(BF16)
            s_next.append(s[k] * dirs[d][4][n, cq[d], :, lanes[k]] + sm[k] + dirs[d][3][n, cq[d], :, lanes[k]])
        s = s_next
    for k, (n, d, p) in enumerate(idx):
        s_ref[n, d, p] = s[k]


def _scan(rbar, y0, mx, nn, wc, s0):
    bsz, l, _ = rbar.shape
    sub, nb = SCAN_CHUNKS, SCAN_BATCH
    assert bsz % nb == 0 and l % (CHUNK * sub) == 0
    nc = l // (CHUNK * sub)
    row_f = pl.BlockSpec((nb, sub * CHUNK, D_RWKV), lambda b, i: (b, i, 0))
    row_b = pl.BlockSpec((nb, sub * CHUNK, D_RWKV), lambda b, i: (b, nc - 1 - i, 1))
    chk_f = pl.BlockSpec((nb, sub, CHUNK, D_RWKV), lambda b, i: (b, i, 0, 0))
    chk_b = pl.BlockSpec((nb, sub, CHUNK, D_RWKV), lambda b, i: (b, nc - 1 - i, 0, 1))
    wc_f = pl.BlockSpec((nb, sub, 1, D_RWKV), lambda b, i: (b, i, 0, 0))
    wc_b = pl.BlockSpec((nb, sub, 1, D_RWKV), lambda b, i: (b, nc - 1 - i, 0, 1))
    st_spec = pl.BlockSpec((nb, 2, N_PAIR, HEAD, PAIR), lambda b, i: (b, 0, 0, 0, 0))
    return pl.pallas_call(
        _scan_kernel,
        grid=(bsz // nb, nc),
        in_specs=[row_f, row_b, row_f, row_b, chk_f, chk_b, chk_f, chk_b, wc_f, wc_b, st_spec],
        out_specs=[row_f, pl.BlockSpec((nb, sub * CHUNK, D_RWKV), lambda b, i: (b, nc - 1 - i, 0)), st_spec],
        out_shape=[jax.ShapeDtypeStruct((bsz, l, D_RWKV), BF16),
                   jax.ShapeDtypeStruct((bsz, l, D_RWKV), BF16),
                   jax.ShapeDtypeStruct((bsz, 2, N_PAIR, HEAD, PAIR), F32)],
        compiler_params=_cparams(("parallel", "arbitrary")),
        name="scan",
    )(rbar, rbar, y0, y0, mx, mx, nn, nn, wc, wc, s0)


OUT_ROWS = 128


def _out_kernel(tt, yf_ref, yb_ref, bonus_ref, g_ref, cv_ref, cvp_ref, cvn_ref, x_ref, g1_ref, sh2_ref,
                sc2_ref, n2g_ref, gng_ref, gnb_ref, convw_ref, wout_ref, rwh_ref, rwl_ref, avg_ref,
                xm_ref, hx_ref, aff_ref):
    i = pl.program_id(1)
    n = pl.num_programs(1)

    cv = cv_ref[0]
    b_gate = cv[:, 0:D_CONV]
    cu = cv[:, D_CONV:2 * D_CONV] * cv[:, 2 * D_CONV:]
    cvp = cvp_ref[0]
    cvn = cvn_ref[0]
    cu_prev = jnp.where(i > 0, cvp[7:8, D_CONV:2 * D_CONV] * cvp[7:8, 2 * D_CONV:], 0.0)
    cu_next = jnp.where(i < n - 1, cvn[0:1, D_CONV:2 * D_CONV] * cvn[0:1, 2 * D_CONV:], 0.0)
    ridx = lax.broadcasted_iota(jnp.int32, cu.shape, 0)
    cu_m1 = jnp.where(ridx == 0, cu_prev, pltpu.roll(cu, 1, 0))
    cu_p1 = jnp.where(ridx == tt - 1, cu_next, pltpu.roll(cu, tt - 1, 0))
    conv = convw_ref[0:1, :] * cu_m1 + convw_ref[1:2, :] * cu + convw_ref[2:3, :] * cu_p1
    bx = (b_gate * conv).astype(BF16)

    parts = [slice(k * OUT_ROWS, (k + 1) * OUT_ROWS) for k in range(tt // OUT_ROWS)]
    avg = avg_ref[...]
    rwh = rwh_ref[...]
    rwl = rwl_ref[...]
    y = [yf_ref[0, r, :].astype(F32) + yb_ref[0, r, :].astype(F32) for r in parts]
    mu = [_head_sums(v, avg) for v in y]
    dlt = [a - b for a, b in zip(y, mu)]
    var = [_head_sums(v * v, avg) for v in dlt]
    yn = [a * lax.rsqrt(b + GN_EPS) * gng_ref[...] + gnb_ref[...] for a, b in zip(dlt, var)]
    ax = [((a + bonus_ref[0, r, :].astype(F32)) * g_ref[0, r, :].astype(F32)).astype(BF16) for a, r in zip(yn, parts)]
    mix = [_dot(a, wout_ref[0:D_RWKV, :]) + _dot(bx[r], wout_ref[D_RWKV:, :]) for a, r in zip(ax, parts)]
    xm = [x_ref[0, r, :] + g1_ref[0] * a for a, r in zip(mix, parts)]
    ms = [jnp.mean(v * v, axis=-1, keepdims=True) for v in xm]
    hx = [a * lax.rsqrt(b + NORM_EPS) * n2g_ref[...] for a, b in zip(xm, ms)]
    hx = [_split2(v * (1.0 + sc2_ref[0]) + sh2_ref[0]) for v in hx]
    logits = [_dot_nt(rwh, hi) + _dot_nt(rwh, lo) + _dot_nt(rwl, hi) for hi, lo in hx]
    for k, r in enumerate(parts):
        xm_ref[0, r, :] = xm[k]
        hx_ref[0, r, :] = hx[k][0]
        m = jnp.max(logits[k], axis=0, keepdims=True)
        ex = jnp.exp(logits[k] - m)
        aff_ref[0, :, r] = ex / jnp.sum(ex, axis=0, keepdims=True)


def _out(yf, yb, bonus, g, p_cv, x, g1, sh2, sc2, n2g, gng, gnb, convw, wout, rwh, rwl, avg, tt):
    bsz, t, d = x.shape
    ne = rwh.shape[0]
    nb8 = t // 8
    row = lambda b, i: (b, i, 0)
    per_b = lambda b, i: (b, 0, 0)
    const2 = lambda b, i: (0, 0)
    return pl.pallas_call(
        functools.partial(_out_kernel, tt),
        grid=(bsz, t // tt),
        in_specs=[pl.BlockSpec((1, tt, D_RWKV), row), pl.BlockSpec((1, tt, D_RWKV), row),
                  pl.BlockSpec((1, tt, D_RWKV), row), pl.BlockSpec((1, tt, D_RWKV), row),
                  pl.BlockSpec((1, tt, CONV_COLS), row),
                  pl.BlockSpec((1, 8, CONV_COLS), lambda b, i: (b, jnp.maximum(i * (tt // 8) - 1, 0), 0)),
                  pl.BlockSpec((1, 8, CONV_COLS), lambda b, i: (b, jnp.minimum((i + 1) * (tt // 8), nb8 - 1), 0)),
                  pl.BlockSpec((1, tt, d), row),
                  pl.BlockSpec((1, 1, d), per_b), pl.BlockSpec((1, 1, d), per_b), pl.BlockSpec((1, 1, d), per_b),
                  pl.BlockSpec((1, d), const2), pl.BlockSpec((1, D_RWKV), const2), pl.BlockSpec((1, D_RWKV), const2),
                  pl.BlockSpec((3, D_CONV), const2), pl.BlockSpec((D_RWKV + D_CONV, d), const2),
                  pl.BlockSpec((ne, d), const2), pl.BlockSpec((ne, d), const2),
                  pl.BlockSpec((PAIR, PAIR), const2)],
        out_specs=[pl.BlockSpec((1, tt, d), row), pl.BlockSpec((1, tt, d), row),
                   pl.BlockSpec((1, ne, tt), lambda b, i: (b, 0, i))],
        out_shape=[jax.ShapeDtypeStruct((bsz, t, d), F32), jax.ShapeDtypeStruct((bsz, t, d), BF16),
                   jax.ShapeDtypeStruct((bsz, ne, t), F32)],
        compiler_params=_cparams(("parallel", "parallel")),
        name="out",
    )(yf, yb, bonus, g, p_cv, p_cv, p_cv, x, g1, sh2, sc2, n2g, gng, gnb, convw, wout, rwh, rwl, avg)


def _prefix_blocks(mask_fn, t, tri, emit):
    carry = None
    for j in range(t // LANES):
        m = mask_fn(j)
        inc = _dot(m.astype(BF16), tri)
        carry = jnp.zeros_like(inc[:, 0:1]) if carry is None else carry
        emit(j, m, inc - m + carry)
        carry = carry + inc[:, LANES - 1:LANES]


def _topk_kernel(cap, sb, aff_ref, tri_ref, cnt_ref, slot_ref, edge_ref):
    t = aff_ref.shape[2]
    aff = aff_ref[0]

    def body(k, bits):
        cand = bits | jnp.left_shift(jnp.int32(1), 30 - k)
        cnt = jnp.sum(jnp.where(aff >= pltpu.bitcast(cand, F32), 1, 0), axis=-1, keepdims=True)
        return jnp.where(cnt >= cap, cand, bits)

    bits = lax.fori_loop(0, 31, body, jnp.zeros((aff.shape[0], 1), jnp.int32))
    thr = pltpu.bitcast(bits, F32)
    above = pltpu.bitcast(bits + 1, F32)
    n_gt = jnp.sum(jnp.where(aff >= above, 1, 0), axis=-1, keepdims=True)
    need = (cap - n_gt).astype(F32)
    tri = tri_ref[...]

    def blk(j):
        return aff[:, j * LANES:(j + 1) * LANES]

    def emit_sel(j, eq, before):
        take = (blk(j) >= above) | ((eq > 0.5) & (before < need))
        slot_ref[0, :, j * LANES:(j + 1) * LANES] = jnp.where(take, 1, 0)

    _prefix_blocks(lambda j: jnp.where((blk(j) >= thr) & (blk(j) < above), 1.0, 0.0), t, tri, emit_sel)

    def emit_slot(j, m, before):
        count = before.astype(jnp.int32)
        cnt_ref[0, :, j * LANES:(j + 1) * LANES] = count
        slot_ref[0, :, j * LANES:(j + 1) * LANES] = jnp.where(m > 0.5, count, -1)

    _prefix_blocks(lambda j: slot_ref[0, :, j * LANES:(j + 1) * LANES].astype(F32), t, tri, emit_slot)

    cnt = cnt_ref[0]
    lane = lax.broadcasted_iota(jnp.int32, (cnt.shape[0], LANES), 1)
    edges = jnp.zeros((cnt.shape[0], LANES), jnp.int32)
    for s in range(1, cap // sb + 1):
        below = jnp.sum(jnp.where(cnt < s * sb, 1, 0), axis=-1, keepdims=True)
        edges = jnp.where(lane == s, below, edges)
    edge_ref[0] = edges


def _topk(aff_t, tri, cap, sb):
    bsz, ne, t = aff_t.shape
    spec = pl.BlockSpec((1, ne, t), lambda b: (b, 0, 0))
    return pl.pallas_call(
        functools.partial(_topk_kernel, cap, sb),
        grid=(bsz,),
        in_specs=[spec, pl.BlockSpec((LANES, LANES), lambda b: (0, 0))],
        out_specs=[spec, spec, pl.BlockSpec((1, ne, LANES), lambda b: (b, 0, 0))],
        out_shape=[jax.ShapeDtypeStruct((bsz, ne, t), jnp.int32)] * 2 + [jax.ShapeDtypeStruct((bsz, ne, LANES), jnp.int32)],
        compiler_params=_cparams(("parallel",)),
        name="topk",
    )(aff_t, tri)


TOKEN_ROW = LANES


def _moe_kernel(win, sb, edge_ref, hx_ref, slot_ref, wg_ref, wu_ref, wd_ref, ye_ref, xs_ref):
    b = pl.program_id(0)
    e = pl.program_id(1)
    ne = pl.num_programs(1)
    cap = xs_ref.shape[0]
    nblk = cap // sb
    nrow = hx_ref.shape[1] // TOKEN_ROW
    wrows = win // TOKEN_ROW
    base = (b * ne + e) * (nblk + 1)
    for s in range(nblk):
        blk = slice(s * sb, (s + 1) * sb)
        target = lax.broadcasted_iota(jnp.int32, (sb, TOKEN_ROW), 0) + s * sb
        r0 = jnp.minimum(edge_ref[base + s] // TOKEN_ROW, nrow - wrows)
        rows = slot_ref[0, 0, pl.ds(r0, wrows), :]
        onehot = jnp.concatenate([jnp.where(rows[k:k + 1, :] == target, 1.0, 0.0).astype(BF16)
                                  for k in range(wrows)], axis=1)
        t0 = pl.multiple_of(r0 * TOKEN_ROW, TOKEN_ROW)
        xs_ref[blk, :] = _dot(onehot, hx_ref[0, pl.ds(t0, win), :])

        def extra_row(r, carry, blk=blk, target=target):
            hit = jnp.where(slot_ref[0, 0, pl.ds(r, 1), :] == target, 1.0, 0.0).astype(BF16)
            tr = pl.multiple_of(r * TOKEN_ROW, TOKEN_ROW)
            xs_ref[blk, :] += _dot(hit, hx_ref[0, pl.ds(tr, TOKEN_ROW), :])
            return carry

        r_end = (edge_ref[base + s + 1] + TOKEN_ROW - 1) // TOKEN_ROW
        lax.fori_loop(r0 + wrows, r_end, extra_row, 0)

    xs = xs_ref[...].astype(BF16)
    h1 = _dot(xs, wg_ref[0].astype(BF16))
    h2 = _dot(xs, wu_ref[0].astype(BF16))
    hid = (h1 * jax.nn.sigmoid(h1) * h2).astype(BF16)
    ye_ref[0, 0] = _dot(hid, wd_ref[0].astype(BF16)).astype(BF16)


def _moe(edges, hx, slot4, wg, wu, wd, cap, win, sb):
    bsz, t, d = hx.shape
    ne, _, f = wg.shape
    nrow = t // TOKEN_ROW
    grid_spec = pltpu.PrefetchScalarGridSpec(
        num_scalar_prefetch=1,
        grid=(bsz, ne),
        in_specs=[pl.BlockSpec((1, t, d), lambda b, e, s: (b, 0, 0), pipeline_mode=pl.Buffered(1)),
                  pl.BlockSpec((1, 1, nrow, TOKEN_ROW), lambda b, e, s: (b, e, 0, 0)),
                  pl.BlockSpec((1, d, f), lambda b, e, s: (e, 0, 0)),
                  pl.BlockSpec((1, d, f), lambda b, e, s: (e, 0, 0)),
                  pl.BlockSpec((1, f, d), lambda b, e, s: (e, 0, 0))],
        out_specs=pl.BlockSpec((1, 1, cap, d), lambda b, e, s: (b, e, 0, 0)),
        scratch_shapes=[pltpu.VMEM((cap, d), F32)],
    )
    return pl.pallas_call(
        functools.partial(_moe_kernel, win, sb),
        grid_spec=grid_spec,
        out_shape=jax.ShapeDtypeStruct((bsz, ne, cap, d), BF16),
        compiler_params=_cparams(("parallel", "arbitrary")),
        name="moe",
    )(edges, hx, slot4, wg, wu, wd)


COMB_ROWS = 128
COMB_WIN = 64
SLOT_ALIGN = 16


def _comb_rows(win, groups, xm_ref, ye_ref, slotc, affc, g2_ref, fg_ref, o_ref):
    ne, cap = ye_ref.shape[1], ye_ref.shape[2]
    wide = ne * win
    lane_w = lax.broadcasted_iota(jnp.int32, (ne, wide), 1)
    expand = jnp.where(lane_w // win == lax.broadcasted_iota(jnp.int32, (ne, wide), 0), 1.0, 0.0).astype(BF16)
    slot_in_win = (lax.broadcasted_iota(jnp.int32, (COMB_ROWS, wide), 1) % win).astype(F32)
    lane_e = lax.broadcasted_iota(jnp.int32, (1, ne), 1)
    lhs, rhs = [], []
    for h, first in groups:
        rows = slice(h * COMB_ROWS, (h + 1) * COMB_ROWS)
        starts = [pl.multiple_of(jnp.minimum(first[e] & ~(SLOT_ALIGN - 1), cap - win), SLOT_ALIGN) for e in range(ne)]
        start_row = jnp.zeros((1, ne), jnp.int32)
        for e in range(ne):
            start_row = jnp.where(lane_e == e, starts[e], start_row)
        offset = _dot((slotc[rows, :] - start_row).astype(F32).astype(BF16), expand)
        val_hi, val_lo = _split2(affc[rows, :])
        hit = offset == slot_in_win
        hi = jnp.where(hit, _dot(val_hi, expand), 0.0).astype(BF16)
        lo = jnp.where(hit, _dot(val_lo, expand), 0.0).astype(BF16)
        lhs.append(jnp.concatenate([hi, lo], axis=0))
        rhs.append(jnp.concatenate([ye_ref[0, e, pl.ds(starts[e], win), :] for e in range(ne)], axis=0))
    both = [_dot(a, y) for a, y in zip(lhs, rhs)]
    for (h, _), bt in zip(groups, both):
        rows = slice(h * COMB_ROWS, (h + 1) * COMB_ROWS)
        x = xm_ref[0, rows, :] + g2_ref[0] * (bt[:COMB_ROWS] + bt[COMB_ROWS:])
        ms = jnp.mean(x * x, axis=-1, keepdims=True)
        o_ref[0, rows, :] = x * lax.rsqrt(ms + NORM_EPS) * fg_ref[...]


def _comb_kernel(tk, win, full, tsp_ref, xm_ref, ye_ref, slotc_ref, affc_ref, g2_ref, fg_ref, o_ref):
    b = pl.program_id(0)
    j = pl.program_id(1)
    nh = tk // COMB_ROWS
    ntile = pl.num_programs(1) * nh + 1
    ne, cap = ye_ref.shape[1], ye_ref.shape[2]
    slotc = slotc_ref[0]
    affc = affc_ref[0]
    refs = (xm_ref, ye_ref, slotc, affc, g2_ref, fg_ref, o_ref)
    groups = [(h, [tsp_ref[(b * ne + e) * ntile + j * nh + h] for e in range(ne)]) for h in range(nh)]
    _comb_rows(win, groups, *refs)
    if full != win:
        for h, first in groups:
            over = None
            for e in range(ne):
                start = jnp.minimum(first[e] & ~(SLOT_ALIGN - 1), cap - win)
                miss = tsp_ref[(b * ne + e) * ntile + j * nh + h + 1] > start + win
                over = miss if over is None else over | miss

            @pl.when(over)
            def _(h=h, first=first):
                _comb_rows(full, [(h, first)], *refs)


def _comb(tsp, xm, ye, slotc, affc, g2, fg, tk):
    bsz, t, d = xm.shape
    ne, cap = ye.shape[1], ye.shape[2]
    full = min(2 * COMB_ROWS, cap)
    assert full == cap or full >= COMB_ROWS + SLOT_ALIGN
    assert full <= 256
    win = min(COMB_WIN, full)
    grid_spec = pltpu.PrefetchScalarGridSpec(
        num_scalar_prefetch=1,
        grid=(bsz, t // tk),
        in_specs=[pl.BlockSpec((1, tk, d), lambda b, j, s: (b, j, 0)),
                  pl.BlockSpec((1, ne, cap, d), lambda b, j, s: (b, 0, 0, 0)),
                  pl.BlockSpec((1, tk, ne), lambda b, j, s: (b, j, 0)),
                  pl.BlockSpec((1, tk, ne), lambda b, j, s: (b, j, 0)),
                  pl.BlockSpec((1, 1, d), lambda b, j, s: (b, 0, 0)),
                  pl.BlockSpec((1, d), lambda b, j, s: (0, 0))],
        out_specs=pl.BlockSpec((1, tk, d), lambda b, j, s: (b, j, 0)),
    )
    return pl.pallas_call(
        functools.partial(_comb_kernel, tk, win, full),
        grid_spec=grid_spec,
        out_shape=jax.ShapeDtypeStruct((bsz, t, d), F32),
        compiler_params=_cparams(("parallel", "arbitrary")),
        name="comb",
    )(tsp, xm, ye, slotc, affc, g2, fg)


def _block_diag_ones(n, blk, value=1.0):
    r = jnp.arange(n)
    return jnp.where((r[:, None] // blk) == (r[None, :] // blk), value, 0.0)


def kernel(x, c, ctx, c_ctx, ada_w, ada_b, norm1_g, norm2_g, w_in, shift_mu, w0, w_lora_up, a0, a_lora_up, k_k, k_a,
           r_k, g_lora_up, gn_g, gn_b, conv_w, w_out, router_w, exp_w_gate, exp_w_up, exp_w_down, final_g):
    bsz, t, d = x.shape
    lc = ctx.shape[1]
    ne = router_w.shape[-1]
    cap = EC_CAPACITY * t // ne
    sb = min(SLOT_BLOCK, cap)
    win = min(GATHER_WIN, t)
    l = 0

    rows = ((bsz + 1 + 7) // 8) * 8
    cc = jnp.zeros((rows, d), F32).at[:bsz].set(c).at[bsz].set(c_ctx)
    mod = _mod(cc, ada_w[l], ada_b[l][None, :])
    sh1, sc1, g1, sh2, sc2, g2 = (m[:, None, :] for m in jnp.split(mod[:bsz], 6, axis=-1))
    csh1, csc1 = (jnp.broadcast_to(m[None, None, :], (bsz, 1, d)) for m in jnp.split(mod[bsz], 6)[:2])

    w_rw = w_in[l][:, :RWKV_COLS].astype(BF16)
    w_cv = w_in[l][:, RWKV_COLS:].astype(BF16)
    n1g = norm1_g[l][None, :]
    px_rw, px_cv = _in_proj(x, sh1, sc1, n1g, w_rw, w_cv, IN_PROJ_ROWS)
    pc_rw, _ = _in_proj(ctx, csh1, csc1, n1g, w_rw, w_cv, min(IN_PROJ_ROWS, lc))

    zw = jnp.zeros((LORA_W, 2 * D_RWKV), F32)
    lora = jnp.concatenate([
        jnp.concatenate([w_lora_up[l, 0], w_lora_up[l, 1], zw], axis=1),
        jnp.concatenate([zw, a_lora_up[l, 0], a_lora_up[l, 1]], axis=1)], axis=0).astype(BF16)
    ridx = jnp.arange(CHUNK)
    tril = jnp.where(ridx[None, :] <= ridx[:, None], 1.0, 0.0).astype(BF16)
    triu = jnp.where(ridx[None, :] >= ridx[:, None], 1.0, 0.0).astype(BF16)
    ones_bd = _block_diag_ones(PAIR, HEAD).astype(BF16)
    consts = (shift_mu[l][None, :], lora, w0[l], a0[l], k_k[l][None, :], k_a[l][None, :],
              r_k[l].reshape(1, D_RWKV), g_lora_up[l].astype(BF16), ones_bd, tril, triu)

    chunks_c = _prep(pc_rw, consts, False, lc)[:5]
    *chunks_x, bonus, gate = _prep(px_rw, consts, True, PREP_ROWS)

    s_zero = jnp.zeros((bsz, 2, N_PAIR, HEAD, PAIR), F32)
    _, _, s_ctx = _scan(*chunks_c, s_zero)
    yf, yb, _ = _scan(*chunks_x, s_ctx)

    rw_t = router_w[l].T
    rwh = rw_t.astype(BF16)
    rwl = (rw_t - rwh.astype(F32)).astype(BF16)
    avg = _block_diag_ones(PAIR, HEAD, 1.0 / HEAD).astype(BF16)
    xm, hx, aff_t = _out(yf, yb, bonus, gate, px_cv, x, g1, sh2, sc2, norm2_g[l][None, :], gn_g[l][None, :],
                         gn_b[l][None, :], conv_w[l], w_out[l].astype(BF16), rwh, rwl, avg, OUT_TILE)

    lane_idx = jnp.arange(TOKEN_ROW)
    tri_lanes = jnp.where(lane_idx[:, None] <= lane_idx[None, :], 1.0, 0.0).astype(BF16)
    cnt, slot, edges = _topk(aff_t, tri_lanes, cap, sb)

    ye = _moe(edges[:, :, :cap // sb + 1].reshape(-1), hx, slot.reshape(bsz, ne, t // TOKEN_ROW, TOKEN_ROW),
              exp_w_gate[l], exp_w_up[l], exp_w_down[l], cap, win, sb)
    tr = lambda a: jnp.transpose(a, (0, 2, 1))
    first_slot = jnp.concatenate([cnt[:, :, ::COMB_ROWS], jnp.full((bsz, ne, 1), cap, jnp.int32)], axis=-1).reshape(-1)
    return _comb(first_slot, xm, ye, tr(slot), tr(aff_t), g2, final_g[None, :], COMB_TILE)
```

```python
import functools
import math

import jax
import jax.numpy as jnp
from jax import lax
from jax.experimental import pallas as pl
from jax.experimental.pallas import tpu as pltpu

F32 = jnp.float32
BF16 = jnp.bfloat16
HIGHEST = lax.Precision.HIGHEST

GRID_W = 64
D_RWKV = 512
D_CONV = 512
HEAD = 64
LORA_W = 64
LORA_A = 64
LORA_G = 128
EC_CAPACITY = 2
NORM_EPS = 1e-6
GN_EPS = 64e-5
RWKV_COLS = 3 * D_RWKV + LORA_W + LORA_A + LORA_G
CONV_COLS = 3 * D_CONV

LANES = 128
CHUNK = 64
PAIR = 2 * HEAD
N_PAIR = D_RWKV // PAIR
VMEM_LIMIT = 48 * 1024 * 1024

IN_PROJ_ROWS = 512
PREP_ROWS = 512
OUT_TILE = 512
COMB_TILE = 512
SLOT_BLOCK = 128
GATHER_WIN = 1536


def _cparams(sem):
    return pltpu.CompilerParams(dimension_semantics=sem, vmem_limit_bytes=VMEM_LIMIT)


def _dot(a, b):
    return jnp.dot(a, b, preferred_element_type=F32)


def _dot_nt(a, b):
    return lax.dot_general(a, b, (((1,), (1,)), ((), ())), preferred_element_type=F32)


def _split2(x):
    hi = x.astype(BF16)
    lo = (x - hi.astype(F32)).astype(BF16)
    return hi, lo


def _split3(x):
    hi = x.astype(BF16)
    r = x - hi.astype(F32)
    mid = r.astype(BF16)
    lo = (r - mid.astype(F32)).astype(BF16)
    return hi, mid, lo


def _seg_dot(x, m):
    hi, lo = _split2(x)
    return _dot(hi, m) + _dot(lo, m)


def _mod_kernel(c_ref, w_ref, b_ref, o_ref):
    c = c_ref[...]
    s = c * jax.nn.sigmoid(c)
    o_ref[...] = jnp.dot(s, w_ref[...], precision=HIGHEST, preferred_element_type=F32) + b_ref[...]


def _mod(cc, w, b):
    rows, d = cc.shape
    n = w.shape[1]
    tn = 1024
    return pl.pallas_call(
        _mod_kernel,
        grid=(n // tn,),
        in_specs=[pl.BlockSpec((rows, d), lambda j: (0, 0)),
                  pl.BlockSpec((d, tn), lambda j: (0, j)),
                  pl.BlockSpec((1, tn), lambda j: (0, j))],
        out_specs=pl.BlockSpec((rows, tn), lambda j: (0, j)),
        out_shape=jax.ShapeDtypeStruct((rows, n), F32),
        compiler_params=_cparams(("parallel",)),
        name="mod",
    )(cc, w, b)


def _in_proj_kernel(x_ref, sh_ref, sc_ref, g_ref, wrw_ref, wcv_ref, orw_ref, ocv_ref):
    x = x_ref[0]
    ms = jnp.mean(x * x, axis=-1, keepdims=True)
    h = x * lax.rsqrt(ms + NORM_EPS) * g_ref[...]
    h = (h * (1.0 + sc_ref[0]) + sh_ref[0]).astype(BF16)
    orw_ref[0] = _dot(h, wrw_ref[...])
    ocv_ref[0] = _dot(h, wcv_ref[...]).astype(BF16)


def _in_proj(x, sh, sc, g, w_rw, w_cv, tm):
    bsz, l, d = x.shape
    return pl.pallas_call(
        _in_proj_kernel,
        grid=(bsz, l // tm),
        in_specs=[pl.BlockSpec((1, tm, d), lambda b, i: (b, i, 0)),
                  pl.BlockSpec((1, 1, d), lambda b, i: (b, 0, 0)),
                  pl.BlockSpec((1, 1, d), lambda b, i: (b, 0, 0)),
                  pl.BlockSpec((1, d), lambda b, i: (0, 0)),
                  pl.BlockSpec((d, RWKV_COLS), lambda b, i: (0, 0)),
                  pl.BlockSpec((d, CONV_COLS), lambda b, i: (0, 0))],
        out_specs=[pl.BlockSpec((1, tm, RWKV_COLS), lambda b, i: (b, i, 0)),
                   pl.BlockSpec((1, tm, CONV_COLS), lambda b, i: (b, i, 0))],
        out_shape=[jax.ShapeDtypeStruct((bsz, l, RWKV_COLS), F32),
                   jax.ShapeDtypeStruct((bsz, l, CONV_COLS), BF16)],
        compiler_params=_cparams(("parallel", "parallel")),
        name="in_proj",
    )(x, sh, sc, g, w_rw, w_cv)


def _bf(x):
    return x.astype(BF16)


def _chunk_local(chains):
    c = CHUNK
    n = range(len(chains))
    lane = lax.broadcasted_iota(jnp.int32, (c, PAIR), 1)
    h0 = lane < HEAD
    tcol = lane & (c - 1)
    trow = lax.broadcasted_iota(jnp.int32, (c, PAIR), 0)
    eye = jnp.where(tcol == trow, 1.0, 0.0)
    masks = {rev: ((tcol > trow) if rev else (tcol < trow), (tcol >= trow) if rev else (tcol <= trow))
             for rev in (False, True)}
    zb = jnp.zeros((c, PAIR), BF16)

    def stack(x):
        return jnp.concatenate([jnp.where(h0, x, zb), jnp.where(h0, zb, x)], axis=0)

    gram = [_dot_nt(jnp.concatenate([ch[1], ch[2]], axis=0), jnp.concatenate([stack(ch[3]), stack(ch[4])], axis=0))
            for ch in chains]
    lab = [jnp.where(masks[chains[i][0]][0], gram[i][:c, :PAIR], 0.0) for i in n]
    lak = [jnp.where(masks[chains[i][0]][0], gram[i][:c, PAIR:], 0.0) for i in n]
    mrb = [jnp.where(masks[chains[i][0]][1], gram[i][c:, :PAIR], 0.0) for i in n]
    mrk = [jnp.where(masks[chains[i][0]][1], gram[i][c:, PAIR:], 0.0) for i in n]

    labb = [_bf(x) for x in lab]
    pw = [_dot(labb[i], stack(labb[i])) for i in n]
    tp = [eye + lab[i] for i in n]
    for _ in range(4):
        pwb = [_bf(x) for x in pw]
        both = [_dot(pwb[i], jnp.concatenate([stack(pwb[i]), stack(_bf(tp[i]))], axis=1)) for i in n]
        pw = [x[:, :PAIR] for x in both]
        tp = [tp[i] + both[i][:, PAIR:] for i in n]
    tinv = [tp[i] + _dot(_bf(pw[i]), stack(_bf(tp[i]))) for i in n]

    lmv = [_dot(_bf(jnp.concatenate([lak[i], mrk[i]], axis=0)), stack(chains[i][7])) for i in n]
    x = [_dot(_bf(tinv[i]), jnp.concatenate([stack(chains[i][1]), stack(_bf(lmv[i][:c]))], axis=1))
         for i in n]
    z = [_dot(_bf(mrb[i]), jnp.concatenate([stack(_bf(x[i][:, :PAIR])), stack(_bf(x[i][:, PAIR:]))], axis=1))
         for i in n]
    rbar = [_bf(chains[i][2].astype(F32) + z[i][:, :PAIR]) for i in n]
    y0 = [z[i][:, PAIR:] + lmv[i][c:] for i in n]

    uv = [jnp.concatenate([x[i][:, PAIR:], chains[i][7].astype(F32)], axis=0) for i in n]
    mxf = [_dot(_bf(x[i][:, :PAIR].T), chains[i][5]) for i in n]
    nnf = [_dot(_bf(uv[i].T), jnp.concatenate([chains[i][5], chains[i][6]], axis=0)) for i in n]
    mx = [_bf(jnp.where(h0, m[:c], m[c:])) for m in mxf]
    nn = [jnp.where(h0, m[:c], m[c:]) for m in nnf]
    return list(zip(rbar, y0, mx, nn))


DIR_COLS = 6 * D_RWKV
SCR_COLS = 2 * DIR_COLS + D_RWKV


def _head_sums(x, ones):
    return jnp.concatenate([_seg_dot(x[:, g * PAIR:(g + 1) * PAIR], ones) for g in range(N_PAIR)], axis=1)


def _prep_rows(c, p, shifted, lo, mu_ref, w0_ref, a0_ref, kk_ref, ka_ref, rk_ref, ones_ref, tril_ref, triu_ref,
               wc_ref, bonus_ref, scr_ref):
    rows = slice(c * CHUNK, (c + 1) * CHUNK)
    pm = p + mu_ref[:, 0:3 * D_RWKV] * (shifted - p)
    xr = pm[:, 0:D_RWKV]
    xk = pm[:, D_RWKV:2 * D_RWKV]
    xv = pm[:, 2 * D_RWKV:3 * D_RWKV]

    ones = ones_ref[...]
    kraw = xk * kk_ref[...]
    kk = kraw / jnp.maximum(jnp.sqrt(_head_sums(kraw * kraw, ones)), 1e-12)

    scr_ref[:, 2 * DIR_COLS:] = xv.astype(BF16)
    ksum = None
    for d in range(2):
        z = w0_ref[d:d + 1, :] + lo[:, d * D_RWKV:(d + 1) * D_RWKV]
        ld = -math.exp(-0.5) * jax.nn.sigmoid(z)
        ag = jax.nn.sigmoid(a0_ref[d:d + 1, :] + lo[:, (2 + d) * D_RWKV:(3 + d) * D_RWKV])
        kd = xk * (1.0 + (ag - 1.0) * ka_ref[...])
        bb = kk * ag
        ksum = kd if ksum is None else ksum + kd
        tri = tril_ref[...] if d == 0 else triu_ref[...]
        h3 = _split3(ld)
        cs = _dot(tri, h3[0]) + _dot(tri, h3[1]) + _dot(tri, h3[2])
        e_in = jnp.exp(cs)
        e_ex = jnp.exp(cs - ld)
        e_neg = jnp.exp(-cs)
        bt = bb * e_neg
        kt = kd * e_neg
        edge = CHUNK - 1 if d == 0 else 0
        wrow = e_in[edge:edge + 1, :]
        wc_ref[0, c, :, d * D_RWKV:(d + 1) * D_RWKV] = wrow
        base = d * DIR_COLS
        for j, val in enumerate((-kk * e_ex, xr * e_in, bt, kt, bt * wrow, kt * wrow)):
            scr_ref[:, base + j * D_RWKV:base + (j + 1) * D_RWKV] = val.astype(BF16)

    bonus_ref[0, rows, :] = (_head_sums(xr * ksum * rk_ref[...], ones) * xv).astype(BF16)


PREP_GROUP = 2


def _prep_chunks(cs, scr_refs, rbar_ref, y0_ref, mx_ref, nn_ref):
    chains = []
    for scr_ref in scr_refs:
        for d in range(2):
            for pp in range(N_PAIR):
                cols = [d * DIR_COLS + j * D_RWKV + pp * PAIR for j in range(6)] + [2 * DIR_COLS + pp * PAIR]
                chains.append((d == 1,) + tuple(scr_ref[:, k:k + PAIR] for k in cols))
    per_chunk = 2 * N_PAIR
    for idx, (rbar, y0, mx, nn) in enumerate(_chunk_local(chains)):
        c = cs[idx // per_chunk]
        rows = slice(c * CHUNK, (c + 1) * CHUNK)
        col = (idx % per_chunk // N_PAIR) * D_RWKV + (idx % N_PAIR) * PAIR
        rbar_ref[0, rows, col:col + PAIR] = rbar
        y0_ref[0, rows, col:col + PAIR] = y0.astype(BF16)
        mx_ref[0, c, :, col:col + PAIR] = mx
        nn_ref[0, c, :, col:col + PAIR] = nn.astype(BF16)


N_PREP_CONSTS = 11


def _prep_kernel(latent, tt, p_ref, *rest):
    if latent:
        prev_ref, next_ref = rest[:2]
        rest = rest[2:]
    (mu_ref, lw_ref, w0_ref, a0_ref, kk_ref, ka_ref, rk_ref, gup_ref, ones_ref, tril_ref,
     triu_ref) = rest[:N_PREP_CONSTS]
    rbar_ref, y0_ref, mx_ref, nn_ref, wc_ref, bonus_ref, g_ref = rest[N_PREP_CONSTS:N_PREP_CONSTS + 7]
    *scr, lo_ref, ext_ref = rest[N_PREP_CONSTS + 7:]
    if latent:
        i = pl.program_id(1)
        n = pl.num_programs(1)
        ext_ref[0:GRID_W] = jnp.where(i > 0, prev_ref[0], 0.0)
        ext_ref[GRID_W:GRID_W + tt] = p_ref[0]
        ext_ref[GRID_W + tt:] = jnp.where(i < n - 1, next_ref[0], 0.0)

        def shifted(r0, nrows, cols):
            shape = (nrows, cols.stop - cols.start)
            trow = lax.broadcasted_iota(jnp.int32, shape, 0) & (GRID_W - 1)
            q = lax.broadcasted_iota(jnp.int32, shape, 1) & 3
            left = jnp.where(trow == 0, 0.0, ext_ref[GRID_W - 1 + r0:GRID_W - 1 + r0 + nrows, cols])
            right = jnp.where(trow == GRID_W - 1, 0.0, ext_ref[GRID_W + 1 + r0:GRID_W + 1 + r0 + nrows, cols])
            up = ext_ref[r0:r0 + nrows, cols]
            down = ext_ref[2 * GRID_W + r0:2 * GRID_W + r0 + nrows, cols]
            return jnp.where(q == 0, left, jnp.where(q == 1, right, jnp.where(q == 2, up, down)))
    else:
        ext_ref[0:8] = jnp.zeros((8, RWKV_COLS), F32)
        ext_ref[8:8 + tt] = p_ref[0]
        ext_ref[8 + tt:] = jnp.zeros((8, RWKV_COLS), F32)

        def shifted(r0, nrows, cols):
            q = lax.broadcasted_iota(jnp.int32, (nrows, cols.stop - cols.start), 1) & 1
            return jnp.where(q == 0, ext_ref[7 + r0:7 + r0 + nrows, cols], ext_ref[9 + r0:9 + r0 + nrows, cols])

    tail = slice(3 * D_RWKV, RWKV_COLS)
    p_t = p_ref[0, :, tail]
    pm_t = p_t + mu_ref[:, tail] * (shifted(0, tt, tail) - p_t)
    xwa = pm_t[:, 0:LORA_W + LORA_A]
    lane = lax.broadcasted_iota(jnp.int32, xwa.shape, 1)
    lin = jnp.where(lane < LORA_W, jnp.tanh(xwa), xwa).astype(BF16)
    lo_ref[...] = _dot(lin, lw_ref[...])
    g_ref[0] = _dot(jax.nn.sigmoid(pm_t[:, LORA_W + LORA_A:]).astype(BF16), gup_ref[...]).astype(BF16)

    head = slice(0, 3 * D_RWKV)

    def rows_part(c):
        rows = slice(c * CHUNK, (c + 1) * CHUNK)
        _prep_rows(c, p_ref[0, rows, head], shifted(c * CHUNK, CHUNK, head), lo_ref[rows, :], mu_ref, w0_ref,
                   a0_ref, kk_ref, ka_ref, rk_ref, ones_ref, tril_ref, triu_ref, wc_ref, bonus_ref,
                   scr[c % (2 * PREP_GROUP)])

    groups = [list(range(g, g + PREP_GROUP)) for g in range(0, tt // CHUNK, PREP_GROUP)]
    for c in groups[0]:
        rows_part(c)
    for k, cs in enumerate(groups):
        if k + 1 < len(groups):
            for c in groups[k + 1]:
                rows_part(c)
        _prep_chunks(cs, [scr[c % (2 * PREP_GROUP)] for c in cs], rbar_ref, y0_ref, mx_ref, nn_ref)


def _prep(p_rw, consts, latent, tt):
    bsz, l, _ = p_rw.shape
    nc = l // CHUNK
    cpt = tt // CHUNK
    assert len(consts) == N_PREP_CONSTS
    const_specs = [pl.BlockSpec(c.shape, lambda b, i, nd=c.ndim: (0,) * nd) for c in consts]
    kern = functools.partial(_prep_kernel, latent, tt)
    if latent:
        nblk = l // GRID_W
        in_specs = [pl.BlockSpec((1, tt, RWKV_COLS), lambda b, i: (b, i, 0)),
                    pl.BlockSpec((1, GRID_W, RWKV_COLS),
                                 lambda b, i: (b, jnp.maximum(i * (tt // GRID_W) - 1, 0), 0)),
                    pl.BlockSpec((1, GRID_W, RWKV_COLS),
                                 lambda b, i: (b, jnp.minimum((i + 1) * (tt // GRID_W), nblk - 1), 0))]
        args = (p_rw, p_rw, p_rw)
        ext_rows = tt + 2 * GRID_W
    else:
        assert tt == l
        in_specs = [pl.BlockSpec((1, tt, RWKV_COLS), lambda b, i: (b, i, 0))]
        args = (p_rw,)
        ext_rows = tt + 16
    row = lambda b, i: (b, i, 0)
    chunk = lambda b, i: (b, i, 0, 0)
    w2 = 2 * D_RWKV
    return pl.pallas_call(
        kern,
        grid=(bsz, l // tt),
        in_specs=in_specs + const_specs,
        out_specs=[pl.BlockSpec((1, tt, w2), row), pl.BlockSpec((1, tt, w2), row),
                   pl.BlockSpec((1, cpt, CHUNK, w2), chunk), pl.BlockSpec((1, cpt, CHUNK, w2), chunk),
                   pl.BlockSpec((1, cpt, 1, w2), chunk),
                   pl.BlockSpec((1, tt, D_RWKV), row), pl.BlockSpec((1, tt, D_RWKV), row)],
        out_shape=[jax.ShapeDtypeStruct((bsz, l, w2), BF16), jax.ShapeDtypeStruct((bsz, l, w2), BF16),
                   jax.ShapeDtypeStruct((bsz, nc, CHUNK, w2), BF16), jax.ShapeDtypeStruct((bsz, nc, CHUNK, w2), BF16),
                   jax.ShapeDtypeStruct((bsz, nc, 1, w2), F32),
                   jax.ShapeDtypeStruct((bsz, l, D_RWKV), BF16), jax.ShapeDtypeStruct((bsz, l, D_RWKV), BF16)],
        scratch_shapes=[pltpu.VMEM((CHUNK, SCR_COLS), BF16)] * (2 * PREP_GROUP)
        + [pltpu.VMEM((tt, 4 * D_RWKV), F32), pltpu.VMEM((ext_rows, RWKV_COLS), F32)],
        compiler_params=_cparams(("parallel", "parallel")),
        name="prep_latent" if latent else "prep_ctx",
    )(*args, *consts)


SCAN_CHUNKS = 4
SCAN_BATCH = 4


def _scan_kernel(rbf_ref, rbb_ref, y0f_ref, y0b_ref, mxf_ref, mxb_ref, nnf_ref, nnb_ref, wcf_ref, wcb_ref,
                 s0_ref, yf_ref, yb_ref, s_ref):
    i = pl.program_id(1)

    @pl.when(i == 0)
    def _():
        s_ref[...] = s0_ref[...]

    c = CHUNK
    h0 = lax.broadcasted_iota(jnp.int32, (c, PAIR), 1) < HEAD
    zb = jnp.zeros((c, PAIR), BF16)

    def stack(x):
        return jnp.concatenate([jnp.where(h0, x, zb), jnp.where(h0, zb, x)], axis=0)

    dirs = ((rbf_ref, y0f_ref, mxf_ref, nnf_ref, wcf_ref, yf_ref), (rbb_ref, y0b_ref, mxb_ref, nnb_ref, wcb_ref, yb_ref))
    idx = [(n, d, p) for n in range(SCAN_BATCH) for d in range(2) for p in range(N_PAIR)]
    lanes = [slice(p * PAIR, (p + 1) * PAIR) for _, _, p in idx]
    s = [s_ref[n, d, p] for n, d, p in idx]
    for q in range(SCAN_CHUNKS):
        cq = (q, SCAN_CHUNKS - 1 - q)
        rows = [slice(cq[d] * c, (cq[d] + 1) * c) for _, d, _ in idx]
        sb = [_bf(x) for x in s]
        y = [_dot_nt(dirs[d][0][n, rows[k], lanes[k]], stack(sb[k])) for k, (n, d, _) in enumerate(idx)]
        sm = [_dot(sb[k], stack(dirs[d][2][n, cq[d], :, lanes[k]])) for k, (n, d, _) in enumerate(idx)]
        s_next = []
        for k, (n, d, p) in enumerate(idx):
            dirs[d][5][n, rows[k], lanes[k]] = (y[k] + dirs[d][1][n, rows[k], lanes[k]].astype(F32)).astype(BF16)
            s_next.append(s[k] * dirs[d][4][n, cq[d], :, lanes[k]] + sm[k] + dirs[d][3][n, cq[d], :, lanes[k]].astype(F32))
        s = s_next
    for k, (n, d, p) in enumerate(idx):
        s_ref[n, d, p] = s[k]


def _scan(rbar, y0, mx, nn, wc, s0):
    bsz, l, _ = rbar.shape
    sub, nb = SCAN_CHUNKS, SCAN_BATCH
    assert bsz % nb == 0 and l % (CHUNK * sub) == 0
    nc = l // (CHUNK * sub)
    row_f = pl.BlockSpec((nb, sub * CHUNK, D_RWKV), lambda b, i: (b, i, 0))
    row_b = pl.BlockSpec((nb, sub * CHUNK, D_RWKV), lambda b, i: (b, nc - 1 - i, 1))
    chk_f = pl.BlockSpec((nb, sub, CHUNK, D_RWKV), lambda b, i: (b, i, 0, 0))
    chk_b = pl.BlockSpec((nb, sub, CHUNK, D_RWKV), lambda b, i: (b, nc - 1 - i, 0, 1))
    wc_f = pl.BlockSpec((nb, sub, 1, D_RWKV), lambda b, i: (b, i, 0, 0))
    wc_b = pl.BlockSpec((nb, sub, 1, D_RWKV), lambda b, i: (b, nc - 1 - i, 0, 1))
    st_spec = pl.BlockSpec((nb, 2, N_PAIR, HEAD, PAIR), lambda b, i: (b, 0, 0, 0, 0))
    return pl.pallas_call(
        _scan_kernel,
        grid=(bsz // nb, nc),
        in_specs=[row_f, row_b, row_f, row_b, chk_f, chk_b, chk_f, chk_b, wc_f, wc_b, st_spec],
        out_specs=[row_f, pl.BlockSpec((nb, sub * CHUNK, D_RWKV), lambda b, i: (b, nc - 1 - i, 0)), st_spec],
        out_shape=[jax.ShapeDtypeStruct((bsz, l, D_RWKV), BF16),
                   jax.ShapeDtypeStruct((bsz, l, D_RWKV), BF16),
                   jax.ShapeDtypeStruct((bsz, 2, N_PAIR, HEAD, PAIR), F32)],
        compiler_params=_cparams(("parallel", "arbitrary")),
        name="scan",
    )(rbar, rbar, y0, y0, mx, mx, nn, nn, wc, wc, s0)


OUT_ROWS = 128
HALO = 16


def _out_kernel(tt, yf_ref, yb_ref, bonus_ref, g_ref, cv_ref, cvp_ref, cvn_ref, x_ref, g1_ref, sh2_ref,
                sc2_ref, n2g_ref, gng_ref, gnb_ref, convw_ref, wout_ref, rwh_ref, rwl_ref, avg_ref,
                xm_ref, hx_ref, aff_ref):
    i = pl.program_id(1)
    n = pl.num_programs(1)

    cv = cv_ref[0].astype(F32)
    b_gate = cv[:, 0:D_CONV]
    cu = cv[:, D_CONV:2 * D_CONV] * cv[:, 2 * D_CONV:]
    cvp = cvp_ref[0].astype(F32)
    cvn = cvn_ref[0].astype(F32)
    cu_prev = jnp.where(i > 0, cvp[HALO - 1:HALO, D_CONV:2 * D_CONV] * cvp[HALO - 1:HALO, 2 * D_CONV:], 0.0)
    cu_next = jnp.where(i < n - 1, cvn[0:1, D_CONV:2 * D_CONV] * cvn[0:1, 2 * D_CONV:], 0.0)
    ridx = lax.broadcasted_iota(jnp.int32, cu.shape, 0)
    cu_m1 = jnp.where(ridx == 0, cu_prev, pltpu.roll(cu, 1, 0))
    cu_p1 = jnp.where(ridx == tt - 1, cu_next, pltpu.roll(cu, tt - 1, 0))
    conv = convw_ref[0:1, :] * cu_m1 + convw_ref[1:2, :] * cu + convw_ref[2:3, :] * cu_p1
    bx = (b_gate * conv).astype(BF16)

    parts = [slice(k * OUT_ROWS, (k + 1) * OUT_ROWS) for k in range(tt // OUT_ROWS)]
    avg = avg_ref[...]
    rwh = rwh_ref[...]
    rwl = rwl_ref[...]
    y = [yf_ref[0, r, :].astype(F32) + yb_ref[0, r, :].astype(F32) for r in parts]
    mu = [_head_sums(v, avg) for v in y]
    dlt = [a - b for a, b in zip(y, mu)]
    var = [_head_sums(v * v, avg) for v in dlt]
    yn = [a * lax.rsqrt(b + GN_EPS) * gng_ref[...] + gnb_ref[...] for a, b in zip(dlt, var)]
    ax = [((a + bonus_ref[0, r, :].astype(F32)) * g_ref[0, r, :].astype(F32)).astype(BF16) for a, r in zip(yn, parts)]
    mix = [_dot(a, wout_ref[0:D_RWKV, :]) + _dot(bx[r], wout_ref[D_RWKV:, :]) for a, r in zip(ax, parts)]
    xm = [x_ref[0, r, :] + g1_ref[0] * a for a, r in zip(mix, parts)]
    ms = [jnp.mean(v * v, axis=-1, keepdims=True) for v in xm]
    hx = [a * lax.rsqrt(b + NORM_EPS) * n2g_ref[...] for a, b in zip(xm, ms)]
    hx = [_split2(v * (1.0 + sc2_ref[0]) + sh2_ref[0]) for v in hx]
    logits = [_dot_nt(rwh, hi) + _dot_nt(rwh, lo) + _dot_nt(rwl, hi) for hi, lo in hx]
    for k, r in enumerate(parts):
        xm_ref[0, r, :] = xm[k]
        hx_ref[0, r, :] = hx[k][0]
        m = jnp.max(logits[k], axis=0, keepdims=True)
        ex = jnp.exp(logits[k] - m)
        aff_ref[0, :, r] = ex / jnp.sum(ex, axis=0, keepdims=True)


def _out(yf, yb, bonus, g, p_cv, x, g1, sh2, sc2, n2g, gng, gnb, convw, wout, rwh, rwl, avg, tt):
    bsz, t, d = x.shape
    ne = rwh.shape[0]
    nblk = t // HALO
    row = lambda b, i: (b, i, 0)
    per_b = lambda b, i: (b, 0, 0)
    const2 = lambda b, i: (0, 0)
    return pl.pallas_call(
        functools.partial(_out_kernel, tt),
        grid=(bsz, t // tt),
        in_specs=[pl.BlockSpec((1, tt, D_RWKV), row), pl.BlockSpec((1, tt, D_RWKV), row),
                  pl.BlockSpec((1, tt, D_RWKV), row), pl.BlockSpec((1, tt, D_RWKV), row),
                  pl.BlockSpec((1, tt, CONV_COLS), row),
                  pl.BlockSpec((1, HALO, CONV_COLS), lambda b, i: (b, jnp.maximum(i * (tt // HALO) - 1, 0), 0)),
                  pl.BlockSpec((1, HALO, CONV_COLS), lambda b, i: (b, jnp.minimum((i + 1) * (tt // HALO), nblk - 1), 0)),
                  pl.BlockSpec((1, tt, d), row),
                  pl.BlockSpec((1, 1, d), per_b), pl.BlockSpec((1, 1, d), per_b), pl.BlockSpec((1, 1, d), per_b),
                  pl.BlockSpec((1, d), const2), pl.BlockSpec((1, D_RWKV), const2), pl.BlockSpec((1, D_RWKV), const2),
                  pl.BlockSpec((3, D_CONV), const2), pl.BlockSpec((D_RWKV + D_CONV, d), const2),
                  pl.BlockSpec((ne, d), const2), pl.BlockSpec((ne, d), const2),
                  pl.BlockSpec((PAIR, PAIR), const2)],
        out_specs=[pl.BlockSpec((1, tt, d), row), pl.BlockSpec((1, tt, d), row),
                   pl.BlockSpec((1, ne, tt), lambda b, i: (b, 0, i))],
        out_shape=[jax.ShapeDtypeStruct((bsz, t, d), F32), jax.ShapeDtypeStruct((bsz, t, d), BF16),
                   jax.ShapeDtypeStruct((bsz, ne, t), F32)],
        compiler_params=_cparams(("parallel", "parallel")),
        name="out",
    )(yf, yb, bonus, g, p_cv, p_cv, p_cv, x, g1, sh2, sc2, n2g, gng, gnb, convw, wout, rwh, rwl, avg)


def _prefix_blocks(mask_fn, t, tri, emit):
    carry = None
    for j in range(t // LANES):
        m = mask_fn(j)
        inc = _dot(m.astype(BF16), tri)
        carry = jnp.zeros_like(inc[:, 0:1]) if carry is None else carry
        emit(j, m, inc - m + carry)
        carry = carry + inc[:, LANES - 1:LANES]


def _topk_kernel(cap, sb, aff_ref, tri_ref, cnt_ref, slot_ref, edge_ref):
    t = aff_ref.shape[2]
    aff = aff_ref[0]

    def body(k, bits):
        cand = bits | jnp.left_shift(jnp.int32(1), 30 - k)
        cnt = jnp.sum(jnp.where(aff >= pltpu.bitcast(cand, F32), 1, 0), axis=-1, keepdims=True)
        return jnp.where(cnt >= cap, cand, bits)

    bits = lax.fori_loop(0, 31, body, jnp.zeros((aff.shape[0], 1), jnp.int32))
    thr = pltpu.bitcast(bits, F32)
    above = pltpu.bitcast(bits + 1, F32)
    n_gt = jnp.sum(jnp.where(aff >= above, 1, 0), axis=-1, keepdims=True)
    need = (cap - n_gt).astype(F32)
    tri = tri_ref[...]

    def blk(j):
        return aff[:, j * LANES:(j + 1) * LANES]

    def emit_sel(j, eq, before):
        take = (blk(j) >= above) | ((eq > 0.5) & (before < need))
        slot_ref[0, :, j * LANES:(j + 1) * LANES] = jnp.where(take, 1, 0)

    _prefix_blocks(lambda j: jnp.where((blk(j) >= thr) & (blk(j) < above), 1.0, 0.0), t, tri, emit_sel)

    def emit_slot(j, m, before):
        count = before.astype(jnp.int32)
        cnt_ref[0, :, j * LANES:(j + 1) * LANES] = count
        slot_ref[0, :, j * LANES:(j + 1) * LANES] = jnp.where(m > 0.5, count, -1)

    _prefix_blocks(lambda j: slot_ref[0, :, j * LANES:(j + 1) * LANES].astype(F32), t, tri, emit_slot)

    cnt = cnt_ref[0]
    lane = lax.broadcasted_iota(jnp.int32, (cnt.shape[0], LANES), 1)
    edges = jnp.zeros((cnt.shape[0], LANES), jnp.int32)
    for s in range(1, cap // sb + 1):
        below = jnp.sum(jnp.where(cnt < s * sb, 1, 0), axis=-1, keepdims=True)
        edges = jnp.where(lane == s, below, edges)
    edge_ref[0] = edges


def _topk(aff_t, tri, cap, sb):
    bsz, ne, t = aff_t.shape
    spec = pl.BlockSpec((1, ne, t), lambda b: (b, 0, 0))
    return pl.pallas_call(
        functools.partial(_topk_kernel, cap, sb),
        grid=(bsz,),
        in_specs=[spec, pl.BlockSpec((LANES, LANES), lambda b: (0, 0))],
        out_specs=[spec, spec, pl.BlockSpec((1, ne, LANES), lambda b: (b, 0, 0))],
        out_shape=[jax.ShapeDtypeStruct((bsz, ne, t), jnp.int32)] * 2 + [jax.ShapeDtypeStruct((bsz, ne, LANES), jnp.int32)],
        compiler_params=_cparams(("parallel",)),
        name="topk",
    )(aff_t, tri)


TOKEN_ROW = LANES


def _moe_kernel(win, sb, edge_ref, hx_ref, slot_ref, wg_ref, wu_ref, wd_ref, ye_ref, xs_ref):
    b = pl.program_id(0)
    e = pl.program_id(1)
    ne = pl.num_programs(1)
    cap = xs_ref.shape[0]
    nblk = cap // sb
    nrow = hx_ref.shape[1] // TOKEN_ROW
    wrows = win // TOKEN_ROW
    base = (b * ne + e) * (nblk + 1)
    for s in range(nblk):
        blk = slice(s * sb, (s + 1) * sb)
        target = lax.broadcasted_iota(jnp.int32, (sb, TOKEN_ROW), 0) + s * sb
        r0 = jnp.minimum(edge_ref[base + s] // TOKEN_ROW, nrow - wrows)
        rows = slot_ref[0, 0, pl.ds(r0, wrows), :]
        onehot = jnp.concatenate([jnp.where(rows[k:k + 1, :] == target, 1.0, 0.0).astype(BF16)
                                  for k in range(wrows)], axis=1)
        t0 = pl.multiple_of(r0 * TOKEN_ROW, TOKEN_ROW)
        xs_ref[blk, :] = _dot(onehot, hx_ref[0, pl.ds(t0, win), :])

        def extra_row(r, carry, blk=blk, target=target):
            hit = jnp.where(slot_ref[0, 0, pl.ds(r, 1), :] == target, 1.0, 0.0).astype(BF16)
            tr = pl.multiple_of(r * TOKEN_ROW, TOKEN_ROW)
            xs_ref[blk, :] += _dot(hit, hx_ref[0, pl.ds(tr, TOKEN_ROW), :])
            return carry

        r_end = (edge_ref[base + s + 1] + TOKEN_ROW - 1) // TOKEN_ROW
        lax.fori_loop(r0 + wrows, r_end, extra_row, 0)

    xs = xs_ref[...].astype(BF16)
    h1 = _dot(xs, wg_ref[0].astype(BF16))
    h2 = _dot(xs, wu_ref[0].astype(BF16))
    hid = (h1 * jax.nn.sigmoid(h1) * h2).astype(BF16)
    ye_ref[0, 0] = _dot(hid, wd_ref[0].astype(BF16)).astype(BF16)


def _moe(edges, hx, slot4, wg, wu, wd, cap, win, sb):
    bsz, t, d = hx.shape
    ne, _, f = wg.shape
    nrow = t // TOKEN_ROW
    grid_spec = pltpu.PrefetchScalarGridSpec(
        num_scalar_prefetch=1,
        grid=(bsz, ne),
        in_specs=[pl.BlockSpec((1, t, d), lambda b, e, s: (b, 0, 0), pipeline_mode=pl.Buffered(1)),
                  pl.BlockSpec((1, 1, nrow, TOKEN_ROW), lambda b, e, s: (b, e, 0, 0)),
                  pl.BlockSpec((1, d, f), lambda b, e, s: (e, 0, 0)),
                  pl.BlockSpec((1, d, f), lambda b, e, s: (e, 0, 0)),
                  pl.BlockSpec((1, f, d), lambda b, e, s: (e, 0, 0))],
        out_specs=pl.BlockSpec((1, 1, cap, d), lambda b, e, s: (b, e, 0, 0)),
        scratch_shapes=[pltpu.VMEM((cap, d), F32)],
    )
    return pl.pallas_call(
        functools.partial(_moe_kernel, win, sb),
        grid_spec=grid_spec,
        out_shape=jax.ShapeDtypeStruct((bsz, ne, cap, d), BF16),
        compiler_params=_cparams(("parallel", "arbitrary")),
        name="moe",
    )(edges, hx, slot4, wg, wu, wd)


COMB_ROWS = 128
COMB_WIN = 64
SLOT_ALIGN = 16


def _comb_rows(win, groups, xm_ref, ye_ref, slotc, affc, g2_ref, fg_ref, o_ref):
    ne, cap = ye_ref.shape[1], ye_ref.shape[2]
    wide = ne * win
    lane_w = lax.broadcasted_iota(jnp.int32, (ne, wide), 1)
    expand = jnp.where(lane_w // win == lax.broadcasted_iota(jnp.int32, (ne, wide), 0), 1.0, 0.0).astype(BF16)
    slot_in_win = (lax.broadcasted_iota(jnp.int32, (COMB_ROWS, wide), 1) % win).astype(F32)
    lane_e = lax.broadcasted_iota(jnp.int32, (1, ne), 1)
    lhs, rhs = [], []
    for h, first in groups:
        rows = slice(h * COMB_ROWS, (h + 1) * COMB_ROWS)
        starts = [pl.multiple_of(jnp.minimum(first[e] & ~(SLOT_ALIGN - 1), cap - win), SLOT_ALIGN) for e in range(ne)]
        start_row = jnp.zeros((1, ne), jnp.int32)
        for e in range(ne):
            start_row = jnp.where(lane_e == e, starts[e], start_row)
        offset = _dot((slotc[rows, :] - start_row).astype(F32).astype(BF16), expand)
        val_hi, val_lo = _split2(affc[rows, :])
        hit = offset == slot_in_win
        hi = jnp.where(hit, _dot(val_hi, expand), 0.0).astype(BF16)
        lo = jnp.where(hit, _dot(val_lo, expand), 0.0).astype(BF16)
        lhs.append(jnp.concatenate([hi, lo], axis=0))
        rhs.append(jnp.concatenate([ye_ref[0, e, pl.ds(starts[e], win), :] for e in range(ne)], axis=0))
    both = [_dot(a, y) for a, y in zip(lhs, rhs)]
    for (h, _), bt in zip(groups, both):
        rows = slice(h * COMB_ROWS, (h + 1) * COMB_ROWS)
        x = xm_ref[0, rows, :] + g2_ref[0] * (bt[:COMB_ROWS] + bt[COMB_ROWS:])
        ms = jnp.mean(x * x, axis=-1, keepdims=True)
        o_ref[0, rows, :] = x * lax.rsqrt(ms + NORM_EPS) * fg_ref[...]


def _comb_kernel(tk, win, full, tsp_ref, xm_ref, ye_ref, slotc_ref, affc_ref, g2_ref, fg_ref, o_ref):
    b = pl.program_id(0)
    j = pl.program_id(1)
    nh = tk // COMB_ROWS
    ntile = pl.num_programs(1) * nh + 1
    ne, cap = ye_ref.shape[1], ye_ref.shape[2]
    slotc = slotc_ref[0]
    affc = affc_ref[0]
    refs = (xm_ref, ye_ref, slotc, affc, g2_ref, fg_ref, o_ref)
    groups = [(h, [tsp_ref[(b * ne + e) * ntile + j * nh + h] for e in range(ne)]) for h in range(nh)]
    _comb_rows(win, groups, *refs)
    if full != win:
        for h, first in groups:
            over = None
            for e in range(ne):
                start = jnp.minimum(first[e] & ~(SLOT_ALIGN - 1), cap - win)
                miss = tsp_ref[(b * ne + e) * ntile + j * nh + h + 1] > start + win
                over = miss if over is None else over | miss

            @pl.when(over)
            def _(h=h, first=first):
                _comb_rows(full, [(h, first)], *refs)


def _comb(tsp, xm, ye, slotc, affc, g2, fg, tk):
    bsz, t, d = xm.shape
    ne, cap = ye.shape[1], ye.shape[2]
    full = min(2 * COMB_ROWS, cap)
    assert full == cap or full >= COMB_ROWS + SLOT_ALIGN
    assert full <= 256
    win = min(COMB_WIN, full)
    grid_spec = pltpu.PrefetchScalarGridSpec(
        num_scalar_prefetch=1,
        grid=(bsz, t // tk),
        in_specs=[pl.BlockSpec((1, tk, d), lambda b, j, s: (b, j, 0)),
                  pl.BlockSpec((1, ne, cap, d), lambda b, j, s: (b, 0, 0, 0)),
                  pl.BlockSpec((1, tk, ne), lambda b, j, s: (b, j, 0)),
                  pl.BlockSpec((1, tk, ne), lambda b, j, s: (b, j, 0)),
                  pl.BlockSpec((1, 1, d), lambda b, j, s: (b, 0, 0)),
                  pl.BlockSpec((1, d), lambda b, j, s: (0, 0))],
        out_specs=pl.BlockSpec((1, tk, d), lambda b, j, s: (b, j, 0)),
    )
    return pl.pallas_call(
        functools.partial(_comb_kernel, tk, win, full),
        grid_spec=grid_spec,
        out_shape=jax.ShapeDtypeStruct((bsz, t, d), F32),
        compiler_params=_cparams(("parallel", "arbitrary")),
        name="comb",
    )(tsp, xm, ye, slotc, affc, g2, fg)


def _block_diag_ones(n, blk, value=1.0):
    r = jnp.arange(n)
    return jnp.where((r[:, None] // blk) == (r[None, :] // blk), value, 0.0)


def kernel(x, c, ctx, c_ctx, ada_w, ada_b, norm1_g, norm2_g, w_in, shift_mu, w0, w_lora_up, a0, a_lora_up, k_k, k_a,
           r_k, g_lora_up, gn_g, gn_b, conv_w, w_out, router_w, exp_w_gate, exp_w_up, exp_w_down, final_g):
    bsz, t, d = x.shape
    lc = ctx.shape[1]
    ne = router_w.shape[-1]
    cap = EC_CAPACITY * t // ne
    sb = min(SLOT_BLOCK, cap)
    win = min(GATHER_WIN, t)
    l = 0

    rows = ((bsz + 1 + 7) // 8) * 8
    cc = jnp.zeros((rows, d), F32).at[:bsz].set(c).at[bsz].set(c_ctx)
    mod = _mod(cc, ada_w[l], ada_b[l][None, :])
    sh1, sc1, g1, sh2, sc2, g2 = (m[:, None, :] for m in jnp.split(mod[:bsz], 6, axis=-1))
    csh1, csc1 = (jnp.broadcast_to(m[None, None, :], (bsz, 1, d)) for m in jnp.split(mod[bsz], 6)[:2])

    w_rw = w_in[l][:, :RWKV_COLS].astype(BF16)
    w_cv = w_in[l][:, RWKV_COLS:].astype(BF16)
    n1g = norm1_g[l][None, :]
    px_rw, px_cv = _in_proj(x, sh1, sc1, n1g, w_rw, w_cv, IN_PROJ_ROWS)
    pc_rw, _ = _in_proj(ctx, csh1, csc1, n1g, w_rw, w_cv, min(IN_PROJ_ROWS, lc))

    zw = jnp.zeros((LORA_W, 2 * D_RWKV), F32)
    lora = jnp.concatenate([
        jnp.concatenate([w_lora_up[l, 0], w_lora_up[l, 1], zw], axis=1),
        jnp.concatenate([zw, a_lora_up[l, 0], a_lora_up[l, 1]], axis=1)], axis=0).astype(BF16)
    ridx = jnp.arange(CHUNK)
    tril = jnp.where(ridx[None, :] <= ridx[:, None], 1.0, 0.0).astype(BF16)
    triu = jnp.where(ridx[None, :] >= ridx[:, None], 1.0, 0.0).astype(BF16)
    ones_bd = _block_diag_ones(PAIR, HEAD).astype(BF16)
    consts = (shift_mu[l][None, :], lora, w0[l], a0[l], k_k[l][None, :], k_a[l][None, :],
              r_k[l].reshape(1, D_RWKV), g_lora_up[l].astype(BF16), ones_bd, tril, triu)

    chunks_c = _prep(pc_rw, consts, False, lc)[:5]
    *chunks_x, bonus, gate = _prep(px_rw, consts, True, PREP_ROWS)

    s_zero = jnp.zeros((bsz, 2, N_PAIR, HEAD, PAIR), F32)
    _, _, s_ctx = _scan(*chunks_c, s_zero)
    yf, yb, _ = _scan(*chunks_x, s_ctx)

    rw_t = router_w[l].T
    rwh = rw_t.astype(BF16)
    rwl = (rw_t - rwh.astype(F32)).astype(BF16)
    avg = _block_diag_ones(PAIR, HEAD, 1.0 / HEAD).astype(BF16)
    xm, hx, aff_t = _out(yf, yb, bonus, gate, px_cv, x, g1, sh2, sc2, norm2_g[l][None, :], gn_g[l][None, :],
                         gn_b[l][None, :], conv_w[l], w_out[l].astype(BF16), rwh, rwl, avg, OUT_TILE)

    lane_idx = jnp.arange(TOKEN_ROW)
    tri_lanes = jnp.where(lane_idx[:, None] <= lane_idx[None, :], 1.0, 0.0).astype(BF16)
    cnt, slot, edges = _topk(aff_t, tri_lanes, cap, sb)

    ye = _moe(edges[:, :, :cap // sb + 1].reshape(-1), hx, slot.reshape(bsz, ne, t // TOKEN_ROW, TOKEN_ROW),
              exp_w_gate[l], exp_w_up[l], exp_w_down[l], cap, win, sb)
    tr = lambda a: jnp.transpose(a, (0, 2, 1))
    first_slot = jnp.concatenate([cnt[:, :, ::COMB_ROWS], jnp.full((bsz, ne, 1), cap, jnp.int32)], axis=-1).reshape(-1)
    return _comb(first_slot, xm, ye, tr(slot), tr(aff_t), g2, final_g[None, :], COMB_TILE)
```

```python
import functools
import math

import jax
import jax.numpy as jnp
from jax import lax
from jax.experimental import pallas as pl
from jax.experimental.pallas import tpu as pltpu

F32 = jnp.float32
BF16 = jnp.bfloat16
HIGHEST = lax.Precision.HIGHEST

GRID_W = 64
D_RWKV = 512
D_CONV = 512
HEAD = 64
LORA_W = 64
LORA_A = 64
LORA_G = 128
EC_CAPACITY = 2
NORM_EPS = 1e-6
GN_EPS = 64e-5
RWKV_COLS = 3 * D_RWKV + LORA_W + LORA_A + LORA_G
CONV_COLS = 3 * D_CONV

LANES = 128
CHUNK = 64
PAIR = 2 * HEAD
N_PAIR = D_RWKV // PAIR
VMEM_LIMIT = 48 * 1024 * 1024

IN_PROJ_ROWS = 512
PREP_ROWS = 512
OUT_TILE = 512
COMB_TILE = 512
SLOT_BLOCK = 128
GATHER_WIN = 1280


def _cparams(sem):
    return pltpu.CompilerParams(dimension_semantics=sem, vmem_limit_bytes=VMEM_LIMIT)


def _dot(a, b):
    return jnp.dot(a, b, preferred_element_type=F32)


def _dot_nt(a, b):
    return lax.dot_general(a, b, (((1,), (1,)), ((), ())), preferred_element_type=F32)


def _split2(x):
    hi = x.astype(BF16)
    lo = (x - hi.astype(F32)).astype(BF16)
    return hi, lo


def _split3(x):
    hi = x.astype(BF16)
    r = x - hi.astype(F32)
    mid = r.astype(BF16)
    lo = (r - mid.astype(F32)).astype(BF16)
    return hi, mid, lo


def _seg_dot(x, m):
    hi, lo = _split2(x)
    return _dot(hi, m) + _dot(lo, m)


def _mod_kernel(c_ref, w_ref, b_ref, o_ref):
    c = c_ref[...]
    s = c * jax.nn.sigmoid(c)
    o_ref[...] = jnp.dot(s, w_ref[...], precision=HIGHEST, preferred_element_type=F32) + b_ref[...]


def _mod(cc, w, b):
    rows, d = cc.shape
    n = w.shape[1]
    tn = 1024
    return pl.pallas_call(
        _mod_kernel,
        grid=(n // tn,),
        in_specs=[pl.BlockSpec((rows, d), lambda j: (0, 0)),
                  pl.BlockSpec((d, tn), lambda j: (0, j)),
                  pl.BlockSpec((1, tn), lambda j: (0, j))],
        out_specs=pl.BlockSpec((rows, tn), lambda j: (0, j)),
        out_shape=jax.ShapeDtypeStruct((rows, n), F32),
        compiler_params=_cparams(("parallel",)),
        name="mod",
    )(cc, w, b)


def _in_proj_kernel(x_ref, sh_ref, sc_ref, g_ref, wrw_ref, wcv_ref, orw_ref, ocv_ref):
    x = x_ref[0]
    ms = jnp.mean(x * x, axis=-1, keepdims=True)
    h = x * lax.rsqrt(ms + NORM_EPS) * g_ref[...]
    h = (h * (1.0 + sc_ref[0]) + sh_ref[0]).astype(BF16)
    orw_ref[0] = _dot(h, wrw_ref[...])
    ocv_ref[0] = _dot(h, wcv_ref[...]).astype(BF16)


def _in_proj(x, sh, sc, g, w_rw, w_cv, tm):
    bsz, l, d = x.shape
    return pl.pallas_call(
        _in_proj_kernel,
        grid=(bsz, l // tm),
        in_specs=[pl.BlockSpec((1, tm, d), lambda b, i: (b, i, 0)),
                  pl.BlockSpec((1, 1, d), lambda b, i: (b, 0, 0)),
                  pl.BlockSpec((1, 1, d), lambda b, i: (b, 0, 0)),
                  pl.BlockSpec((1, d), lambda b, i: (0, 0)),
                  pl.BlockSpec((d, RWKV_COLS), lambda b, i: (0, 0)),
                  pl.BlockSpec((d, CONV_COLS), lambda b, i: (0, 0))],
        out_specs=[pl.BlockSpec((1, tm, RWKV_COLS), lambda b, i: (b, i, 0)),
                   pl.BlockSpec((1, tm, CONV_COLS), lambda b, i: (b, i, 0))],
        out_shape=[jax.ShapeDtypeStruct((bsz, l, RWKV_COLS), F32),
                   jax.ShapeDtypeStruct((bsz, l, CONV_COLS), BF16)],
        compiler_params=_cparams(("parallel", "parallel")),
        name="in_proj",
    )(x, sh, sc, g, w_rw, w_cv)


def _bf(x):
    return x.astype(BF16)


def _chunk_local(chains):
    c = CHUNK
    n = range(len(chains))
    lane = lax.broadcasted_iota(jnp.int32, (c, PAIR), 1)
    h0 = lane < HEAD
    tcol = lane & (c - 1)
    trow = lax.broadcasted_iota(jnp.int32, (c, PAIR), 0)
    eye = jnp.where(tcol == trow, 1.0, 0.0)
    masks = {rev: ((tcol > trow) if rev else (tcol < trow), (tcol >= trow) if rev else (tcol <= trow))
             for rev in (False, True)}
    zb = jnp.zeros((c, PAIR), BF16)

    def stack(x):
        return jnp.concatenate([jnp.where(h0, x, zb), jnp.where(h0, zb, x)], axis=0)

    gram = [_dot_nt(jnp.concatenate([ch[1], ch[2]], axis=0), jnp.concatenate([stack(ch[3]), stack(ch[4])], axis=0))
            for ch in chains]
    lab = [jnp.where(masks[chains[i][0]][0], gram[i][:c, :PAIR], 0.0) for i in n]
    lak = [jnp.where(masks[chains[i][0]][0], gram[i][:c, PAIR:], 0.0) for i in n]
    mrb = [jnp.where(masks[chains[i][0]][1], gram[i][c:, :PAIR], 0.0) for i in n]
    mrk = [jnp.where(masks[chains[i][0]][1], gram[i][c:, PAIR:], 0.0) for i in n]

    labb = [_bf(x) for x in lab]
    pw = [_dot(labb[i], stack(labb[i])) for i in n]
    tp = [eye + lab[i] for i in n]
    for _ in range(4):
        pwb = [_bf(x) for x in pw]
        both = [_dot(pwb[i], jnp.concatenate([stack(pwb[i]), stack(_bf(tp[i]))], axis=1)) for i in n]
        pw = [x[:, :PAIR] for x in both]
        tp = [tp[i] + both[i][:, PAIR:] for i in n]
    tinv = [tp[i] + _dot(_bf(pw[i]), stack(_bf(tp[i]))) for i in n]

    lmv = [_dot(_bf(jnp.concatenate([lak[i], mrk[i]], axis=0)), stack(chains[i][7])) for i in n]
    x = [_dot(_bf(tinv[i]), jnp.concatenate([stack(chains[i][1]), stack(_bf(lmv[i][:c]))], axis=1))
         for i in n]
    z = [_dot(_bf(mrb[i]), jnp.concatenate([stack(_bf(x[i][:, :PAIR])), stack(_bf(x[i][:, PAIR:]))], axis=1))
         for i in n]
    rbar = [_bf(chains[i][2].astype(F32) + z[i][:, :PAIR]) for i in n]
    y0 = [z[i][:, PAIR:] + lmv[i][c:] for i in n]

    uv = [jnp.concatenate([x[i][:, PAIR:], chains[i][7].astype(F32)], axis=0) for i in n]
    mxf = [_dot(_bf(x[i][:, :PAIR].T), chains[i][5]) for i in n]
    nnf = [_dot(_bf(uv[i].T), jnp.concatenate([chains[i][5], chains[i][6]], axis=0)) for i in n]
    mx = [_bf(jnp.where(h0, m[:c], m[c:])) for m in mxf]
    nn = [jnp.where(h0, m[:c], m[c:]) for m in nnf]
    return list(zip(rbar, y0, mx, nn))


DIR_COLS = 6 * D_RWKV
SCR_COLS = 2 * DIR_COLS + D_RWKV


def _head_sums(x, ones):
    return jnp.concatenate([_seg_dot(x[:, g * PAIR:(g + 1) * PAIR], ones) for g in range(N_PAIR)], axis=1)


def _prep_rows(c, p, shifted, lo, mu_ref, w0_ref, a0_ref, kk_ref, ka_ref, rk_ref, ones_ref, tril_ref, triu_ref,
               wc_ref, bonus_ref, scr_ref):
    rows = slice(c * CHUNK, (c + 1) * CHUNK)
    pm = p + mu_ref[:, 0:3 * D_RWKV] * (shifted - p)
    xr = pm[:, 0:D_RWKV]
    xk = pm[:, D_RWKV:2 * D_RWKV]
    xv = pm[:, 2 * D_RWKV:3 * D_RWKV]

    ones = ones_ref[...]
    kraw = xk * kk_ref[...]
    kk = kraw / jnp.maximum(jnp.sqrt(_head_sums(kraw * kraw, ones)), 1e-12)

    scr_ref[:, 2 * DIR_COLS:] = xv.astype(BF16)
    ksum = None
    for d in range(2):
        z = w0_ref[d:d + 1, :] + lo[:, d * D_RWKV:(d + 1) * D_RWKV]
        ld = -math.exp(-0.5) * jax.nn.sigmoid(z)
        ag = jax.nn.sigmoid(a0_ref[d:d + 1, :] + lo[:, (2 + d) * D_RWKV:(3 + d) * D_RWKV])
        kd = xk * (1.0 + (ag - 1.0) * ka_ref[...])
        bb = kk * ag
        ksum = kd if ksum is None else ksum + kd
        tri = tril_ref[...] if d == 0 else triu_ref[...]
        h3 = _split3(ld)
        cs = _dot(tri, h3[0]) + _dot(tri, h3[1]) + _dot(tri, h3[2])
        e_in = jnp.exp(cs)
        e_ex = jnp.exp(cs - ld)
        e_neg = jnp.exp(-cs)
        bt = bb * e_neg
        kt = kd * e_neg
        edge = CHUNK - 1 if d == 0 else 0
        wrow = e_in[edge:edge + 1, :]
        wc_ref[0, c, :, d * D_RWKV:(d + 1) * D_RWKV] = wrow
        base = d * DIR_COLS
        for j, val in enumerate((-kk * e_ex, xr * e_in, bt, kt, bt * wrow, kt * wrow)):
            scr_ref[:, base + j * D_RWKV:base + (j + 1) * D_RWKV] = val.astype(BF16)

    bonus_ref[0, rows, :] = (_head_sums(xr * ksum * rk_ref[...], ones) * xv).astype(BF16)


PREP_GROUP = 2


def _prep_chunks(cs, scr_refs, rbar_ref, y0_ref, mx_ref, nn_ref):
    chains = []
    for scr_ref in scr_refs:
        for d in range(2):
            for pp in range(N_PAIR):
                cols = [d * DIR_COLS + j * D_RWKV + pp * PAIR for j in range(6)] + [2 * DIR_COLS + pp * PAIR]
                chains.append((d == 1,) + tuple(scr_ref[:, k:k + PAIR] for k in cols))
    per_chunk = 2 * N_PAIR
    for idx, (rbar, y0, mx, nn) in enumerate(_chunk_local(chains)):
        c = cs[idx // per_chunk]
        rows = slice(c * CHUNK, (c + 1) * CHUNK)
        col = (idx % per_chunk // N_PAIR) * D_RWKV + (idx % N_PAIR) * PAIR
        rbar_ref[0, rows, col:col + PAIR] = rbar
        y0_ref[0, rows, col:col + PAIR] = y0.astype(BF16)
        mx_ref[0, c, :, col:col + PAIR] = mx
        nn_ref[0, c, :, col:col + PAIR] = nn.astype(BF16)


N_PREP_CONSTS = 11


def _prep_kernel(latent, tt, p_ref, *rest):
    if latent:
        prev_ref, next_ref = rest[:2]
        rest = rest[2:]
    (mu_ref, lw_ref, w0_ref, a0_ref, kk_ref, ka_ref, rk_ref, gup_ref, ones_ref, tril_ref,
     triu_ref) = rest[:N_PREP_CONSTS]
    rbar_ref, y0_ref, mx_ref, nn_ref, wc_ref, bonus_ref, g_ref = rest[N_PREP_CONSTS:N_PREP_CONSTS + 7]
    *scr, lo_ref, ext_ref = rest[N_PREP_CONSTS + 7:]
    if latent:
        i = pl.program_id(1)
        n = pl.num_programs(1)
        ext_ref[0:GRID_W] = jnp.where(i > 0, prev_ref[0], 0.0)
        ext_ref[GRID_W:GRID_W + tt] = p_ref[0]
        ext_ref[GRID_W + tt:] = jnp.where(i < n - 1, next_ref[0], 0.0)

        def shifted(r0, nrows, cols):
            shape = (nrows, cols.stop - cols.start)
            trow = lax.broadcasted_iota(jnp.int32, shape, 0) & (GRID_W - 1)
            q = lax.broadcasted_iota(jnp.int32, shape, 1) & 3
            left = jnp.where(trow == 0, 0.0, ext_ref[GRID_W - 1 + r0:GRID_W - 1 + r0 + nrows, cols])
            right = jnp.where(trow == GRID_W - 1, 0.0, ext_ref[GRID_W + 1 + r0:GRID_W + 1 + r0 + nrows, cols])
            up = ext_ref[r0:r0 + nrows, cols]
            down = ext_ref[2 * GRID_W + r0:2 * GRID_W + r0 + nrows, cols]
            return jnp.where(q == 0, left, jnp.where(q == 1, right, jnp.where(q == 2, up, down)))
    else:
        ext_ref[0:8] = jnp.zeros((8, RWKV_COLS), F32)
        ext_ref[8:8 + tt] = p_ref[0]
        ext_ref[8 + tt:] = jnp.zeros((8, RWKV_COLS), F32)

        def shifted(r0, nrows, cols):
            q = lax.broadcasted_iota(jnp.int32, (nrows, cols.stop - cols.start), 1) & 1
            return jnp.where(q == 0, ext_ref[7 + r0:7 + r0 + nrows, cols], ext_ref[9 + r0:9 + r0 + nrows, cols])

    tail = slice(3 * D_RWKV, RWKV_COLS)
    p_t = p_ref[0, :, tail]
    pm_t = p_t + mu_ref[:, tail] * (shifted(0, tt, tail) - p_t)
    xwa = pm_t[:, 0:LORA_W + LORA_A]
    lane = lax.broadcasted_iota(jnp.int32, xwa.shape, 1)
    lin = jnp.where(lane < LORA_W, jnp.tanh(xwa), xwa).astype(BF16)
    lo_ref[...] = _dot(lin, lw_ref[...])
    g_ref[0] = _dot(jax.nn.sigmoid(pm_t[:, LORA_W + LORA_A:]).astype(BF16), gup_ref[...]).astype(BF16)

    head = slice(0, 3 * D_RWKV)

    def rows_part(c):
        rows = slice(c * CHUNK, (c + 1) * CHUNK)
        _prep_rows(c, p_ref[0, rows, head], shifted(c * CHUNK, CHUNK, head), lo_ref[rows, :], mu_ref, w0_ref,
                   a0_ref, kk_ref, ka_ref, rk_ref, ones_ref, tril_ref, triu_ref, wc_ref, bonus_ref,
                   scr[c % (2 * PREP_GROUP)])

    groups = [list(range(g, g + PREP_GROUP)) for g in range(0, tt // CHUNK, PREP_GROUP)]
    for c in groups[0]:
        rows_part(c)
    for k, cs in enumerate(groups):
        if k + 1 < len(groups):
            for c in groups[k + 1]:
                rows_part(c)
        _prep_chunks(cs, [scr[c % (2 * PREP_GROUP)] for c in cs], rbar_ref, y0_ref, mx_ref, nn_ref)


def _prep(p_rw, consts, latent, tt):
    bsz, l, _ = p_rw.shape
    nc = l // CHUNK
    cpt = tt // CHUNK
    assert len(consts) == N_PREP_CONSTS
    const_specs = [pl.BlockSpec(c.shape, lambda b, i, nd=c.ndim: (0,) * nd) for c in consts]
    kern = functools.partial(_prep_kernel, latent, tt)
    if latent:
        nblk = l // GRID_W
        in_specs = [pl.BlockSpec((1, tt, RWKV_COLS), lambda b, i: (b, i, 0)),
                    pl.BlockSpec((1, GRID_W, RWKV_COLS),
                                 lambda b, i: (b, jnp.maximum(i * (tt // GRID_W) - 1, 0), 0)),
                    pl.BlockSpec((1, GRID_W, RWKV_COLS),
                                 lambda b, i: (b, jnp.minimum((i + 1) * (tt // GRID_W), nblk - 1), 0))]
        args = (p_rw, p_rw, p_rw)
        ext_rows = tt + 2 * GRID_W
    else:
        assert tt == l
        in_specs = [pl.BlockSpec((1, tt, RWKV_COLS), lambda b, i: (b, i, 0))]
        args = (p_rw,)
        ext_rows = tt + 16
    row = lambda b, i: (b, i, 0)
    chunk = lambda b, i: (b, i, 0, 0)
    w2 = 2 * D_RWKV
    return pl.pallas_call(
        kern,
        grid=(bsz, l // tt),
        in_specs=in_specs + const_specs,
        out_specs=[pl.BlockSpec((1, tt, w2), row), pl.BlockSpec((1, tt, w2), row),
                   pl.BlockSpec((1, cpt, CHUNK, w2), chunk), pl.BlockSpec((1, cpt, CHUNK, w2), chunk),
                   pl.BlockSpec((1, cpt, 1, w2), chunk),
                   pl.BlockSpec((1, tt, D_RWKV), row), pl.BlockSpec((1, tt, D_RWKV), row)],
        out_shape=[jax.ShapeDtypeStruct((bsz, l, w2), BF16), jax.ShapeDtypeStruct((bsz, l, w2), BF16),
                   jax.ShapeDtypeStruct((bsz, nc, CHUNK, w2), BF16), jax.ShapeDtypeStruct((bsz, nc, CHUNK, w2), BF16),
                   jax.ShapeDtypeStruct((bsz, nc, 1, w2), F32),
                   jax.ShapeDtypeStruct((bsz, l, D_RWKV), BF16), jax.ShapeDtypeStruct((bsz, l, D_RWKV), BF16)],
        scratch_shapes=[pltpu.VMEM((CHUNK, SCR_COLS), BF16)] * (2 * PREP_GROUP)
        + [pltpu.VMEM((tt, 4 * D_RWKV), F32), pltpu.VMEM((ext_rows, RWKV_COLS), F32)],
        compiler_params=_cparams(("parallel", "parallel")),
        name="prep_latent" if latent else "prep_ctx",
    )(*args, *consts)


SCAN_CHUNKS = 4
SCAN_BATCH = 4


def _scan_kernel(rbf_ref, rbb_ref, y0f_ref, y0b_ref, mxf_ref, mxb_ref, nnf_ref, nnb_ref, wcf_ref, wcb_ref,
                 s0_ref, yf_ref, yb_ref, s_ref):
    i = pl.program_id(1)

    @pl.when(i == 0)
    def _():
        s_ref[...] = s0_ref[...]

    c = CHUNK
    h0 = lax.broadcasted_iota(jnp.int32, (c, PAIR), 1) < HEAD
    zb = jnp.zeros((c, PAIR), BF16)

    def stack(x):
        return jnp.concatenate([jnp.where(h0, x, zb), jnp.where(h0, zb, x)], axis=0)

    dirs = ((rbf_ref, y0f_ref, mxf_ref, nnf_ref, wcf_ref, yf_ref), (rbb_ref, y0b_ref, mxb_ref, nnb_ref, wcb_ref, yb_ref))
    idx = [(n, d, p) for n in range(SCAN_BATCH) for d in range(2) for p in range(N_PAIR)]
    lanes = [slice(p * PAIR, (p + 1) * PAIR) for _, _, p in idx]
    s = [s_ref[n, d, p] for n, d, p in idx]
    for q in range(SCAN_CHUNKS):
        cq = (q, SCAN_CHUNKS - 1 - q)
        rows = [slice(cq[d] * c, (cq[d] + 1) * c) for _, d, _ in idx]
        sb = [_bf(x) for x in s]
        y = [_dot_nt(dirs[d][0][n, rows[k], lanes[k]], stack(sb[k])) for k, (n, d, _) in enumerate(idx)]
        sm = [_dot(sb[k], stack(dirs[d][2][n, cq[d], :, lanes[k]])) for k, (n, d, _) in enumerate(idx)]
        s_next = []
        for k, (n, d, p) in enumerate(idx):
            dirs[d][5][n, rows[k], lanes[k]] = (y[k] + dirs[d][1][n, rows[k], lanes[k]].astype(F32)).astype(BF16)
            s_next.append(s[k] * dirs[d][4][n, cq[d], :, lanes[k]] + sm[k] + dirs[d][3][n, cq[d], :, lanes[k]].astype(F32))
        s = s_next
    for k, (n, d, p) in enumerate(idx):
        s_ref[n, d, p] = s[k]


def _scan(rbar, y0, mx, nn, wc, s0):
    bsz, l, _ = rbar.shape
    sub, nb = SCAN_CHUNKS, SCAN_BATCH
    assert bsz % nb == 0 and l % (CHUNK * sub) == 0
    nc = l // (CHUNK * sub)
    row_f = pl.BlockSpec((nb, sub * CHUNK, D_RWKV), lambda b, i: (b, i, 0))
    row_b = pl.BlockSpec((nb, sub * CHUNK, D_RWKV), lambda b, i: (b, nc - 1 - i, 1))
    chk_f = pl.BlockSpec((nb, sub, CHUNK, D_RWKV), lambda b, i: (b, i, 0, 0))
    chk_b = pl.BlockSpec((nb, sub, CHUNK, D_RWKV), lambda b, i: (b, nc - 1 - i, 0, 1))
    wc_f = pl.BlockSpec((nb, sub, 1, D_RWKV), lambda b, i: (b, i, 0, 0))
    wc_b = pl.BlockSpec((nb, sub, 1, D_RWKV), lambda b, i: (b, nc - 1 - i, 0, 1))
    st_spec = pl.BlockSpec((nb, 2, N_PAIR, HEAD, PAIR), lambda b, i: (b, 0, 0, 0, 0))
    return pl.pallas_call(
        _scan_kernel,
        grid=(bsz // nb, nc),
        in_specs=[row_f, row_b, row_f, row_b, chk_f, chk_b, chk_f, chk_b, wc_f, wc_b, st_spec],
        out_specs=[row_f, pl.BlockSpec((nb, sub * CHUNK, D_RWKV), lambda b, i: (b, nc - 1 - i, 0)), st_spec],
        out_shape=[jax.ShapeDtypeStruct((bsz, l, D_RWKV), BF16),
                   jax.ShapeDtypeStruct((bsz, l, D_RWKV), BF16),
                   jax.ShapeDtypeStruct((bsz, 2, N_PAIR, HEAD, PAIR), F32)],
        compiler_params=_cparams(("parallel", "arbitrary")),
        name="scan",
    )(rbar, rbar, y0, y0, mx, mx, nn, nn, wc, wc, s0)


OUT_ROWS = 128
HALO = 16


def _out_kernel(tt, yf_ref, yb_ref, bonus_ref, g_ref, cv_ref, cvp_ref, cvn_ref, x_ref, g1_ref, sh2_ref,
                sc2_ref, n2g_ref, gng_ref, gnb_ref, convw_ref, wout_ref, rwh_ref, rwl_ref, avg_ref,
                xm_ref, hx_ref, aff_ref):
    i = pl.program_id(1)
    n = pl.num_programs(1)

    cv = cv_ref[0].astype(F32)
    b_gate = cv[:, 0:D_CONV]
    cu = cv[:, D_CONV:2 * D_CONV] * cv[:, 2 * D_CONV:]
    cvp = cvp_ref[0].astype(F32)
    cvn = cvn_ref[0].astype(F32)
    cu_prev = jnp.where(i > 0, cvp[HALO - 1:HALO, D_CONV:2 * D_CONV] * cvp[HALO - 1:HALO, 2 * D_CONV:], 0.0)
    cu_next = jnp.where(i < n - 1, cvn[0:1, D_CONV:2 * D_CONV] * cvn[0:1, 2 * D_CONV:], 0.0)
    ridx = lax.broadcasted_iota(jnp.int32, cu.shape, 0)
    cu_m1 = jnp.where(ridx == 0, cu_prev, pltpu.roll(cu, 1, 0))
    cu_p1 = jnp.where(ridx == tt - 1, cu_next, pltpu.roll(cu, tt - 1, 0))
    conv = convw_ref[0:1, :] * cu_m1 + convw_ref[1:2, :] * cu + convw_ref[2:3, :] * cu_p1
    bx = (b_gate * conv).astype(BF16)

    parts = [slice(k * OUT_ROWS, (k + 1) * OUT_ROWS) for k in range(tt // OUT_ROWS)]
    avg = avg_ref[...]
    rwh = rwh_ref[...]
    rwl = rwl_ref[...]
    y = [yf_ref[0, r, :].astype(F32) + yb_ref[0, r, :].astype(F32) for r in parts]
    mu = [_head_sums(v, avg) for v in y]
    dlt = [a - b for a, b in zip(y, mu)]
    var = [_head_sums(v * v, avg) for v in dlt]
    yn = [a * lax.rsqrt(b + GN_EPS) * gng_ref[...] + gnb_ref[...] for a, b in zip(dlt, var)]
    ax = [((a + bonus_ref[0, r, :].astype(F32)) * g_ref[0, r, :].astype(F32)).astype(BF16) for a, r in zip(yn, parts)]
    mix = [_dot(a, wout_ref[0:D_RWKV, :]) + _dot(bx[r], wout_ref[D_RWKV:, :]) for a, r in zip(ax, parts)]
    xm = [x_ref[0, r, :] + g1_ref[0] * a for a, r in zip(mix, parts)]
    ms = [jnp.mean(v * v, axis=-1, keepdims=True) for v in xm]
    hx = [a * lax.rsqrt(b + NORM_EPS) * n2g_ref[...] for a, b in zip(xm, ms)]
    hx = [_split2(v * (1.0 + sc2_ref[0]) + sh2_ref[0]) for v in hx]
    logits = [_dot_nt(rwh, hi) + _dot_nt(rwh, lo) + _dot_nt(rwl, hi) for hi, lo in hx]
    for k, r in enumerate(parts):
        xm_ref[0, r, :] = xm[k]
        hx_ref[0, r, :] = hx[k][0]
        m = jnp.max(logits[k], axis=0, keepdims=True)
        ex = jnp.exp(logits[k] - m)
        aff_ref[0, :, r] = ex / jnp.sum(ex, axis=0, keepdims=True)


def _out(yf, yb, bonus, g, p_cv, x, g1, sh2, sc2, n2g, gng, gnb, convw, wout, rwh, rwl, avg, tt):
    bsz, t, d = x.shape
    ne = rwh.shape[0]
    nblk = t // HALO
    row = lambda b, i: (b, i, 0)
    per_b = lambda b, i: (b, 0, 0)
    const2 = lambda b, i: (0, 0)
    return pl.pallas_call(
        functools.partial(_out_kernel, tt),
        grid=(bsz, t // tt),
        in_specs=[pl.BlockSpec((1, tt, D_RWKV), row), pl.BlockSpec((1, tt, D_RWKV), row),
                  pl.BlockSpec((1, tt, D_RWKV), row), pl.BlockSpec((1, tt, D_RWKV), row),
                  pl.BlockSpec((1, tt, CONV_COLS), row),
                  pl.BlockSpec((1, HALO, CONV_COLS), lambda b, i: (b, jnp.maximum(i * (tt // HALO) - 1, 0), 0)),
                  pl.BlockSpec((1, HALO, CONV_COLS), lambda b, i: (b, jnp.minimum((i + 1) * (tt // HALO), nblk - 1), 0)),
                  pl.BlockSpec((1, tt, d), row),
                  pl.BlockSpec((1, 1, d), per_b), pl.BlockSpec((1, 1, d), per_b), pl.BlockSpec((1, 1, d), per_b),
                  pl.BlockSpec((1, d), const2), pl.BlockSpec((1, D_RWKV), const2), pl.BlockSpec((1, D_RWKV), const2),
                  pl.BlockSpec((3, D_CONV), const2), pl.BlockSpec((D_RWKV + D_CONV, d), const2),
                  pl.BlockSpec((ne, d), const2), pl.BlockSpec((ne, d), const2),
                  pl.BlockSpec((PAIR, PAIR), const2)],
        out_specs=[pl.BlockSpec((1, tt, d), row), pl.BlockSpec((1, tt, d), row),
                   pl.BlockSpec((1, ne, tt), lambda b, i: (b, 0, i))],
        out_shape=[jax.ShapeDtypeStruct((bsz, t, d), F32), jax.ShapeDtypeStruct((bsz, t, d), BF16),
                   jax.ShapeDtypeStruct((bsz, ne, t), F32)],
        compiler_params=_cparams(("parallel", "parallel")),
        name="out",
    )(yf, yb, bonus, g, p_cv, p_cv, p_cv, x, g1, sh2, sc2, n2g, gng, gnb, convw, wout, rwh, rwl, avg)


def _prefix_blocks(mask_fn, t, tri, emit):
    carry = None
    for j in range(t // LANES):
        m = mask_fn(j)
        inc = _dot(m.astype(BF16), tri)
        carry = jnp.zeros_like(inc[:, 0:1]) if carry is None else carry
        emit(j, m, inc - m + carry)
        carry = carry + inc[:, LANES - 1:LANES]


def _topk_kernel(cap, sb, aff_ref, tri_ref, cnt_ref, slot_ref, edge_ref):
    t = aff_ref.shape[2]
    aff = aff_ref[0]

    def body(k, bits):
        cand = bits | jnp.left_shift(jnp.int32(1), 30 - k)
        cnt = jnp.sum(jnp.where(aff >= pltpu.bitcast(cand, F32), 1, 0), axis=-1, keepdims=True)
        return jnp.where(cnt >= cap, cand, bits)

    bits = lax.fori_loop(0, 31, body, jnp.zeros((aff.shape[0], 1), jnp.int32))
    thr = pltpu.bitcast(bits, F32)
    above = pltpu.bitcast(bits + 1, F32)
    n_gt = jnp.sum(jnp.where(aff >= above, 1, 0), axis=-1, keepdims=True)
    need = (cap - n_gt).astype(F32)
    tri = tri_ref[...]

    def blk(j):
        return aff[:, j * LANES:(j + 1) * LANES]

    def emit_sel(j, eq, before):
        take = (blk(j) >= above) | ((eq > 0.5) & (before < need))
        slot_ref[0, :, j * LANES:(j + 1) * LANES] = jnp.where(take, 1, 0)

    _prefix_blocks(lambda j: jnp.where((blk(j) >= thr) & (blk(j) < above), 1.0, 0.0), t, tri, emit_sel)

    def emit_slot(j, m, before):
        count = before.astype(jnp.int32)
        cnt_ref[0, :, j * LANES:(j + 1) * LANES] = count
        slot_ref[0, :, j * LANES:(j + 1) * LANES] = jnp.where(m > 0.5, count, -1)

    _prefix_blocks(lambda j: slot_ref[0, :, j * LANES:(j + 1) * LANES].astype(F32), t, tri, emit_slot)

    cnt = cnt_ref[0]
    lane = lax.broadcasted_iota(jnp.int32, (cnt.shape[0], LANES), 1)
    edges = jnp.zeros((cnt.shape[0], LANES), jnp.int32)
    for s in range(1, cap // sb + 1):
        below = jnp.sum(jnp.where(cnt < s * sb, 1, 0), axis=-1, keepdims=True)
        edges = jnp.where(lane == s, below, edges)
    edge_ref[0] = edges


def _topk(aff_t, tri, cap, sb):
    bsz, ne, t = aff_t.shape
    spec = pl.BlockSpec((1, ne, t), lambda b: (b, 0, 0))
    return pl.pallas_call(
        functools.partial(_topk_kernel, cap, sb),
        grid=(bsz,),
        in_specs=[spec, pl.BlockSpec((LANES, LANES), lambda b: (0, 0))],
        out_specs=[spec, spec, pl.BlockSpec((1, ne, LANES), lambda b: (b, 0, 0))],
        out_shape=[jax.ShapeDtypeStruct((bsz, ne, t), jnp.int32)] * 2 + [jax.ShapeDtypeStruct((bsz, ne, LANES), jnp.int32)],
        compiler_params=_cparams(("parallel",)),
        name="topk",
    )(aff_t, tri)


TOKEN_ROW = LANES


def _moe_kernel(win, sb, edge_ref, hx_ref, slot_ref, wg_ref, wu_ref, wd_ref, ye_ref, xs_ref):
    b = pl.program_id(0)
    e = pl.program_id(1)
    ne = pl.num_programs(1)
    cap = xs_ref.shape[0]
    nblk = cap // sb
    nrow = hx_ref.shape[1] // TOKEN_ROW
    wrows = win // TOKEN_ROW
    base = (b * ne + e) * (nblk + 1)
    for s in range(nblk):
        blk = slice(s * sb, (s + 1) * sb)
        target = lax.broadcasted_iota(jnp.int32, (sb, TOKEN_ROW), 0) + s * sb
        r0 = jnp.minimum(edge_ref[base + s] // TOKEN_ROW, nrow - wrows)
        rows = slot_ref[0, 0, pl.ds(r0, wrows), :]
        onehot = jnp.concatenate([jnp.where(rows[k:k + 1, :] == target, 1.0, 0.0).astype(BF16)
                                  for k in range(wrows)], axis=1)
        t0 = pl.multiple_of(r0 * TOKEN_ROW, TOKEN_ROW)
        xs_ref[blk, :] = _dot(onehot, hx_ref[0, pl.ds(t0, win), :])

        def extra_row(r, carry, blk=blk, target=target):
            hit = jnp.where(slot_ref[0, 0, pl.ds(r, 1), :] == target, 1.0, 0.0).astype(BF16)
            tr = pl.multiple_of(r * TOKEN_ROW, TOKEN_ROW)
            xs_ref[blk, :] += _dot(hit, hx_ref[0, pl.ds(tr, TOKEN_ROW), :])
            return carry

        r_end = (edge_ref[base + s + 1] + TOKEN_ROW - 1) // TOKEN_ROW
        lax.fori_loop(r0 + wrows, r_end, extra_row, 0)

    xs = xs_ref[...].astype(BF16)
    h1 = _dot(xs, wg_ref[0].astype(BF16))
    h2 = _dot(xs, wu_ref[0].astype(BF16))
    hid = (h1 * jax.nn.sigmoid(h1) * h2).astype(BF16)
    ye_ref[0, 0] = _dot(hid, wd_ref[0].astype(BF16)).astype(BF16)


def _moe(edges, hx, slot4, wg, wu, wd, cap, win, sb):
    bsz, t, d = hx.shape
    ne, _, f = wg.shape
    nrow = t // TOKEN_ROW
    grid_spec = pltpu.PrefetchScalarGridSpec(
        num_scalar_prefetch=1,
        grid=(bsz, ne),
        in_specs=[pl.BlockSpec((1, t, d), lambda b, e, s: (b, 0, 0), pipeline_mode=pl.Buffered(1)),
                  pl.BlockSpec((1, 1, nrow, TOKEN_ROW), lambda b, e, s: (b, e, 0, 0)),
                  pl.BlockSpec((1, d, f), lambda b, e, s: (e, 0, 0)),
                  pl.BlockSpec((1, d, f), lambda b, e, s: (e, 0, 0)),
                  pl.BlockSpec((1, f, d), lambda b, e, s: (e, 0, 0))],
        out_specs=pl.BlockSpec((1, 1, cap, d), lambda b, e, s: (b, e, 0, 0)),
        scratch_shapes=[pltpu.VMEM((cap, d), F32)],
    )
    return pl.pallas_call(
        functools.partial(_moe_kernel, win, sb),
        grid_spec=grid_spec,
        out_shape=jax.ShapeDtypeStruct((bsz, ne, cap, d), BF16),
        compiler_params=_cparams(("parallel", "arbitrary")),
        name="moe",
    )(edges, hx, slot4, wg, wu, wd)


COMB_ROWS = 128
COMB_WIN = 64
SLOT_ALIGN = 16


def _comb_rows(win, groups, xm_ref, ye_ref, slotc, affc, g2_ref, fg_ref, o_ref):
    ne, cap = ye_ref.shape[1], ye_ref.shape[2]
    wide = ne * win
    lane_w = lax.broadcasted_iota(jnp.int32, (ne, wide), 1)
    expand = jnp.where(lane_w // win == lax.broadcasted_iota(jnp.int32, (ne, wide), 0), 1.0, 0.0).astype(BF16)
    slot_in_win = (lax.broadcasted_iota(jnp.int32, (COMB_ROWS, wide), 1) % win).astype(F32)
    lane_e = lax.broadcasted_iota(jnp.int32, (1, ne), 1)
    lhs, rhs = [], []
    for h, first in groups:
        rows = slice(h * COMB_ROWS, (h + 1) * COMB_ROWS)
        starts = [pl.multiple_of(jnp.minimum(first[e] & ~(SLOT_ALIGN - 1), cap - win), SLOT_ALIGN) for e in range(ne)]
        start_row = jnp.zeros((1, ne), jnp.int32)
        for e in range(ne):
            start_row = jnp.where(lane_e == e, starts[e], start_row)
        offset = _dot((slotc[rows, :] - start_row).astype(F32).astype(BF16), expand)
        val_hi, val_lo = _split2(affc[rows, :])
        hit = offset == slot_in_win
        hi = jnp.where(hit, _dot(val_hi, expand), 0.0).astype(BF16)
        lo = jnp.where(hit, _dot(val_lo, expand), 0.0).astype(BF16)
        lhs.append(jnp.concatenate([hi, lo], axis=0))
        rhs.append(jnp.concatenate([ye_ref[0, e, pl.ds(starts[e], win), :] for e in range(ne)], axis=0))
    both = [_dot(a, y) for a, y in zip(lhs, rhs)]
    for (h, _), bt in zip(groups, both):
        rows = slice(h * COMB_ROWS, (h + 1) * COMB_ROWS)
        x = xm_ref[0, rows, :] + g2_ref[0] * (bt[:COMB_ROWS] + bt[COMB_ROWS:])
        ms = jnp.mean(x * x, axis=-1, keepdims=True)
        o_ref[0, rows, :] = x * lax.rsqrt(ms + NORM_EPS) * fg_ref[...]


def _comb_kernel(tk, win, full, tsp_ref, xm_ref, ye_ref, slotc_ref, affc_ref, g2_ref, fg_ref, o_ref):
    b = pl.program_id(0)
    j = pl.program_id(1)
    nh = tk // COMB_ROWS
    ntile = pl.num_programs(1) * nh + 1
    ne, cap = ye_ref.shape[1], ye_ref.shape[2]
    slotc = slotc_ref[0]
    affc = affc_ref[0]
    refs = (xm_ref, ye_ref, slotc, affc, g2_ref, fg_ref, o_ref)
    groups = [(h, [tsp_ref[(b * ne + e) * ntile + j * nh + h] for e in range(ne)]) for h in range(nh)]
    _comb_rows(win, groups, *refs)
    if full != win:
        for h, first in groups:
            over = None
            for e in range(ne):
                start = jnp.minimum(first[e] & ~(SLOT_ALIGN - 1), cap - win)
                miss = tsp_ref[(b * ne + e) * ntile + j * nh + h + 1] > start + win
                over = miss if over is None else over | miss

            @pl.when(over)
            def _(h=h, first=first):
                _comb_rows(full, [(h, first)], *refs)


def _comb(tsp, xm, ye, slotc, affc, g2, fg, tk):
    bsz, t, d = xm.shape
    ne, cap = ye.shape[1], ye.shape[2]
    full = min(2 * COMB_ROWS, cap)
    assert full == cap or full >= COMB_ROWS + SLOT_ALIGN
    assert full <= 256
    win = min(COMB_WIN, full)
    grid_spec = pltpu.PrefetchScalarGridSpec(
        num_scalar_prefetch=1,
        grid=(bsz, t // tk),
        in_specs=[pl.BlockSpec((1, tk, d), lambda b, j, s: (b, j, 0)),
                  pl.BlockSpec((1, ne, cap, d), lambda b, j, s: (b, 0, 0, 0)),
                  pl.BlockSpec((1, tk, ne), lambda b, j, s: (b, j, 0)),
                  pl.BlockSpec((1, tk, ne), lambda b, j, s: (b, j, 0)),
                  pl.BlockSpec((1, 1, d), lambda b, j, s: (b, 0, 0)),
                  pl.BlockSpec((1, d), lambda b, j, s: (0, 0))],
        out_specs=pl.BlockSpec((1, tk, d), lambda b, j, s: (b, j, 0)),
    )
    return pl.pallas_call(
        functools.partial(_comb_kernel, tk, win, full),
        grid_spec=grid_spec,
        out_shape=jax.ShapeDtypeStruct((bsz, t, d), F32),
        compiler_params=_cparams(("parallel", "arbitrary")),
        name="comb",
    )(tsp, xm, ye, slotc, affc, g2, fg)


def _block_diag_ones(n, blk, value=1.0):
    r = jnp.arange(n)
    return jnp.where((r[:, None] // blk) == (r[None, :] // blk), value, 0.0)


def kernel(x, c, ctx, c_ctx, ada_w, ada_b, norm1_g, norm2_g, w_in, shift_mu, w0, w_lora_up, a0, a_lora_up, k_k, k_a,
           r_k, g_lora_up, gn_g, gn_b, conv_w, w_out, router_w, exp_w_gate, exp_w_up, exp_w_down, final_g):
    bsz, t, d = x.shape
    lc = ctx.shape[1]
    ne = router_w.shape[-1]
    cap = EC_CAPACITY * t // ne
    sb = min(SLOT_BLOCK, cap)
    win = min(GATHER_WIN, t)
    l = 0

    rows = ((bsz + 1 + 7) // 8) * 8
    cc = jnp.zeros((rows, d), F32).at[:bsz].set(c).at[bsz].set(c_ctx)
    mod = _mod(cc, ada_w[l], ada_b[l][None, :])
    sh1, sc1, g1, sh2, sc2, g2 = (m[:, None, :] for m in jnp.split(mod[:bsz], 6, axis=-1))
    csh1, csc1 = (jnp.broadcast_to(m[None, None, :], (bsz, 1, d)) for m in jnp.split(mod[bsz], 6)[:2])

    w_rw = w_in[l][:, :RWKV_COLS].astype(BF16)
    w_cv = w_in[l][:, RWKV_COLS:].astype(BF16)
    n1g = norm1_g[l][None, :]
    px_rw, px_cv = _in_proj(x, sh1, sc1, n1g, w_rw, w_cv, IN_PROJ_ROWS)
    pc_rw, _ = _in_proj(ctx, csh1, csc1, n1g, w_rw, w_cv, min(IN_PROJ_ROWS, lc))

    zw = jnp.zeros((LORA_W, 2 * D_RWKV), F32)
    lora = jnp.concatenate([
        jnp.concatenate([w_lora_up[l, 0], w_lora_up[l, 1], zw], axis=1),
        jnp.concatenate([zw, a_lora_up[l, 0], a_lora_up[l, 1]], axis=1)], axis=0).astype(BF16)
    ridx = jnp.arange(CHUNK)
    tril = jnp.where(ridx[None, :] <= ridx[:, None], 1.0, 0.0).astype(BF16)
    triu = jnp.where(ridx[None, :] >= ridx[:, None], 1.0, 0.0).astype(BF16)
    ones_bd = _block_diag_ones(PAIR, HEAD).astype(BF16)
    consts = (shift_mu[l][None, :], lora, w0[l], a0[l], k_k[l][None, :], k_a[l][None, :],
              r_k[l].reshape(1, D_RWKV), g_lora_up[l].astype(BF16), ones_bd, tril, triu)

    chunks_c = _prep(pc_rw, consts, False, lc)[:5]
    *chunks_x, bonus, gate = _prep(px_rw, consts, True, PREP_ROWS)

    s_zero = jnp.zeros((bsz, 2, N_PAIR, HEAD, PAIR), F32)
    _, _, s_ctx = _scan(*chunks_c, s_zero)
    yf, yb, _ = _scan(*chunks_x, s_ctx)

    rw_t = router_w[l].T
    rwh = rw_t.astype(BF16)
    rwl = (rw_t - rwh.astype(F32)).astype(BF16)
    avg = _block_diag_ones(PAIR, HEAD, 1.0 / HEAD).astype(BF16)
    xm, hx, aff_t = _out(yf, yb, bonus, gate, px_cv, x, g1, sh2, sc2, norm2_g[l][None, :], gn_g[l][None, :],
                         gn_b[l][None, :], conv_w[l], w_out[l].astype(BF16), rwh, rwl, avg, OUT_TILE)

    lane_idx = jnp.arange(TOKEN_ROW)
    tri_lanes = jnp.where(lane_idx[:, None] <= lane_idx[None, :], 1.0, 0.0).astype(BF16)
    cnt, slot, edges = _topk(aff_t, tri_lanes, cap, sb)

    ye = _moe(edges[:, :, :cap // sb + 1].reshape(-1), hx, slot.reshape(bsz, ne, t // TOKEN_ROW, TOKEN_ROW),
              exp_w_gate[l], exp_w_up[l], exp_w_down[l], cap, win, sb)
    tr = lambda a: jnp.transpose(a, (0, 2, 1))
    first_slot = jnp.concatenate([cnt[:, :, ::COMB_ROWS], jnp.full((bsz, ne, 1), cap, jnp.int32)], axis=-1).reshape(-1)
    return _comb(first_slot, xm, ye, tr(slot), tr(aff_t), g2, final_g[None, :], COMB_TILE)
```
